```python
import math
import jax, jax.numpy as jnp
from jax import lax
import numpy as np

D_MODEL = 1024
BATCH = 8
SEQ = 4096
DEPTH = 2

N_MIXERS = 4
GROUP_WIDTH = D_MODEL // N_MIXERS
HEAD_DIM = 64
N_GROUP_HEADS = GROUP_WIDTH // HEAD_DIM
IN_WIDTH = 9 * GROUP_WIDTH
CONV_WIDTH = 31
POOL_WINDOWS = (2, 4, 8, 16)
POOL_GROUP = GROUP_WIDTH // len(POOL_WINDOWS)
DILATED_PATTERNS = ((128, 1), (512, 4), (2048, 16))
BLOCK = 128
DIFF_QK_DIM = HEAD_DIM // 2
ROPE_THETA = 500000.0
ROPE_FRACTION = 4
D_FF = 2816
N_EXPERTS = 8
TOP_K = 2
D_FF_EXPERT = 3584
N_DENSE = (DEPTH + 1) // 2
N_MOE = DEPTH // 2
NORM_EPS = 1e-5

kernel_name = "hymba_style_four_mixer_hybrid_moe"


def rms_norm(x, g):
    xf = x.astype(jnp.float32)
    y = xf * lax.rsqrt(jnp.mean(xf * xf, axis=-1, keepdims=True) + NORM_EPS)
    return (y * g.astype(jnp.float32)).astype(x.dtype)


def rope_tables(seq, rot_dim):
    pos = jnp.arange(seq, dtype=jnp.float32)
    inv = ROPE_THETA ** (-jnp.arange(0, rot_dim, 2, dtype=jnp.float32) / rot_dim)
    ang = pos[:, None] * inv[None, :]
    return jnp.cos(ang), jnp.sin(ang)


def apply_partial_rope(x, cos, sin):
    rot = 2 * cos.shape[-1]
    half = rot // 2
    xr = x[..., :rot].astype(jnp.float32)
    x1, x2 = xr[..., :half], xr[..., half:]
    c = cos[None, :, None, :]
    s = sin[None, :, None, :]
    out = jnp.concatenate([x1 * c - x2 * s, x2 * c + x1 * s], axis=-1).astype(x.dtype)
    return jnp.concatenate([out, x[..., rot:]], axis=-1)


def conformer_conv(u, dw_w, dw_b, ln_g, ln_b, pw_w):
    a, b = jnp.split(u, 2, axis=-1)
    z = a * jax.nn.sigmoid(b)
    z = lax.conv_general_dilated(
        z, dw_w[:, None, :], window_strides=(1,), padding=[(CONV_WIDTH - 1, 0)],
        dimension_numbers=("NWC", "WIO", "NWC"), feature_group_count=GROUP_WIDTH) + dw_b
    zf = z.astype(jnp.float32)
    mu = jnp.mean(zf, axis=-1, keepdims=True)
    var = jnp.mean(jnp.square(zf - mu), axis=-1, keepdims=True)
    zf = (zf - mu) * lax.rsqrt(var + NORM_EPS) * ln_g.astype(jnp.float32) + ln_b.astype(jnp.float32)
    z = jax.nn.silu(zf).astype(u.dtype)
    return z @ pw_w


def pool_mixer(u, pool_w, pool_scale):
    B, S, _ = u.shape
    uf = u.astype(jnp.float32)
    cs = jnp.cumsum(uf, axis=1)
    pos = jnp.arange(S)
    outs = []
    for g, w in enumerate(POOL_WINDOWS):
        sl = slice(g * POOL_GROUP, (g + 1) * POOL_GROUP)
        c = cs[..., sl]
        lag = jnp.pad(c[:, :-w], ((0, 0), (w, 0), (0, 0)))
        cnt = jnp.minimum(pos + 1, w).astype(jnp.float32)[None, :, None]
        outs.append((c - lag) / cnt - uf[..., sl])
    y = jnp.stack(outs, axis=2).astype(u.dtype)
    y = jnp.einsum("bsgp,gpq->bsgq", y, pool_w).reshape(B, S, GROUP_WIDTH)
    return y * pool_scale


def dilated_branch(q, k, v, dil, span):
    B, S, H, Dh = q.shape
    nb = -(-S // (dil * BLOCK))
    Sp = nb * BLOCK * dil

    def to_blocks(t):
        t = jnp.pad(t, ((0, 0), (0, Sp - S), (0, 0), (0, 0)))
        return t.reshape(B, nb, BLOCK, dil, H, Dh)

    def with_prev(t):
        prev = jnp.pad(t[:, :-1], ((0, 0), (1, 0), (0, 0), (0, 0), (0, 0), (0, 0)))
        return jnp.concatenate([prev, t], axis=2)

    qb = to_blocks(q)
    kk = with_prev(to_blocks(k))
    vv = with_prev(to_blocks(v))
    i = jnp.arange(BLOCK)[:, None]
    j = jnp.arange(2 * BLOCK)[None, :]
    dist = BLOCK + i - j
    band = (dist >= 0) & (dist <= span)
    valid = (jnp.arange(nb)[:, None, None] > 0) | (j >= BLOCK)[None]
    mask = band[None] & valid
    s = jnp.einsum("bnirhd,bnjrhd->bnrhij", qb, kk).astype(jnp.float32) * (Dh ** -0.5)
    s = jnp.where(mask[None, :, None, None], s, -jnp.inf)
    m = jnp.max(s, axis=-1, keepdims=True)
    p = jnp.exp(s - m)
    l = jnp.sum(p, axis=-1)
    o = jnp.einsum("bnrhij,bnjrhd->bnirhd", p.astype(v.dtype), vv).astype(jnp.float32)
    l_t = jnp.transpose(l, (0, 1, 4, 2, 3))
    lse = jnp.transpose(m[..., 0], (0, 1, 4, 2, 3)) + jnp.log(l_t)
    o = (o / l_t[..., None]).reshape(B, Sp, H, Dh)[:, :S]
    lse = lse.reshape(B, Sp, H)[:, :S]
    return o, lse


def dilated_attention(q, k, v):
    outs, lses = [], []
    for window, dil in DILATED_PATTERNS:
        o, lse = dilated_branch(q, k, v, dil, window // dil)
        outs.append(o)
        lses.append(lse)
    wts = jax.nn.softmax(jnp.stack(lses), axis=0)
    out = jnp.sum(wts[..., None] * jnp.stack(outs), axis=0)
    return out.astype(q.dtype)


def diff_attention(q, k, v, lam, lam_init, ln_g):
    B, S, H, Dv = v.shape
    nb = S // BLOCK
    q1, q2 = jnp.split(q, 2, axis=-1)
    k1, k2 = jnp.split(k, 2, axis=-1)
    scale = DIFF_QK_DIM ** -0.5
    q1b = jnp.moveaxis(q1.reshape(B, nb, BLOCK, H, DIFF_QK_DIM), 1, 0)
    q2b = jnp.moveaxis(q2.reshape(B, nb, BLOCK, H, DIFF_QK_DIM), 1, 0)
    kpos = jnp.arange(S)

    def one_block(args):
        n, qb1, qb2 = args
        qpos = n * BLOCK + jnp.arange(BLOCK)
        causal = kpos[None, :] <= qpos[:, None]

        def probs(qb, kf):
            s = jnp.einsum("bihd,bjhd->bhij", qb, kf).astype(jnp.float32) * scale
            return jax.nn.softmax(jnp.where(causal, s, -jnp.inf), axis=-1)

        a = probs(qb1, k1) - lam * probs(qb2, k2)
        return jnp.einsum("bhij,bjhd->bihd", a.astype(v.dtype), v)

    o = lax.map(one_block, (jnp.arange(nb), q1b, q2b))
    o = jnp.moveaxis(o, 0, 1).reshape(B, S, H, Dv)
    o = rms_norm(o, ln_g) * (1.0 - lam_init)
    return o.reshape(B, S, H * Dv)


def token_mixing(h, w_in, dw_w, dw_b, ln_g, ln_b, pw_w, pool_w, pool_scale,
                 lam_vecs, diff_g, w_out, lam_init, cos_c, sin_c, cos_d, sin_d):
    B, S, _ = h.shape
    G = GROUP_WIDTH
    u = h @ w_in
    y_a = conformer_conv(u[..., :2 * G], dw_w, dw_b, ln_g, ln_b, pw_w)
    y_b = pool_mixer(u[..., 2 * G:3 * G], pool_w, pool_scale)
    heads = lambda t: t.reshape(B, S, N_GROUP_HEADS, HEAD_DIM)
    qc = apply_partial_rope(heads(u[..., 3 * G:4 * G]), cos_c, sin_c)
    kc = apply_partial_rope(heads(u[..., 4 * G:5 * G]), cos_c, sin_c)
    vc = heads(u[..., 5 * G:6 * G])
    y_c = dilated_attention(qc, kc, vc).reshape(B, S, G)
    comp = lambda t: t.reshape(B, S, 2 * N_GROUP_HEADS, DIFF_QK_DIM)
    qd = apply_partial_rope(comp(u[..., 6 * G:7 * G]), cos_d, sin_d).reshape(B, S, N_GROUP_HEADS, HEAD_DIM)
    kd = apply_partial_rope(comp(u[..., 7 * G:8 * G]), cos_d, sin_d).reshape(B, S, N_GROUP_HEADS, HEAD_DIM)
    vd = heads(u[..., 8 * G:9 * G])
    lv = lam_vecs.astype(jnp.float32)
    lam = jnp.exp(jnp.sum(lv[0] * lv[1])) - jnp.exp(jnp.sum(lv[2] * lv[3])) + lam_init
    y_d = diff_attention(qd, kd, vd, lam, lam_init, diff_g)
    y = jnp.concatenate([y_a, y_b, y_c, y_d], axis=-1)
    return y @ w_out


def swiglu(h, wg, wu, wd):
    return (jax.nn.silu(h @ wg) * (h @ wu)) @ wd


def moe_swiglu(h, router, wg, wu, wd):
    B, S, D = h.shape
    t = h.reshape(-1, D)
    logits = (t @ router).astype(jnp.float32)
    top_vals, top_idx = lax.top_k(logits, TOP_K)
    gates = jax.nn.softmax(top_vals, axis=-1)
    combine = jnp.sum(jax.nn.one_hot(top_idx, N_EXPERTS, dtype=jnp.float32) * gates[..., None], axis=1)
    y = jnp.zeros(t.shape, jnp.float32)
    for e in range(N_EXPERTS):
        y = y + combine[:, e:e + 1] * swiglu(t, wg[e], wu[e], wd[e]).astype(jnp.float32)
    return y.astype(h.dtype).reshape(B, S, D)


def setup_inputs(seed: int = 0) -> dict:
    key = jax.random.key(seed)
    ks = jax.random.split(key, 24)
    f32 = jnp.float32
    nrm = lambda k, shape, scale: jax.random.normal(k, shape, f32) * scale
    G, L = GROUP_WIDTH, DEPTH
    return {
        "x": jax.random.normal(ks[0], (BATCH, SEQ, D_MODEL), f32),
        "norm1_g": 1.0 + nrm(ks[1], (L, D_MODEL), 0.02),
        "w_in": nrm(ks[2], (L, D_MODEL, IN_WIDTH), D_MODEL ** -0.5),
        "conv_dw_w": nrm(ks[3], (L, CONV_WIDTH, G), CONV_WIDTH ** -0.5),
        "conv_dw_b": nrm(ks[4], (L, G), 0.02),
        "conv_ln_g": 1.0 + nrm(ks[5], (L, G), 0.02),
        "conv_ln_b": nrm(ks[6], (L, G), 0.02),
        "conv_pw_w": nrm(ks[7], (L, G, G), G ** -0.5),
        "pool_w": nrm(ks[8], (L, len(POOL_WINDOWS), POOL_GROUP, POOL_GROUP), POOL_GROUP ** -0.5),
        "pool_scale": 1.0 + nrm(ks[9], (L, G), 0.1),
        "diff_lam": nrm(ks[10], (L, 4, DIFF_QK_DIM), 0.1),
        "diff_ln_g": 1.0 + nrm(ks[11], (L, HEAD_DIM), 0.02),
        "w_out": nrm(ks[12], (L, D_MODEL, D_MODEL), D_MODEL ** -0.5),
        "norm2_g": 1.0 + nrm(ks[13], (L, D_MODEL), 0.02),
        "ffn_w_gate": nrm(ks[14], (N_DENSE, D_MODEL, D_FF), D_MODEL ** -0.5),
        "ffn_w_up": nrm(ks[15], (N_DENSE, D_MODEL, D_FF), D_MODEL ** -0.5),
        "ffn_w_down": nrm(ks[16], (N_DENSE, D_FF, D_MODEL), D_FF ** -0.5),
        "moe_router": nrm(ks[17], (N_MOE, D_MODEL, N_EXPERTS), D_MODEL ** -0.5),
        "moe_w_gate": nrm(ks[18], (N_MOE, N_EXPERTS, D_MODEL, D_FF_EXPERT), D_MODEL ** -0.5),
        "moe_w_up": nrm(ks[19], (N_MOE, N_EXPERTS, D_MODEL, D_FF_EXPERT), D_MODEL ** -0.5),
        "moe_w_down": nrm(ks[20], (N_MOE, N_EXPERTS, D_FF_EXPERT, D_MODEL), D_FF_EXPERT ** -0.5),
        "final_g": 1.0 + nrm(ks[21], (D_MODEL,), 0.02),
    }


def reference(x, norm1_g, w_in, conv_dw_w, conv_dw_b, conv_ln_g, conv_ln_b, conv_pw_w,
              pool_w, pool_scale, diff_lam, diff_ln_g, w_out, norm2_g,
              ffn_w_gate, ffn_w_up, ffn_w_down, moe_router, moe_w_gate, moe_w_up, moe_w_down,
              final_g):
    S = x.shape[1]
    cos_c, sin_c = rope_tables(S, HEAD_DIM // ROPE_FRACTION)
    cos_d, sin_d = rope_tables(S, DIFF_QK_DIM // ROPE_FRACTION)
    for layer in range(DEPTH):
        lam_init = 0.8 - 0.6 * math.exp(-0.3 * layer)
        h = rms_norm(x, norm1_g[layer])
        x = x + token_mixing(h, w_in[layer], conv_dw_w[layer], conv_dw_b[layer], conv_ln_g[layer],
                             conv_ln_b[layer], conv_pw_w[layer], pool_w[layer], pool_scale[layer],
                             diff_lam[layer], diff_ln_g[layer], w_out[layer], lam_init,
                             cos_c, sin_c, cos_d, sin_d)
        h = rms_norm(x, norm2_g[layer])
        if layer % 2 == 0:
            i = layer // 2
            x = x + swiglu(h, ffn_w_gate[i], ffn_w_up[i], ffn_w_down[i])
        else:
            i = layer // 2
            x = x + moe_swiglu(h, moe_router[i], moe_w_gate[i], moe_w_up[i], moe_w_down[i])
    return rms_norm(x, final_g)
```

```python
import functools
import math

import jax
import jax.numpy as jnp
from jax import lax
from jax.experimental import pallas as pl
from jax.experimental.pallas import tpu as pltpu

D_MODEL = 1024
GW = 256
N_HEADS = 4
HEAD_DIM = 64
QK_DIM = 32
CONV_WIDTH = 31
POOL_WINDOWS = (2, 4, 8, 16)
DILATIONS = (1, 4, 16)
BAND = 128
ROPE_THETA = 500000.0
N_EXPERTS = 8
NORM_EPS = 1e-5
HALO = 32
LANES = 128
VMEM_LIMIT = 56 * 1024 * 1024

BF16 = jnp.bfloat16
F32 = jnp.float32
NT_DIMS = (((1,), (1,)), ((), ()))


def _params(*sem):
    return pltpu.CompilerParams(dimension_semantics=sem, vmem_limit_bytes=VMEM_LIMIT)


def _lanes_in(shape, start, width):
    lane = lax.broadcasted_iota(jnp.int32, shape, len(shape) - 1)
    return (lane >= start) & (lane < start + width)


def _rms(x, g):
    ms = jnp.mean(x * x, axis=-1, keepdims=True)
    return x * lax.rsqrt(ms + NORM_EPS) * g


def _rope_table(seq, width, rot):
    half = rot // 2
    pos = jnp.arange(seq, dtype=F32)
    inv = ROPE_THETA ** (-jnp.arange(0, rot, 2, dtype=F32) / rot)
    ang = pos[:, None] * inv[None, :]
    cos, sin = jnp.cos(ang), jnp.sin(ang)
    lane = jnp.arange(GW) % width
    f = lane % half
    first = lane < half
    second = (lane >= half) & (lane < rot)
    c = jnp.where((lane < rot)[None, :], cos[:, f], 1.0)
    sn = jnp.where(first[None, :], -sin[:, f], 0.0)
    sp = jnp.where(second[None, :], sin[:, f], 0.0)
    return jnp.concatenate([c, sn, sp], axis=1)


def _rope(u, tab_ref, half):
    return (u * tab_ref[:, 0:GW]
            + pltpu.roll(u, GW - half, 1) * tab_ref[:, GW:2 * GW]
            + pltpu.roll(u, half, 1) * tab_ref[:, 2 * GW:3 * GW])


def _norm_proj_kernel(x_ref, g_ref, w_ref, rc_ref, rd_ref, uab_ref, qkvc_ref, qkvd_ref):
    h = _rms(x_ref[0], g_ref[...]).astype(BF16)

    def proj(grp):
        return jnp.dot(h, w_ref[:, grp * GW:(grp + 1) * GW], preferred_element_type=F32)

    for grp in range(3):
        uab_ref[0, :, grp * GW:(grp + 1) * GW] = proj(grp)
    c_scale = HEAD_DIM ** -0.5
    d_scale = QK_DIM ** -0.5
    qkvc_ref[0, :, 0:GW] = (_rope(proj(3), rc_ref, 8) * c_scale).astype(BF16)
    qkvc_ref[0, :, GW:2 * GW] = _rope(proj(4), rc_ref, 8).astype(BF16)
    qkvc_ref[0, :, 2 * GW:3 * GW] = proj(5).astype(BF16)
    qkvd_ref[0, :, 0:GW] = (_rope(proj(6), rd_ref, 4) * d_scale).astype(BF16)
    qkvd_ref[0, :, GW:2 * GW] = _rope(proj(7), rd_ref, 4).astype(BF16)
    qkvd_ref[0, :, 2 * GW:3 * GW] = proj(8).astype(BF16)


def _norm_proj(x, g, w, rope_c, rope_d, tm):
    b, s, d = x.shape
    grid = (s // tm, b)
    return pl.pallas_call(
        _norm_proj_kernel,
        grid=grid,
        in_specs=[
            pl.BlockSpec((1, tm, d), lambda i, j: (j, i, 0)),
            pl.BlockSpec((1, d), lambda i, j: (0, 0)),
            pl.BlockSpec((d, 9 * GW), lambda i, j: (0, 0)),
            pl.BlockSpec((tm, 3 * GW), lambda i, j: (i, 0)),
            pl.BlockSpec((tm, 3 * GW), lambda i, j: (i, 0)),
        ],
        out_specs=[
            pl.BlockSpec((1, tm, 3 * GW), lambda i, j: (j, i, 0)),
            pl.BlockSpec((1, tm, 3 * GW), lambda i, j: (j, i, 0)),
            pl.BlockSpec((1, tm, 3 * GW), lambda i, j: (j, i, 0)),
        ],
        out_shape=[
            jax.ShapeDtypeStruct((b, s, 3 * GW), F32),
            jax.ShapeDtypeStruct((b, s, 3 * GW), BF16),
            jax.ShapeDtypeStruct((b, s, 3 * GW), BF16),
        ],
        compiler_params=_params("arbitrary", "arbitrary"),
        name="norm_proj",
    )(x, g, w, rope_c, rope_d)


def _local_kernel(u_ref, dww_ref, dwb_ref, lng_ref, lnb_ref, pww_ref, plw_ref, pls_ref,
                  y_ref, zbuf, pbuf, *, ts):
    si = pl.program_id(1)

    @pl.when(si == 0)
    def _():
        zbuf[0:HALO, :] = jnp.zeros((HALO, GW), F32)
        pbuf[0:HALO, :] = jnp.zeros((HALO, GW), F32)

    a = u_ref[0, :, 0:GW]
    b = u_ref[0, :, GW:2 * GW]
    zbuf[HALO:HALO + ts, :] = a * jax.nn.sigmoid(b)
    pbuf[HALO:HALO + ts, :] = u_ref[0, :, 2 * GW:3 * GW]

    base = HALO - (CONV_WIDTH - 1)
    acc = jnp.zeros((ts, GW), F32)
    for k in range(CONV_WIDTH):
        acc = acc + dww_ref[k:k + 1, :] * zbuf[base + k:base + k + ts, :]
    z = acc + dwb_ref[...]
    mu = jnp.mean(z, axis=-1, keepdims=True)
    zc = z - mu
    var = jnp.mean(zc * zc, axis=-1, keepdims=True)
    zn = zc * lax.rsqrt(var + NORM_EPS) * lng_ref[...] + lnb_ref[...]
    zs = zn * jax.nn.sigmoid(zn)
    y_ref[0, :, 0:GW] = jnp.dot(zs.astype(BF16), pww_ref[...], preferred_element_type=F32).astype(BF16)

    lane = lax.broadcasted_iota(jnp.int32, (ts, GW), 1)
    pool_group = GW // len(POOL_WINDOWS)
    win = jnp.zeros((ts, GW), jnp.int32)
    for gi, w in enumerate(POOL_WINDOWS):
        win = jnp.where(lane >= gi * pool_group, w, win)
    p0 = pbuf[HALO:HALO + ts, :]
    tot = p0
    for k in range(1, max(POOL_WINDOWS)):
        pk = pbuf[HALO - k:HALO - k + ts, :]
        tot = tot + (pk if k < min(POOL_WINDOWS) else jnp.where(win > k, pk, 0.0))
    pos = si * ts + lax.broadcasted_iota(jnp.int32, (ts, GW), 0)
    cnt = jnp.minimum(pos + 1, win).astype(F32)
    pooled = tot / cnt - p0
    yb = jnp.dot(pooled.astype(BF16), plw_ref[...], preferred_element_type=F32) * pls_ref[...]
    y_ref[0, :, GW:2 * GW] = yb.astype(BF16)

    zbuf[0:HALO, :] = zbuf[ts:ts + HALO, :]
    pbuf[0:HALO, :] = pbuf[ts:ts + HALO, :]


def _local_mixers(uab, dww, dwb, lng, lnb, pww, plw, pls, ts):
    b, s, _ = uab.shape
    vec = lambda: pl.BlockSpec((1, GW), lambda i, j: (0, 0))
    mat = lambda: pl.BlockSpec((GW, GW), lambda i, j: (0, 0))
    return pl.pallas_call(
        functools.partial(_local_kernel, ts=ts),
        grid=(b, s // ts),
        in_specs=[
            pl.BlockSpec((1, ts, 3 * GW), lambda i, j: (i, j, 0)),
            pl.BlockSpec((HALO, GW), lambda i, j: (0, 0)),
            vec(), vec(), vec(), mat(), mat(), vec(),
        ],
        out_specs=pl.BlockSpec((1, ts, 2 * GW), lambda i, j: (i, j, 0)),
        out_shape=jax.ShapeDtypeStruct((b, s, 2 * GW), BF16),
        scratch_shapes=[pltpu.VMEM((HALO + ts, GW), F32), pltpu.VMEM((HALO + ts, GW), F32)],
        compiler_params=_params("arbitrary", "arbitrary"),
        name="local_mixers",
    )(uab, dww, dwb, lng, lnb, pww, plw, pls)


def _band_attn_kernel(q_ref, kc_ref, kp_ref, vc_ref, vp_ref, o_ref, lse_ref):
    n = pl.program_id(1)
    q = q_ref[0].astype(F32)
    kc, kp, vc, vp = kc_ref[0], kp_ref[0], vc_ref[0], vp_ref[0]
    row = lax.broadcasted_iota(jnp.int32, (BAND, BAND), 0)
    col = lax.broadcasted_iota(jnp.int32, (BAND, BAND), 1)
    mask_c = col <= row
    mask_p = (col >= row) & (n > 0)
    o = jnp.zeros((BAND, GW), F32)
    lse = jnp.zeros((BAND, GW), F32)
    for h in range(N_HEADS):
        hm = _lanes_in((BAND, GW), h * HEAD_DIM, HEAD_DIM)
        qh = jnp.where(hm, q, 0.0).astype(BF16)
        sc = lax.dot_general(qh, kc, NT_DIMS, preferred_element_type=F32)
        sp = lax.dot_general(qh, kp, NT_DIMS, preferred_element_type=F32)
        sc = jnp.where(mask_c, sc, -jnp.inf)
        sp = jnp.where(mask_p, sp, -jnp.inf)
        m = jnp.maximum(jnp.max(sc, axis=-1, keepdims=True), jnp.max(sp, axis=-1, keepdims=True))
        pc = jnp.exp(sc - m)
        pp = jnp.exp(sp - m)
        l = jnp.sum(pc, axis=-1, keepdims=True) + jnp.sum(pp, axis=-1, keepdims=True)
        acc = (jnp.dot(pc.astype(BF16), vc, preferred_element_type=F32)
               + jnp.dot(pp.astype(BF16), vp, preferred_element_type=F32))
        o = jnp.where(hm, acc / l, o)
        lse = jnp.where(hm, m + jnp.log(l), lse)
    o_ref[0] = o
    lse_ref[0] = lse


def _band_attn(qkv):
    nb, l, _ = qkv.shape
    cur = lambda c: pl.BlockSpec((1, BAND, GW), lambda i, j: (i, j, c))
    prev = lambda c: pl.BlockSpec((1, BAND, GW), lambda i, j: (i, jnp.maximum(j - 1, 0), c))
    out = pl.BlockSpec((1, BAND, GW), lambda i, j: (i, j, 0))
    return pl.pallas_call(
        _band_attn_kernel,
        grid=(nb, l // BAND),
        in_specs=[cur(0), cur(1), prev(1), cur(2), prev(2)],
        out_specs=[out, out],
        out_shape=[jax.ShapeDtypeStruct((nb, l, GW), F32)] * 2,
        compiler_params=_params("arbitrary", "arbitrary"),
        name="band_attn",
    )(qkv, qkv, qkv, qkv, qkv)


def _deinterleave(a, d):
    b, s, c = a.shape
    return a.reshape(b, s // d, d, c).transpose(0, 2, 1, 3).reshape(b * d, s // d, c)


def _interleave(a, d, b):
    _, l, c = a.shape
    return a.reshape(b, d, l, c).transpose(0, 2, 1, 3).reshape(b, l * d, c)


def _diff_attn_kernel(lam_ref, g_ref, q_ref, k_ref, v_ref, o_ref, acc_ref, *, tq, lam_init):
    i = pl.program_id(1)
    q = q_ref[0].astype(F32)
    lv = lam_ref[...]
    lam = (jnp.exp(jnp.sum(lv[0:1] * lv[1:2], axis=-1, keepdims=True))
           - jnp.exp(jnp.sum(lv[2:3] * lv[3:4], axis=-1, keepdims=True)) + lam_init)
    row = lax.broadcasted_iota(jnp.int32, (tq, tq), 0)
    col = lax.broadcasted_iota(jnp.int32, (tq, tq), 1)

    def attend(qm):
        def step(j, carry, diagonal):
            m, l = carry
            start = pl.multiple_of(j * tq, tq)
            kb = k_ref[0, pl.ds(start, tq), :]
            vb = v_ref[0, pl.ds(start, tq), :]
            s = lax.dot_general(qm, kb, NT_DIMS, preferred_element_type=F32)
            if diagonal:
                s = jnp.where(col <= row, s, -jnp.inf)
            m_new = jnp.maximum(m, jnp.max(s, axis=-1, keepdims=True))
            alpha = jnp.exp(m - m_new)
            p = jnp.exp(s - m_new)
            l = alpha * l + jnp.sum(p, axis=-1, keepdims=True)
            acc_ref[...] = alpha * acc_ref[...] + jnp.dot(p.astype(BF16), vb, preferred_element_type=F32)
            return m_new, l

        acc_ref[...] = jnp.zeros((tq, GW), F32)
        carry = (jnp.full((tq, 1), -jnp.inf, F32), jnp.zeros((tq, 1), F32))
        carry = lax.fori_loop(0, i, lambda j, c: step(j, c, False), carry)
        _, l = step(i, carry, True)
        return acc_ref[...] / l

    out = jnp.zeros((tq, GW), F32)
    for h in range(N_HEADS):
        o1 = attend(jnp.where(_lanes_in((tq, GW), h * QK_DIM, QK_DIM), q, 0.0).astype(BF16))
        o2 = attend(jnp.where(_lanes_in((tq, GW), LANES + h * QK_DIM, QK_DIM), q, 0.0).astype(BF16))
        oh = o1 - lam * o2
        hm = _lanes_in((tq, GW), h * HEAD_DIM, HEAD_DIM)
        ms = jnp.sum(jnp.where(hm, oh * oh, 0.0), axis=-1, keepdims=True) * (1.0 / HEAD_DIM)
        out = jnp.where(hm, oh * lax.rsqrt(ms + NORM_EPS), out)
    o_ref[0] = (out * g_ref[...] * (1.0 - lam_init)).astype(BF16)


def _diff_attn(qkv, lam_vecs, g_tiled, lam_init, tq):
    b, s, _ = qkv.shape
    return pl.pallas_call(
        functools.partial(_diff_attn_kernel, tq=tq, lam_init=lam_init),
        grid=(b, s // tq),
        in_specs=[
            pl.BlockSpec((4, QK_DIM), lambda i, j: (0, 0)),
            pl.BlockSpec((1, GW), lambda i, j: (0, 0)),
            pl.BlockSpec((1, tq, GW), lambda i, j: (i, j, 0)),
            pl.BlockSpec((1, s, GW), lambda i, j: (i, 0, 1)),
            pl.BlockSpec((1, s, GW), lambda i, j: (i, 0, 2)),
        ],
        out_specs=pl.BlockSpec((1, tq, GW), lambda i, j: (i, j, 0)),
        out_shape=jax.ShapeDtypeStruct((b, s, GW), BF16),
        scratch_shapes=[pltpu.VMEM((tq, GW), F32)],
        compiler_params=_params("arbitrary", "arbitrary"),
        name="diff_attn",
    )(lam_vecs, g_tiled, qkv, qkv, qkv)


def _out_proj_kernel(x_ref, yab_ref, o1_ref, o2_ref, o3_ref, l1_ref, l2_ref, l3_ref, yd_ref, w_ref, out_ref):
    l1, l2, l3 = l1_ref[...], l2_ref[...], l3_ref[...]
    m = jnp.maximum(jnp.maximum(l1, l2), l3)
    w1, w2, w3 = jnp.exp(l1 - m), jnp.exp(l2 - m), jnp.exp(l3 - m)
    yc = (w1 * o1_ref[...] + w2 * o2_ref[...] + w3 * o3_ref[...]) / (w1 + w2 + w3)
    acc = jnp.dot(yab_ref[...], w_ref[0:2 * GW, :], preferred_element_type=F32)
    acc = acc + jnp.dot(yc.astype(BF16), w_ref[2 * GW:3 * GW, :], preferred_element_type=F32)
    acc = acc + jnp.dot(yd_ref[...], w_ref[3 * GW:4 * GW, :], preferred_element_type=F32)
    out_ref[...] = x_ref[...] + acc


def _out_proj(x, yab, os_, lses, yd, w, tm):
    t, d = x.shape
    rows = lambda c: pl.BlockSpec((tm, c), lambda i: (i, 0))
    return pl.pallas_call(
        _out_proj_kernel,
        grid=(t // tm,),
        in_specs=[rows(d), rows(2 * GW)] + [rows(GW)] * 6 + [rows(GW), pl.BlockSpec((d, d), lambda i: (0, 0))],
        out_specs=rows(d),
        out_shape=jax.ShapeDtypeStruct((t, d), F32),
        compiler_params=_params("arbitrary"),
        name="out_proj",
    )(x, yab, *os_, *lses, yd, w)


def _ffn_kernel(x_ref, g_ref, wg_ref, wu_ref, wd_ref, out_ref, hn_ref, acc_ref):
    f = pl.program_id(1)

    @pl.when(f == 0)
    def _():
        hn_ref[...] = _rms(x_ref[...], g_ref[...]).astype(BF16)
        acc_ref[...] = jnp.zeros_like(acc_ref)

    hn = hn_ref[...]
    gate = jnp.dot(hn, wg_ref[...], preferred_element_type=F32)
    up = jnp.dot(hn, wu_ref[...], preferred_element_type=F32)
    act = (gate * jax.nn.sigmoid(gate) * up).astype(BF16)
    acc_ref[...] += jnp.dot(act, wd_ref[...], preferred_element_type=F32)

    @pl.when(f == pl.num_programs(1) - 1)
    def _():
        out_ref[...] = x_ref[...] + acc_ref[...]


def _ffn(x, g, wg, wu, wd, tm, tf):
    t, d = x.shape
    ff = wg.shape[1]
    return pl.pallas_call(
        _ffn_kernel,
        grid=(t // tm, ff // tf),
        in_specs=[
            pl.BlockSpec((tm, d), lambda i, f: (i, 0)),
            pl.BlockSpec((1, d), lambda i, f: (0, 0)),
            pl.BlockSpec((d, tf), lambda i, f: (0, f)),
            pl.BlockSpec((d, tf), lambda i, f: (0, f)),
            pl.BlockSpec((tf, d), lambda i, f: (f, 0)),
        ],
        out_specs=pl.BlockSpec((tm, d), lambda i, f: (i, 0)),
        out_shape=jax.ShapeDtypeStruct((t, d), F32),
        scratch_shapes=[pltpu.VMEM((tm, d), BF16), pltpu.VMEM((tm, d), F32)],
        compiler_params=_params("arbitrary", "arbitrary"),
        name="ffn_dense",
    )(x, g, wg, wu, wd)


def _router_kernel(x_ref, g_ref, r_ref, comb_ref):
    h = _rms(x_ref[...], g_ref[...])
    logits = jnp.dot(h, r_ref[...], preferred_element_type=F32, precision=lax.Precision.HIGHEST)
    lane = lax.broadcasted_iota(jnp.int32, logits.shape, 1)
    logits = jnp.where(lane < N_EXPERTS, logits, -jnp.inf)
    m1 = jnp.max(logits, axis=-1, keepdims=True)
    i1 = jnp.min(jnp.where(logits == m1, lane, LANES), axis=-1, keepdims=True)
    rest = jnp.where(lane == i1, -jnp.inf, logits)
    m2 = jnp.max(rest, axis=-1, keepdims=True)
    i2 = jnp.min(jnp.where(rest == m2, lane, LANES), axis=-1, keepdims=True)
    e2 = jnp.exp(m2 - m1)
    g1 = 1.0 / (1.0 + e2)
    g2 = e2 / (1.0 + e2)
    comb_ref[...] = jnp.where(lane == i1, g1, 0.0) + jnp.where(lane == i2, g2, 0.0)


def _router(x, g, r_pad, tm):
    t, d = x.shape
    return pl.pallas_call(
        _router_kernel,
        grid=(t // tm,),
        in_specs=[
            pl.BlockSpec((tm, d), lambda i: (i, 0)),
            pl.BlockSpec((1, d), lambda i: (0, 0)),
            pl.BlockSpec((d, LANES), lambda i: (0, 0)),
        ],
        out_specs=pl.BlockSpec((tm, LANES), lambda i: (i, 0)),
        out_shape=jax.ShapeDtypeStruct((t, LANES), F32),
        compiler_params=_params("arbitrary"),
        name="router",
    )(x, g, r_pad)


def _moe_kernel(x_ref, g_ref, comb_ref, wg_ref, wu_ref, wd_ref, fg_ref, out_ref, hn_ref, acc_ref):
    e = pl.program_id(1)
    f = pl.program_id(2)

    @pl.when((e == 0) & (f == 0))
    def _():
        hn_ref[...] = _rms(x_ref[...], g_ref[...]).astype(BF16)
        acc_ref[...] = jnp.zeros_like(acc_ref)

    comb = comb_ref[...]
    lane = lax.broadcasted_iota(jnp.int32, comb.shape, 1)
    ce = jnp.sum(jnp.where(lane == e, comb, 0.0), axis=-1, keepdims=True)
    hn = hn_ref[...]
    gate = jnp.dot(hn, wg_ref[...], preferred_element_type=F32)
    up = jnp.dot(hn, wu_ref[...], preferred_element_type=F32)
    act = (gate * jax.nn.sigmoid(gate) * up * ce).astype(BF16)
    acc_ref[...] += jnp.dot(act, wd_ref[...], preferred_element_type=F32)

    @pl.when((e == pl.num_programs(1) - 1) & (f == pl.num_programs(2) - 1))
    def _():
        out_ref[...] = _rms(x_ref[...] + acc_ref[...], fg_ref[...])


def _moe(x, g, comb, wg, wu, wd, final_g, tm, tf):
    t, d = x.shape
    ne, _, ff = wg.shape
    return pl.pallas_call(
        _moe_kernel,
        grid=(t // tm, ne, ff // tf),
        in_specs=[
            pl.BlockSpec((tm, d), lambda i, e, f: (i, 0)),
            pl.BlockSpec((1, d), lambda i, e, f: (0, 0)),
            pl.BlockSpec((tm, LANES), lambda i, e, f: (i, 0)),
            pl.BlockSpec((None, d, tf), lambda i, e, f: (e, 0, f)),
            pl.BlockSpec((None, d, tf), lambda i, e, f: (e, 0, f)),
            pl.BlockSpec((None, tf, d), lambda i, e, f: (e, f, 0)),
            pl.BlockSpec((1, d), lambda i, e, f: (0, 0)),
        ],
        out_specs=pl.BlockSpec((tm, d), lambda i, e, f: (i, 0)),
        out_shape=jax.ShapeDtypeStruct((t, d), F32),
        scratch_shapes=[pltpu.VMEM((tm, d), BF16), pltpu.VMEM((tm, d), F32)],
        compiler_params=_params("arbitrary", "arbitrary", "arbitrary"),
        name="moe",
    )(x, g, comb, wg, wu, wd, final_g)


def _permute_w_in(w):
    def perm(block):
        d = block.shape[0]
        return block.reshape(d, N_HEADS, 2, QK_DIM).transpose(0, 2, 1, 3).reshape(d, GW)
    parts = [w[:, :6 * GW], perm(w[:, 6 * GW:7 * GW]), perm(w[:, 7 * GW:8 * GW]), w[:, 8 * GW:]]
    return jnp.concatenate(parts, axis=1)


def _block_diag(pool_w):
    n, p, _ = pool_w.shape
    out = jnp.zeros((n * p, n * p), pool_w.dtype)
    for gi in range(n):
        out = out.at[gi * p:(gi + 1) * p, gi * p:(gi + 1) * p].set(pool_w[gi])
    return out


def _token_mixing(x, layer, lam_init, p, rope_c, rope_d):
    b, s, d = x.shape
    row = lambda v: v.reshape(1, -1)
    w_in = _permute_w_in(p["w_in"][layer]).astype(BF16)
    uab, qkvc, qkvd = _norm_proj(x, row(p["norm1_g"][layer]), w_in, rope_c, rope_d, tm=512)
    dww = jnp.zeros((HALO, GW), F32).at[:CONV_WIDTH].set(p["conv_dw_w"][layer])
    yab = _local_mixers(
        uab, dww, row(p["conv_dw_b"][layer]), row(p["conv_ln_g"][layer]), row(p["conv_ln_b"][layer]),
        p["conv_pw_w"][layer].astype(BF16), _block_diag(p["pool_w"][layer]).astype(BF16),
        row(p["pool_scale"][layer]), ts=512)
    os_, lses = [], []
    for dil in DILATIONS:
        o, lse = _band_attn(qkvc if dil == 1 else _deinterleave(qkvc, dil))
        if dil > 1:
            o, lse = _interleave(o, dil, b), _interleave(lse, dil, b)
        os_.append(o.reshape(b * s, GW))
        lses.append(lse.reshape(b * s, GW))
    yd = _diff_attn(qkvd, p["diff_lam"][layer], jnp.tile(p["diff_ln_g"][layer], N_HEADS).reshape(1, GW),
                    lam_init, tq=256)
    return _out_proj(x.reshape(b * s, d), yab.reshape(b * s, 2 * GW), os_, lses, yd.reshape(b * s, GW),
                     p["w_out"][layer].astype(BF16), tm=512)


def kernel(x, norm1_g, w_in, conv_dw_w, conv_dw_b, conv_ln_g, conv_ln_b, conv_pw_w, pool_w, pool_scale,
           diff_lam, diff_ln_g, w_out, norm2_g, ffn_w_gate, ffn_w_up, ffn_w_down, moe_router, moe_w_gate,
           moe_w_up, moe_w_down, final_g):
    b, s, d = x.shape
    p = dict(norm1_g=norm1_g, w_in=w_in, conv_dw_w=conv_dw_w, conv_dw_b=conv_dw_b, conv_ln_g=conv_ln_g,
             conv_ln_b=conv_ln_b, conv_pw_w=conv_pw_w, pool_w=pool_w, pool_scale=pool_scale,
             diff_lam=diff_lam, diff_ln_g=diff_ln_g, w_out=w_out)
    rope_c = _rope_table(s, HEAD_DIM, HEAD_DIM // 4)
    rope_d = _rope_table(s, QK_DIM, QK_DIM // 4)
    row = lambda v: v.reshape(1, -1)

    x1 = _token_mixing(x, 0, 0.8 - 0.6 * math.exp(-0.3 * 0), p, rope_c, rope_d)
    x2 = _ffn(x1, row(norm2_g[0]), ffn_w_gate[0].astype(BF16), ffn_w_up[0].astype(BF16),
              ffn_w_down[0].astype(BF16), tm=512, tf=1408)

    x3 = _token_mixing(x2.reshape(b, s, d), 1, 0.8 - 0.6 * math.exp(-0.3 * 1), p, rope_c, rope_d)
    r_pad = jnp.zeros((d, LANES), F32).at[:, :N_EXPERTS].set(moe_router[0])
    comb = _router(x3, row(norm2_g[1]), r_pad, tm=512)
    out = _moe(x3, row(norm2_g[1]), comb, moe_w_gate[0].astype(BF16), moe_w_up[0].astype(BF16),
               moe_w_down[0].astype(BF16), row(final_g), tm=1024, tf=512)
    return out.reshape(b, s, d)
```

```python
import functools
import math

import jax
import jax.numpy as jnp
from jax import lax
from jax.experimental import pallas as pl
from jax.experimental.pallas import tpu as pltpu

D_MODEL = 1024
GW = 256
N_HEADS = 4
HEAD_DIM = 64
QK_DIM = 32
CONV_WIDTH = 31
POOL_WINDOWS = (2, 4, 8, 16)
DILATIONS = (1, 4, 16)
BAND = 128
ROPE_THETA = 500000.0
N_EXPERTS = 8
NORM_EPS = 1e-5
HALO = 32
LANES = 128
VMEM_LIMIT = 56 * 1024 * 1024

BF16 = jnp.bfloat16
F32 = jnp.float32
NT_DIMS = (((1,), (1,)), ((), ()))


def _params(*sem):
    return pltpu.CompilerParams(dimension_semantics=sem, vmem_limit_bytes=VMEM_LIMIT)


def _lanes_in(shape, start, width):
    lane = lax.broadcasted_iota(jnp.int32, shape, len(shape) - 1)
    return (lane >= start) & (lane < start + width)


def _rms(x, g):
    ms = jnp.mean(x * x, axis=-1, keepdims=True)
    return x * lax.rsqrt(ms + NORM_EPS) * g


def _rope_table(seq, width, rot):
    half = rot // 2
    pos = jnp.arange(seq, dtype=F32)
    inv = ROPE_THETA ** (-jnp.arange(0, rot, 2, dtype=F32) / rot)
    ang = pos[:, None] * inv[None, :]
    cos, sin = jnp.cos(ang), jnp.sin(ang)
    lane = jnp.arange(GW) % width
    f = lane % half
    first = lane < half
    second = (lane >= half) & (lane < rot)
    c = jnp.where((lane < rot)[None, :], cos[:, f], 1.0)
    sn = jnp.where(first[None, :], -sin[:, f], 0.0)
    sp = jnp.where(second[None, :], sin[:, f], 0.0)
    return jnp.concatenate([c, sn, sp], axis=1)


def _rope(u, tab_ref, half):
    return (u * tab_ref[:, 0:GW]
            + pltpu.roll(u, GW - half, 1) * tab_ref[:, GW:2 * GW]
            + pltpu.roll(u, half, 1) * tab_ref[:, 2 * GW:3 * GW])


def _norm_proj_kernel(x_ref, g_ref, w_ref, rc_ref, rd_ref, uab_ref, qkvc_ref, qkvd_ref):
    h = _rms(x_ref[0], g_ref[...]).astype(BF16)

    def proj(grp):
        return jnp.dot(h, w_ref[:, grp * GW:(grp + 1) * GW], preferred_element_type=F32)

    for grp in range(3):
        uab_ref[0, :, grp * GW:(grp + 1) * GW] = proj(grp)
    c_scale = HEAD_DIM ** -0.5
    d_scale = QK_DIM ** -0.5 * math.log2(math.e)
    qkvc_ref[0, :, 0:GW] = (_rope(proj(3), rc_ref, 8) * c_scale).astype(BF16)
    qkvc_ref[0, :, GW:2 * GW] = _rope(proj(4), rc_ref, 8).astype(BF16)
    qkvc_ref[0, :, 2 * GW:3 * GW] = proj(5).astype(BF16)
    qkvd_ref[0, :, 0:GW] = (_rope(proj(6), rd_ref, 4) * d_scale).astype(BF16)
    qkvd_ref[0, :, GW:2 * GW] = _rope(proj(7), rd_ref, 4).astype(BF16)
    qkvd_ref[0, :, 2 * GW:3 * GW] = proj(8).astype(BF16)


def _norm_proj(x, g, w, rope_c, rope_d, tm):
    b, s, d = x.shape
    grid = (s // tm, b)
    return pl.pallas_call(
        _norm_proj_kernel,
        grid=grid,
        in_specs=[
            pl.BlockSpec((1, tm, d), lambda i, j: (j, i, 0)),
            pl.BlockSpec((1, d), lambda i, j: (0, 0)),
            pl.BlockSpec((d, 9 * GW), lambda i, j: (0, 0)),
            pl.BlockSpec((tm, 3 * GW), lambda i, j: (i, 0)),
            pl.BlockSpec((tm, 3 * GW), lambda i, j: (i, 0)),
        ],
        out_specs=[
            pl.BlockSpec((1, tm, 3 * GW), lambda i, j: (j, i, 0)),
            pl.BlockSpec((1, tm, 3 * GW), lambda i, j: (j, i, 0)),
            pl.BlockSpec((1, tm, 3 * GW), lambda i, j: (j, i, 0)),
        ],
        out_shape=[
            jax.ShapeDtypeStruct((b, s, 3 * GW), F32),
            jax.ShapeDtypeStruct((b, s, 3 * GW), BF16),
            jax.ShapeDtypeStruct((b, s, 3 * GW), BF16),
        ],
        compiler_params=_params("arbitrary", "arbitrary"),
        name="norm_proj",
    )(x, g, w, rope_c, rope_d)


def _local_kernel(u_ref, dww_ref, dwb_ref, lng_ref, lnb_ref, pww_ref, plw_ref, pls_ref,
                  y_ref, zbuf, pbuf, *, ts):
    si = pl.program_id(1)

    @pl.when(si == 0)
    def _():
        zbuf[0:HALO, :] = jnp.zeros((HALO, GW), F32)
        pbuf[0:HALO, :] = jnp.zeros((HALO, GW), F32)

    a = u_ref[0, :, 0:GW]
    b = u_ref[0, :, GW:2 * GW]
    zbuf[HALO:HALO + ts, :] = a * jax.nn.sigmoid(b)
    pbuf[HALO:HALO + ts, :] = u_ref[0, :, 2 * GW:3 * GW]

    base = HALO - (CONV_WIDTH - 1)
    acc = jnp.zeros((ts, GW), F32)
    for k in range(CONV_WIDTH):
        acc = acc + dww_ref[k:k + 1, :] * zbuf[base + k:base + k + ts, :]
    z = acc + dwb_ref[...]
    mu = jnp.mean(z, axis=-1, keepdims=True)
    zc = z - mu
    var = jnp.mean(zc * zc, axis=-1, keepdims=True)
    zn = zc * lax.rsqrt(var + NORM_EPS) * lng_ref[...] + lnb_ref[...]
    zs = zn * jax.nn.sigmoid(zn)
    y_ref[0, :, 0:GW] = jnp.dot(zs.astype(BF16), pww_ref[...], preferred_element_type=F32).astype(BF16)

    lane = lax.broadcasted_iota(jnp.int32, (ts, GW), 1)
    pool_group = GW // len(POOL_WINDOWS)
    win = jnp.zeros((ts, GW), jnp.int32)
    for gi, w in enumerate(POOL_WINDOWS):
        win = jnp.where(lane >= gi * pool_group, w, win)
    p0 = pbuf[HALO:HALO + ts, :]
    tot = p0
    for k in range(1, max(POOL_WINDOWS)):
        pk = pbuf[HALO - k:HALO - k + ts, :]
        tot = tot + (pk if k < min(POOL_WINDOWS) else jnp.where(win > k, pk, 0.0))
    pos = si * ts + lax.broadcasted_iota(jnp.int32, (ts, GW), 0)
    cnt = jnp.minimum(pos + 1, win).astype(F32)
    pooled = tot / cnt - p0
    yb = jnp.dot(pooled.astype(BF16), plw_ref[...], preferred_element_type=F32) * pls_ref[...]
    y_ref[0, :, GW:2 * GW] = yb.astype(BF16)

    zbuf[0:HALO, :] = zbuf[ts:ts + HALO, :]
    pbuf[0:HALO, :] = pbuf[ts:ts + HALO, :]


def _local_mixers(uab, dww, dwb, lng, lnb, pww, plw, pls, ts):
    b, s, _ = uab.shape
    vec = lambda: pl.BlockSpec((1, GW), lambda i, j: (0, 0))
    mat = lambda: pl.BlockSpec((GW, GW), lambda i, j: (0, 0))
    return pl.pallas_call(
        functools.partial(_local_kernel, ts=ts),
        grid=(b, s // ts),
        in_specs=[
            pl.BlockSpec((1, ts, 3 * GW), lambda i, j: (i, j, 0)),
            pl.BlockSpec((HALO, GW), lambda i, j: (0, 0)),
            vec(), vec(), vec(), mat(), mat(), vec(),
        ],
        out_specs=pl.BlockSpec((1, ts, 2 * GW), lambda i, j: (i, j, 0)),
        out_shape=jax.ShapeDtypeStruct((b, s, 2 * GW), BF16),
        scratch_shapes=[pltpu.VMEM((HALO + ts, GW), F32), pltpu.VMEM((HALO + ts, GW), F32)],
        compiler_params=_params("arbitrary", "arbitrary"),
        name="local_mixers",
    )(uab, dww, dwb, lng, lnb, pww, plw, pls)


def _band_attn_kernel(q_ref, kc_ref, kp_ref, vc_ref, vp_ref, o_ref, lse_ref):
    n = pl.program_id(1)
    q = q_ref[0].astype(F32)
    kc, kp, vc, vp = kc_ref[0], kp_ref[0], vc_ref[0], vp_ref[0]
    row = lax.broadcasted_iota(jnp.int32, (BAND, BAND), 0)
    col = lax.broadcasted_iota(jnp.int32, (BAND, BAND), 1)
    mask_c = col <= row
    mask_p = (col >= row) & (n > 0)
    o = jnp.zeros((BAND, GW), F32)
    lse = jnp.zeros((BAND, GW), F32)
    for h in range(N_HEADS):
        hm = _lanes_in((BAND, GW), h * HEAD_DIM, HEAD_DIM)
        qh = jnp.where(hm, q, 0.0).astype(BF16)
        sc = lax.dot_general(qh, kc, NT_DIMS, preferred_element_type=F32)
        sp = lax.dot_general(qh, kp, NT_DIMS, preferred_element_type=F32)
        sc = jnp.where(mask_c, sc, -jnp.inf)
        sp = jnp.where(mask_p, sp, -jnp.inf)
        m = jnp.maximum(jnp.max(sc, axis=-1, keepdims=True), jnp.max(sp, axis=-1, keepdims=True))
        pc = jnp.exp(sc - m)
        pp = jnp.exp(sp - m)
        l = jnp.sum(pc, axis=-1, keepdims=True) + jnp.sum(pp, axis=-1, keepdims=True)
        acc = (jnp.dot(pc.astype(BF16), vc, preferred_element_type=F32)
               + jnp.dot(pp.astype(BF16), vp, preferred_element_type=F32))
        o = jnp.where(hm, acc / l, o)
        lse = jnp.where(hm, m + jnp.log(l), lse)
    o_ref[0] = o
    lse_ref[0] = lse


def _band_attn(qkv):
    nb, l, _ = qkv.shape
    cur = lambda c: pl.BlockSpec((1, BAND, GW), lambda i, j: (i, j, c))
    prev = lambda c: pl.BlockSpec((1, BAND, GW), lambda i, j: (i, jnp.maximum(j - 1, 0), c))
    out = pl.BlockSpec((1, BAND, GW), lambda i, j: (i, j, 0))
    return pl.pallas_call(
        _band_attn_kernel,
        grid=(nb, l // BAND),
        in_specs=[cur(0), cur(1), prev(1), cur(2), prev(2)],
        out_specs=[out, out],
        out_shape=[jax.ShapeDtypeStruct((nb, l, GW), F32)] * 2,
        compiler_params=_params("arbitrary", "arbitrary"),
        name="band_attn",
    )(qkv, qkv, qkv, qkv, qkv)


def _deinterleave(a, d):
    b, s, c = a.shape
    return a.reshape(b, s // d, d, c).transpose(0, 2, 1, 3).reshape(b * d, s // d, c)


def _interleave(a, d, b):
    _, l, c = a.shape
    return a.reshape(b, d, l, c).transpose(0, 2, 1, 3).reshape(b, l * d, c)


def _diff_attn_kernel(lam_ref, g_ref, qt_ref, k_ref, vt_ref, o_ref, qm_ref, m_ref, l_ref, acc_ref, yt_ref,
                      st_ref, mx_ref, *, tq, lam_init):
    n_chain = 2 * N_HEADS
    sub = 8
    i = pl.program_id(1)
    qt = qt_ref[0].astype(F32)
    chan = lax.broadcasted_iota(jnp.int32, (GW, tq), 0)
    for c in range(n_chain):
        own = (chan >= c * QK_DIM) & (chan < (c + 1) * QK_DIM)
        qm_ref[c] = jnp.where(own, qt, 0.0).astype(BF16)
    m_ref[...] = jnp.full(m_ref.shape, -jnp.inf, F32)
    l_ref[...] = jnp.zeros(l_ref.shape, F32)
    acc_ref[...] = jnp.zeros(acc_ref.shape, F32)
    key = lax.broadcasted_iota(jnp.int32, (tq, tq), 0)
    qry = lax.broadcasted_iota(jnp.int32, (tq, tq), 1)

    def step(j, diagonal):
        kb = k_ref[0, pl.ds(pl.multiple_of(j * tq, tq), tq), :]
        for c in range(n_chain):
            st = jnp.dot(kb, qm_ref[c], preferred_element_type=F32)
            if diagonal:
                st = jnp.where(key <= qry, st, -jnp.inf)
            st_ref[c] = st
            mx_ref[c] = jnp.max(st, axis=0, keepdims=True)
        for c in range(n_chain):
            h = c % N_HEADS
            m_prev = m_ref[c]
            m_next = jnp.maximum(m_prev, mx_ref[c])
            alpha = jnp.exp2(m_prev - m_next)
            p = jnp.exp2(st_ref[c] - m_next)
            l_ref[c] = alpha * l_ref[c] + jnp.sum(p.reshape(tq // sub, sub, tq), axis=0)
            vt = vt_ref[0, j, h * HEAD_DIM:(h + 1) * HEAD_DIM, :]
            acc_ref[c] = alpha * acc_ref[c] + jnp.dot(vt, p.astype(BF16), preferred_element_type=F32)
            m_ref[c] = m_next

    def body(j, carry):
        step(j, False)
        return carry

    lax.fori_loop(0, i, body, 0)
    step(i, True)

    lv = lam_ref[...]
    lam = (jnp.exp(jnp.sum(lv[0:1] * lv[1:2], axis=-1, keepdims=True))
           - jnp.exp(jnp.sum(lv[2:3] * lv[3:4], axis=-1, keepdims=True)) + lam_init)
    for h in range(N_HEADS):
        l1 = jnp.sum(l_ref[h], axis=0, keepdims=True)
        l2 = jnp.sum(l_ref[N_HEADS + h], axis=0, keepdims=True)
        oh = acc_ref[h] / l1 - lam * (acc_ref[N_HEADS + h] / l2)
        ms = jnp.mean(oh * oh, axis=0, keepdims=True)
        g = g_ref[h * HEAD_DIM:(h + 1) * HEAD_DIM, :]
        yt_ref[h * HEAD_DIM:(h + 1) * HEAD_DIM, :] = oh * lax.rsqrt(ms + NORM_EPS) * (g * (1.0 - lam_init))
    o_ref[0] = jnp.transpose(yt_ref[...]).astype(BF16)


def _diff_attn(qkv, lam_vecs, g_col, lam_init, tq):
    b, s, _ = qkv.shape
    nk = s // tq
    qt = jnp.swapaxes(qkv[..., 0:GW], 1, 2)
    vt = jnp.swapaxes(qkv[..., 2 * GW:3 * GW].reshape(b, nk, tq, GW), 2, 3)
    n_chain = 2 * N_HEADS
    return pl.pallas_call(
        functools.partial(_diff_attn_kernel, tq=tq, lam_init=lam_init),
        grid=(b, nk),
        in_specs=[
            pl.BlockSpec((4, QK_DIM), lambda i, j: (0, 0)),
            pl.BlockSpec((GW, 1), lambda i, j: (0, 0)),
            pl.BlockSpec((1, GW, tq), lambda i, j: (i, 0, j)),
            pl.BlockSpec((1, s, GW), lambda i, j: (i, 0, 1)),
            pl.BlockSpec((1, nk, GW, tq), lambda i, j: (i, 0, 0, 0)),
        ],
        out_specs=pl.BlockSpec((1, tq, GW), lambda i, j: (i, j, 0)),
        out_shape=jax.ShapeDtypeStruct((b, s, GW), BF16),
        scratch_shapes=[
            pltpu.VMEM((n_chain, GW, tq), BF16),
            pltpu.VMEM((n_chain, 1, tq), F32),
            pltpu.VMEM((n_chain, 8, tq), F32),
            pltpu.VMEM((n_chain, HEAD_DIM, tq), F32),
            pltpu.VMEM((GW, tq), F32),
            pltpu.VMEM((n_chain, tq, tq), F32),
            pltpu.VMEM((n_chain, 1, tq), F32),
        ],
        compiler_params=_params("arbitrary", "arbitrary"),
        name="diff_attn",
    )(lam_vecs, g_col, qt, qkv, vt)


def _out_proj_kernel(x_ref, yab_ref, o1_ref, o2_ref, o3_ref, l1_ref, l2_ref, l3_ref, yd_ref, w_ref, out_ref):
    l1, l2, l3 = l1_ref[...], l2_ref[...], l3_ref[...]
    m = jnp.maximum(jnp.maximum(l1, l2), l3)
    w1, w2, w3 = jnp.exp(l1 - m), jnp.exp(l2 - m), jnp.exp(l3 - m)
    yc = (w1 * o1_ref[...] + w2 * o2_ref[...] + w3 * o3_ref[...]) / (w1 + w2 + w3)
    acc = jnp.dot(yab_ref[...], w_ref[0:2 * GW, :], preferred_element_type=F32)
    acc = acc + jnp.dot(yc.astype(BF16), w_ref[2 * GW:3 * GW, :], preferred_element_type=F32)
    acc = acc + jnp.dot(yd_ref[...], w_ref[3 * GW:4 * GW, :], preferred_element_type=F32)
    out_ref[...] = x_ref[...] + acc


def _out_proj(x, yab, os_, lses, yd, w, tm):
    t, d = x.shape
    rows = lambda c: pl.BlockSpec((tm, c), lambda i: (i, 0))
    return pl.pallas_call(
        _out_proj_kernel,
        grid=(t // tm,),
        in_specs=[rows(d), rows(2 * GW)] + [rows(GW)] * 6 + [rows(GW), pl.BlockSpec((d, d), lambda i: (0, 0))],
        out_specs=rows(d),
        out_shape=jax.ShapeDtypeStruct((t, d), F32),
        compiler_params=_params("arbitrary"),
        name="out_proj",
    )(x, yab, *os_, *lses, yd, w)


def _ffn_kernel(x_ref, g_ref, wg_ref, wu_ref, wd_ref, out_ref, hn_ref, acc_ref):
    f = pl.program_id(1)

    @pl.when(f == 0)
    def _():
        hn_ref[...] = _rms(x_ref[...], g_ref[...]).astype(BF16)
        acc_ref[...] = jnp.zeros_like(acc_ref)

    hn = hn_ref[...]
    gate = jnp.dot(hn, wg_ref[...], preferred_element_type=F32)
    up = jnp.dot(hn, wu_ref[...], preferred_element_type=F32)
    act = (gate * jax.nn.sigmoid(gate) * up).astype(BF16)
    acc_ref[...] += jnp.dot(act, wd_ref[...], preferred_element_type=F32)

    @pl.when(f == pl.num_programs(1) - 1)
    def _():
        out_ref[...] = x_ref[...] + acc_ref[...]


def _ffn(x, g, wg, wu, wd, tm, tf):
    t, d = x.shape
    ff = wg.shape[1]
    return pl.pallas_call(
        _ffn_kernel,
        grid=(t // tm, ff // tf),
        in_specs=[
            pl.BlockSpec((tm, d), lambda i, f: (i, 0)),
            pl.BlockSpec((1, d), lambda i, f: (0, 0)),
            pl.BlockSpec((d, tf), lambda i, f: (0, f)),
            pl.BlockSpec((d, tf), lambda i, f: (0, f)),
            pl.BlockSpec((tf, d), lambda i, f: (f, 0)),
        ],
        out_specs=pl.BlockSpec((tm, d), lambda i, f: (i, 0)),
        out_shape=jax.ShapeDtypeStruct((t, d), F32),
        scratch_shapes=[pltpu.VMEM((tm, d), BF16), pltpu.VMEM((tm, d), F32)],
        compiler_params=_params("arbitrary", "arbitrary"),
        name="ffn_dense",
    )(x, g, wg, wu, wd)


def _router_kernel(x_ref, g_ref, r_ref, comb_ref):
    h = _rms(x_ref[...], g_ref[...])
    logits = jnp.dot(h, r_ref[...], preferred_element_type=F32, precision=lax.Precision.HIGHEST)
    lane = lax.broadcasted_iota(jnp.int32, logits.shape, 1)
    logits = jnp.where(lane < N_EXPERTS, logits, -jnp.inf)
    m1 = jnp.max(logits, axis=-1, keepdims=True)
    i1 = jnp.min(jnp.where(logits == m1, lane, LANES), axis=-1, keepdims=True)
    rest = jnp.where(lane == i1, -jnp.inf, logits)
    m2 = jnp.max(rest, axis=-1, keepdims=True)
    i2 = jnp.min(jnp.where(rest == m2, lane, LANES), axis=-1, keepdims=True)
    e2 = jnp.exp(m2 - m1)
    g1 = 1.0 / (1.0 + e2)
    g2 = e2 / (1.0 + e2)
    comb_ref[...] = jnp.where(lane == i1, g1, 0.0) + jnp.where(lane == i2, g2, 0.0)


def _router(x, g, r_pad, tm):
    t, d = x.shape
    return pl.pallas_call(
        _router_kernel,
        grid=(t // tm,),
        in_specs=[
            pl.BlockSpec((tm, d), lambda i: (i, 0)),
            pl.BlockSpec((1, d), lambda i: (0, 0)),
            pl.BlockSpec((d, LANES), lambda i: (0, 0)),
        ],
        out_specs=pl.BlockSpec((tm, LANES), lambda i: (i, 0)),
        out_shape=jax.ShapeDtypeStruct((t, LANES), F32),
        compiler_params=_params("arbitrary"),
        name="router",
    )(x, g, r_pad)


def _moe_kernel(x_ref, g_ref, comb_ref, wg_ref, wu_ref, wd_ref, fg_ref, out_ref, hn_ref, acc_ref):
    e = pl.program_id(1)
    f = pl.program_id(2)

    @pl.when((e == 0) & (f == 0))
    def _():
        hn_ref[...] = _rms(x_ref[...], g_ref[...]).astype(BF16)
        acc_ref[...] = jnp.zeros_like(acc_ref)

    comb = comb_ref[...]
    lane = lax.broadcasted_iota(jnp.int32, comb.shape, 1)
    ce = jnp.sum(jnp.where(lane == e, comb, 0.0), axis=-1, keepdims=True)
    hn = hn_ref[...]
    gate = jnp.dot(hn, wg_ref[...], preferred_element_type=F32)
    up = jnp.dot(hn, wu_ref[...], preferred_element_type=F32)
    act = (gate * jax.nn.sigmoid(gate) * up * ce).astype(BF16)
    acc_ref[...] += jnp.dot(act, wd_ref[...], preferred_element_type=F32)

    @pl.when((e == pl.num_programs(1) - 1) & (f == pl.num_programs(2) - 1))
    def _():
        out_ref[...] = _rms(x_ref[...] + acc_ref[...], fg_ref[...])


def _moe(x, g, comb, wg, wu, wd, final_g, tm, tf):
    t, d = x.shape
    ne, _, ff = wg.shape
    return pl.pallas_call(
        _moe_kernel,
        grid=(t // tm, ne, ff // tf),
        in_specs=[
            pl.BlockSpec((tm, d), lambda i, e, f: (i, 0)),
            pl.BlockSpec((1, d), lambda i, e, f: (0, 0)),
            pl.BlockSpec((tm, LANES), lambda i, e, f: (i, 0)),
            pl.BlockSpec((None, d, tf), lambda i, e, f: (e, 0, f)),
            pl.BlockSpec((None, d, tf), lambda i, e, f: (e, 0, f)),
            pl.BlockSpec((None, tf, d), lambda i, e, f: (e, f, 0)),
            pl.BlockSpec((1, d), lambda i, e, f: (0, 0)),
        ],
        out_specs=pl.BlockSpec((tm, d), lambda i, e, f: (i, 0)),
        out_shape=jax.ShapeDtypeStruct((t, d), F32),
        scratch_shapes=[pltpu.VMEM((tm, d), BF16), pltpu.VMEM((tm, d), F32)],
        compiler_params=_params("arbitrary", "arbitrary", "arbitrary"),
        name="moe",
    )(x, g, comb, wg, wu, wd, final_g)


def _permute_w_in(w):
    def perm(block):
        d = block.shape[0]
        return block.reshape(d, N_HEADS, 2, QK_DIM).transpose(0, 2, 1, 3).reshape(d, GW)
    parts = [w[:, :6 * GW], perm(w[:, 6 * GW:7 * GW]), perm(w[:, 7 * GW:8 * GW]), w[:, 8 * GW:]]
    return jnp.concatenate(parts, axis=1)


def _block_diag(pool_w):
    n, p, _ = pool_w.shape
    out = jnp.zeros((n * p, n * p), pool_w.dtype)
    for gi in range(n):
        out = out.at[gi * p:(gi + 1) * p, gi * p:(gi + 1) * p].set(pool_w[gi])
    return out


def _token_mixing(x, layer, lam_init, p, rope_c, rope_d):
    b, s, d = x.shape
    row = lambda v: v.reshape(1, -1)
    w_in = _permute_w_in(p["w_in"][layer]).astype(BF16)
    uab, qkvc, qkvd = _norm_proj(x, row(p["norm1_g"][layer]), w_in, rope_c, rope_d, tm=512)
    dww = jnp.zeros((HALO, GW), F32).at[:CONV_WIDTH].set(p["conv_dw_w"][layer])
    yab = _local_mixers(
        uab, dww, row(p["conv_dw_b"][layer]), row(p["conv_ln_g"][layer]), row(p["conv_ln_b"][layer]),
        p["conv_pw_w"][layer].astype(BF16), _block_diag(p["pool_w"][layer]).astype(BF16),
        row(p["pool_scale"][layer]), ts=512)
    os_, lses = [], []
    for dil in DILATIONS:
        o, lse = _band_attn(qkvc if dil == 1 else _deinterleave(qkvc, dil))
        if dil > 1:
            o, lse = _interleave(o, dil, b), _interleave(lse, dil, b)
        os_.append(o.reshape(b * s, GW))
        lses.append(lse.reshape(b * s, GW))
    yd = _diff_attn(qkvd, p["diff_lam"][layer], jnp.tile(p["diff_ln_g"][layer], N_HEADS).reshape(GW, 1),
                    lam_init, tq=256)
    return _out_proj(x.reshape(b * s, d), yab.reshape(b * s, 2 * GW), os_, lses, yd.reshape(b * s, GW),
                     p["w_out"][layer].astype(BF16), tm=512)


def kernel(x, norm1_g, w_in, conv_dw_w, conv_dw_b, conv_ln_g, conv_ln_b, conv_pw_w, pool_w, pool_scale,
           diff_lam, diff_ln_g, w_out, norm2_g, ffn_w_gate, ffn_w_up, ffn_w_down, moe_router, moe_w_gate,
           moe_w_up, moe_w_down, final_g):
    b, s, d = x.shape
    p = dict(norm1_g=norm1_g, w_in=w_in, conv_dw_w=conv_dw_w, conv_dw_b=conv_dw_b, conv_ln_g=conv_ln_g,
             conv_ln_b=conv_ln_b, conv_pw_w=conv_pw_w, pool_w=pool_w, pool_scale=pool_scale,
             diff_lam=diff_lam, diff_ln_g=diff_ln_g, w_out=w_out)
    rope_c = _rope_table(s, HEAD_DIM, HEAD_DIM // 4)
    rope_d = _rope_table(s, QK_DIM, QK_DIM // 4)
    row = lambda v: v.reshape(1, -1)

    x1 = _token_mixing(x, 0, 0.8 - 0.6 * math.exp(-0.3 * 0), p, rope_c, rope_d)
    x2 = _ffn(x1, row(norm2_g[0]), ffn_w_gate[0].astype(BF16), ffn_w_up[0].astype(BF16),
              ffn_w_down[0].astype(BF16), tm=512, tf=1408)

    x3 = _token_mixing(x2.reshape(b, s, d), 1, 0.8 - 0.6 * math.exp(-0.3 * 1), p, rope_c, rope_d)
    r_pad = jnp.zeros((d, LANES), F32).at[:, :N_EXPERTS].set(moe_router[0])
    comb = _router(x3, row(norm2_g[1]), r_pad, tm=512)
    out = _moe(x3, row(norm2_g[1]), comb, moe_w_gate[0].astype(BF16), moe_w_up[0].astype(BF16),
               moe_w_down[0].astype(BF16), row(final_g), tm=1024, tf=512)
    return out.reshape(b, s, d)
```

```python
import functools
import math

import jax
import jax.numpy as jnp
from jax import lax
from jax.experimental import pallas as pl
from jax.experimental.pallas import tpu as pltpu

D_MODEL = 1024
GW = 256
N_HEADS = 4
HEAD_DIM = 64
QK_DIM = 32
CONV_WIDTH = 31
POOL_WINDOWS = (2, 4, 8, 16)
DILATIONS = (1, 4, 16)
BAND = 128
ROPE_THETA = 500000.0
N_EXPERTS = 8
NORM_EPS = 1e-5
HALO = 32
LANES = 128
VMEM_LIMIT = 56 * 1024 * 1024

BF16 = jnp.bfloat16
F32 = jnp.float32
NT_DIMS = (((1,), (1,)), ((), ()))


def _params(*sem):
    return pltpu.CompilerParams(dimension_semantics=sem, vmem_limit_bytes=VMEM_LIMIT)


def _lanes_in(shape, start, width):
    lane = lax.broadcasted_iota(jnp.int32, shape, len(shape) - 1)
    return (lane >= start) & (lane < start + width)


def _rms(x, g):
    ms = jnp.mean(x * x, axis=-1, keepdims=True)
    return x * lax.rsqrt(ms + NORM_EPS) * g


def _rope_table(seq, width, rot):
    half = rot // 2
    pos = jnp.arange(seq, dtype=F32)
    inv = ROPE_THETA ** (-jnp.arange(0, rot, 2, dtype=F32) / rot)
    ang = pos[:, None] * inv[None, :]
    cos, sin = jnp.cos(ang), jnp.sin(ang)
    lane = jnp.arange(GW) % width
    f = lane % half
    first = lane < half
    second = (lane >= half) & (lane < rot)
    c = jnp.where((lane < rot)[None, :], cos[:, f], 1.0)
    sn = jnp.where(first[None, :], -sin[:, f], 0.0)
    sp = jnp.where(second[None, :], sin[:, f], 0.0)
    return jnp.concatenate([c, sn, sp], axis=1)


def _rope(u, tab_ref, half):
    return (u * tab_ref[:, 0:GW]
            + pltpu.roll(u, GW - half, 1) * tab_ref[:, GW:2 * GW]
            + pltpu.roll(u, half, 1) * tab_ref[:, 2 * GW:3 * GW])


def _norm_proj_kernel(x_ref, g_ref, w_ref, rc_ref, rd_ref, uab_ref, qkvc_ref, qkvd_ref):
    h = _rms(x_ref[0], g_ref[...]).astype(BF16)

    def proj(grp):
        return jnp.dot(h, w_ref[:, grp * GW:(grp + 1) * GW], preferred_element_type=F32)

    for grp in range(3):
        uab_ref[0, :, grp * GW:(grp + 1) * GW] = proj(grp)
    c_scale = HEAD_DIM ** -0.5
    d_scale = QK_DIM ** -0.5 * math.log2(math.e)
    qkvc_ref[0, :, 0:GW] = (_rope(proj(3), rc_ref, 8) * c_scale).astype(BF16)
    qkvc_ref[0, :, GW:2 * GW] = _rope(proj(4), rc_ref, 8).astype(BF16)
    qkvc_ref[0, :, 2 * GW:3 * GW] = proj(5).astype(BF16)
    qkvd_ref[0, :, 0:GW] = (_rope(proj(6), rd_ref, 4) * d_scale).astype(BF16)
    qkvd_ref[0, :, GW:2 * GW] = _rope(proj(7), rd_ref, 4).astype(BF16)
    qkvd_ref[0, :, 2 * GW:3 * GW] = proj(8).astype(BF16)


def _norm_proj(x, g, w, rope_c, rope_d, tm):
    b, s, d = x.shape
    grid = (s // tm, b)
    return pl.pallas_call(
        _norm_proj_kernel,
        grid=grid,
        in_specs=[
            pl.BlockSpec((1, tm, d), lambda i, j: (j, i, 0)),
            pl.BlockSpec((1, d), lambda i, j: (0, 0)),
            pl.BlockSpec((d, 9 * GW), lambda i, j: (0, 0)),
            pl.BlockSpec((tm, 3 * GW), lambda i, j: (i, 0)),
            pl.BlockSpec((tm, 3 * GW), lambda i, j: (i, 0)),
        ],
        out_specs=[
            pl.BlockSpec((1, tm, 3 * GW), lambda i, j: (j, i, 0)),
            pl.BlockSpec((1, tm, 3 * GW), lambda i, j: (j, i, 0)),
            pl.BlockSpec((1, tm, 3 * GW), lambda i, j: (j, i, 0)),
        ],
        out_shape=[
            jax.ShapeDtypeStruct((b, s, 3 * GW), F32),
            jax.ShapeDtypeStruct((b, s, 3 * GW), BF16),
            jax.ShapeDtypeStruct((b, s, 3 * GW), BF16),
        ],
        compiler_params=_params("arbitrary", "arbitrary"),
        name="norm_proj",
    )(x, g, w, rope_c, rope_d)


def _local_kernel(u_ref, dww_ref, dwb_ref, lng_ref, lnb_ref, pww_ref, plw_ref, pls_ref,
                  y_ref, zbuf, pbuf, *, ts):
    si = pl.program_id(1)

    @pl.when(si == 0)
    def _():
        zbuf[0:HALO, :] = jnp.zeros((HALO, GW), F32)
        pbuf[0:HALO, :] = jnp.zeros((HALO, GW), F32)

    a = u_ref[0, :, 0:GW]
    b = u_ref[0, :, GW:2 * GW]
    zbuf[HALO:HALO + ts, :] = a * jax.nn.sigmoid(b)
    pbuf[HALO:HALO + ts, :] = u_ref[0, :, 2 * GW:3 * GW]

    base = HALO - (CONV_WIDTH - 1)
    acc = jnp.zeros((ts, GW), F32)
    for k in range(CONV_WIDTH):
        acc = acc + dww_ref[k:k + 1, :] * zbuf[base + k:base + k + ts, :]
    z = acc + dwb_ref[...]
    mu = jnp.mean(z, axis=-1, keepdims=True)
    zc = z - mu
    var = jnp.mean(zc * zc, axis=-1, keepdims=True)
    zn = zc * lax.rsqrt(var + NORM_EPS) * lng_ref[...] + lnb_ref[...]
    zs = zn * jax.nn.sigmoid(zn)
    y_ref[0, :, 0:GW] = jnp.dot(zs.astype(BF16), pww_ref[...], preferred_element_type=F32).astype(BF16)

    lane = lax.broadcasted_iota(jnp.int32, (ts, GW), 1)
    pool_group = GW // len(POOL_WINDOWS)
    win = jnp.zeros((ts, GW), jnp.int32)
    for gi, w in enumerate(POOL_WINDOWS):
        win = jnp.where(lane >= gi * pool_group, w, win)
    p0 = pbuf[HALO:HALO + ts, :]
    tot = p0
    for k in range(1, max(POOL_WINDOWS)):
        pk = pbuf[HALO - k:HALO - k + ts, :]
        tot = tot + (pk if k < min(POOL_WINDOWS) else jnp.where(win > k, pk, 0.0))
    pos = si * ts + lax.broadcasted_iota(jnp.int32, (ts, GW), 0)
    cnt = jnp.minimum(pos + 1, win).astype(F32)
    pooled = tot / cnt - p0
    yb = jnp.dot(pooled.astype(BF16), plw_ref[...], preferred_element_type=F32) * pls_ref[...]
    y_ref[0, :, GW:2 * GW] = yb.astype(BF16)

    zbuf[0:HALO, :] = zbuf[ts:ts + HALO, :]
    pbuf[0:HALO, :] = pbuf[ts:ts + HALO, :]


def _local_mixers(uab, dww, dwb, lng, lnb, pww, plw, pls, ts):
    b, s, _ = uab.shape
    vec = lambda: pl.BlockSpec((1, GW), lambda i, j: (0, 0))
    mat = lambda: pl.BlockSpec((GW, GW), lambda i, j: (0, 0))
    return pl.pallas_call(
        functools.partial(_local_kernel, ts=ts),
        grid=(b, s // ts),
        in_specs=[
            pl.BlockSpec((1, ts, 3 * GW), lambda i, j: (i, j, 0)),
            pl.BlockSpec((HALO, GW), lambda i, j: (0, 0)),
            vec(), vec(), vec(), mat(), mat(), vec(),
        ],
        out_specs=pl.BlockSpec((1, ts, 2 * GW), lambda i, j: (i, j, 0)),
        out_shape=jax.ShapeDtypeStruct((b, s, 2 * GW), BF16),
        scratch_shapes=[pltpu.VMEM((HALO + ts, GW), F32), pltpu.VMEM((HALO + ts, GW), F32)],
        compiler_params=_params("arbitrary", "arbitrary"),
        name="local_mixers",
    )(uab, dww, dwb, lng, lnb, pww, plw, pls)


def _band_attn_kernel(q_ref, kc_ref, kp_ref, vc_ref, vp_ref, o_ref, lse_ref):
    n = pl.program_id(1)
    q = q_ref[0].astype(F32)
    kc, kp, vc, vp = kc_ref[0], kp_ref[0], vc_ref[0], vp_ref[0]
    row = lax.broadcasted_iota(jnp.int32, (BAND, BAND), 0)
    col = lax.broadcasted_iota(jnp.int32, (BAND, BAND), 1)
    mask_c = col <= row
    mask_p = (col >= row) & (n > 0)
    o = jnp.zeros((BAND, GW), F32)
    lse = jnp.zeros((BAND, GW), F32)
    for h in range(N_HEADS):
        hm = _lanes_in((BAND, GW), h * HEAD_DIM, HEAD_DIM)
        qh = jnp.where(hm, q, 0.0).astype(BF16)
        sc = lax.dot_general(qh, kc, NT_DIMS, preferred_element_type=F32)
        sp = lax.dot_general(qh, kp, NT_DIMS, preferred_element_type=F32)
        sc = jnp.where(mask_c, sc, -jnp.inf)
        sp = jnp.where(mask_p, sp, -jnp.inf)
        m = jnp.maximum(jnp.max(sc, axis=-1, keepdims=True), jnp.max(sp, axis=-1, keepdims=True))
        pc = jnp.exp(sc - m)
        pp = jnp.exp(sp - m)
        l = jnp.sum(pc, axis=-1, keepdims=True) + jnp.sum(pp, axis=-1, keepdims=True)
        acc = (jnp.dot(pc.astype(BF16), vc, preferred_element_type=F32)
               + jnp.dot(pp.astype(BF16), vp, preferred_element_type=F32))
        o = jnp.where(hm, acc / l, o)
        lse = jnp.where(hm, m + jnp.log(l), lse)
    o_ref[0] = o
    lse_ref[0] = lse


def _band_attn(qkv):
    nb, l, _ = qkv.shape
    cur = lambda c: pl.BlockSpec((1, BAND, GW), lambda i, j: (i, j, c))
    prev = lambda c: pl.BlockSpec((1, BAND, GW), lambda i, j: (i, jnp.maximum(j - 1, 0), c))
    out = pl.BlockSpec((1, BAND, GW), lambda i, j: (i, j, 0))
    return pl.pallas_call(
        _band_attn_kernel,
        grid=(nb, l // BAND),
        in_specs=[cur(0), cur(1), prev(1), cur(2), prev(2)],
        out_specs=[out, out],
        out_shape=[jax.ShapeDtypeStruct((nb, l, GW), F32)] * 2,
        compiler_params=_params("arbitrary", "arbitrary"),
        name="band_attn",
    )(qkv, qkv, qkv, qkv, qkv)


def _deinterleave(a, d):
    b, s, c = a.shape
    return a.reshape(b, s // d, d, c).transpose(0, 2, 1, 3).reshape(b * d, s // d, c)


def _interleave(a, d, b):
    _, l, c = a.shape
    return a.reshape(b, d, l, c).transpose(0, 2, 1, 3).reshape(b, l * d, c)


def _diff_attn_kernel(lam_ref, g_ref, qt_ref, k_ref, vt_ref, o_ref, qm_ref, m_ref, l_ref, acc_ref, yt_ref,
                      st_ref, mx_ref, *, tq, lam_init):
    n_chain = 2 * N_HEADS
    sub = 8
    i = pl.program_id(1)
    qt = qt_ref[0].astype(F32)
    chan = lax.broadcasted_iota(jnp.int32, (GW, tq), 0)
    for c in range(n_chain):
        own = (chan >= c * QK_DIM) & (chan < (c + 1) * QK_DIM)
        qm_ref[c] = jnp.where(own, qt, 0.0).astype(BF16)
    m_ref[...] = jnp.full(m_ref.shape, -jnp.inf, F32)
    l_ref[...] = jnp.zeros(l_ref.shape, F32)
    acc_ref[...] = jnp.zeros(acc_ref.shape, F32)
    key = lax.broadcasted_iota(jnp.int32, (tq, tq), 0)
    qry = lax.broadcasted_iota(jnp.int32, (tq, tq), 1)

    def step(j, diagonal):
        kb = k_ref[0, pl.ds(pl.multiple_of(j * tq, tq), tq), :]
        for c in range(n_chain):
            st = jnp.dot(kb, qm_ref[c], preferred_element_type=F32)
            if diagonal:
                st = jnp.where(key <= qry, st, -jnp.inf)
            st_ref[c] = st
            mx_ref[c] = jnp.max(st, axis=0, keepdims=True)
        for c in range(n_chain):
            h = c % N_HEADS
            m_prev = m_ref[c]
            m_next = jnp.maximum(m_prev, mx_ref[c])
            alpha = jnp.exp2(m_prev - m_next)
            p = jnp.exp2(st_ref[c] - m_next)
            l_ref[c] = alpha * l_ref[c] + jnp.sum(p.reshape(tq // sub, sub, tq), axis=0)
            vt = vt_ref[0, j, h * HEAD_DIM:(h + 1) * HEAD_DIM, :]
            acc_ref[c] = alpha * acc_ref[c] + jnp.dot(vt, p.astype(BF16), preferred_element_type=F32)
            m_ref[c] = m_next

    def body(j, carry):
        step(j, False)
        return carry

    lax.fori_loop(0, i, body, 0)
    step(i, True)

    lv = lam_ref[...]
    lam = (jnp.exp(jnp.sum(lv[0:1] * lv[1:2], axis=-1, keepdims=True))
           - jnp.exp(jnp.sum(lv[2:3] * lv[3:4], axis=-1, keepdims=True)) + lam_init)
    for h in range(N_HEADS):
        l1 = jnp.sum(l_ref[h], axis=0, keepdims=True)
        l2 = jnp.sum(l_ref[N_HEADS + h], axis=0, keepdims=True)
        oh = acc_ref[h] / l1 - lam * (acc_ref[N_HEADS + h] / l2)
        ms = jnp.mean(oh * oh, axis=0, keepdims=True)
        g = g_ref[h * HEAD_DIM:(h + 1) * HEAD_DIM, :]
        yt_ref[h * HEAD_DIM:(h + 1) * HEAD_DIM, :] = oh * lax.rsqrt(ms + NORM_EPS) * (g * (1.0 - lam_init))
    o_ref[0] = jnp.transpose(yt_ref[...]).astype(BF16)


def _diff_attn(qkv, lam_vecs, g_col, lam_init, tq):
    b, s, _ = qkv.shape
    nk = s // tq
    qt = jnp.swapaxes(qkv[..., 0:GW], 1, 2)
    vt = jnp.swapaxes(qkv[..., 2 * GW:3 * GW].reshape(b, nk, tq, GW), 2, 3)
    n_chain = 2 * N_HEADS
    return pl.pallas_call(
        functools.partial(_diff_attn_kernel, tq=tq, lam_init=lam_init),
        grid=(b, nk),
        in_specs=[
            pl.BlockSpec((4, QK_DIM), lambda i, j: (0, 0)),
            pl.BlockSpec((GW, 1), lambda i, j: (0, 0)),
            pl.BlockSpec((1, GW, tq), lambda i, j: (i, 0, j)),
            pl.BlockSpec((1, s, GW), lambda i, j: (i, 0, 1)),
            pl.BlockSpec((1, nk, GW, tq), lambda i, j: (i, 0, 0, 0)),
        ],
        out_specs=pl.BlockSpec((1, tq, GW), lambda i, j: (i, j, 0)),
        out_shape=jax.ShapeDtypeStruct((b, s, GW), BF16),
        scratch_shapes=[
            pltpu.VMEM((n_chain, GW, tq), BF16),
            pltpu.VMEM((n_chain, 1, tq), F32),
            pltpu.VMEM((n_chain, 8, tq), F32),
            pltpu.VMEM((n_chain, HEAD_DIM, tq), F32),
            pltpu.VMEM((GW, tq), F32),
            pltpu.VMEM((n_chain, tq, tq), F32),
            pltpu.VMEM((n_chain, 1, tq), F32),
        ],
        compiler_params=_params("arbitrary", "arbitrary"),
        name="diff_attn",
    )(lam_vecs, g_col, qt, qkv, vt)


def _out_proj_kernel(x_ref, yab_ref, o1_ref, o2_ref, o3_ref, l1_ref, l2_ref, l3_ref, yd_ref, w_ref, out_ref):
    l1, l2, l3 = l1_ref[...], l2_ref[...], l3_ref[...]
    m = jnp.maximum(jnp.maximum(l1, l2), l3)
    w1, w2, w3 = jnp.exp(l1 - m), jnp.exp(l2 - m), jnp.exp(l3 - m)
    yc = (w1 * o1_ref[...] + w2 * o2_ref[...] + w3 * o3_ref[...]) / (w1 + w2 + w3)
    acc = jnp.dot(yab_ref[...], w_ref[0:2 * GW, :], preferred_element_type=F32)
    acc = acc + jnp.dot(yc.astype(BF16), w_ref[2 * GW:3 * GW, :], preferred_element_type=F32)
    acc = acc + jnp.dot(yd_ref[...], w_ref[3 * GW:4 * GW, :], preferred_element_type=F32)
    out_ref[...] = x_ref[...] + acc


def _out_proj(x, yab, os_, lses, yd, w, tm):
    t, d = x.shape
    rows = lambda c: pl.BlockSpec((tm, c), lambda i: (i, 0))
    return pl.pallas_call(
        _out_proj_kernel,
        grid=(t // tm,),
        in_specs=[rows(d), rows(2 * GW)] + [rows(GW)] * 6 + [rows(GW), pl.BlockSpec((d, d), lambda i: (0, 0))],
        out_specs=rows(d),
        out_shape=jax.ShapeDtypeStruct((t, d), F32),
        compiler_params=_params("arbitrary"),
        name="out_proj",
    )(x, yab, *os_, *lses, yd, w)


def _ffn_kernel(x_ref, g_ref, wg_ref, wu_ref, wd_ref, out_ref, hn_ref, acc_ref):
    f = pl.program_id(1)

    @pl.when(f == 0)
    def _():
        hn_ref[...] = _rms(x_ref[...], g_ref[...]).astype(BF16)
        acc_ref[...] = jnp.zeros_like(acc_ref)

    hn = hn_ref[...]
    gate = jnp.dot(hn, wg_ref[...], preferred_element_type=F32)
    up = jnp.dot(hn, wu_ref[...], preferred_element_type=F32)
    act = (gate * jax.nn.sigmoid(gate) * up).astype(BF16)
    acc_ref[...] += jnp.dot(act, wd_ref[...], preferred_element_type=F32)

    @pl.when(f == pl.num_programs(1) - 1)
    def _():
        out_ref[...] = x_ref[...] + acc_ref[...]


def _ffn(x, g, wg, wu, wd, tm, tf):
    t, d = x.shape
    ff = wg.shape[1]
    return pl.pallas_call(
        _ffn_kernel,
        grid=(t // tm, ff // tf),
        in_specs=[
            pl.BlockSpec((tm, d), lambda i, f: (i, 0)),
            pl.BlockSpec((1, d), lambda i, f: (0, 0)),
            pl.BlockSpec((d, tf), lambda i, f: (0, f)),
            pl.BlockSpec((d, tf), lambda i, f: (0, f)),
            pl.BlockSpec((tf, d), lambda i, f: (f, 0)),
        ],
        out_specs=pl.BlockSpec((tm, d), lambda i, f: (i, 0)),
        out_shape=jax.ShapeDtypeStruct((t, d), F32),
        scratch_shapes=[pltpu.VMEM((tm, d), BF16), pltpu.VMEM((tm, d), F32)],
        compiler_params=_params("arbitrary", "arbitrary"),
        name="ffn_dense",
    )(x, g, wg, wu, wd)


META_E1, META_E2, META_RANK1, META_RANK2, META_G1, META_G2 = range(6)


def _router_kernel(x_ref, g_ref, r_ref, hn_ref, meta_ref, after_ref, carry_ref, *, blocks_per_chunk):
    @pl.when(pl.program_id(0) % blocks_per_chunk == 0)
    def _():
        carry_ref[...] = jnp.zeros_like(carry_ref)

    h = _rms(x_ref[...], g_ref[...])
    hn_ref[...] = h.astype(BF16)
    logits = jnp.dot(h, r_ref[...], preferred_element_type=F32, precision=lax.Precision.HIGHEST)
    tm = logits.shape[0]
    lane = lax.broadcasted_iota(jnp.int32, logits.shape, 1)
    logits = jnp.where(lane < N_EXPERTS, logits, -jnp.inf)
    m1 = jnp.max(logits, axis=-1, keepdims=True)
    i1 = jnp.min(jnp.where(logits == m1, lane, LANES), axis=-1, keepdims=True)
    rest = jnp.where(lane == i1, -jnp.inf, logits)
    m2 = jnp.max(rest, axis=-1, keepdims=True)
    i2 = jnp.min(jnp.where(rest == m2, lane, LANES), axis=-1, keepdims=True)
    e2 = jnp.exp(m2 - m1)
    g1 = 1.0 / (1.0 + e2)
    g2 = e2 / (1.0 + e2)

    oh1 = lane == i1
    oh2 = lane == i2
    both = jnp.where(oh1, 1.0, jnp.where(oh2, 1.0, 0.0))
    row = lax.broadcasted_iota(jnp.int32, (tm, tm), 0)
    col = lax.broadcasted_iota(jnp.int32, (tm, tm), 1)
    earlier = jnp.where(col < row, 1.0, 0.0).astype(BF16)
    carry = carry_ref[0:1, :]
    prefix = jnp.dot(earlier, both.astype(BF16), preferred_element_type=F32) + carry
    rank1 = jnp.sum(jnp.where(oh1, prefix, 0.0), axis=-1, keepdims=True)
    rank2 = jnp.sum(jnp.where(oh2, prefix, 0.0), axis=-1, keepdims=True)
    total = carry + jnp.sum(both, axis=0, keepdims=True)

    meta = jnp.zeros(logits.shape, F32)
    for k, val in ((META_E1, i1.astype(F32)), (META_E2, i2.astype(F32)), (META_RANK1, rank1),
                   (META_RANK2, rank2), (META_G1, g1), (META_G2, g2)):
        meta = jnp.where(lane == k, val, meta)
    meta_ref[...] = meta
    after_ref[0] = jnp.broadcast_to(total, after_ref.shape[1:])
    carry_ref[...] = jnp.broadcast_to(total, carry_ref.shape)


def _router(x, g, r_pad, tm, blocks_per_chunk):
    t, d = x.shape
    nblk = t // tm
    counts = pl.BlockSpec((1, 8, LANES), lambda i: (i, 0, 0))
    return pl.pallas_call(
        functools.partial(_router_kernel, blocks_per_chunk=blocks_per_chunk),
        grid=(nblk,),
        in_specs=[
            pl.BlockSpec((tm, d), lambda i: (i, 0)),
            pl.BlockSpec((1, d), lambda i: (0, 0)),
            pl.BlockSpec((d, LANES), lambda i: (0, 0)),
        ],
        out_specs=[
            pl.BlockSpec((tm, d), lambda i: (i, 0)),
            pl.BlockSpec((tm, LANES), lambda i: (i, 0)),
            counts,
        ],
        out_shape=[
            jax.ShapeDtypeStruct((t, d), BF16),
            jax.ShapeDtypeStruct((t, LANES), F32),
            jax.ShapeDtypeStruct((nblk, 8, LANES), F32),
        ],
        scratch_shapes=[pltpu.VMEM((8, LANES), F32)],
        compiler_params=_params("arbitrary"),
        name="router",
    )(x, g, r_pad)


SLOT_ROW1, SLOT_ROW2, SLOT_G1, SLOT_G2 = range(4)


def _moe_kernel(nt_ref, toff_ref, hn_ref, slotc_ref, slotr_ref, wg_ref, wu_ref, wd_ref, out_ref,
                xs_ref, ys_ref, *, tr):
    c, e, f = pl.program_id(0), pl.program_id(1), pl.program_id(2)
    last_f = pl.num_programs(2) - 1
    grp = c * N_EXPERTS + e
    n_tiles = nt_ref[grp]
    row0 = toff_ref[grp] * tr
    tc = hn_ref.shape[0]

    @pl.when((e == 0) & (f == 0))
    def _():
        out_ref[...] = jnp.zeros_like(out_ref)

    def rows_of(r):
        return pl.ds(pl.multiple_of(r * tr, tr), tr)

    def gather_tile(r, carry):
        target = (lax.broadcasted_iota(jnp.int32, (tr, tc), 0) + (row0 + r * tr)).astype(F32)
        hit1 = slotr_ref[SLOT_ROW1:SLOT_ROW1 + 1, :] == target
        hit2 = slotr_ref[SLOT_ROW2:SLOT_ROW2 + 1, :] == target
        oh = jnp.where(hit1, 1.0, jnp.where(hit2, 1.0, 0.0)).astype(BF16)
        xs_ref[rows_of(r), :] = jnp.dot(oh, hn_ref[...], preferred_element_type=F32).astype(BF16)
        return carry

    def ffn_tile(r, first):
        xt = xs_ref[rows_of(r), :]
        gate = jnp.dot(xt, wg_ref[...], preferred_element_type=F32)
        up = jnp.dot(xt, wu_ref[...], preferred_element_type=F32)
        act = (gate * jax.nn.sigmoid(gate) * up).astype(BF16)
        y = jnp.dot(act, wd_ref[...], preferred_element_type=F32)
        if first:
            ys_ref[rows_of(r), :] = y
        else:
            ys_ref[rows_of(r), :] += y

    def scatter_tile(r, carry):
        yb = ys_ref[rows_of(r), :].astype(BF16)
        target = (lax.broadcasted_iota(jnp.int32, (tc, tr), 1) + (row0 + r * tr)).astype(F32)
        hit1 = slotc_ref[:, SLOT_ROW1:SLOT_ROW1 + 1] == target
        hit2 = slotc_ref[:, SLOT_ROW2:SLOT_ROW2 + 1] == target
        gh = jnp.where(hit1, slotc_ref[:, SLOT_G1:SLOT_G1 + 1],
                       jnp.where(hit2, slotc_ref[:, SLOT_G2:SLOT_G2 + 1], 0.0)).astype(BF16)
        out_ref[...] += jnp.dot(gh, yb, preferred_element_type=F32)
        return carry

    def ffn_first(r, carry):
        ffn_tile(r, True)
        return carry

    def ffn_next(r, carry):
        ffn_tile(r, False)
        return carry

    @pl.when(f == 0)
    def _():
        lax.fori_loop(0, n_tiles, gather_tile, 0)
        lax.fori_loop(0, n_tiles, ffn_first, 0)

    @pl.when(f > 0)
    def _():
        lax.fori_loop(0, n_tiles, ffn_next, 0)

    @pl.when(f == last_f)
    def _():
        lax.fori_loop(0, n_tiles, scatter_tile, 0)


def _moe(hn, slot_cols, slot_rows, nt, toff, wg, wu, wd, tc, tr, tf):
    t, d = hn.shape
    ne, _, ff = wg.shape
    grid_spec = pltpu.PrefetchScalarGridSpec(
        num_scalar_prefetch=2,
        grid=(t // tc, ne, ff // tf),
        in_specs=[
            pl.BlockSpec((tc, d), lambda c, e, f, *_: (c, 0)),
            pl.BlockSpec((tc, 4), lambda c, e, f, *_: (c, 0)),
            pl.BlockSpec((8, tc), lambda c, e, f, *_: (0, c)),
            pl.BlockSpec((None, d, tf), lambda c, e, f, *_: (e, 0, f)),
            pl.BlockSpec((None, d, tf), lambda c, e, f, *_: (e, 0, f)),
            pl.BlockSpec((None, tf, d), lambda c, e, f, *_: (e, f, 0)),
        ],
        out_specs=pl.BlockSpec((tc, d), lambda c, e, f, *_: (c, 0)),
        scratch_shapes=[
            pltpu.VMEM((tc + tr, d), BF16),
            pltpu.VMEM((tc + tr, d), F32),
        ],
    )
    return pl.pallas_call(
        functools.partial(_moe_kernel, tr=tr),
        grid_spec=grid_spec,
        out_shape=jax.ShapeDtypeStruct((t, d), F32),
        compiler_params=_params("arbitrary", "arbitrary", "arbitrary"),
        name="moe",
    )(nt, toff, hn, slot_cols, slot_rows, wg, wu, wd)


def _final_kernel(x_ref, y_ref, g_ref, out_ref):
    out_ref[...] = _rms(x_ref[...] + y_ref[...], g_ref[...])


def _final_norm(x, y, g, tm):
    t, d = x.shape
    rows = pl.BlockSpec((tm, d), lambda i: (i, 0))
    return pl.pallas_call(
        _final_kernel,
        grid=(t // tm,),
        in_specs=[rows, rows, pl.BlockSpec((1, d), lambda i: (0, 0))],
        out_specs=rows,
        out_shape=jax.ShapeDtypeStruct((t, d), F32),
        compiler_params=_params("arbitrary"),
        name="final_norm",
    )(x, y, g)


def _routed_swiglu(x, norm_g, router_w, wg, wu, wd, final_g, tc, tr, tf, sub):
    t, d = x.shape
    nc, nsub = t // tc, tc // sub
    r_pad = jnp.zeros((d, LANES), F32).at[:, :N_EXPERTS].set(router_w)
    hn, meta, after = _router(x, norm_g, r_pad, tm=sub, blocks_per_chunk=nsub)
    total = after[:, 0, :N_EXPERTS].astype(jnp.int32).reshape(nc, nsub, N_EXPERTS)[:, -1]
    nt = (total + tr - 1) // tr
    toff = jnp.cumsum(nt, axis=1) - nt
    chunk = jnp.arange(t, dtype=jnp.int32) // tc
    e1 = meta[:, META_E1].astype(jnp.int32)
    e2 = meta[:, META_E2].astype(jnp.int32)
    row1 = (toff[chunk, e1] * tr).astype(F32) + meta[:, META_RANK1]
    row2 = (toff[chunk, e2] * tr).astype(F32) + meta[:, META_RANK2]
    slot_cols = jnp.stack([row1, row2, meta[:, META_G1], meta[:, META_G2]], axis=1)
    slot_rows = jnp.zeros((8, t), F32).at[0:4].set(slot_cols.T)
    y = _moe(hn, slot_cols, slot_rows, nt.reshape(-1), toff.reshape(-1), wg, wu, wd, tc=tc, tr=tr, tf=tf)
    return _final_norm(x, y, final_g, tm=min(1024, t))


def _permute_w_in(w):
    def perm(block):
        d = block.shape[0]
        return block.reshape(d, N_HEADS, 2, QK_DIM).transpose(0, 2, 1, 3).reshape(d, GW)
    parts = [w[:, :6 * GW], perm(w[:, 6 * GW:7 * GW]), perm(w[:, 7 * GW:8 * GW]), w[:, 8 * GW:]]
    return jnp.concatenate(parts, axis=1)


def _block_diag(pool_w):
    n, p, _ = pool_w.shape
    out = jnp.zeros((n * p, n * p), pool_w.dtype)
    for gi in range(n):
        out = out.at[gi * p:(gi + 1) * p, gi * p:(gi + 1) * p].set(pool_w[gi])
    return out


def _token_mixing(x, layer, lam_init, p, rope_c, rope_d):
    b, s, d = x.shape
    row = lambda v: v.reshape(1, -1)
    w_in = _permute_w_in(p["w_in"][layer]).astype(BF16)
    uab, qkvc, qkvd = _norm_proj(x, row(p["norm1_g"][layer]), w_in, rope_c, rope_d, tm=512)
    dww = jnp.zeros((HALO, GW), F32).at[:CONV_WIDTH].set(p["conv_dw_w"][layer])
    yab = _local_mixers(
        uab, dww, row(p["conv_dw_b"][layer]), row(p["conv_ln_g"][layer]), row(p["conv_ln_b"][layer]),
        p["conv_pw_w"][layer].astype(BF16), _block_diag(p["pool_w"][layer]).astype(BF16),
        row(p["pool_scale"][layer]), ts=512)
    os_, lses = [], []
    for dil in DILATIONS:
        o, lse = _band_attn(qkvc if dil == 1 else _deinterleave(qkvc, dil))
        if dil > 1:
            o, lse = _interleave(o, dil, b), _interleave(lse, dil, b)
        os_.append(o.reshape(b * s, GW))
        lses.append(lse.reshape(b * s, GW))
    yd = _diff_attn(qkvd, p["diff_lam"][layer], jnp.tile(p["diff_ln_g"][layer], N_HEADS).reshape(GW, 1),
                    lam_init, tq=256)
    return _out_proj(x.reshape(b * s, d), yab.reshape(b * s, 2 * GW), os_, lses, yd.reshape(b * s, GW),
                     p["w_out"][layer].astype(BF16), tm=512)


def kernel(x, norm1_g, w_in, conv_dw_w, conv_dw_b, conv_ln_g, conv_ln_b, conv_pw_w, pool_w, pool_scale,
           diff_lam, diff_ln_g, w_out, norm2_g, ffn_w_gate, ffn_w_up, ffn_w_down, moe_router, moe_w_gate,
           moe_w_up, moe_w_down, final_g):
    b, s, d = x.shape
    p = dict(norm1_g=norm1_g, w_in=w_in, conv_dw_w=conv_dw_w, conv_dw_b=conv_dw_b, conv_ln_g=conv_ln_g,
             conv_ln_b=conv_ln_b, conv_pw_w=conv_pw_w, pool_w=pool_w, pool_scale=pool_scale,
             diff_lam=diff_lam, diff_ln_g=diff_ln_g, w_out=w_out)
    rope_c = _rope_table(s, HEAD_DIM, HEAD_DIM // 4)
    rope_d = _rope_table(s, QK_DIM, QK_DIM // 4)
    row = lambda v: v.reshape(1, -1)

    x1 = _token_mixing(x, 0, 0.8 - 0.6 * math.exp(-0.3 * 0), p, rope_c, rope_d)
    x2 = _ffn(x1, row(norm2_g[0]), ffn_w_gate[0].astype(BF16), ffn_w_up[0].astype(BF16),
              ffn_w_down[0].astype(BF16), tm=512, tf=1408)

    x3 = _token_mixing(x2.reshape(b, s, d), 1, 0.8 - 0.6 * math.exp(-0.3 * 1), p, rope_c, rope_d)
    out = _routed_swiglu(x3, row(norm2_g[1]), moe_router[0], moe_w_gate[0].astype(BF16),
                         moe_w_up[0].astype(BF16), moe_w_down[0].astype(BF16), row(final_g),
                         tc=2048, tr=256, tf=512, sub=256)
    return out.reshape(b, s, d)
```

```python
import functools
import math

import jax
import jax.numpy as jnp
from jax import lax
from jax.experimental import pallas as pl
from jax.experimental.pallas import tpu as pltpu

D_MODEL = 1024
GW = 256
N_HEADS = 4
HEAD_DIM = 64
QK_DIM = 32
CONV_WIDTH = 31
POOL_WINDOWS = (2, 4, 8, 16)
DILATIONS = (1, 4, 16)
BAND = 128
ROPE_THETA = 500000.0
N_EXPERTS = 8
NORM_EPS = 1e-5
HALO = 32
LANES = 128
VMEM_LIMIT = 56 * 1024 * 1024

BF16 = jnp.bfloat16
F32 = jnp.float32
NT_DIMS = (((1,), (1,)), ((), ()))


def _params(*sem):
    return pltpu.CompilerParams(dimension_semantics=sem, vmem_limit_bytes=VMEM_LIMIT)


def _lanes_in(shape, start, width):
    lane = lax.broadcasted_iota(jnp.int32, shape, len(shape) - 1)
    return (lane >= start) & (lane < start + width)


def _rms(x, g):
    ms = jnp.mean(x * x, axis=-1, keepdims=True)
    return x * lax.rsqrt(ms + NORM_EPS) * g


def _rope_table(seq, width, rot):
    half = rot // 2
    pos = jnp.arange(seq, dtype=F32)
    inv = ROPE_THETA ** (-jnp.arange(0, rot, 2, dtype=F32) / rot)
    ang = pos[:, None] * inv[None, :]
    cos, sin = jnp.cos(ang), jnp.sin(ang)
    lane = jnp.arange(GW) % width
    f = lane % half
    first = lane < half
    second = (lane >= half) & (lane < rot)
    c = jnp.where((lane < rot)[None, :], cos[:, f], 1.0)
    sn = jnp.where(first[None, :], -sin[:, f], 0.0)
    sp = jnp.where(second[None, :], sin[:, f], 0.0)
    return jnp.concatenate([c, sn, sp], axis=1)


def _rope(u, tab_ref, half):
    return (u * tab_ref[:, 0:GW]
            + pltpu.roll(u, GW - half, 1) * tab_ref[:, GW:2 * GW]
            + pltpu.roll(u, half, 1) * tab_ref[:, 2 * GW:3 * GW])


def _norm_proj_kernel(x_ref, g_ref, w_ref, rc_ref, rd_ref, uab_ref, qkvc_ref, qkvd_ref):
    h = _rms(x_ref[0], g_ref[...]).astype(BF16)

    def proj(grp):
        return jnp.dot(h, w_ref[:, grp * GW:(grp + 1) * GW], preferred_element_type=F32)

    for grp in range(3):
        uab_ref[0, :, grp * GW:(grp + 1) * GW] = proj(grp)
    c_scale = HEAD_DIM ** -0.5
    d_scale = QK_DIM ** -0.5 * math.log2(math.e)
    qkvc_ref[0, :, 0:GW] = (_rope(proj(3), rc_ref, 8) * c_scale).astype(BF16)
    qkvc_ref[0, :, GW:2 * GW] = _rope(proj(4), rc_ref, 8).astype(BF16)
    qkvc_ref[0, :, 2 * GW:3 * GW] = proj(5).astype(BF16)
    qkvd_ref[0, :, 0:GW] = (_rope(proj(6), rd_ref, 4) * d_scale).astype(BF16)
    qkvd_ref[0, :, GW:2 * GW] = _rope(proj(7), rd_ref, 4).astype(BF16)
    qkvd_ref[0, :, 2 * GW:3 * GW] = proj(8).astype(BF16)


def _norm_proj(x, g, w, rope_c, rope_d, tm):
    b, s, d = x.shape
    grid = (s // tm, b)
    return pl.pallas_call(
        _norm_proj_kernel,
        grid=grid,
        in_specs=[
            pl.BlockSpec((1, tm, d), lambda i, j: (j, i, 0)),
            pl.BlockSpec((1, d), lambda i, j: (0, 0)),
            pl.BlockSpec((d, 9 * GW), lambda i, j: (0, 0)),
            pl.BlockSpec((tm, 3 * GW), lambda i, j: (i, 0)),
            pl.BlockSpec((tm, 3 * GW), lambda i, j: (i, 0)),
        ],
        out_specs=[
            pl.BlockSpec((1, tm, 3 * GW), lambda i, j: (j, i, 0)),
            pl.BlockSpec((1, tm, 3 * GW), lambda i, j: (j, i, 0)),
            pl.BlockSpec((1, tm, 3 * GW), lambda i, j: (j, i, 0)),
        ],
        out_shape=[
            jax.ShapeDtypeStruct((b, s, 3 * GW), F32),
            jax.ShapeDtypeStruct((b, s, 3 * GW), BF16),
            jax.ShapeDtypeStruct((b, s, 3 * GW), BF16),
        ],
        compiler_params=_params("arbitrary", "arbitrary"),
        name="norm_proj",
    )(x, g, w, rope_c, rope_d)


def _local_kernel(u_ref, dww_ref, dwb_ref, lng_ref, lnb_ref, pww_ref, plw_ref, pls_ref,
                  y_ref, zbuf, pbuf, *, ts):
    si = pl.program_id(1)

    @pl.when(si == 0)
    def _():
        zbuf[0:HALO, :] = jnp.zeros((HALO, GW), F32)
        pbuf[0:HALO, :] = jnp.zeros((HALO, GW), F32)

    a = u_ref[0, :, 0:GW]
    b = u_ref[0, :, GW:2 * GW]
    zbuf[HALO:HALO + ts, :] = a * jax.nn.sigmoid(b)
    pbuf[HALO:HALO + ts, :] = u_ref[0, :, 2 * GW:3 * GW]

    base = HALO - (CONV_WIDTH - 1)
    acc = jnp.zeros((ts, GW), F32)
    for k in range(CONV_WIDTH):
        acc = acc + dww_ref[k:k + 1, :] * zbuf[base + k:base + k + ts, :]
    z = acc + dwb_ref[...]
    mu = jnp.mean(z, axis=-1, keepdims=True)
    zc = z - mu
    var = jnp.mean(zc * zc, axis=-1, keepdims=True)
    zn = zc * lax.rsqrt(var + NORM_EPS) * lng_ref[...] + lnb_ref[...]
    zs = zn * jax.nn.sigmoid(zn)
    y_ref[0, :, 0:GW] = jnp.dot(zs.astype(BF16), pww_ref[...], preferred_element_type=F32).astype(BF16)

    lane = lax.broadcasted_iota(jnp.int32, (ts, GW), 1)
    pool_group = GW // len(POOL_WINDOWS)
    win = jnp.zeros((ts, GW), jnp.int32)
    for gi, w in enumerate(POOL_WINDOWS):
        win = jnp.where(lane >= gi * pool_group, w, win)
    p0 = pbuf[HALO:HALO + ts, :]
    tot = p0
    for k in range(1, max(POOL_WINDOWS)):
        pk = pbuf[HALO - k:HALO - k + ts, :]
        tot = tot + (pk if k < min(POOL_WINDOWS) else jnp.where(win > k, pk, 0.0))
    pos = si * ts + lax.broadcasted_iota(jnp.int32, (ts, GW), 0)
    cnt = jnp.minimum(pos + 1, win).astype(F32)
    pooled = tot / cnt - p0
    yb = jnp.dot(pooled.astype(BF16), plw_ref[...], preferred_element_type=F32) * pls_ref[...]
    y_ref[0, :, GW:2 * GW] = yb.astype(BF16)

    zbuf[0:HALO, :] = zbuf[ts:ts + HALO, :]
    pbuf[0:HALO, :] = pbuf[ts:ts + HALO, :]


def _local_mixers(uab, dww, dwb, lng, lnb, pww, plw, pls, ts):
    b, s, _ = uab.shape
    vec = lambda: pl.BlockSpec((1, GW), lambda i, j: (0, 0))
    mat = lambda: pl.BlockSpec((GW, GW), lambda i, j: (0, 0))
    return pl.pallas_call(
        functools.partial(_local_kernel, ts=ts),
        grid=(b, s // ts),
        in_specs=[
            pl.BlockSpec((1, ts, 3 * GW), lambda i, j: (i, j, 0)),
            pl.BlockSpec((HALO, GW), lambda i, j: (0, 0)),
            vec(), vec(), vec(), mat(), mat(), vec(),
        ],
        out_specs=pl.BlockSpec((1, ts, 2 * GW), lambda i, j: (i, j, 0)),
        out_shape=jax.ShapeDtypeStruct((b, s, 2 * GW), BF16),
        scratch_shapes=[pltpu.VMEM((HALO + ts, GW), F32), pltpu.VMEM((HALO + ts, GW), F32)],
        compiler_params=_params("arbitrary", "arbitrary"),
        name="local_mixers",
    )(uab, dww, dwb, lng, lnb, pww, plw, pls)


def _band_attn_kernel(q_ref, k_ref, v_ref, o_ref, lse_ref, s_ref):
    row = lax.broadcasted_iota(jnp.int32, (BAND, 2 * BAND), 0)
    col = lax.broadcasted_iota(jnp.int32, (BAND, 2 * BAND), 1)
    halves = [_lanes_in((BAND, LANES), sub * HEAD_DIM, HEAD_DIM) for sub in range(LANES // HEAD_DIM)]

    def block(n, carry):
        q0 = pl.multiple_of(n * BAND, BAND)
        k0 = pl.multiple_of(jnp.maximum(n - 1, 0) * BAND, BAND)
        q = q_ref[0, pl.ds(q0, BAND), :].astype(F32)
        kw = k_ref[0, pl.ds(k0, 2 * BAND), :]
        dist = (row - col) + (q0 - k0)
        mask = (dist >= 0) & (dist <= BAND)
        for h in range(N_HEADS):
            qh = jnp.where(_lanes_in((BAND, GW), h * HEAD_DIM, HEAD_DIM), q, 0.0).astype(BF16)
            s = lax.dot_general(qh, kw, NT_DIMS, preferred_element_type=F32)
            s_ref[h] = jnp.where(mask, s, -jnp.inf)
        for tile in range(GW // LANES):
            vw = v_ref[0, pl.ds(k0, 2 * BAND), tile * LANES:(tile + 1) * LANES]
            o = jnp.zeros((BAND, LANES), F32)
            lse = jnp.zeros((BAND, LANES), F32)
            for sub in range(LANES // HEAD_DIM):
                s = s_ref[tile * (LANES // HEAD_DIM) + sub]
                m = jnp.max(s, axis=-1, keepdims=True)
                p = jnp.exp(s - m)
                l = jnp.sum(p, axis=-1, keepdims=True)
                acc = jnp.dot(p.astype(BF16), vw, preferred_element_type=F32)
                mine = halves[sub]
                o = jnp.where(mine, acc / l, o)
                lse = jnp.where(mine, m + jnp.log(l), lse)
            o_ref[0, pl.ds(q0, BAND), tile * LANES:(tile + 1) * LANES] = o
            lse_ref[0, pl.ds(q0, BAND), tile * LANES:(tile + 1) * LANES] = lse
        return carry

    lax.fori_loop(0, q_ref.shape[1] // BAND, block, 0, unroll=2)


def _band_attn(qkv):
    nb, l, _ = qkv.shape
    assert l >= 2 * BAND and l % BAND == 0
    part = lambda c: pl.BlockSpec((1, l, GW), lambda i: (i, 0, c))
    return pl.pallas_call(
        _band_attn_kernel,
        grid=(nb,),
        in_specs=[part(0), part(1), part(2)],
        out_specs=[part(0), part(0)],
        out_shape=[jax.ShapeDtypeStruct((nb, l, GW), F32)] * 2,
        scratch_shapes=[pltpu.VMEM((N_HEADS, BAND, 2 * BAND), F32)],
        compiler_params=_params("arbitrary"),
        name="band_attn",
    )(qkv, qkv, qkv)


def _deinterleave(a, d):
    b, s, c = a.shape
    return a.reshape(b, s // d, d, c).transpose(0, 2, 1, 3).reshape(b * d, s // d, c)


def _interleave(a, d, b):
    _, l, c = a.shape
    return a.reshape(b, d, l, c).transpose(0, 2, 1, 3).reshape(b, l * d, c)


def _diff_attn_kernel(lam_ref, g_ref, qt_ref, k_ref, vt_ref, o_ref, qm_ref, m_ref, l_ref, acc_ref, yt_ref,
                      st_ref, mx_ref, *, tq, lam_init):
    n_chain = 2 * N_HEADS
    sub = 8
    i = pl.program_id(1)
    qt = qt_ref[0].astype(F32)
    chan = lax.broadcasted_iota(jnp.int32, (GW, tq), 0)
    for c in range(n_chain):
        own = (chan >= c * QK_DIM) & (chan < (c + 1) * QK_DIM)
        qm_ref[c] = jnp.where(own, qt, 0.0).astype(BF16)
    m_ref[...] = jnp.full(m_ref.shape, -jnp.inf, F32)
    l_ref[...] = jnp.zeros(l_ref.shape, F32)
    acc_ref[...] = jnp.zeros(acc_ref.shape, F32)
    key = lax.broadcasted_iota(jnp.int32, (tq, tq), 0)
    qry = lax.broadcasted_iota(jnp.int32, (tq, tq), 1)

    def step(j, diagonal):
        kb = k_ref[0, pl.ds(pl.multiple_of(j * tq, tq), tq), :]
        for c in range(n_chain):
            st = jnp.dot(kb, qm_ref[c], preferred_element_type=F32)
            if diagonal:
                st = jnp.where(key <= qry, st, -jnp.inf)
            st_ref[c] = st
            mx_ref[c] = jnp.max(st, axis=0, keepdims=True)
        for c in range(n_chain):
            h = c % N_HEADS
            m_prev = m_ref[c]
            m_next = jnp.maximum(m_prev, mx_ref[c])
            alpha = jnp.exp2(m_prev - m_next)
            p = jnp.exp2(st_ref[c] - m_next)
            l_ref[c] = alpha * l_ref[c] + jnp.sum(p.reshape(tq // sub, sub, tq), axis=0)
            vt = vt_ref[0, j, h * HEAD_DIM:(h + 1) * HEAD_DIM, :]
            acc_ref[c] = alpha * acc_ref[c] + jnp.dot(vt, p.astype(BF16), preferred_element_type=F32)
            m_ref[c] = m_next

    def body(j, carry):
        step(j, False)
        return carry

    lax.fori_loop(0, i, body, 0)
    step(i, True)

    lv = lam_ref[...]
    lam = (jnp.exp(jnp.sum(lv[0:1] * lv[1:2], axis=-1, keepdims=True))
           - jnp.exp(jnp.sum(lv[2:3] * lv[3:4], axis=-1, keepdims=True)) + lam_init)
    for h in range(N_HEADS):
        l1 = jnp.sum(l_ref[h], axis=0, keepdims=True)
        l2 = jnp.sum(l_ref[N_HEADS + h], axis=0, keepdims=True)
        oh = acc_ref[h] / l1 - lam * (acc_ref[N_HEADS + h] / l2)
        ms = jnp.mean(oh * oh, axis=0, keepdims=True)
        g = g_ref[h * HEAD_DIM:(h + 1) * HEAD_DIM, :]
        yt_ref[h * HEAD_DIM:(h + 1) * HEAD_DIM, :] = oh * lax.rsqrt(ms + NORM_EPS) * (g * (1.0 - lam_init))
    o_ref[0] = jnp.transpose(yt_ref[...]).astype(BF16)


def _diff_attn(qkv, lam_vecs, g_col, lam_init, tq):
    b, s, _ = qkv.shape
    nk = s // tq
    qt = jnp.swapaxes(qkv[..., 0:GW], 1, 2)
    vt = jnp.swapaxes(qkv[..., 2 * GW:3 * GW].reshape(b, nk, tq, GW), 2, 3)
    n_chain = 2 * N_HEADS
    return pl.pallas_call(
        functools.partial(_diff_attn_kernel, tq=tq, lam_init=lam_init),
        grid=(b, nk),
        in_specs=[
            pl.BlockSpec((4, QK_DIM), lambda i, j: (0, 0)),
            pl.BlockSpec((GW, 1), lambda i, j: (0, 0)),
            pl.BlockSpec((1, GW, tq), lambda i, j: (i, 0, j)),
            pl.BlockSpec((1, s, GW), lambda i, j: (i, 0, 1)),
            pl.BlockSpec((1, nk, GW, tq), lambda i, j: (i, 0, 0, 0)),
        ],
        out_specs=pl.BlockSpec((1, tq, GW), lambda i, j: (i, j, 0)),
        out_shape=jax.ShapeDtypeStruct((b, s, GW), BF16),
        scratch_shapes=[
            pltpu.VMEM((n_chain, GW, tq), BF16),
            pltpu.VMEM((n_chain, 1, tq), F32),
            pltpu.VMEM((n_chain, 8, tq), F32),
            pltpu.VMEM((n_chain, HEAD_DIM, tq), F32),
            pltpu.VMEM((GW, tq), F32),
            pltpu.VMEM((n_chain, tq, tq), F32),
            pltpu.VMEM((n_chain, 1, tq), F32),
        ],
        compiler_params=_params("arbitrary", "arbitrary"),
        name="diff_attn",
    )(lam_vecs, g_col, qt, qkv, vt)


def _out_proj_kernel(x_ref, yab_ref, o1_ref, o2_ref, o3_ref, l1_ref, l2_ref, l3_ref, yd_ref, w_ref, out_ref):
    l1, l2, l3 = l1_ref[...], l2_ref[...], l3_ref[...]
    m = jnp.maximum(jnp.maximum(l1, l2), l3)
    w1, w2, w3 = jnp.exp(l1 - m), jnp.exp(l2 - m), jnp.exp(l3 - m)
    yc = (w1 * o1_ref[...] + w2 * o2_ref[...] + w3 * o3_ref[...]) / (w1 + w2 + w3)
    acc = jnp.dot(yab_ref[...], w_ref[0:2 * GW, :], preferred_element_type=F32)
    acc = acc + jnp.dot(yc.astype(BF16), w_ref[2 * GW:3 * GW, :], preferred_element_type=F32)
    acc = acc + jnp.dot(yd_ref[...], w_ref[3 * GW:4 * GW, :], preferred_element_type=F32)
    out_ref[...] = x_ref[...] + acc


def _out_proj(x, yab, os_, lses, yd, w, tm):
    t, d = x.shape
    rows = lambda c: pl.BlockSpec((tm, c), lambda i: (i, 0))
    return pl.pallas_call(
        _out_proj_kernel,
        grid=(t // tm,),
        in_specs=[rows(d), rows(2 * GW)] + [rows(GW)] * 6 + [rows(GW), pl.BlockSpec((d, d), lambda i: (0, 0))],
        out_specs=rows(d),
        out_shape=jax.ShapeDtypeStruct((t, d), F32),
        compiler_params=_params("arbitrary"),
        name="out_proj",
    )(x, yab, *os_, *lses, yd, w)


def _ffn_kernel(x_ref, g_ref, wg_ref, wu_ref, wd_ref, out_ref, hn_ref, acc_ref):
    f = pl.program_id(1)

    @pl.when(f == 0)
    def _():
        hn_ref[...] = _rms(x_ref[...], g_ref[...]).astype(BF16)
        acc_ref[...] = jnp.zeros_like(acc_ref)

    hn = hn_ref[...]
    gate = jnp.dot(hn, wg_ref[...], preferred_element_type=F32)
    up = jnp.dot(hn, wu_ref[...], preferred_element_type=F32)
    act = (gate * jax.nn.sigmoid(gate) * up).astype(BF16)
    acc_ref[...] += jnp.dot(act, wd_ref[...], preferred_element_type=F32)

    @pl.when(f == pl.num_programs(1) - 1)
    def _():
        out_ref[...] = x_ref[...] + acc_ref[...]


def _ffn(x, g, wg, wu, wd, tm, tf):
    t, d = x.shape
    ff = wg.shape[1]
    return pl.pallas_call(
        _ffn_kernel,
        grid=(t // tm, ff // tf),
        in_specs=[
            pl.BlockSpec((tm, d), lambda i, f: (i, 0)),
            pl.BlockSpec((1, d), lambda i, f: (0, 0)),
            pl.BlockSpec((d, tf), lambda i, f: (0, f)),
            pl.BlockSpec((d, tf), lambda i, f: (0, f)),
            pl.BlockSpec((tf, d), lambda i, f: (f, 0)),
        ],
        out_specs=pl.BlockSpec((tm, d), lambda i, f: (i, 0)),
        out_shape=jax.ShapeDtypeStruct((t, d), F32),
        scratch_shapes=[pltpu.VMEM((tm, d), BF16), pltpu.VMEM((tm, d), F32)],
        compiler_params=_params("arbitrary", "arbitrary"),
        name="ffn_dense",
    )(x, g, wg, wu, wd)


META_E1, META_E2, META_RANK1, META_RANK2, META_G1, META_G2 = range(6)


def _router_kernel(x_ref, g_ref, r_ref, hn_ref, meta_ref, after_ref, carry_ref, *, blocks_per_chunk):
    @pl.when(pl.program_id(0) % blocks_per_chunk == 0)
    def _():
        carry_ref[...] = jnp.zeros_like(carry_ref)

    h = _rms(x_ref[...], g_ref[...])
    hn_ref[...] = h.astype(BF16)
    logits = jnp.dot(h, r_ref[...], preferred_element_type=F32, precision=lax.Precision.HIGHEST)
    tm = logits.shape[0]
    lane = lax.broadcasted_iota(jnp.int32, logits.shape, 1)
    logits = jnp.where(lane < N_EXPERTS, logits, -jnp.inf)
    m1 = jnp.max(logits, axis=-1, keepdims=True)
    i1 = jnp.min(jnp.where(logits == m1, lane, LANES), axis=-1, keepdims=True)
    rest = jnp.where(lane == i1, -jnp.inf, logits)
    m2 = jnp.max(rest, axis=-1, keepdims=True)
    i2 = jnp.min(jnp.where(rest == m2, lane, LANES), axis=-1, keepdims=True)
    e2 = jnp.exp(m2 - m1)
    g1 = 1.0 / (1.0 + e2)
    g2 = e2 / (1.0 + e2)

    oh1 = lane == i1
    oh2 = lane == i2
    both = jnp.where(oh1, 1.0, jnp.where(oh2, 1.0, 0.0))
    row = lax.broadcasted_iota(jnp.int32, (tm, tm), 0)
    col = lax.broadcasted_iota(jnp.int32, (tm, tm), 1)
    earlier = jnp.where(col < row, 1.0, 0.0).astype(BF16)
    carry = carry_ref[0:1, :]
    prefix = jnp.dot(earlier, both.astype(BF16), preferred_element_type=F32) + carry
    rank1 = jnp.sum(jnp.where(oh1, prefix, 0.0), axis=-1, keepdims=True)
    rank2 = jnp.sum(jnp.where(oh2, prefix, 0.0), axis=-1, keepdims=True)
    total = carry + jnp.sum(both, axis=0, keepdims=True)

    meta = jnp.zeros(logits.shape, F32)
    for k, val in ((META_E1, i1.astype(F32)), (META_E2, i2.astype(F32)), (META_RANK1, rank1),
                   (META_RANK2, rank2), (META_G1, g1), (META_G2, g2)):
        meta = jnp.where(lane == k, val, meta)
    meta_ref[...] = meta
    after_ref[0] = jnp.broadcast_to(total, after_ref.shape[1:])
    carry_ref[...] = jnp.broadcast_to(total, carry_ref.shape)


def _router(x, g, r_pad, tm, blocks_per_chunk):
    t, d = x.shape
    nblk = t // tm
    counts = pl.BlockSpec((1, 8, LANES), lambda i: (i, 0, 0))
    return pl.pallas_call(
        functools.partial(_router_kernel, blocks_per_chunk=blocks_per_chunk),
        grid=(nblk,),
        in_specs=[
            pl.BlockSpec((tm, d), lambda i: (i, 0)),
            pl.BlockSpec((1, d), lambda i: (0, 0)),
            pl.BlockSpec((d, LANES), lambda i: (0, 0)),
        ],
        out_specs=[
            pl.BlockSpec((tm, d), lambda i: (i, 0)),
            pl.BlockSpec((tm, LANES), lambda i: (i, 0)),
            counts,
        ],
        out_shape=[
            jax.ShapeDtypeStruct((t, d), BF16),
            jax.ShapeDtypeStruct((t, LANES), F32),
            jax.ShapeDtypeStruct((nblk, 8, LANES), F32),
        ],
        scratch_shapes=[pltpu.VMEM((8, LANES), F32)],
        compiler_params=_params("arbitrary"),
        name="router",
    )(x, g, r_pad)


SLOT_ROW1, SLOT_ROW2, SLOT_G1, SLOT_G2 = range(4)


def _moe_kernel(nt_ref, toff_ref, hn_ref, slotc_ref, slotr_ref, wg_ref, wu_ref, wd_ref, out_ref,
                xs_ref, ys_ref, *, tr):
    c, e, f = pl.program_id(0), pl.program_id(1), pl.program_id(2)
    last_f = pl.num_programs(2) - 1
    grp = c * N_EXPERTS + e
    n_tiles = nt_ref[grp]
    row0 = toff_ref[grp] * tr
    tc = hn_ref.shape[0]

    @pl.when((e == 0) & (f == 0))
    def _():
        out_ref[...] = jnp.zeros_like(out_ref)

    def rows_of(r):
        return pl.ds(pl.multiple_of(r * tr, tr), tr)

    def gather_tile(r, carry):
        target = (lax.broadcasted_iota(jnp.int32, (tr, tc), 0) + (row0 + r * tr)).astype(F32)
        hit1 = slotr_ref[SLOT_ROW1:SLOT_ROW1 + 1, :] == target
        hit2 = slotr_ref[SLOT_ROW2:SLOT_ROW2 + 1, :] == target
        oh = jnp.where(hit1, 1.0, jnp.where(hit2, 1.0, 0.0)).astype(BF16)
        xs_ref[rows_of(r), :] = jnp.dot(oh, hn_ref[...], preferred_element_type=F32).astype(BF16)
        return carry

    def ffn_tile(r, first):
        xt = xs_ref[rows_of(r), :]
        gate = jnp.dot(xt, wg_ref[...], preferred_element_type=F32)
        up = jnp.dot(xt, wu_ref[...], preferred_element_type=F32)
        act = (gate * jax.nn.sigmoid(gate) * up).astype(BF16)
        y = jnp.dot(act, wd_ref[...], preferred_element_type=F32)
        if first:
            ys_ref[rows_of(r), :] = y
        else:
            ys_ref[rows_of(r), :] += y

    def scatter_tile(r, carry):
        yb = ys_ref[rows_of(r), :].astype(BF16)
        target = (lax.broadcasted_iota(jnp.int32, (tc, tr), 1) + (row0 + r * tr)).astype(F32)
        hit1 = slotc_ref[:, SLOT_ROW1:SLOT_ROW1 + 1] == target
        hit2 = slotc_ref[:, SLOT_ROW2:SLOT_ROW2 + 1] == target
        gh = jnp.where(hit1, slotc_ref[:, SLOT_G1:SLOT_G1 + 1],
                       jnp.where(hit2, slotc_ref[:, SLOT_G2:SLOT_G2 + 1], 0.0)).astype(BF16)
        out_ref[...] += jnp.dot(gh, yb, preferred_element_type=F32)
        return carry

    def ffn_first(r, carry):
        ffn_tile(r, True)
        return carry

    def ffn_next(r, carry):
        ffn_tile(r, False)
        return carry

    @pl.when(f == 0)
    def _():
        lax.fori_loop(0, n_tiles, gather_tile, 0)
        lax.fori_loop(0, n_tiles, ffn_first, 0)

    @pl.when(f > 0)
    def _():
        lax.fori_loop(0, n_tiles, ffn_next, 0)

    @pl.when(f == last_f)
    def _():
        lax.fori_loop(0, n_tiles, scatter_tile, 0)


def _moe(hn, slot_cols, slot_rows, nt, toff, wg, wu, wd, tc, tr, tf):
    t, d = hn.shape
    ne, _, ff = wg.shape
    grid_spec = pltpu.PrefetchScalarGridSpec(
        num_scalar_prefetch=2,
        grid=(t // tc, ne, ff // tf),
        in_specs=[
            pl.BlockSpec((tc, d), lambda c, e, f, *_: (c, 0)),
            pl.BlockSpec((tc, 4), lambda c, e, f, *_: (c, 0)),
            pl.BlockSpec((8, tc), lambda c, e, f, *_: (0, c)),
            pl.BlockSpec((None, d, tf), lambda c, e, f, *_: (e, 0, f)),
            pl.BlockSpec((None, d, tf), lambda c, e, f, *_: (e, 0, f)),
            pl.BlockSpec((None, tf, d), lambda c, e, f, *_: (e, f, 0)),
        ],
        out_specs=pl.BlockSpec((tc, d), lambda c, e, f, *_: (c, 0)),
        scratch_shapes=[
            pltpu.VMEM((tc + tr, d), BF16),
            pltpu.VMEM((tc + tr, d), F32),
        ],
    )
    return pl.pallas_call(
        functools.partial(_moe_kernel, tr=tr),
        grid_spec=grid_spec,
        out_shape=jax.ShapeDtypeStruct((t, d), F32),
        compiler_params=_params("arbitrary", "arbitrary", "arbitrary"),
        name="moe",
    )(nt, toff, hn, slot_cols, slot_rows, wg, wu, wd)


def _final_kernel(x_ref, y_ref, g_ref, out_ref):
    out_ref[...] = _rms(x_ref[...] + y_ref[...], g_ref[...])


def _final_norm(x, y, g, tm):
    t, d = x.shape
    rows = pl.BlockSpec((tm, d), lambda i: (i, 0))
    return pl.pallas_call(
        _final_kernel,
        grid=(t // tm,),
        in_specs=[rows, rows, pl.BlockSpec((1, d), lambda i: (0, 0))],
        out_specs=rows,
        out_shape=jax.ShapeDtypeStruct((t, d), F32),
        compiler_params=_params("arbitrary"),
        name="final_norm",
    )(x, y, g)


def _routed_swiglu(x, norm_g, router_w, wg, wu, wd, final_g, tc, tr, tf, sub):
    t, d = x.shape
    nc, nsub = t // tc, tc // sub
    r_pad = jnp.zeros((d, LANES), F32).at[:, :N_EXPERTS].set(router_w)
    hn, meta, after = _router(x, norm_g, r_pad, tm=sub, blocks_per_chunk=nsub)
    total = after[:, 0, :N_EXPERTS].astype(jnp.int32).reshape(nc, nsub, N_EXPERTS)[:, -1]
    nt = (total + tr - 1) // tr
    toff = jnp.cumsum(nt, axis=1) - nt
    first_row = jnp.repeat((toff * tr).astype(F32), tc, axis=0)
    experts = jnp.arange(N_EXPERTS, dtype=F32)
    pick = lambda e: jnp.sum(jnp.where(e[:, None] == experts, first_row, 0.0), axis=1)
    row1 = pick(meta[:, META_E1]) + meta[:, META_RANK1]
    row2 = pick(meta[:, META_E2]) + meta[:, META_RANK2]
    slot_cols = jnp.stack([row1, row2, meta[:, META_G1], meta[:, META_G2]], axis=1)
    slot_rows = jnp.zeros((8, t), F32).at[0:4].set(slot_cols.T)
    y = _moe(hn, slot_cols, slot_rows, nt.reshape(-1), toff.reshape(-1), wg, wu, wd, tc=tc, tr=tr, tf=tf)
    return _final_norm(x, y, final_g, tm=min(1024, t))


def _permute_w_in(w):
    def perm(block):
        d = block.shape[0]
        return block.reshape(d, N_HEADS, 2, QK_DIM).transpose(0, 2, 1, 3).reshape(d, GW)
    parts = [w[:, :6 * GW], perm(w[:, 6 * GW:7 * GW]), perm(w[:, 7 * GW:8 * GW]), w[:, 8 * GW:]]
    return jnp.concatenate(parts, axis=1)


def _block_diag(pool_w):
    n, p, _ = pool_w.shape
    out = jnp.zeros((n * p, n * p), pool_w.dtype)
    for gi in range(n):
        out = out.at[gi * p:(gi + 1) * p, gi * p:(gi + 1) * p].set(pool_w[gi])
    return out


def _token_mixing(x, layer, lam_init, p, rope_c, rope_d):
    b, s, d = x.shape
    row = lambda v: v.reshape(1, -1)
    w_in = _permute_w_in(p["w_in"][layer]).astype(BF16)
    uab, qkvc, qkvd = _norm_proj(x, row(p["norm1_g"][layer]), w_in, rope_c, rope_d, tm=512)
    dww = jnp.zeros((HALO, GW), F32).at[:CONV_WIDTH].set(p["conv_dw_w"][layer])
    yab = _local_mixers(
        uab, dww, row(p["conv_dw_b"][layer]), row(p["conv_ln_g"][layer]), row(p["conv_ln_b"][layer]),
        p["conv_pw_w"][layer].astype(BF16), _block_diag(p["pool_w"][layer]).astype(BF16),
        row(p["pool_scale"][layer]), ts=512)
    os_, lses = [], []
    for dil in DILATIONS:
        o, lse = _band_attn(qkvc if dil == 1 else _deinterleave(qkvc, dil))
        if dil > 1:
            o, lse = _interleave(o, dil, b), _interleave(lse, dil, b)
        os_.append(o.reshape(b * s, GW))
        lses.append(lse.reshape(b * s, GW))
    yd = _diff_attn(qkvd, p["diff_lam"][layer], jnp.tile(p["diff_ln_g"][layer], N_HEADS).reshape(GW, 1),
                    lam_init, tq=256)
    return _out_proj(x.reshape(b * s, d), yab.reshape(b * s, 2 * GW), os_, lses, yd.reshape(b * s, GW),
                     p["w_out"][layer].astype(BF16), tm=512)


def kernel(x, norm1_g, w_in, conv_dw_w, conv_dw_b, conv_ln_g, conv_ln_b, conv_pw_w, pool_w, pool_scale,
           diff_lam, diff_ln_g, w_out, norm2_g, ffn_w_gate, ffn_w_up, ffn_w_down, moe_router, moe_w_gate,
           moe_w_up, moe_w_down, final_g):
    b, s, d = x.shape
    p = dict(norm1_g=norm1_g, w_in=w_in, conv_dw_w=conv_dw_w, conv_dw_b=conv_dw_b, conv_ln_g=conv_ln_g,
             conv_ln_b=conv_ln_b, conv_pw_w=conv_pw_w, pool_w=pool_w, pool_scale=pool_scale,
             diff_lam=diff_lam, diff_ln_g=diff_ln_g, w_out=w_out)
    rope_c = _rope_table(s, HEAD_DIM, HEAD_DIM // 4)
    rope_d = _rope_table(s, QK_DIM, QK_DIM // 4)
    row = lambda v: v.reshape(1, -1)

    x1 = _token_mixing(x, 0, 0.8 - 0.6 * math.exp(-0.3 * 0), p, rope_c, rope_d)
    x2 = _ffn(x1, row(norm2_g[0]), ffn_w_gate[0].astype(BF16), ffn_w_up[0].astype(BF16),
              ffn_w_down[0].astype(BF16), tm=512, tf=1408)

    x3 = _token_mixing(x2.reshape(b, s, d), 1, 0.8 - 0.6 * math.exp(-0.3 * 1), p, rope_c, rope_d)
    out = _routed_swiglu(x3, row(norm2_g[1]), moe_router[0], moe_w_gate[0].astype(BF16),
                         moe_w_up[0].astype(BF16), moe_w_down[0].astype(BF16), row(final_g),
                         tc=2048, tr=256, tf=512, sub=256)
    return out.reshape(b, s, d)
```

```python
import functools
import math

import jax
import jax.numpy as jnp
from jax import lax
from jax.experimental import pallas as pl
from jax.experimental.pallas import tpu as pltpu

D_MODEL = 1024
GW = 256
N_HEADS = 4
HEAD_DIM = 64
QK_DIM = 32
V_ROWS = HEAD_DIM + 16
CONV_WIDTH = 31
POOL_WINDOWS = (2, 4, 8, 16)
DILATIONS = (1, 4, 16)
BAND = 128
ROPE_THETA = 500000.0
N_EXPERTS = 8
NORM_EPS = 1e-5
HALO = 32
LANES = 128
VMEM_LIMIT = 56 * 1024 * 1024

BF16 = jnp.bfloat16
F32 = jnp.float32
NT_DIMS = (((1,), (1,)), ((), ()))


def _params(*sem):
    return pltpu.CompilerParams(dimension_semantics=sem, vmem_limit_bytes=VMEM_LIMIT)


def _lanes_in(shape, start, width):
    lane = lax.broadcasted_iota(jnp.int32, shape, len(shape) - 1)
    return (lane >= start) & (lane < start + width)


def _rms(x, g):
    ms = jnp.mean(x * x, axis=-1, keepdims=True)
    return x * lax.rsqrt(ms + NORM_EPS) * g


def _rope_table(seq, width, rot):
    half = rot // 2
    pos = jnp.arange(seq, dtype=F32)
    inv = ROPE_THETA ** (-jnp.arange(0, rot, 2, dtype=F32) / rot)
    ang = pos[:, None] * inv[None, :]
    cos, sin = jnp.cos(ang), jnp.sin(ang)
    lane = jnp.arange(GW) % width
    f = lane % half
    first = lane < half
    second = (lane >= half) & (lane < rot)
    c = jnp.where((lane < rot)[None, :], cos[:, f], 1.0)
    sn = jnp.where(first[None, :], -sin[:, f], 0.0)
    sp = jnp.where(second[None, :], sin[:, f], 0.0)
    return jnp.concatenate([c, sn, sp], axis=1)


def _rope(u, tab_ref, half):
    return (u * tab_ref[:, 0:GW]
            + pltpu.roll(u, GW - half, 1) * tab_ref[:, GW:2 * GW]
            + pltpu.roll(u, half, 1) * tab_ref[:, 2 * GW:3 * GW])


def _norm_proj_kernel(x_ref, g_ref, w_ref, rc_ref, rd_ref, uab_ref, qkvc_ref, qkvd_ref):
    h = _rms(x_ref[0], g_ref[...]).astype(BF16)

    def proj(grp):
        return jnp.dot(h, w_ref[:, grp * GW:(grp + 1) * GW], preferred_element_type=F32)

    for grp in range(3):
        uab_ref[0, :, grp * GW:(grp + 1) * GW] = proj(grp)
    c_scale = HEAD_DIM ** -0.5
    d_scale = QK_DIM ** -0.5 * math.log2(math.e)
    qkvc_ref[0, :, 0:GW] = (_rope(proj(3), rc_ref, 8) * c_scale).astype(BF16)
    qkvc_ref[0, :, GW:2 * GW] = _rope(proj(4), rc_ref, 8).astype(BF16)
    qkvc_ref[0, :, 2 * GW:3 * GW] = proj(5).astype(BF16)
    qkvd_ref[0, :, 0:GW] = (_rope(proj(6), rd_ref, 4) * d_scale).astype(BF16)
    qkvd_ref[0, :, GW:2 * GW] = _rope(proj(7), rd_ref, 4).astype(BF16)
    qkvd_ref[0, :, 2 * GW:3 * GW] = proj(8).astype(BF16)


def _norm_proj(x, g, w, rope_c, rope_d, tm):
    b, s, d = x.shape
    grid = (s // tm, b)
    return pl.pallas_call(
        _norm_proj_kernel,
        grid=grid,
        in_specs=[
            pl.BlockSpec((1, tm, d), lambda i, j: (j, i, 0)),
            pl.BlockSpec((1, d), lambda i, j: (0, 0)),
            pl.BlockSpec((d, 9 * GW), lambda i, j: (0, 0)),
            pl.BlockSpec((tm, 3 * GW), lambda i, j: (i, 0)),
            pl.BlockSpec((tm, 3 * GW), lambda i, j: (i, 0)),
        ],
        out_specs=[
            pl.BlockSpec((1, tm, 3 * GW), lambda i, j: (j, i, 0)),
            pl.BlockSpec((1, tm, 3 * GW), lambda i, j: (j, i, 0)),
            pl.BlockSpec((1, tm, 3 * GW), lambda i, j: (j, i, 0)),
        ],
        out_shape=[
            jax.ShapeDtypeStruct((b, s, 3 * GW), F32),
            jax.ShapeDtypeStruct((b, s, 3 * GW), BF16),
            jax.ShapeDtypeStruct((b, s, 3 * GW), BF16),
        ],
        compiler_params=_params("arbitrary", "arbitrary"),
        name="norm_proj",
    )(x, g, w, rope_c, rope_d)


def _local_kernel(u_ref, dww_ref, dwb_ref, lng_ref, lnb_ref, pww_ref, plw_ref, pls_ref,
                  y_ref, zbuf, pbuf, *, ts):
    si = pl.program_id(1)

    @pl.when(si == 0)
    def _():
        zbuf[0:HALO, :] = jnp.zeros((HALO, GW), F32)
        pbuf[0:HALO, :] = jnp.zeros((HALO, GW), F32)

    a = u_ref[0, :, 0:GW]
    b = u_ref[0, :, GW:2 * GW]
    zbuf[HALO:HALO + ts, :] = a * jax.nn.sigmoid(b)
    pbuf[HALO:HALO + ts, :] = u_ref[0, :, 2 * GW:3 * GW]

    base = HALO - (CONV_WIDTH - 1)
    acc = jnp.zeros((ts, GW), F32)
    for k in range(CONV_WIDTH):
        acc = acc + dww_ref[k:k + 1, :] * zbuf[base + k:base + k + ts, :]
    z = acc + dwb_ref[...]
    mu = jnp.mean(z, axis=-1, keepdims=True)
    zc = z - mu
    var = jnp.mean(zc * zc, axis=-1, keepdims=True)
    zn = zc * lax.rsqrt(var + NORM_EPS) * lng_ref[...] + lnb_ref[...]
    zs = zn * jax.nn.sigmoid(zn)
    y_ref[0, :, 0:GW] = jnp.dot(zs.astype(BF16), pww_ref[...], preferred_element_type=F32).astype(BF16)

    lane = lax.broadcasted_iota(jnp.int32, (ts, GW), 1)
    pool_group = GW // len(POOL_WINDOWS)
    win = jnp.zeros((ts, GW), jnp.int32)
    for gi, w in enumerate(POOL_WINDOWS):
        win = jnp.where(lane >= gi * pool_group, w, win)
    p0 = pbuf[HALO:HALO + ts, :]
    tot = p0
    for k in range(1, max(POOL_WINDOWS)):
        pk = pbuf[HALO - k:HALO - k + ts, :]
        tot = tot + (pk if k < min(POOL_WINDOWS) else jnp.where(win > k, pk, 0.0))
    pos = si * ts + lax.broadcasted_iota(jnp.int32, (ts, GW), 0)
    cnt = jnp.minimum(pos + 1, win).astype(F32)
    pooled = tot / cnt - p0
    yb = jnp.dot(pooled.astype(BF16), plw_ref[...], preferred_element_type=F32) * pls_ref[...]
    y_ref[0, :, GW:2 * GW] = yb.astype(BF16)

    zbuf[0:HALO, :] = zbuf[ts:ts + HALO, :]
    pbuf[0:HALO, :] = pbuf[ts:ts + HALO, :]


def _local_mixers(uab, dww, dwb, lng, lnb, pww, plw, pls, ts):
    b, s, _ = uab.shape
    vec = lambda: pl.BlockSpec((1, GW), lambda i, j: (0, 0))
    mat = lambda: pl.BlockSpec((GW, GW), lambda i, j: (0, 0))
    return pl.pallas_call(
        functools.partial(_local_kernel, ts=ts),
        grid=(b, s // ts),
        in_specs=[
            pl.BlockSpec((1, ts, 3 * GW), lambda i, j: (i, j, 0)),
            pl.BlockSpec((HALO, GW), lambda i, j: (0, 0)),
            vec(), vec(), vec(), mat(), mat(), vec(),
        ],
        out_specs=pl.BlockSpec((1, ts, 2 * GW), lambda i, j: (i, j, 0)),
        out_shape=jax.ShapeDtypeStruct((b, s, 2 * GW), BF16),
        scratch_shapes=[pltpu.VMEM((HALO + ts, GW), F32), pltpu.VMEM((HALO + ts, GW), F32)],
        compiler_params=_params("arbitrary", "arbitrary"),
        name="local_mixers",
    )(uab, dww, dwb, lng, lnb, pww, plw, pls)


def _band_attn_kernel(q_ref, k_ref, v_ref, o_ref, lse_ref, s_ref):
    row = lax.broadcasted_iota(jnp.int32, (BAND, 2 * BAND), 0)
    col = lax.broadcasted_iota(jnp.int32, (BAND, 2 * BAND), 1)
    halves = [_lanes_in((BAND, LANES), sub * HEAD_DIM, HEAD_DIM) for sub in range(LANES // HEAD_DIM)]

    def block(n, carry):
        q0 = pl.multiple_of(n * BAND, BAND)
        k0 = pl.multiple_of(jnp.maximum(n - 1, 0) * BAND, BAND)
        q = q_ref[0, pl.ds(q0, BAND), :].astype(F32)
        kw = k_ref[0, pl.ds(k0, 2 * BAND), :]
        dist = (row - col) + (q0 - k0)
        mask = (dist >= 0) & (dist <= BAND)
        for h in range(N_HEADS):
            qh = jnp.where(_lanes_in((BAND, GW), h * HEAD_DIM, HEAD_DIM), q, 0.0).astype(BF16)
            s = lax.dot_general(qh, kw, NT_DIMS, preferred_element_type=F32)
            s_ref[h] = jnp.where(mask, s, -jnp.inf)
        for tile in range(GW // LANES):
            vw = v_ref[0, pl.ds(k0, 2 * BAND), tile * LANES:(tile + 1) * LANES]
            o = jnp.zeros((BAND, LANES), F32)
            lse = jnp.zeros((BAND, LANES), F32)
            for sub in range(LANES // HEAD_DIM):
                s = s_ref[tile * (LANES // HEAD_DIM) + sub]
                m = jnp.max(s, axis=-1, keepdims=True)
                p = jnp.exp(s - m)
                l = jnp.sum(p, axis=-1, keepdims=True)
                acc = jnp.dot(p.astype(BF16), vw, preferred_element_type=F32)
                mine = halves[sub]
                o = jnp.where(mine, acc / l, o)
                lse = jnp.where(mine, m + jnp.log(l), lse)
            o_ref[0, pl.ds(q0, BAND), tile * LANES:(tile + 1) * LANES] = o
            lse_ref[0, pl.ds(q0, BAND), tile * LANES:(tile + 1) * LANES] = lse
        return carry

    lax.fori_loop(0, q_ref.shape[1] // BAND, block, 0, unroll=2)


def _band_attn(qkv):
    nb, l, _ = qkv.shape
    assert l >= 2 * BAND and l % BAND == 0
    part = lambda c: pl.BlockSpec((1, l, GW), lambda i: (i, 0, c))
    return pl.pallas_call(
        _band_attn_kernel,
        grid=(nb,),
        in_specs=[part(0), part(1), part(2)],
        out_specs=[part(0), part(0)],
        out_shape=[jax.ShapeDtypeStruct((nb, l, GW), F32)] * 2,
        scratch_shapes=[pltpu.VMEM((N_HEADS, BAND, 2 * BAND), F32)],
        compiler_params=_params("arbitrary"),
        name="band_attn",
    )(qkv, qkv, qkv)


def _deinterleave(a, d):
    b, s, c = a.shape
    return a.reshape(b, s // d, d, c).transpose(0, 2, 1, 3).reshape(b * d, s // d, c)


def _interleave(a, d, b):
    _, l, c = a.shape
    return a.reshape(b, d, l, c).transpose(0, 2, 1, 3).reshape(b, l * d, c)


def _diff_attn_kernel(lam_ref, g_ref, qt_ref, k_ref, vt_ref, o_ref, qm_ref, m_ref, acc_ref, yt_ref,
                      st_ref, mx_ref, *, tq, lam_init):
    n_chain = 2 * N_HEADS
    i = pl.program_id(1)
    qt = qt_ref[0].astype(F32)
    chan = lax.broadcasted_iota(jnp.int32, (GW, tq), 0)
    for c in range(n_chain):
        own = (chan >= c * QK_DIM) & (chan < (c + 1) * QK_DIM)
        qm_ref[c] = jnp.where(own, qt, 0.0).astype(BF16)
    m_ref[...] = jnp.full(m_ref.shape, -jnp.inf, F32)
    acc_ref[...] = jnp.zeros(acc_ref.shape, F32)
    key = lax.broadcasted_iota(jnp.int32, (tq, tq), 0)
    qry = lax.broadcasted_iota(jnp.int32, (tq, tq), 1)

    def score(c, kb, diagonal):
        st = jnp.dot(kb, qm_ref[c], preferred_element_type=F32)
        if diagonal:
            st = jnp.where(key <= qry, st, -jnp.inf)
        st_ref[c] = st
        mx_ref[c] = jnp.max(st, axis=0, keepdims=True)

    def value(c, j):
        h = c % N_HEADS
        m_prev = m_ref[c]
        m_next = jnp.maximum(m_prev, mx_ref[c])
        alpha = jnp.exp2(m_prev - m_next)
        p = jnp.exp2(st_ref[c] - m_next).astype(BF16)
        vt = vt_ref[0, j, h * V_ROWS:(h + 1) * V_ROWS, :]
        acc_ref[c] = alpha * acc_ref[c] + jnp.dot(vt, p, preferred_element_type=F32)
        m_ref[c] = m_next

    def k_block(j):
        return k_ref[0, pl.ds(pl.multiple_of(j * tq, tq), tq), :]

    def scores(j, diagonal):
        kb = k_block(j)
        for c in range(n_chain):
            score(c, kb, diagonal)

    def values(j):
        for c in range(n_chain):
            value(c, j)

    def values_then_scores(j, diagonal):
        kb = k_block(j + 1)
        for c in range(n_chain):
            value(c, j)
            score(c, kb, diagonal)

    @pl.when(i == 0)
    def _():
        scores(0, True)

    @pl.when(i > 0)
    def _():
        scores(0, False)

    def body(j, carry):
        values_then_scores(j, False)
        return carry

    lax.fori_loop(0, i - 1, body, 0)

    @pl.when(i > 0)
    def _():
        values_then_scores(i - 1, True)

    values(i)

    lv = lam_ref[...]
    lam = (jnp.exp(jnp.sum(lv[0:1] * lv[1:2], axis=-1, keepdims=True))
           - jnp.exp(jnp.sum(lv[2:3] * lv[3:4], axis=-1, keepdims=True)) + lam_init)
    for h in range(N_HEADS):
        a1, a2 = acc_ref[h], acc_ref[N_HEADS + h]
        oh = (a1[:HEAD_DIM] / a1[HEAD_DIM:HEAD_DIM + 1]
              - lam * (a2[:HEAD_DIM] / a2[HEAD_DIM:HEAD_DIM + 1]))
        ms = jnp.mean(oh * oh, axis=0, keepdims=True)
        g = g_ref[h * HEAD_DIM:(h + 1) * HEAD_DIM, :]
        yt_ref[h * HEAD_DIM:(h + 1) * HEAD_DIM, :] = oh * lax.rsqrt(ms + NORM_EPS) * (g * (1.0 - lam_init))
    o_ref[0] = jnp.transpose(yt_ref[...]).astype(BF16)


def _diff_attn(qkv, lam_vecs, g_col, lam_init, tq):
    b, s, _ = qkv.shape
    nk = s // tq
    qt = jnp.swapaxes(qkv[..., 0:GW], 1, 2)
    vt = qkv[..., 2 * GW:3 * GW].reshape(b, nk, tq, N_HEADS, HEAD_DIM).transpose(0, 1, 3, 4, 2)
    ones = jnp.ones((b, nk, N_HEADS, V_ROWS - HEAD_DIM, tq), BF16)
    vt = jnp.concatenate([vt, ones], axis=3).reshape(b, nk, N_HEADS * V_ROWS, tq)
    n_chain = 2 * N_HEADS
    return pl.pallas_call(
        functools.partial(_diff_attn_kernel, tq=tq, lam_init=lam_init),
        grid=(b, nk),
        in_specs=[
            pl.BlockSpec((4, QK_DIM), lambda i, j: (0, 0)),
            pl.BlockSpec((GW, 1), lambda i, j: (0, 0)),
            pl.BlockSpec((1, GW, tq), lambda i, j: (i, 0, j)),
            pl.BlockSpec((1, s, GW), lambda i, j: (i, 0, 1)),
            pl.BlockSpec((1, nk, N_HEADS * V_ROWS, tq), lambda i, j: (i, 0, 0, 0)),
        ],
        out_specs=pl.BlockSpec((1, tq, GW), lambda i, j: (i, j, 0)),
        out_shape=jax.ShapeDtypeStruct((b, s, GW), BF16),
        scratch_shapes=[
            pltpu.VMEM((n_chain, GW, tq), BF16),
            pltpu.VMEM((n_chain, 1, tq), F32),
            pltpu.VMEM((n_chain, V_ROWS, tq), F32),
            pltpu.VMEM((GW, tq), F32),
            pltpu.VMEM((n_chain, tq, tq), F32),
            pltpu.VMEM((n_chain, 1, tq), F32),
        ],
        compiler_params=_params("arbitrary", "arbitrary"),
        name="diff_attn",
    )(lam_vecs, g_col, qt, qkv, vt)


def _out_proj_kernel(x_ref, yab_ref, o1_ref, o2_ref, o3_ref, l1_ref, l2_ref, l3_ref, yd_ref, w_ref, out_ref):
    l1, l2, l3 = l1_ref[...], l2_ref[...], l3_ref[...]
    m = jnp.maximum(jnp.maximum(l1, l2), l3)
    w1, w2, w3 = jnp.exp(l1 - m), jnp.exp(l2 - m), jnp.exp(l3 - m)
    yc = (w1 * o1_ref[...] + w2 * o2_ref[...] + w3 * o3_ref[...]) / (w1 + w2 + w3)
    acc = jnp.dot(yab_ref[...], w_ref[0:2 * GW, :], preferred_element_type=F32)
    acc = acc + jnp.dot(yc.astype(BF16), w_ref[2 * GW:3 * GW, :], preferred_element_type=F32)
    acc = acc + jnp.dot(yd_ref[...], w_ref[3 * GW:4 * GW, :], preferred_element_type=F32)
    out_ref[...] = x_ref[...] + acc


def _out_proj(x, yab, os_, lses, yd, w, tm):
    t, d = x.shape
    rows = lambda c: pl.BlockSpec((tm, c), lambda i: (i, 0))
    return pl.pallas_call(
        _out_proj_kernel,
        grid=(t // tm,),
        in_specs=[rows(d), rows(2 * GW)] + [rows(GW)] * 6 + [rows(GW), pl.BlockSpec((d, d), lambda i: (0, 0))],
        out_specs=rows(d),
        out_shape=jax.ShapeDtypeStruct((t, d), F32),
        compiler_params=_params("arbitrary"),
        name="out_proj",
    )(x, yab, *os_, *lses, yd, w)


def _ffn_kernel(x_ref, g_ref, wg_ref, wu_ref, wd_ref, out_ref, hn_ref, acc_ref):
    f = pl.program_id(1)

    @pl.when(f == 0)
    def _():
        hn_ref[...] = _rms(x_ref[...], g_ref[...]).astype(BF16)
        acc_ref[...] = jnp.zeros_like(acc_ref)

    hn = hn_ref[...]
    gate = jnp.dot(hn, wg_ref[...], preferred_element_type=F32)
    up = jnp.dot(hn, wu_ref[...], preferred_element_type=F32)
    act = (gate * jax.nn.sigmoid(gate) * up).astype(BF16)
    acc_ref[...] += jnp.dot(act, wd_ref[...], preferred_element_type=F32)

    @pl.when(f == pl.num_programs(1) - 1)
    def _():
        out_ref[...] = x_ref[...] + acc_ref[...]


def _ffn(x, g, wg, wu, wd, tm, tf):
    t, d = x.shape
    ff = wg.shape[1]
    return pl.pallas_call(
        _ffn_kernel,
        grid=(t // tm, ff // tf),
        in_specs=[
            pl.BlockSpec((tm, d), lambda i, f: (i, 0)),
            pl.BlockSpec((1, d), lambda i, f: (0, 0)),
            pl.BlockSpec((d, tf), lambda i, f: (0, f)),
            pl.BlockSpec((d, tf), lambda i, f: (0, f)),
            pl.BlockSpec((tf, d), lambda i, f: (f, 0)),
        ],
        out_specs=pl.BlockSpec((tm, d), lambda i, f: (i, 0)),
        out_shape=jax.ShapeDtypeStruct((t, d), F32),
        scratch_shapes=[pltpu.VMEM((tm, d), BF16), pltpu.VMEM((tm, d), F32)],
        compiler_params=_params("arbitrary", "arbitrary"),
        name="ffn_dense",
    )(x, g, wg, wu, wd)


META_E1, META_E2, META_RANK1, META_RANK2, META_G1, META_G2 = range(6)


def _router_kernel(x_ref, g_ref, r_ref, hn_ref, meta_ref, after_ref, carry_ref, *, blocks_per_chunk):
    @pl.when(pl.program_id(0) % blocks_per_chunk == 0)
    def _():
        carry_ref[...] = jnp.zeros_like(carry_ref)

    h = _rms(x_ref[...], g_ref[...])
    hn_ref[...] = h.astype(BF16)
    logits = jnp.dot(h, r_ref[...], preferred_element_type=F32, precision=lax.Precision.HIGHEST)
    tm = logits.shape[0]
    lane = lax.broadcasted_iota(jnp.int32, logits.shape, 1)
    logits = jnp.where(lane < N_EXPERTS, logits, -jnp.inf)
    m1 = jnp.max(logits, axis=-1, keepdims=True)
    i1 = jnp.min(jnp.where(logits == m1, lane, LANES), axis=-1, keepdims=True)
    rest = jnp.where(lane == i1, -jnp.inf, logits)
    m2 = jnp.max(rest, axis=-1, keepdims=True)
    i2 = jnp.min(jnp.where(rest == m2, lane, LANES), axis=-1, keepdims=True)
    e2 = jnp.exp(m2 - m1)
    g1 = 1.0 / (1.0 + e2)
    g2 = e2 / (1.0 + e2)

    oh1 = lane == i1
    oh2 = lane == i2
    both = jnp.where(oh1, 1.0, jnp.where(oh2, 1.0, 0.0))
    row = lax.broadcasted_iota(jnp.int32, (tm, tm), 0)
    col = lax.broadcasted_iota(jnp.int32, (tm, tm), 1)
    earlier = jnp.where(col < row, 1.0, 0.0).astype(BF16)
    carry = carry_ref[0:1, :]
    prefix = jnp.dot(earlier, both.astype(BF16), preferred_element_type=F32) + carry
    rank1 = jnp.sum(jnp.where(oh1, prefix, 0.0), axis=-1, keepdims=True)
    rank2 = jnp.sum(jnp.where(oh2, prefix, 0.0), axis=-1, keepdims=True)
    total = carry + jnp.sum(both, axis=0, keepdims=True)

    meta = jnp.zeros(logits.shape, F32)
    for k, val in ((META_E1, i1.astype(F32)), (META_E2, i2.astype(F32)), (META_RANK1, rank1),
                   (META_RANK2, rank2), (META_G1, g1), (META_G2, g2)):
        meta = jnp.where(lane == k, val, meta)
    meta_ref[...] = meta
    after_ref[0] = jnp.broadcast_to(total, after_ref.shape[1:])
    carry_ref[...] = jnp.broadcast_to(total, carry_ref.shape)


def _router(x, g, r_pad, tm, blocks_per_chunk):
    t, d = x.shape
    nblk = t // tm
    counts = pl.BlockSpec((1, 8, LANES), lambda i: (i, 0, 0))
    return pl.pallas_call(
        functools.partial(_router_kernel, blocks_per_chunk=blocks_per_chunk),
        grid=(nblk,),
        in_specs=[
            pl.BlockSpec((tm, d), lambda i: (i, 0)),
            pl.BlockSpec((1, d), lambda i: (0, 0)),
            pl.BlockSpec((d, LANES), lambda i: (0, 0)),
        ],
        out_specs=[
            pl.BlockSpec((tm, d), lambda i: (i, 0)),
            pl.BlockSpec((tm, LANES), lambda i: (i, 0)),
            counts,
        ],
        out_shape=[
            jax.ShapeDtypeStruct((t, d), BF16),
            jax.ShapeDtypeStruct((t, LANES), F32),
            jax.ShapeDtypeStruct((nblk, 8, LANES), F32),
        ],
        scratch_shapes=[pltpu.VMEM((8, LANES), F32)],
        compiler_params=_params("arbitrary"),
        name="router",
    )(x, g, r_pad)


SLOT_ROW1, SLOT_ROW2, SLOT_G1, SLOT_G2 = range(4)


def _moe_kernel(nt_ref, toff_ref, cuts_ref, hn_ref, slotc_ref, slotr_ref, wg_ref, wu_ref, wd_ref, out_ref,
                xs_ref, ys_ref, *, tr, sub, win):
    c, e, f = pl.program_id(0), pl.program_id(1), pl.program_id(2)
    last_f = pl.num_programs(2) - 1
    grp = c * N_EXPERTS + e
    n_tiles = nt_ref[grp]
    row0 = toff_ref[grp] * tr
    tc = hn_ref.shape[0]
    nsub = tc // sub

    @pl.when((e == 0) & (f == 0))
    def _():
        out_ref[...] = jnp.zeros_like(out_ref)

    def rows_of(r):
        return pl.ds(pl.multiple_of(r * tr, tr), tr)

    def token_window(r):
        base = grp * nsub
        first = r * tr
        last = jnp.minimum(first + tr, cuts_ref[base + nsub - 1]) - 1
        lo = hi = 0
        for k in range(nsub - 1):
            cut = cuts_ref[base + k]
            lo = lo + (cut <= first).astype(jnp.int32)
            hi = hi + (cut <= last).astype(jnp.int32)
        start = jnp.minimum(lo, nsub - win)
        return hi < start + win, pl.multiple_of(start * sub, sub)

    def spans(r, fn):
        covered, tok0 = token_window(r)

        @pl.when(covered)
        def _():
            fn(r, tok0, win * sub)

        @pl.when(jnp.logical_not(covered))
        def _():
            fn(r, 0, tc)

    def gather_span(r, tok0, ntok):
        toks = pl.ds(tok0, ntok)
        target = (lax.broadcasted_iota(jnp.int32, (tr, ntok), 0) + (row0 + r * tr)).astype(F32)
        hit1 = slotr_ref[SLOT_ROW1:SLOT_ROW1 + 1, toks] == target
        hit2 = slotr_ref[SLOT_ROW2:SLOT_ROW2 + 1, toks] == target
        oh = jnp.where(hit1, 1.0, jnp.where(hit2, 1.0, 0.0)).astype(BF16)
        xs_ref[rows_of(r), :] = jnp.dot(oh, hn_ref[toks, :], preferred_element_type=F32).astype(BF16)

    def gather_tile(r, carry):
        spans(r, gather_span)
        return carry

    def ffn_tile(r, first):
        xt = xs_ref[rows_of(r), :]
        gate = jnp.dot(xt, wg_ref[...], preferred_element_type=F32)
        up = jnp.dot(xt, wu_ref[...], preferred_element_type=F32)
        act = (gate * jax.nn.sigmoid(gate) * up).astype(BF16)
        y = jnp.dot(act, wd_ref[...], preferred_element_type=F32)
        if first:
            ys_ref[rows_of(r), :] = y
        else:
            ys_ref[rows_of(r), :] += y

    def scatter_span(r, tok0, ntok):
        toks = pl.ds(tok0, ntok)
        yb = ys_ref[rows_of(r), :].astype(BF16)
        target = (lax.broadcasted_iota(jnp.int32, (ntok, tr), 1) + (row0 + r * tr)).astype(F32)
        hit1 = slotc_ref[toks, SLOT_ROW1:SLOT_ROW1 + 1] == target
        hit2 = slotc_ref[toks, SLOT_ROW2:SLOT_ROW2 + 1] == target
        gh = jnp.where(hit1, slotc_ref[toks, SLOT_G1:SLOT_G1 + 1],
                       jnp.where(hit2, slotc_ref[toks, SLOT_G2:SLOT_G2 + 1], 0.0)).astype(BF16)
        out_ref[toks, :] += jnp.dot(gh, yb, preferred_element_type=F32)

    def scatter_tile(r, carry):
        spans(r, scatter_span)
        return carry

    def ffn_all(first):
        def pair(t, carry):
            ffn_tile(2 * t, first)
            ffn_tile(2 * t + 1, first)
            return carry

        lax.fori_loop(0, n_tiles // 2, pair, 0)

        @pl.when(n_tiles % 2 == 1)
        def _():
            ffn_tile(n_tiles - 1, first)

    @pl.when(f == 0)
    def _():
        lax.fori_loop(0, n_tiles, gather_tile, 0)
        ffn_all(True)

    @pl.when(f > 0)
    def _():
        ffn_all(False)

    @pl.when(f == last_f)
    def _():
        lax.fori_loop(0, n_tiles, scatter_tile, 0)


def _moe(hn, slot_cols, slot_rows, nt, toff, cuts, wg, wu, wd, tc, tr, tf, sub, win):
    t, d = hn.shape
    ne, _, ff = wg.shape
    grid_spec = pltpu.PrefetchScalarGridSpec(
        num_scalar_prefetch=3,
        grid=(t // tc, ne, ff // tf),
        in_specs=[
            pl.BlockSpec((tc, d), lambda c, e, f, *_: (c, 0)),
            pl.BlockSpec((tc, 4), lambda c, e, f, *_: (c, 0)),
            pl.BlockSpec((8, tc), lambda c, e, f, *_: (0, c)),
            pl.BlockSpec((None, d, tf), lambda c, e, f, *_: (e, 0, f)),
            pl.BlockSpec((None, d, tf), lambda c, e, f, *_: (e, 0, f)),
            pl.BlockSpec((None, tf, d), lambda c, e, f, *_: (e, f, 0)),
        ],
        out_specs=pl.BlockSpec((tc, d), lambda c, e, f, *_: (c, 0)),
        scratch_shapes=[
            pltpu.VMEM((tc + tr, d), BF16),
            pltpu.VMEM((tc + tr, d), F32),
        ],
    )
    return pl.pallas_call(
        functools.partial(_moe_kernel, tr=tr, sub=sub, win=win),
        grid_spec=grid_spec,
        out_shape=jax.ShapeDtypeStruct((t, d), F32),
        compiler_params=_params("arbitrary", "arbitrary", "arbitrary"),
        name="moe",
    )(nt, toff, cuts, hn, slot_cols, slot_rows, wg, wu, wd)


def _final_kernel(x_ref, y_ref, g_ref, out_ref):
    out_ref[...] = _rms(x_ref[...] + y_ref[...], g_ref[...])


def _final_norm(x, y, g, tm):
    t, d = x.shape
    rows = pl.BlockSpec((tm, d), lambda i: (i, 0))
    return pl.pallas_call(
        _final_kernel,
        grid=(t // tm,),
        in_specs=[rows, rows, pl.BlockSpec((1, d), lambda i: (0, 0))],
        out_specs=rows,
        out_shape=jax.ShapeDtypeStruct((t, d), F32),
        compiler_params=_params("arbitrary"),
        name="final_norm",
    )(x, y, g)


def _routed_swiglu(x, norm_g, router_w, wg, wu, wd, final_g, tc, tr, tf, sub, win):
    t, d = x.shape
    nc, nsub = t // tc, tc // sub
    r_pad = jnp.zeros((d, LANES), F32).at[:, :N_EXPERTS].set(router_w)
    hn, meta, after = _router(x, norm_g, r_pad, tm=sub, blocks_per_chunk=nsub)
    through = after[:, 0, :N_EXPERTS].astype(jnp.int32).reshape(nc, nsub, N_EXPERTS)
    cuts = through.transpose(0, 2, 1)
    total = through[:, -1]
    nt = (total + tr - 1) // tr
    toff = jnp.cumsum(nt, axis=1) - nt
    first_row = jnp.repeat((toff * tr).astype(F32), tc, axis=0)
    experts = jnp.arange(N_EXPERTS, dtype=F32)
    pick = lambda e: jnp.sum(jnp.where(e[:, None] == experts, first_row, 0.0), axis=1)
    row1 = pick(meta[:, META_E1]) + meta[:, META_RANK1]
    row2 = pick(meta[:, META_E2]) + meta[:, META_RANK2]
    slot_cols = jnp.stack([row1, row2, meta[:, META_G1], meta[:, META_G2]], axis=1)
    slot_rows = jnp.zeros((8, t), F32).at[0:4].set(slot_cols.T)
    y = _moe(hn, slot_cols, slot_rows, nt.reshape(-1), toff.reshape(-1), cuts.reshape(-1), wg, wu, wd,
             tc=tc, tr=tr, tf=tf, sub=sub, win=win)
    return _final_norm(x, y, final_g, tm=min(1024, t))


def _permute_w_in(w):
    def perm(block):
        d = block.shape[0]
        return block.reshape(d, N_HEADS, 2, QK_DIM).transpose(0, 2, 1, 3).reshape(d, GW)
    parts = [w[:, :6 * GW], perm(w[:, 6 * GW:7 * GW]), perm(w[:, 7 * GW:8 * GW]), w[:, 8 * GW:]]
    return jnp.concatenate(parts, axis=1)


def _block_diag(pool_w):
    n, p, _ = pool_w.shape
    out = jnp.zeros((n * p, n * p), pool_w.dtype)
    for gi in range(n):
        out = out.at[gi * p:(gi + 1) * p, gi * p:(gi + 1) * p].set(pool_w[gi])
    return out


def _token_mixing(x, layer, lam_init, p, rope_c, rope_d):
    b, s, d = x.shape
    row = lambda v: v.reshape(1, -1)
    w_in = _permute_w_in(p["w_in"][layer]).astype(BF16)
    uab, qkvc, qkvd = _norm_proj(x, row(p["norm1_g"][layer]), w_in, rope_c, rope_d, tm=512)
    dww = jnp.zeros((HALO, GW), F32).at[:CONV_WIDTH].set(p["conv_dw_w"][layer])
    yab = _local_mixers(
        uab, dww, row(p["conv_dw_b"][layer]), row(p["conv_ln_g"][layer]), row(p["conv_ln_b"][layer]),
        p["conv_pw_w"][layer].astype(BF16), _block_diag(p["pool_w"][layer]).astype(BF16),
        row(p["pool_scale"][layer]), ts=512)
    os_, lses = [], []
    for dil in DILATIONS:
        o, lse = _band_attn(qkvc if dil == 1 else _deinterleave(qkvc, dil))
        if dil > 1:
            o, lse = _interleave(o, dil, b), _interleave(lse, dil, b)
        os_.append(o.reshape(b * s, GW))
        lses.append(lse.reshape(b * s, GW))
    yd = _diff_attn(qkvd, p["diff_lam"][layer], jnp.tile(p["diff_ln_g"][layer], N_HEADS).reshape(GW, 1),
                    lam_init, tq=256)
    return _out_proj(x.reshape(b * s, d), yab.reshape(b * s, 2 * GW), os_, lses, yd.reshape(b * s, GW),
                     p["w_out"][layer].astype(BF16), tm=512)


def kernel(x, norm1_g, w_in, conv_dw_w, conv_dw_b, conv_ln_g, conv_ln_b, conv_pw_w, pool_w, pool_scale,
           diff_lam, diff_ln_g, w_out, norm2_g, ffn_w_gate, ffn_w_up, ffn_w_down, moe_router, moe_w_gate,
           moe_w_up, moe_w_down, final_g):
    b, s, d = x.shape
    p = dict(norm1_g=norm1_g, w_in=w_in, conv_dw_w=conv_dw_w, conv_dw_b=conv_dw_b, conv_ln_g=conv_ln_g,
             conv_ln_b=conv_ln_b, conv_pw_w=conv_pw_w, pool_w=pool_w, pool_scale=pool_scale,
             diff_lam=diff_lam, diff_ln_g=diff_ln_g, w_out=w_out)
    rope_c = _rope_table(s, HEAD_DIM, HEAD_DIM // 4)
    rope_d = _rope_table(s, QK_DIM, QK_DIM // 4)
    row = lambda v: v.reshape(1, -1)

    x1 = _token_mixing(x, 0, 0.8 - 0.6 * math.exp(-0.3 * 0), p, rope_c, rope_d)
    x2 = _ffn(x1, row(norm2_g[0]), ffn_w_gate[0].astype(BF16), ffn_w_up[0].astype(BF16),
              ffn_w_down[0].astype(BF16), tm=512, tf=1408)

    x3 = _token_mixing(x2.reshape(b, s, d), 1, 0.8 - 0.6 * math.exp(-0.3 * 1), p, rope_c, rope_d)
    out = _routed_swiglu(x3, row(norm2_g[1]), moe_router[0], moe_w_gate[0].astype(BF16),
                         moe_w_up[0].astype(BF16), moe_w_down[0].astype(BF16), row(final_g),
                         tc=2048, tr=256, tf=512, sub=256, win=5)
    return out.reshape(b, s, d)
```

```python
import functools
import math

import jax
import jax.numpy as jnp
from jax import lax
from jax.experimental import pallas as pl
from jax.experimental.pallas import tpu as pltpu

D_MODEL = 1024
GW = 256
N_HEADS = 4
HEAD_DIM = 64
QK_DIM = 32
V_ROWS = HEAD_DIM + 16
CONV_WIDTH = 31
POOL_WINDOWS = (2, 4, 8, 16)
DILATIONS = (1, 4, 16)
BAND = 128
ROPE_THETA = 500000.0
N_EXPERTS = 8
NORM_EPS = 1e-5
HALO = 32
LANES = 128
VMEM_LIMIT = 56 * 1024 * 1024

BF16 = jnp.bfloat16
F32 = jnp.float32
NT_DIMS = (((1,), (1,)), ((), ()))


def _params(*sem):
    return pltpu.CompilerParams(dimension_semantics=sem, vmem_limit_bytes=VMEM_LIMIT)


def _lanes_in(shape, start, width):
    lane = lax.broadcasted_iota(jnp.int32, shape, len(shape) - 1)
    return (lane >= start) & (lane < start + width)


def _rms(x, g):
    ms = jnp.mean(x * x, axis=-1, keepdims=True)
    return x * lax.rsqrt(ms + NORM_EPS) * g


def _rope_table(seq, width, rot):
    half = rot // 2
    pos = jnp.arange(seq, dtype=F32)
    inv = ROPE_THETA ** (-jnp.arange(0, rot, 2, dtype=F32) / rot)
    ang = pos[:, None] * inv[None, :]
    cos, sin = jnp.cos(ang), jnp.sin(ang)
    lane = jnp.arange(GW) % width
    f = lane % half
    first = lane < half
    second = (lane >= half) & (lane < rot)
    c = jnp.where((lane < rot)[None, :], cos[:, f], 1.0)
    sn = jnp.where(first[None, :], -sin[:, f], 0.0)
    sp = jnp.where(second[None, :], sin[:, f], 0.0)
    return jnp.concatenate([c, sn, sp], axis=1)


def _rope(u, tab_ref, half):
    return (u * tab_ref[:, 0:GW]
            + pltpu.roll(u, GW - half, 1) * tab_ref[:, GW:2 * GW]
            + pltpu.roll(u, half, 1) * tab_ref[:, 2 * GW:3 * GW])


def _norm_proj_kernel(x_ref, g_ref, w_ref, rc_ref, rd_ref, uab_ref, qkvc_ref, qt_ref, kd_ref, vt_ref, v_ref,
                      *, tk):
    h = _rms(x_ref[0], g_ref[...]).astype(BF16)

    def proj(grp):
        return jnp.dot(h, w_ref[:, grp * GW:(grp + 1) * GW], preferred_element_type=F32)

    for grp in range(3):
        uab_ref[0, :, grp * GW:(grp + 1) * GW] = proj(grp)
    c_scale = HEAD_DIM ** -0.5
    d_scale = QK_DIM ** -0.5 * math.log2(math.e)
    qkvc_ref[0, :, 0:GW] = (_rope(proj(3), rc_ref, 8) * c_scale).astype(BF16)
    qkvc_ref[0, :, GW:2 * GW] = _rope(proj(4), rc_ref, 8).astype(BF16)
    qkvc_ref[0, :, 2 * GW:3 * GW] = proj(5).astype(BF16)
    qt_ref[0] = jnp.transpose(_rope(proj(6), rd_ref, 4) * d_scale).astype(BF16)
    kd_ref[0] = _rope(proj(7), rd_ref, 4).astype(BF16)
    v_ref[...] = proj(8)
    vt = jnp.transpose(v_ref[...]).astype(BF16)
    ones = jnp.ones((V_ROWS - HEAD_DIM, tk), BF16)
    for blk in range(vt.shape[1] // tk):
        for hd in range(N_HEADS):
            vt_ref[0, blk, hd * V_ROWS:hd * V_ROWS + HEAD_DIM, :] = (
                vt[hd * HEAD_DIM:(hd + 1) * HEAD_DIM, blk * tk:(blk + 1) * tk])
            vt_ref[0, blk, hd * V_ROWS + HEAD_DIM:(hd + 1) * V_ROWS, :] = ones


def _norm_proj(x, g, w, rope_c, rope_d, tm, tk):
    b, s, d = x.shape
    grid = (s // tm, b)
    return pl.pallas_call(
        functools.partial(_norm_proj_kernel, tk=tk),
        grid=grid,
        in_specs=[
            pl.BlockSpec((1, tm, d), lambda i, j: (j, i, 0)),
            pl.BlockSpec((1, d), lambda i, j: (0, 0)),
            pl.BlockSpec((d, 9 * GW), lambda i, j: (0, 0)),
            pl.BlockSpec((tm, 3 * GW), lambda i, j: (i, 0)),
            pl.BlockSpec((tm, 3 * GW), lambda i, j: (i, 0)),
        ],
        out_specs=[
            pl.BlockSpec((1, tm, 3 * GW), lambda i, j: (j, i, 0)),
            pl.BlockSpec((1, tm, 3 * GW), lambda i, j: (j, i, 0)),
            pl.BlockSpec((1, GW, tm), lambda i, j: (j, 0, i)),
            pl.BlockSpec((1, tm, GW), lambda i, j: (j, i, 0)),
            pl.BlockSpec((1, tm // tk, N_HEADS * V_ROWS, tk), lambda i, j: (j, i, 0, 0)),
        ],
        out_shape=[
            jax.ShapeDtypeStruct((b, s, 3 * GW), F32),
            jax.ShapeDtypeStruct((b, s, 3 * GW), BF16),
            jax.ShapeDtypeStruct((b, GW, s), BF16),
            jax.ShapeDtypeStruct((b, s, GW), BF16),
            jax.ShapeDtypeStruct((b, s // tk, N_HEADS * V_ROWS, tk), BF16),
        ],
        scratch_shapes=[pltpu.VMEM((tm, GW), F32)],
        compiler_params=_params("arbitrary", "arbitrary"),
        name="norm_proj",
    )(x, g, w, rope_c, rope_d)


def _local_kernel(u_ref, dww_ref, dwb_ref, lng_ref, lnb_ref, pww_ref, plw_ref, pls_ref,
                  y_ref, zbuf, pbuf, *, ts):
    si = pl.program_id(1)

    @pl.when(si == 0)
    def _():
        zbuf[0:HALO, :] = jnp.zeros((HALO, GW), F32)
        pbuf[0:HALO, :] = jnp.zeros((HALO, GW), F32)

    a = u_ref[0, :, 0:GW]
    b = u_ref[0, :, GW:2 * GW]
    zbuf[HALO:HALO + ts, :] = a * jax.nn.sigmoid(b)
    pbuf[HALO:HALO + ts, :] = u_ref[0, :, 2 * GW:3 * GW]

    base = HALO - (CONV_WIDTH - 1)
    acc = jnp.zeros((ts, GW), F32)
    for k in range(CONV_WIDTH):
        acc = acc + dww_ref[k:k + 1, :] * zbuf[base + k:base + k + ts, :]
    z = acc + dwb_ref[...]
    mu = jnp.mean(z, axis=-1, keepdims=True)
    zc = z - mu
    var = jnp.mean(zc * zc, axis=-1, keepdims=True)
    zn = zc * lax.rsqrt(var + NORM_EPS) * lng_ref[...] + lnb_ref[...]
    zs = zn * jax.nn.sigmoid(zn)
    y_ref[0, :, 0:GW] = jnp.dot(zs.astype(BF16), pww_ref[...], preferred_element_type=F32).astype(BF16)

    lane = lax.broadcasted_iota(jnp.int32, (ts, GW), 1)
    pool_group = GW // len(POOL_WINDOWS)
    win = jnp.zeros((ts, GW), jnp.int32)
    for gi, w in enumerate(POOL_WINDOWS):
        win = jnp.where(lane >= gi * pool_group, w, win)
    p0 = pbuf[HALO:HALO + ts, :]
    tot = p0
    for k in range(1, max(POOL_WINDOWS)):
        pk = pbuf[HALO - k:HALO - k + ts, :]
        tot = tot + (pk if k < min(POOL_WINDOWS) else jnp.where(win > k, pk, 0.0))
    pos = si * ts + lax.broadcasted_iota(jnp.int32, (ts, GW), 0)
    cnt = jnp.minimum(pos + 1, win).astype(F32)
    pooled = tot / cnt - p0
    yb = jnp.dot(pooled.astype(BF16), plw_ref[...], preferred_element_type=F32) * pls_ref[...]
    y_ref[0, :, GW:2 * GW] = yb.astype(BF16)

    zbuf[0:HALO, :] = zbuf[ts:ts + HALO, :]
    pbuf[0:HALO, :] = pbuf[ts:ts + HALO, :]


def _local_mixers(uab, dww, dwb, lng, lnb, pww, plw, pls, ts):
    b, s, _ = uab.shape
    vec = lambda: pl.BlockSpec((1, GW), lambda i, j: (0, 0))
    mat = lambda: pl.BlockSpec((GW, GW), lambda i, j: (0, 0))
    return pl.pallas_call(
        functools.partial(_local_kernel, ts=ts),
        grid=(b, s // ts),
        in_specs=[
            pl.BlockSpec((1, ts, 3 * GW), lambda i, j: (i, j, 0)),
            pl.BlockSpec((HALO, GW), lambda i, j: (0, 0)),
            vec(), vec(), vec(), mat(), mat(), vec(),
        ],
        out_specs=pl.BlockSpec((1, ts, 2 * GW), lambda i, j: (i, j, 0)),
        out_shape=jax.ShapeDtypeStruct((b, s, 2 * GW), BF16),
        scratch_shapes=[pltpu.VMEM((HALO + ts, GW), F32), pltpu.VMEM((HALO + ts, GW), F32)],
        compiler_params=_params("arbitrary", "arbitrary"),
        name="local_mixers",
    )(uab, dww, dwb, lng, lnb, pww, plw, pls)


def _band_attn_kernel(q_ref, k_ref, v_ref, o_ref, lse_ref, s_ref):
    row = lax.broadcasted_iota(jnp.int32, (BAND, 2 * BAND), 0)
    col = lax.broadcasted_iota(jnp.int32, (BAND, 2 * BAND), 1)
    halves = [_lanes_in((BAND, LANES), sub * HEAD_DIM, HEAD_DIM) for sub in range(LANES // HEAD_DIM)]

    def block(n, carry):
        q0 = pl.multiple_of(n * BAND, BAND)
        k0 = pl.multiple_of(jnp.maximum(n - 1, 0) * BAND, BAND)
        q = q_ref[0, pl.ds(q0, BAND), :].astype(F32)
        kw = k_ref[0, pl.ds(k0, 2 * BAND), :]
        dist = (row - col) + (q0 - k0)
        mask = (dist >= 0) & (dist <= BAND)
        for h in range(N_HEADS):
            qh = jnp.where(_lanes_in((BAND, GW), h * HEAD_DIM, HEAD_DIM), q, 0.0).astype(BF16)
            s = lax.dot_general(qh, kw, NT_DIMS, preferred_element_type=F32)
            s_ref[h] = jnp.where(mask, s, -jnp.inf)
        for tile in range(GW // LANES):
            vw = v_ref[0, pl.ds(k0, 2 * BAND), tile * LANES:(tile + 1) * LANES]
            o = jnp.zeros((BAND, LANES), F32)
            lse = jnp.zeros((BAND, LANES), F32)
            for sub in range(LANES // HEAD_DIM):
                s = s_ref[tile * (LANES // HEAD_DIM) + sub]
                m = jnp.max(s, axis=-1, keepdims=True)
                p = jnp.exp(s - m)
                l = jnp.sum(p, axis=-1, keepdims=True)
                acc = jnp.dot(p.astype(BF16), vw, preferred_element_type=F32)
                mine = halves[sub]
                o = jnp.where(mine, acc / l, o)
                lse = jnp.where(mine, m + jnp.log(l), lse)
            o_ref[0, pl.ds(q0, BAND), tile * LANES:(tile + 1) * LANES] = o
            lse_ref[0, pl.ds(q0, BAND), tile * LANES:(tile + 1) * LANES] = lse
        return carry

    lax.fori_loop(0, q_ref.shape[1] // BAND, block, 0, unroll=2)


def _band_attn(qkv, dil):
    b, s, _ = qkv.shape
    l = s // dil
    assert l >= 2 * BAND and l % BAND == 0
    part = lambda c: pl.BlockSpec((1, l, GW), lambda i, r: (i, 0, 3 * r + c))
    out = pl.BlockSpec((1, l, GW), lambda i, r: (i, 0, r))
    o, lse = pl.pallas_call(
        _band_attn_kernel,
        grid=(b, dil),
        in_specs=[part(0), part(1), part(2)],
        out_specs=[out, out],
        out_shape=[jax.ShapeDtypeStruct((b, l, dil * GW), F32)] * 2,
        scratch_shapes=[pltpu.VMEM((N_HEADS, BAND, 2 * BAND), F32)],
        compiler_params=_params("arbitrary", "arbitrary"),
        name="band_attn",
    )(*[qkv.reshape(b, l, dil * 3 * GW)] * 3)
    return o.reshape(b, s, GW), lse.reshape(b, s, GW)


def _diff_attn_kernel(lam_ref, g_ref, qt_ref, k_ref, vt_ref, o_ref, qm_ref, m_ref, acc_ref, yt_ref,
                      st_ref, mx_ref, *, tq, lam_init):
    n_chain = 2 * N_HEADS
    i = pl.program_id(1)
    @pl.when((pl.program_id(0) == 0) & (i == 0))
    def _():
        qm_ref[...] = jnp.zeros(qm_ref.shape, BF16)

    for c in range(n_chain):
        qm_ref[c, c * QK_DIM:(c + 1) * QK_DIM, :] = qt_ref[0, c * QK_DIM:(c + 1) * QK_DIM, :]
    m_ref[...] = jnp.full(m_ref.shape, -jnp.inf, F32)
    acc_ref[...] = jnp.zeros(acc_ref.shape, F32)
    key = lax.broadcasted_iota(jnp.int32, (tq, tq), 0)
    qry = lax.broadcasted_iota(jnp.int32, (tq, tq), 1)

    def score(c, kb, diagonal):
        st = jnp.dot(kb, qm_ref[c], preferred_element_type=F32)
        if diagonal:
            st = jnp.where(key <= qry, st, -jnp.inf)
        st_ref[c] = st
        mx_ref[c] = jnp.max(st, axis=0, keepdims=True)

    def value(c, j):
        h = c % N_HEADS
        m_prev = m_ref[c]
        m_next = jnp.maximum(m_prev, mx_ref[c])
        alpha = jnp.exp2(m_prev - m_next)
        p = jnp.exp2(st_ref[c] - m_next).astype(BF16)
        vt = vt_ref[0, j, h * V_ROWS:(h + 1) * V_ROWS, :]
        acc_ref[c] = alpha * acc_ref[c] + jnp.dot(vt, p, preferred_element_type=F32)
        m_ref[c] = m_next

    def k_block(j):
        return k_ref[0, pl.ds(pl.multiple_of(j * tq, tq), tq), :]

    def scores(j, diagonal):
        kb = k_block(j)
        for c in range(n_chain):
            score(c, kb, diagonal)

    def values(j):
        for c in range(n_chain):
            value(c, j)

    def values_then_scores(j, diagonal):
        kb = k_block(j + 1)
        for c in range(n_chain):
            value(c, j)
            score(c, kb, diagonal)

    @pl.when(i == 0)
    def _():
        scores(0, True)

    @pl.when(i > 0)
    def _():
        scores(0, False)

    def body(j, carry):
        values_then_scores(j, False)
        return carry

    lax.fori_loop(0, i - 1, body, 0)

    @pl.when(i > 0)
    def _():
        values_then_scores(i - 1, True)

    values(i)

    lv = lam_ref[...]
    lam = (jnp.exp(jnp.sum(lv[0:1] * lv[1:2], axis=-1, keepdims=True))
           - jnp.exp(jnp.sum(lv[2:3] * lv[3:4], axis=-1, keepdims=True)) + lam_init)
    for h in range(N_HEADS):
        a1, a2 = acc_ref[h], acc_ref[N_HEADS + h]
        oh = (a1[:HEAD_DIM] / a1[HEAD_DIM:HEAD_DIM + 1]
              - lam * (a2[:HEAD_DIM] / a2[HEAD_DIM:HEAD_DIM + 1]))
        ms = jnp.mean(oh * oh, axis=0, keepdims=True)
        g = g_ref[h * HEAD_DIM:(h + 1) * HEAD_DIM, :]
        yt_ref[h * HEAD_DIM:(h + 1) * HEAD_DIM, :] = oh * lax.rsqrt(ms + NORM_EPS) * (g * (1.0 - lam_init))
    o_ref[0] = jnp.transpose(yt_ref[...]).astype(BF16)


def _diff_attn(qt, k, vt, lam_vecs, g_col, lam_init):
    b, s, _ = k.shape
    nk, tq = vt.shape[1], vt.shape[3]
    n_chain = 2 * N_HEADS
    return pl.pallas_call(
        functools.partial(_diff_attn_kernel, tq=tq, lam_init=lam_init),
        grid=(b, nk),
        in_specs=[
            pl.BlockSpec((4, QK_DIM), lambda i, j: (0, 0)),
            pl.BlockSpec((GW, 1), lambda i, j: (0, 0)),
            pl.BlockSpec((1, GW, tq), lambda i, j: (i, 0, j)),
            pl.BlockSpec((1, s, GW), lambda i, j: (i, 0, 0)),
            pl.BlockSpec((1, nk, N_HEADS * V_ROWS, tq), lambda i, j: (i, 0, 0, 0)),
        ],
        out_specs=pl.BlockSpec((1, tq, GW), lambda i, j: (i, j, 0)),
        out_shape=jax.ShapeDtypeStruct((b, s, GW), BF16),
        scratch_shapes=[
            pltpu.VMEM((n_chain, GW, tq), BF16),
            pltpu.VMEM((n_chain, 1, tq), F32),
            pltpu.VMEM((n_chain, V_ROWS, tq), F32),
            pltpu.VMEM((GW, tq), F32),
            pltpu.VMEM((n_chain, tq, tq), F32),
            pltpu.VMEM((n_chain, 1, tq), F32),
        ],
        compiler_params=_params("arbitrary", "arbitrary"),
        name="diff_attn",
    )(lam_vecs, g_col, qt, k, vt)


def _out_proj_kernel(x_ref, yab_ref, o1_ref, o2_ref, o3_ref, l1_ref, l2_ref, l3_ref, yd_ref, w_ref, out_ref):
    l1, l2, l3 = l1_ref[...], l2_ref[...], l3_ref[...]
    m = jnp.maximum(jnp.maximum(l1, l2), l3)
    w1, w2, w3 = jnp.exp(l1 - m), jnp.exp(l2 - m), jnp.exp(l3 - m)
    yc = (w1 * o1_ref[...] + w2 * o2_ref[...] + w3 * o3_ref[...]) / (w1 + w2 + w3)
    acc = jnp.dot(yab_ref[...], w_ref[0:2 * GW, :], preferred_element_type=F32)
    acc = acc + jnp.dot(yc.astype(BF16), w_ref[2 * GW:3 * GW, :], preferred_element_type=F32)
    acc = acc + jnp.dot(yd_ref[...], w_ref[3 * GW:4 * GW, :], preferred_element_type=F32)
    out_ref[...] = x_ref[...] + acc


def _out_proj(x, yab, os_, lses, yd, w, tm):
    t, d = x.shape
    rows = lambda c: pl.BlockSpec((tm, c), lambda i: (i, 0))
    return pl.pallas_call(
        _out_proj_kernel,
        grid=(t // tm,),
        in_specs=[rows(d), rows(2 * GW)] + [rows(GW)] * 6 + [rows(GW), pl.BlockSpec((d, d), lambda i: (0, 0))],
        out_specs=rows(d),
        out_shape=jax.ShapeDtypeStruct((t, d), F32),
        compiler_params=_params("arbitrary"),
        name="out_proj",
    )(x, yab, *os_, *lses, yd, w)


def _ffn_kernel(x_ref, g_ref, wg_ref, wu_ref, wd_ref, out_ref, hn_ref, acc_ref):
    f = pl.program_id(1)

    @pl.when(f == 0)
    def _():
        hn_ref[...] = _rms(x_ref[...], g_ref[...]).astype(BF16)
        acc_ref[...] = jnp.zeros_like(acc_ref)

    hn = hn_ref[...]
    gate = jnp.dot(hn, wg_ref[...], preferred_element_type=F32)
    up = jnp.dot(hn, wu_ref[...], preferred_element_type=F32)
    act = (gate * jax.nn.sigmoid(gate) * up).astype(BF16)
    acc_ref[...] += jnp.dot(act, wd_ref[...], preferred_element_type=F32)

    @pl.when(f == pl.num_programs(1) - 1)
    def _():
        out_ref[...] = x_ref[...] + acc_ref[...]


def _ffn(x, g, wg, wu, wd, tm, tf):
    t, d = x.shape
    ff = wg.shape[1]
    return pl.pallas_call(
        _ffn_kernel,
        grid=(t // tm, ff // tf),
        in_specs=[
            pl.BlockSpec((tm, d), lambda i, f: (i, 0)),
            pl.BlockSpec((1, d), lambda i, f: (0, 0)),
            pl.BlockSpec((d, tf), lambda i, f: (0, f)),
            pl.BlockSpec((d, tf), lambda i, f: (0, f)),
            pl.BlockSpec((tf, d), lambda i, f: (f, 0)),
        ],
        out_specs=pl.BlockSpec((tm, d), lambda i, f: (i, 0)),
        out_shape=jax.ShapeDtypeStruct((t, d), F32),
        scratch_shapes=[pltpu.VMEM((tm, d), BF16), pltpu.VMEM((tm, d), F32)],
        compiler_params=_params("arbitrary", "arbitrary"),
        name="ffn_dense",
    )(x, g, wg, wu, wd)


META_E1, META_E2, META_RANK1, META_RANK2, META_G1, META_G2 = range(6)


def _router_kernel(x_ref, g_ref, r_ref, hn_ref, meta_ref, after_ref, carry_ref, *, blocks_per_chunk):
    @pl.when(pl.program_id(0) % blocks_per_chunk == 0)
    def _():
        carry_ref[...] = jnp.zeros_like(carry_ref)

    h = _rms(x_ref[...], g_ref[...])
    hn_ref[...] = h.astype(BF16)
    logits = jnp.dot(h, r_ref[...], preferred_element_type=F32, precision=lax.Precision.HIGHEST)
    tm = logits.shape[0]
    lane = lax.broadcasted_iota(jnp.int32, logits.shape, 1)
    logits = jnp.where(lane < N_EXPERTS, logits, -jnp.inf)
    m1 = jnp.max(logits, axis=-1, keepdims=True)
    i1 = jnp.min(jnp.where(logits == m1, lane, LANES), axis=-1, keepdims=True)
    rest = jnp.where(lane == i1, -jnp.inf, logits)
    m2 = jnp.max(rest, axis=-1, keepdims=True)
    i2 = jnp.min(jnp.where(rest == m2, lane, LANES), axis=-1, keepdims=True)
    e2 = jnp.exp(m2 - m1)
    g1 = 1.0 / (1.0 + e2)
    g2 = e2 / (1.0 + e2)

    oh1 = lane == i1
    oh2 = lane == i2
    both = jnp.where(oh1, 1.0, jnp.where(oh2, 1.0, 0.0))
    row = lax.broadcasted_iota(jnp.int32, (tm, tm), 0)
    col = lax.broadcasted_iota(jnp.int32, (tm, tm), 1)
    earlier = jnp.where(col < row, 1.0, 0.0).astype(BF16)
    carry = carry_ref[0:1, :]
    prefix = jnp.dot(earlier, both.astype(BF16), preferred_element_type=F32) + carry
    rank1 = jnp.sum(jnp.where(oh1, prefix, 0.0), axis=-1, keepdims=True)
    rank2 = jnp.sum(jnp.where(oh2, prefix, 0.0), axis=-1, keepdims=True)
    total = carry + jnp.sum(both, axis=0, keepdims=True)

    meta = jnp.zeros(logits.shape, F32)
    for k, val in ((META_E1, i1.astype(F32)), (META_E2, i2.astype(F32)), (META_RANK1, rank1),
                   (META_RANK2, rank2), (META_G1, g1), (META_G2, g2)):
        meta = jnp.where(lane == k, val, meta)
    meta_ref[...] = meta
    after_ref[0] = jnp.broadcast_to(total, after_ref.shape[1:])
    carry_ref[...] = jnp.broadcast_to(total, carry_ref.shape)


def _router(x, g, r_pad, tm, blocks_per_chunk):
    t, d = x.shape
    nblk = t // tm
    counts = pl.BlockSpec((1, 8, LANES), lambda i: (i, 0, 0))
    return pl.pallas_call(
        functools.partial(_router_kernel, blocks_per_chunk=blocks_per_chunk),
        grid=(nblk,),
        in_specs=[
            pl.BlockSpec((tm, d), lambda i: (i, 0)),
            pl.BlockSpec((1, d), lambda i: (0, 0)),
            pl.BlockSpec((d, LANES), lambda i: (0, 0)),
        ],
        out_specs=[
            pl.BlockSpec((tm, d), lambda i: (i, 0)),
            pl.BlockSpec((tm, LANES), lambda i: (i, 0)),
            counts,
        ],
        out_shape=[
            jax.ShapeDtypeStruct((t, d), BF16),
            jax.ShapeDtypeStruct((t, LANES), F32),
            jax.ShapeDtypeStruct((nblk, 8, LANES), F32),
        ],
        scratch_shapes=[pltpu.VMEM((8, LANES), F32)],
        compiler_params=_params("arbitrary"),
        name="router",
    )(x, g, r_pad)


SLOT_ROW1, SLOT_ROW2, SLOT_G1, SLOT_G2 = range(4)


def _moe_kernel(nt_ref, toff_ref, cuts_ref, hn_ref, slotc_ref, slotr_ref, wg_ref, wu_ref, wd_ref, out_ref,
                xs_ref, ys_ref, *, tr, sub, win):
    c, e, f = pl.program_id(0), pl.program_id(1), pl.program_id(2)
    last_f = pl.num_programs(2) - 1
    grp = c * N_EXPERTS + e
    n_tiles = nt_ref[grp]
    row0 = toff_ref[grp] * tr
    tc = hn_ref.shape[0]
    nsub = tc // sub

    @pl.when((e == 0) & (f == 0))
    def _():
        out_ref[...] = jnp.zeros_like(out_ref)

    def rows_of(r):
        return pl.ds(pl.multiple_of(r * tr, tr), tr)

    def token_window(r):
        base = grp * nsub
        first = r * tr
        last = jnp.minimum(first + tr, cuts_ref[base + nsub - 1]) - 1
        lo = hi = 0
        for k in range(nsub - 1):
            cut = cuts_ref[base + k]
            lo = lo + (cut <= first).astype(jnp.int32)
            hi = hi + (cut <= last).astype(jnp.int32)
        start = jnp.minimum(lo, nsub - win)
        return hi < start + win, pl.multiple_of(start * sub, sub)

    def spans(r, fn):
        covered, tok0 = token_window(r)

        @pl.when(covered)
        def _():
            fn(r, tok0, win * sub)

        @pl.when(jnp.logical_not(covered))
        def _():
            fn(r, 0, tc)

    def gather_span(r, tok0, ntok):
        toks = pl.ds(tok0, ntok)
        target = (lax.broadcasted_iota(jnp.int32, (tr, ntok), 0) + (row0 + r * tr)).astype(F32)
        hit1 = slotr_ref[SLOT_ROW1:SLOT_ROW1 + 1, toks] == target
        hit2 = slotr_ref[SLOT_ROW2:SLOT_ROW2 + 1, toks] == target
        oh = jnp.where(hit1, 1.0, jnp.where(hit2, 1.0, 0.0)).astype(BF16)
        xs_ref[rows_of(r), :] = jnp.dot(oh, hn_ref[toks, :], preferred_element_type=F32).astype(BF16)

    def gather_tile(r, carry):
        spans(r, gather_span)
        return carry

    def ffn_tile(r, first):
        xt = xs_ref[rows_of(r), :]
        gate = jnp.dot(xt, wg_ref[...], preferred_element_type=F32)
        up = jnp.dot(xt, wu_ref[...], preferred_element_type=F32)
        act = (gate * jax.nn.sigmoid(gate) * up).astype(BF16)
        y = jnp.dot(act, wd_ref[...], preferred_element_type=F32)
        if first:
            ys_ref[rows_of(r), :] = y
        else:
            ys_ref[rows_of(r), :] += y

    def scatter_span(r, tok0, ntok):
        toks = pl.ds(tok0, ntok)
        yb = ys_ref[rows_of(r), :].astype(BF16)
        target = (lax.broadcasted_iota(jnp.int32, (ntok, tr), 1) + (row0 + r * tr)).astype(F32)
        hit1 = slotc_ref[toks, SLOT_ROW1:SLOT_ROW1 + 1] == target
        hit2 = slotc_ref[toks, SLOT_ROW2:SLOT_ROW2 + 1] == target
        gh = jnp.where(hit1, slotc_ref[toks, SLOT_G1:SLOT_G1 + 1],
                       jnp.where(hit2, slotc_ref[toks, SLOT_G2:SLOT_G2 + 1], 0.0)).astype(BF16)
        out_ref[toks, :] += jnp.dot(gh, yb, preferred_element_type=F32)

    def scatter_tile(r, carry):
        spans(r, scatter_span)
        return carry

    def ffn_all(first):
        def pair(t, carry):
            ffn_tile(2 * t, first)
            ffn_tile(2 * t + 1, first)
            return carry

        lax.fori_loop(0, n_tiles // 2, pair, 0)

        @pl.when(n_tiles % 2 == 1)
        def _():
            ffn_tile(n_tiles - 1, first)

    @pl.when(f == 0)
    def _():
        lax.fori_loop(0, n_tiles, gather_tile, 0)
        ffn_all(True)

    @pl.when(f > 0)
    def _():
        ffn_all(False)

    @pl.when(f == last_f)
    def _():
        lax.fori_loop(0, n_tiles, scatter_tile, 0)


def _moe(hn, slot_cols, slot_rows, nt, toff, cuts, wg, wu, wd, tc, tr, tf, sub, win):
    t, d = hn.shape
    ne, _, ff = wg.shape
    grid_spec = pltpu.PrefetchScalarGridSpec(
        num_scalar_prefetch=3,
        grid=(t // tc, ne, ff // tf),
        in_specs=[
            pl.BlockSpec((tc, d), lambda c, e, f, *_: (c, 0)),
            pl.BlockSpec((tc, 4), lambda c, e, f, *_: (c, 0)),
            pl.BlockSpec((8, tc), lambda c, e, f, *_: (0, c)),
            pl.BlockSpec((None, d, tf), lambda c, e, f, *_: (e, 0, f)),
            pl.BlockSpec((None, d, tf), lambda c, e, f, *_: (e, 0, f)),
            pl.BlockSpec((None, tf, d), lambda c, e, f, *_: (e, f, 0)),
        ],
        out_specs=pl.BlockSpec((tc, d), lambda c, e, f, *_: (c, 0)),
        scratch_shapes=[
            pltpu.VMEM((tc + tr, d), BF16),
            pltpu.VMEM((tc + tr, d), F32),
        ],
    )
    return pl.pallas_call(
        functools.partial(_moe_kernel, tr=tr, sub=sub, win=win),
        grid_spec=grid_spec,
        out_shape=jax.ShapeDtypeStruct((t, d), F32),
        compiler_params=_params("arbitrary", "arbitrary", "arbitrary"),
        name="moe",
    )(nt, toff, cuts, hn, slot_cols, slot_rows, wg, wu, wd)


def _final_kernel(x_ref, y_ref, g_ref, out_ref):
    out_ref[...] = _rms(x_ref[...] + y_ref[...], g_ref[...])


def _final_norm(x, y, g, tm):
    t, d = x.shape
    rows = pl.BlockSpec((tm, d), lambda i: (i, 0))
    return pl.pallas_call(
        _final_kernel,
        grid=(t // tm,),
        in_specs=[rows, rows, pl.BlockSpec((1, d), lambda i: (0, 0))],
        out_specs=rows,
        out_shape=jax.ShapeDtypeStruct((t, d), F32),
        compiler_params=_params("arbitrary"),
        name="final_norm",
    )(x, y, g)


def _routed_swiglu(x, norm_g, router_w, wg, wu, wd, final_g, tc, tr, tf, sub, win):
    t, d = x.shape
    nc, nsub = t // tc, tc // sub
    r_pad = jnp.zeros((d, LANES), F32).at[:, :N_EXPERTS].set(router_w)
    hn, meta, after = _router(x, norm_g, r_pad, tm=sub, blocks_per_chunk=nsub)
    through = after[:, 0, :N_EXPERTS].astype(jnp.int32).reshape(nc, nsub, N_EXPERTS)
    cuts = through.transpose(0, 2, 1)
    total = through[:, -1]
    nt = (total + tr - 1) // tr
    toff = jnp.cumsum(nt, axis=1) - nt
    first_row = jnp.repeat((toff * tr).astype(F32), tc, axis=0)
    experts = jnp.arange(N_EXPERTS, dtype=F32)
    pick = lambda e: jnp.sum(jnp.where(e[:, None] == experts, first_row, 0.0), axis=1)
    row1 = pick(meta[:, META_E1]) + meta[:, META_RANK1]
    row2 = pick(meta[:, META_E2]) + meta[:, META_RANK2]
    slot_cols = jnp.stack([row1, row2, meta[:, META_G1], meta[:, META_G2]], axis=1)
    slot_rows = jnp.zeros((8, t), F32).at[0:4].set(slot_cols.T)
    y = _moe(hn, slot_cols, slot_rows, nt.reshape(-1), toff.reshape(-1), cuts.reshape(-1), wg, wu, wd,
             tc=tc, tr=tr, tf=tf, sub=sub, win=win)
    return _final_norm(x, y, final_g, tm=min(1024, t))


def _permute_w_in(w):
    def perm(block):
        d = block.shape[0]
        return block.reshape(d, N_HEADS, 2, QK_DIM).transpose(0, 2, 1, 3).reshape(d, GW)
    parts = [w[:, :6 * GW], perm(w[:, 6 * GW:7 * GW]), perm(w[:, 7 * GW:8 * GW]), w[:, 8 * GW:]]
    return jnp.concatenate(parts, axis=1)


def _block_diag(pool_w):
    n, p, _ = pool_w.shape
    out = jnp.zeros((n * p, n * p), pool_w.dtype)
    for gi in range(n):
        out = out.at[gi * p:(gi + 1) * p, gi * p:(gi + 1) * p].set(pool_w[gi])
    return out


def _token_mixing(x, layer, lam_init, p, rope_c, rope_d):
    b, s, d = x.shape
    row = lambda v: v.reshape(1, -1)
    w_in = _permute_w_in(p["w_in"][layer]).astype(BF16)
    uab, qkvc, qdt, kd, vdt = _norm_proj(x, row(p["norm1_g"][layer]), w_in, rope_c, rope_d, tm=512, tk=256)
    dww = jnp.zeros((HALO, GW), F32).at[:CONV_WIDTH].set(p["conv_dw_w"][layer])
    yab = _local_mixers(
        uab, dww, row(p["conv_dw_b"][layer]), row(p["conv_ln_g"][layer]), row(p["conv_ln_b"][layer]),
        p["conv_pw_w"][layer].astype(BF16), _block_diag(p["pool_w"][layer]).astype(BF16),
        row(p["pool_scale"][layer]), ts=512)
    os_, lses = [], []
    for dil in DILATIONS:
        o, lse = _band_attn(qkvc, dil)
        os_.append(o.reshape(b * s, GW))
        lses.append(lse.reshape(b * s, GW))
    yd = _diff_attn(qdt, kd, vdt, p["diff_lam"][layer], jnp.tile(p["diff_ln_g"][layer], N_HEADS).reshape(GW, 1),
                    lam_init)
    return _out_proj(x.reshape(b * s, d), yab.reshape(b * s, 2 * GW), os_, lses, yd.reshape(b * s, GW),
                     p["w_out"][layer].astype(BF16), tm=512)


def kernel(x, norm1_g, w_in, conv_dw_w, conv_dw_b, conv_ln_g, conv_ln_b, conv_pw_w, pool_w, pool_scale,
           diff_lam, diff_ln_g, w_out, norm2_g, ffn_w_gate, ffn_w_up, ffn_w_down, moe_router, moe_w_gate,
           moe_w_up, moe_w_down, final_g):
    b, s, d = x.shape
    p = dict(norm1_g=norm1_g, w_in=w_in, conv_dw_w=conv_dw_w, conv_dw_b=conv_dw_b, conv_ln_g=conv_ln_g,
             conv_ln_b=conv_ln_b, conv_pw_w=conv_pw_w, pool_w=pool_w, pool_scale=pool_scale,
             diff_lam=diff_lam, diff_ln_g=diff_ln_g, w_out=w_out)
    rope_c = _rope_table(s, HEAD_DIM, HEAD_DIM // 4)
    rope_d = _rope_table(s, QK_DIM, QK_DIM // 4)
    row = lambda v: v.reshape(1, -1)

    x1 = _token_mixing(x, 0, 0.8 - 0.6 * math.exp(-0.3 * 0), p, rope_c, rope_d)
    x2 = _ffn(x1, row(norm2_g[0]), ffn_w_gate[0].astype(BF16), ffn_w_up[0].astype(BF16),
              ffn_w_down[0].astype(BF16), tm=512, tf=1408)

    x3 = _token_mixing(x2.reshape(b, s, d), 1, 0.8 - 0.6 * math.exp(-0.3 * 1), p, rope_c, rope_d)
    out = _routed_swiglu(x3, row(norm2_g[1]), moe_router[0], moe_w_gate[0].astype(BF16),
                         moe_w_up[0].astype(BF16), moe_w_down[0].astype(BF16), row(final_g),
                         tc=2048, tr=256, tf=512, sub=256, win=5)
    return out.reshape(b, s, d)
```

```python
import functools
import math

import jax
import jax.numpy as jnp
from jax import lax
from jax.experimental import pallas as pl
from jax.experimental.pallas import tpu as pltpu

D_MODEL = 1024
GW = 256
N_HEADS = 4
HEAD_DIM = 64
QK_DIM = 32
V_ROWS = HEAD_DIM + 16
CONV_WIDTH = 31
POOL_WINDOWS = (2, 4, 8, 16)
DILATIONS = (1, 4, 16)
BAND = 128
ROPE_THETA = 500000.0
N_EXPERTS = 8
NORM_EPS = 1e-5
HALO = 32
LANES = 128
SUBLANES = 8
VMEM_LIMIT = 56 * 1024 * 1024

BF16 = jnp.bfloat16
F32 = jnp.float32
NT_DIMS = (((1,), (1,)), ((), ()))


def _params(*sem):
    return pltpu.CompilerParams(dimension_semantics=sem, vmem_limit_bytes=VMEM_LIMIT)


def _lanes_in(shape, start, width):
    lane = lax.broadcasted_iota(jnp.int32, shape, len(shape) - 1)
    return (lane >= start) & (lane < start + width)


def _rms(x, g):
    ms = jnp.mean(x * x, axis=-1, keepdims=True)
    return x * lax.rsqrt(ms + NORM_EPS) * g


def _rope_table(seq, width, rot):
    half = rot // 2
    pos = jnp.arange(seq, dtype=F32)
    inv = ROPE_THETA ** (-jnp.arange(0, rot, 2, dtype=F32) / rot)
    ang = pos[:, None] * inv[None, :]
    cos, sin = jnp.cos(ang), jnp.sin(ang)
    lane = jnp.arange(GW) % width
    f = lane % half
    first = lane < half
    second = (lane >= half) & (lane < rot)
    c = jnp.where((lane < rot)[None, :], cos[:, f], 1.0)
    sn = jnp.where(first[None, :], -sin[:, f], 0.0)
    sp = jnp.where(second[None, :], sin[:, f], 0.0)
    return jnp.concatenate([c, sn, sp], axis=1)


def _rope(u, tab_ref, half):
    return (u * tab_ref[:, 0:GW]
            + pltpu.roll(u, GW - half, 1) * tab_ref[:, GW:2 * GW]
            + pltpu.roll(u, half, 1) * tab_ref[:, 2 * GW:3 * GW])


def _norm_proj_kernel(x_ref, g_ref, w_ref, rc_ref, rd_ref, uab_ref, qkvc_ref, qt_ref, kd_ref, vt_ref, v_ref,
                      *, tk):
    h = _rms(x_ref[0], g_ref[...]).astype(BF16)

    def proj(grp):
        return jnp.dot(h, w_ref[:, grp * GW:(grp + 1) * GW], preferred_element_type=F32)

    for grp in range(3):
        uab_ref[0, :, grp * GW:(grp + 1) * GW] = proj(grp)
    c_scale = HEAD_DIM ** -0.5
    d_scale = QK_DIM ** -0.5 * math.log2(math.e)
    qkvc_ref[0, :, 0:GW] = (_rope(proj(3), rc_ref, 8) * c_scale).astype(BF16)
    qkvc_ref[0, :, GW:2 * GW] = _rope(proj(4), rc_ref, 8).astype(BF16)
    qkvc_ref[0, :, 2 * GW:3 * GW] = proj(5).astype(BF16)
    qt_ref[0] = jnp.transpose(_rope(proj(6), rd_ref, 4) * d_scale).astype(BF16)
    kd_ref[0] = _rope(proj(7), rd_ref, 4).astype(BF16)
    v_ref[...] = proj(8)
    vt = jnp.transpose(v_ref[...]).astype(BF16)
    ones = jnp.ones((V_ROWS - HEAD_DIM, tk), BF16)
    for blk in range(vt.shape[1] // tk):
        for hd in range(N_HEADS):
            vt_ref[0, blk, hd * V_ROWS:hd * V_ROWS + HEAD_DIM, :] = (
                vt[hd * HEAD_DIM:(hd + 1) * HEAD_DIM, blk * tk:(blk + 1) * tk])
            vt_ref[0, blk, hd * V_ROWS + HEAD_DIM:(hd + 1) * V_ROWS, :] = ones


def _norm_proj(x, g, w, rope_c, rope_d, tm, tk):
    b, s, d = x.shape
    grid = (s // tm, b)
    return pl.pallas_call(
        functools.partial(_norm_proj_kernel, tk=tk),
        grid=grid,
        in_specs=[
            pl.BlockSpec((1, tm, d), lambda i, j: (j, i, 0)),
            pl.BlockSpec((1, d), lambda i, j: (0, 0)),
            pl.BlockSpec((d, 9 * GW), lambda i, j: (0, 0)),
            pl.BlockSpec((tm, 3 * GW), lambda i, j: (i, 0)),
            pl.BlockSpec((tm, 3 * GW), lambda i, j: (i, 0)),
        ],
        out_specs=[
            pl.BlockSpec((1, tm, 3 * GW), lambda i, j: (j, i, 0)),
            pl.BlockSpec((1, tm, 3 * GW), lambda i, j: (j, i, 0)),
            pl.BlockSpec((1, GW, tm), lambda i, j: (j, 0, i)),
            pl.BlockSpec((1, tm, GW), lambda i, j: (j, i, 0)),
            pl.BlockSpec((1, tm // tk, N_HEADS * V_ROWS, tk), lambda i, j: (j, i, 0, 0)),
        ],
        out_shape=[
            jax.ShapeDtypeStruct((b, s, 3 * GW), F32),
            jax.ShapeDtypeStruct((b, s, 3 * GW), BF16),
            jax.ShapeDtypeStruct((b, GW, s), BF16),
            jax.ShapeDtypeStruct((b, s, GW), BF16),
            jax.ShapeDtypeStruct((b, s // tk, N_HEADS * V_ROWS, tk), BF16),
        ],
        scratch_shapes=[pltpu.VMEM((tm, GW), F32)],
        compiler_params=_params("arbitrary", "arbitrary"),
        name="norm_proj",
    )(x, g, w, rope_c, rope_d)


def _local_kernel(u_ref, dww_ref, dwb_ref, lng_ref, lnb_ref, pww_ref, plw_ref, pls_ref,
                  y_ref, zbuf, pbuf, shift_ref, *, ts):
    si = pl.program_id(1)

    @pl.when(si == 0)
    def _():
        zbuf[0:HALO, :] = jnp.zeros((HALO, GW), F32)
        pbuf[0:HALO, :] = jnp.zeros((HALO, GW), F32)

    a = u_ref[0, :, 0:GW]
    b = u_ref[0, :, GW:2 * GW]
    zbuf[HALO:HALO + ts, :] = a * jax.nn.sigmoid(b)
    pbuf[HALO:HALO + ts, :] = u_ref[0, :, 2 * GW:3 * GW]

    def by_phase(buf, shifted, offsets, term):
        total = None
        for phase in range(SUBLANES):
            offs = [o for o in offsets if o % SUBLANES == phase]
            if not offs:
                continue
            src = buf
            if phase:
                n = max(offs) - phase + ts
                shifted[phase, 0:n, :] = buf[phase:phase + n, :]
                src = shifted.at[phase]
            for o in offs:
                t = term(o, src[o - phase:o - phase + ts, :])
                total = t if total is None else total + t
        return total

    base = HALO - (CONV_WIDTH - 1)
    acc = by_phase(zbuf, shift_ref.at[0], range(base, base + CONV_WIDTH),
                   lambda o, rows: dww_ref[o - base:o - base + 1, :] * rows)
    z = acc + dwb_ref[...]
    mu = jnp.mean(z, axis=-1, keepdims=True)
    zc = z - mu
    var = jnp.mean(zc * zc, axis=-1, keepdims=True)
    zn = zc * lax.rsqrt(var + NORM_EPS) * lng_ref[...] + lnb_ref[...]
    zs = zn * jax.nn.sigmoid(zn)
    y_ref[0, :, 0:GW] = jnp.dot(zs.astype(BF16), pww_ref[...], preferred_element_type=F32).astype(BF16)

    lane = lax.broadcasted_iota(jnp.int32, (ts, GW), 1)
    pool_group = GW // len(POOL_WINDOWS)
    win = jnp.zeros((ts, GW), jnp.int32)
    for gi, w in enumerate(POOL_WINDOWS):
        win = jnp.where(lane >= gi * pool_group, w, win)
    p0 = pbuf[HALO:HALO + ts, :]
    lag = lambda o: HALO - o
    tot = by_phase(pbuf, shift_ref.at[1], range(HALO - max(POOL_WINDOWS) + 1, HALO + 1),
                   lambda o, rows: rows if lag(o) < min(POOL_WINDOWS) else jnp.where(win > lag(o), rows, 0.0))
    pos = si * ts + lax.broadcasted_iota(jnp.int32, (ts, GW), 0)
    cnt = jnp.minimum(pos + 1, win).astype(F32)
    pooled = tot / cnt - p0
    yb = jnp.dot(pooled.astype(BF16), plw_ref[...], preferred_element_type=F32) * pls_ref[...]
    y_ref[0, :, GW:2 * GW] = yb.astype(BF16)

    zbuf[0:HALO, :] = zbuf[ts:ts + HALO, :]
    pbuf[0:HALO, :] = pbuf[ts:ts + HALO, :]


def _local_mixers(uab, dww, dwb, lng, lnb, pww, plw, pls, ts):
    b, s, _ = uab.shape
    vec = lambda: pl.BlockSpec((1, GW), lambda i, j: (0, 0))
    mat = lambda: pl.BlockSpec((GW, GW), lambda i, j: (0, 0))
    return pl.pallas_call(
        functools.partial(_local_kernel, ts=ts),
        grid=(b, s // ts),
        in_specs=[
            pl.BlockSpec((1, ts, 3 * GW), lambda i, j: (i, j, 0)),
            pl.BlockSpec((HALO, GW), lambda i, j: (0, 0)),
            vec(), vec(), vec(), mat(), mat(), vec(),
        ],
        out_specs=pl.BlockSpec((1, ts, 2 * GW), lambda i, j: (i, j, 0)),
        out_shape=jax.ShapeDtypeStruct((b, s, 2 * GW), BF16),
        scratch_shapes=[
            pltpu.VMEM((HALO + ts, GW), F32),
            pltpu.VMEM((HALO + ts, GW), F32),
            pltpu.VMEM((2, SUBLANES, HALO + ts, GW), F32),
        ],
        compiler_params=_params("arbitrary", "arbitrary"),
        name="local_mixers",
    )(uab, dww, dwb, lng, lnb, pww, plw, pls)


def _band_attn_kernel(q_ref, k_ref, v_ref, o_ref, lse_ref, s_ref):
    row = lax.broadcasted_iota(jnp.int32, (BAND, 2 * BAND), 0)
    col = lax.broadcasted_iota(jnp.int32, (BAND, 2 * BAND), 1)
    halves = [_lanes_in((BAND, LANES), sub * HEAD_DIM, HEAD_DIM) for sub in range(LANES // HEAD_DIM)]

    def block(n, carry):
        q0 = pl.multiple_of(n * BAND, BAND)
        k0 = pl.multiple_of(jnp.maximum(n - 1, 0) * BAND, BAND)
        q = q_ref[0, pl.ds(q0, BAND), :].astype(F32)
        kw = k_ref[0, pl.ds(k0, 2 * BAND), :]
        dist = (row - col) + (q0 - k0)
        mask = (dist >= 0) & (dist <= BAND)
        for h in range(N_HEADS):
            qh = jnp.where(_lanes_in((BAND, GW), h * HEAD_DIM, HEAD_DIM), q, 0.0).astype(BF16)
            s = lax.dot_general(qh, kw, NT_DIMS, preferred_element_type=F32)
            s_ref[h] = jnp.where(mask, s, -jnp.inf)
        for tile in range(GW // LANES):
            vw = v_ref[0, pl.ds(k0, 2 * BAND), tile * LANES:(tile + 1) * LANES]
            o = jnp.zeros((BAND, LANES), F32)
            lse = jnp.zeros((BAND, LANES), F32)
            for sub in range(LANES // HEAD_DIM):
                s = s_ref[tile * (LANES // HEAD_DIM) + sub]
                m = jnp.max(s, axis=-1, keepdims=True)
                p = jnp.exp(s - m)
                l = jnp.sum(p, axis=-1, keepdims=True)
                acc = jnp.dot(p.astype(BF16), vw, preferred_element_type=F32)
                mine = halves[sub]
                o = jnp.where(mine, acc / l, o)
                lse = jnp.where(mine, m + jnp.log(l), lse)
            o_ref[0, pl.ds(q0, BAND), tile * LANES:(tile + 1) * LANES] = o
            lse_ref[0, pl.ds(q0, BAND), tile * LANES:(tile + 1) * LANES] = lse
        return carry

    lax.fori_loop(0, q_ref.shape[1] // BAND, block, 0, unroll=2)


def _band_attn(qkv, dil):
    b, s, _ = qkv.shape
    l = s // dil
    assert l >= 2 * BAND and l % BAND == 0
    part = lambda c: pl.BlockSpec((1, l, GW), lambda i, r: (i, 0, 3 * r + c))
    out = pl.BlockSpec((1, l, GW), lambda i, r: (i, 0, r))
    o, lse = pl.pallas_call(
        _band_attn_kernel,
        grid=(b, dil),
        in_specs=[part(0), part(1), part(2)],
        out_specs=[out, out],
        out_shape=[jax.ShapeDtypeStruct((b, l, dil * GW), F32)] * 2,
        scratch_shapes=[pltpu.VMEM((N_HEADS, BAND, 2 * BAND), F32)],
        compiler_params=_params("arbitrary", "arbitrary"),
        name="band_attn",
    )(*[qkv.reshape(b, l, dil * 3 * GW)] * 3)
    return o.reshape(b, s, GW), lse.reshape(b, s, GW)


def _diff_attn_kernel(lam_ref, g_ref, qt_ref, k_ref, vt_ref, o_ref, qm_ref, m_ref, acc_ref, yt_ref,
                      st_ref, mx_ref, *, tq, lam_init):
    n_chain = 2 * N_HEADS
    i = pl.program_id(1)
    @pl.when((pl.program_id(0) == 0) & (i == 0))
    def _():
        qm_ref[...] = jnp.zeros(qm_ref.shape, BF16)

    for c in range(n_chain):
        qm_ref[c, c * QK_DIM:(c + 1) * QK_DIM, :] = qt_ref[0, c * QK_DIM:(c + 1) * QK_DIM, :]
    m_ref[...] = jnp.full(m_ref.shape, -jnp.inf, F32)
    acc_ref[...] = jnp.zeros(acc_ref.shape, F32)
    tk = st_ref.shape[1]
    ratio = tq // tk
    ahead = (lax.broadcasted_iota(jnp.int32, (tk, tq), 0)
             - lax.broadcasted_iota(jnp.int32, (tk, tq), 1))

    def score(c, kb, diag):
        st = jnp.dot(kb, qm_ref[c], preferred_element_type=F32)
        if diag is not None:
            st = jnp.where(ahead <= -diag * tk, st, -jnp.inf)
        st_ref[c] = st
        mx_ref[c] = jnp.max(st, axis=0, keepdims=True)

    def value(c, j):
        h = c % N_HEADS
        m_prev = m_ref[c]
        m_next = jnp.maximum(m_prev, mx_ref[c])
        alpha = jnp.exp2(m_prev - m_next)
        p = jnp.exp2(st_ref[c] - m_next).astype(BF16)
        vt = vt_ref[0, j, h * V_ROWS:(h + 1) * V_ROWS, :]
        acc_ref[c] = alpha * acc_ref[c] + jnp.dot(vt, p, preferred_element_type=F32)
        m_ref[c] = m_next

    def k_block(j):
        return k_ref[0, pl.ds(pl.multiple_of(j * tk, tk), tk), :]

    def scores(j, diag):
        kb = k_block(j)
        for c in range(n_chain):
            score(c, kb, diag)

    def values(j):
        for c in range(n_chain):
            value(c, j)

    def values_then_scores(j, diag):
        kb = k_block(j + 1)
        for c in range(n_chain):
            value(c, j)
            score(c, kb, diag)

    before = ratio * i

    @pl.when(i == 0)
    def _():
        scores(0, 0)

    @pl.when(i > 0)
    def _():
        scores(0, None)

    def body(j, carry):
        values_then_scores(j, None)
        return carry

    lax.fori_loop(0, before - 1, body, 0)

    @pl.when(i > 0)
    def _():
        values_then_scores(before - 1, 0)

    for diag in range(1, ratio):
        values_then_scores(before + diag - 1, diag)
    values(before + ratio - 1)

    lv = lam_ref[...]
    lam = (jnp.exp(jnp.sum(lv[0:1] * lv[1:2], axis=-1, keepdims=True))
           - jnp.exp(jnp.sum(lv[2:3] * lv[3:4], axis=-1, keepdims=True)) + lam_init)
    for h in range(N_HEADS):
        a1, a2 = acc_ref[h], acc_ref[N_HEADS + h]
        oh = (a1[:HEAD_DIM] / a1[HEAD_DIM:HEAD_DIM + 1]
              - lam * (a2[:HEAD_DIM] / a2[HEAD_DIM:HEAD_DIM + 1]))
        ms = jnp.mean(oh * oh, axis=0, keepdims=True)
        g = g_ref[h * HEAD_DIM:(h + 1) * HEAD_DIM, :]
        yt_ref[h * HEAD_DIM:(h + 1) * HEAD_DIM, :] = oh * lax.rsqrt(ms + NORM_EPS) * (g * (1.0 - lam_init))
    o_ref[0] = jnp.transpose(yt_ref[...]).astype(BF16)


def _diff_attn(qt, k, vt, lam_vecs, g_col, lam_init, tq):
    b, s, _ = k.shape
    nk, tk = vt.shape[1], vt.shape[3]
    assert tq % tk == 0 and s % tq == 0
    n_chain = 2 * N_HEADS
    return pl.pallas_call(
        functools.partial(_diff_attn_kernel, tq=tq, lam_init=lam_init),
        grid=(b, s // tq),
        in_specs=[
            pl.BlockSpec((4, QK_DIM), lambda i, j: (0, 0)),
            pl.BlockSpec((GW, 1), lambda i, j: (0, 0)),
            pl.BlockSpec((1, GW, tq), lambda i, j: (i, 0, j)),
            pl.BlockSpec((1, s, GW), lambda i, j: (i, 0, 0)),
            pl.BlockSpec((1, nk, N_HEADS * V_ROWS, tk), lambda i, j: (i, 0, 0, 0)),
        ],
        out_specs=pl.BlockSpec((1, tq, GW), lambda i, j: (i, j, 0)),
        out_shape=jax.ShapeDtypeStruct((b, s, GW), BF16),
        scratch_shapes=[
            pltpu.VMEM((n_chain, GW, tq), BF16),
            pltpu.VMEM((n_chain, 1, tq), F32),
            pltpu.VMEM((n_chain, V_ROWS, tq), F32),
            pltpu.VMEM((GW, tq), F32),
            pltpu.VMEM((n_chain, tk, tq), F32),
            pltpu.VMEM((n_chain, 1, tq), F32),
        ],
        compiler_params=_params("arbitrary", "arbitrary"),
        name="diff_attn",
    )(lam_vecs, g_col, qt, k, vt)


def _out_proj_kernel(x_ref, yab_ref, o1_ref, o2_ref, o3_ref, l1_ref, l2_ref, l3_ref, yd_ref, w_ref, out_ref):
    l1, l2, l3 = l1_ref[...], l2_ref[...], l3_ref[...]
    m = jnp.maximum(jnp.maximum(l1, l2), l3)
    w1, w2, w3 = jnp.exp(l1 - m), jnp.exp(l2 - m), jnp.exp(l3 - m)
    yc = (w1 * o1_ref[...] + w2 * o2_ref[...] + w3 * o3_ref[...]) / (w1 + w2 + w3)
    acc = jnp.dot(yab_ref[...], w_ref[0:2 * GW, :], preferred_element_type=F32)
    acc = acc + jnp.dot(yc.astype(BF16), w_ref[2 * GW:3 * GW, :], preferred_element_type=F32)
    acc = acc + jnp.dot(yd_ref[...], w_ref[3 * GW:4 * GW, :], preferred_element_type=F32)
    out_ref[...] = x_ref[...] + acc


def _out_proj(x, yab, os_, lses, yd, w, tm):
    t, d = x.shape
    rows = lambda c: pl.BlockSpec((tm, c), lambda i: (i, 0))
    return pl.pallas_call(
        _out_proj_kernel,
        grid=(t // tm,),
        in_specs=[rows(d), rows(2 * GW)] + [rows(GW)] * 6 + [rows(GW), pl.BlockSpec((d, d), lambda i: (0, 0))],
        out_specs=rows(d),
        out_shape=jax.ShapeDtypeStruct((t, d), F32),
        compiler_params=_params("arbitrary"),
        name="out_proj",
    )(x, yab, *os_, *lses, yd, w)


def _ffn_kernel(x_ref, g_ref, wg_ref, wu_ref, wd_ref, out_ref, hn_ref, acc_ref):
    f = pl.program_id(1)

    @pl.when(f == 0)
    def _():
        hn_ref[...] = _rms(x_ref[...], g_ref[...]).astype(BF16)
        acc_ref[...] = jnp.zeros_like(acc_ref)

    hn = hn_ref[...]
    gate = jnp.dot(hn, wg_ref[...], preferred_element_type=F32)
    up = jnp.dot(hn, wu_ref[...], preferred_element_type=F32)
    act = (gate * jax.nn.sigmoid(gate) * up).astype(BF16)
    acc_ref[...] += jnp.dot(act, wd_ref[...], preferred_element_type=F32)

    @pl.when(f == pl.num_programs(1) - 1)
    def _():
        out_ref[...] = x_ref[...] + acc_ref[...]


def _ffn(x, g, wg, wu, wd, tm, tf):
    t, d = x.shape
    ff = wg.shape[1]
    return pl.pallas_call(
        _ffn_kernel,
        grid=(t // tm, ff // tf),
        in_specs=[
            pl.BlockSpec((tm, d), lambda i, f: (i, 0)),
            pl.BlockSpec((1, d), lambda i, f: (0, 0)),
            pl.BlockSpec((d, tf), lambda i, f: (0, f)),
            pl.BlockSpec((d, tf), lambda i, f: (0, f)),
            pl.BlockSpec((tf, d), lambda i, f: (f, 0)),
        ],
        out_specs=pl.BlockSpec((tm, d), lambda i, f: (i, 0)),
        out_shape=jax.ShapeDtypeStruct((t, d), F32),
        scratch_shapes=[pltpu.VMEM((tm, d), BF16), pltpu.VMEM((tm, d), F32)],
        compiler_params=_params("arbitrary", "arbitrary"),
        name="ffn_dense",
    )(x, g, wg, wu, wd)


META_E1, META_E2, META_RANK1, META_RANK2, META_G1, META_G2 = range(6)


def _router_kernel(x_ref, g_ref, r_ref, hn_ref, meta_ref, after_ref, carry_ref, *, blocks_per_chunk):
    @pl.when(pl.program_id(0) % blocks_per_chunk == 0)
    def _():
        carry_ref[...] = jnp.zeros_like(carry_ref)

    h = _rms(x_ref[...], g_ref[...])
    hn_ref[...] = h.astype(BF16)
    logits = jnp.dot(h, r_ref[...], preferred_element_type=F32, precision=lax.Precision.HIGHEST)
    tm = logits.shape[0]
    lane = lax.broadcasted_iota(jnp.int32, logits.shape, 1)
    logits = jnp.where(lane < N_EXPERTS, logits, -jnp.inf)
    m1 = jnp.max(logits, axis=-1, keepdims=True)
    i1 = jnp.min(jnp.where(logits == m1, lane, LANES), axis=-1, keepdims=True)
    rest = jnp.where(lane == i1, -jnp.inf, logits)
    m2 = jnp.max(rest, axis=-1, keepdims=True)
    i2 = jnp.min(jnp.where(rest == m2, lane, LANES), axis=-1, keepdims=True)
    e2 = jnp.exp(m2 - m1)
    g1 = 1.0 / (1.0 + e2)
    g2 = e2 / (1.0 + e2)

    oh1 = lane == i1
    oh2 = lane == i2
    both = jnp.where(oh1, 1.0, jnp.where(oh2, 1.0, 0.0))
    row = lax.broadcasted_iota(jnp.int32, (tm, tm), 0)
    col = lax.broadcasted_iota(jnp.int32, (tm, tm), 1)
    earlier = jnp.where(col < row, 1.0, 0.0).astype(BF16)
    carry = carry_ref[0:1, :]
    prefix = jnp.dot(earlier, both.astype(BF16), preferred_element_type=F32) + carry
    rank1 = jnp.sum(jnp.where(oh1, prefix, 0.0), axis=-1, keepdims=True)
    rank2 = jnp.sum(jnp.where(oh2, prefix, 0.0), axis=-1, keepdims=True)
    total = carry + jnp.sum(both, axis=0, keepdims=True)

    meta = jnp.zeros(logits.shape, F32)
    for k, val in ((META_E1, i1.astype(F32)), (META_E2, i2.astype(F32)), (META_RANK1, rank1),
                   (META_RANK2, rank2), (META_G1, g1), (META_G2, g2)):
        meta = jnp.where(lane == k, val, meta)
    meta_ref[...] = meta
    after_ref[0] = jnp.broadcast_to(total, after_ref.shape[1:])
    carry_ref[...] = jnp.broadcast_to(total, carry_ref.shape)


def _router(x, g, r_pad, tm, blocks_per_chunk):
    t, d = x.shape
    nblk = t // tm
    counts = pl.BlockSpec((1, 8, LANES), lambda i: (i, 0, 0))
    return pl.pallas_call(
        functools.partial(_router_kernel, blocks_per_chunk=blocks_per_chunk),
        grid=(nblk,),
        in_specs=[
            pl.BlockSpec((tm, d), lambda i: (i, 0)),
            pl.BlockSpec((1, d), lambda i: (0, 0)),
            pl.BlockSpec((d, LANES), lambda i: (0, 0)),
        ],
        out_specs=[
            pl.BlockSpec((tm, d), lambda i: (i, 0)),
            pl.BlockSpec((tm, LANES), lambda i: (i, 0)),
            counts,
        ],
        out_shape=[
            jax.ShapeDtypeStruct((t, d), BF16),
            jax.ShapeDtypeStruct((t, LANES), F32),
            jax.ShapeDtypeStruct((nblk, 8, LANES), F32),
        ],
        scratch_shapes=[pltpu.VMEM((8, LANES), F32)],
        compiler_params=_params("arbitrary"),
        name="router",
    )(x, g, r_pad)


SLOT_ROW1, SLOT_ROW2, SLOT_G1, SLOT_G2 = range(4)


def _moe_kernel(nt_ref, toff_ref, cuts_ref, hn_ref, slotc_ref, slotr_ref, wg_ref, wu_ref, wd_ref, out_ref,
                xs_ref, ys_ref, *, tr, sub, win):
    c, e, f = pl.program_id(0), pl.program_id(1), pl.program_id(2)
    last_f = pl.num_programs(2) - 1
    grp = c * N_EXPERTS + e
    n_tiles = nt_ref[grp]
    row0 = toff_ref[grp] * tr
    tc = hn_ref.shape[0]
    nsub = tc // sub

    @pl.when((e == 0) & (f == 0))
    def _():
        out_ref[...] = jnp.zeros_like(out_ref)

    def rows_of(r):
        return pl.ds(pl.multiple_of(r * tr, tr), tr)

    def token_window(r):
        base = grp * nsub
        first = r * tr
        last = jnp.minimum(first + tr, cuts_ref[base + nsub - 1]) - 1
        lo = hi = 0
        for k in range(nsub - 1):
            cut = cuts_ref[base + k]
            lo = lo + (cut <= first).astype(jnp.int32)
            hi = hi + (cut <= last).astype(jnp.int32)
        start = jnp.minimum(lo, nsub - win)
        return hi < start + win, pl.multiple_of(start * sub, sub)

    def spans(r, fn):
        covered, tok0 = token_window(r)

        @pl.when(covered)
        def _():
            fn(r, tok0, win * sub)

        @pl.when(jnp.logical_not(covered))
        def _():
            fn(r, 0, tc)

    def gather_span(r, tok0, ntok):
        toks = pl.ds(tok0, ntok)
        target = (lax.broadcasted_iota(jnp.int32, (tr, ntok), 0) + (row0 + r * tr)).astype(F32)
        hit1 = slotr_ref[SLOT_ROW1:SLOT_ROW1 + 1, toks] == target
        hit2 = slotr_ref[SLOT_ROW2:SLOT_ROW2 + 1, toks] == target
        oh = jnp.where(hit1, 1.0, jnp.where(hit2, 1.0, 0.0)).astype(BF16)
        xs_ref[rows_of(r), :] = jnp.dot(oh, hn_ref[toks, :], preferred_element_type=F32).astype(BF16)

    def gather_tile(r, carry):
        spans(r, gather_span)
        return carry

    def ffn_tile(r, first):
        xt = xs_ref[rows_of(r), :]
        gate = jnp.dot(xt, wg_ref[...], preferred_element_type=F32)
        up = jnp.dot(xt, wu_ref[...], preferred_element_type=F32)
        act = (gate * jax.nn.sigmoid(gate) * up).astype(BF16)
        y = jnp.dot(act, wd_ref[...], preferred_element_type=F32)
        if first:
            ys_ref[rows_of(r), :] = y
        else:
            ys_ref[rows_of(r), :] += y

    def scatter_span(r, tok0, ntok):
        toks = pl.ds(tok0, ntok)
        yb = ys_ref[rows_of(r), :].astype(BF16)
        target = (lax.broadcasted_iota(jnp.int32, (ntok, tr), 1) + (row0 + r * tr)).astype(F32)
        hit1 = slotc_ref[toks, SLOT_ROW1:SLOT_ROW1 + 1] == target
        hit2 = slotc_ref[toks, SLOT_ROW2:SLOT_ROW2 + 1] == target
        gh = jnp.where(hit1, slotc_ref[toks, SLOT_G1:SLOT_G1 + 1],
                       jnp.where(hit2, slotc_ref[toks, SLOT_G2:SLOT_G2 + 1], 0.0)).astype(BF16)
        out_ref[toks, :] += jnp.dot(gh, yb, preferred_element_type=F32)

    def scatter_tile(r, carry):
        spans(r, scatter_span)
        return carry

    def ffn_all(first):
        def pair(t, carry):
            ffn_tile(2 * t, first)
            ffn_tile(2 * t + 1, first)
            return carry

        lax.fori_loop(0, n_tiles // 2, pair, 0)

        @pl.when(n_tiles % 2 == 1)
        def _():
            ffn_tile(n_tiles - 1, first)

    @pl.when(f == 0)
    def _():
        lax.fori_loop(0, n_tiles, gather_tile, 0)
        ffn_all(True)

    @pl.when(f > 0)
    def _():
        ffn_all(False)

    @pl.when(f == last_f)
    def _():
        lax.fori_loop(0, n_tiles, scatter_tile, 0)


def _moe(hn, slot_cols, slot_rows, nt, toff, cuts, wg, wu, wd, tc, tr, tf, sub, win):
    t, d = hn.shape
    ne, _, ff = wg.shape
    grid_spec = pltpu.PrefetchScalarGridSpec(
        num_scalar_prefetch=3,
        grid=(t // tc, ne, ff // tf),
        in_specs=[
            pl.BlockSpec((tc, d), lambda c, e, f, *_: (c, 0)),
            pl.BlockSpec((tc, 4), lambda c, e, f, *_: (c, 0)),
            pl.BlockSpec((8, tc), lambda c, e, f, *_: (0, c)),
            pl.BlockSpec((None, d, tf), lambda c, e, f, *_: (e, 0, f)),
            pl.BlockSpec((None, d, tf), lambda c, e, f, *_: (e, 0, f)),
            pl.BlockSpec((None, tf, d), lambda c, e, f, *_: (e, f, 0)),
        ],
        out_specs=pl.BlockSpec((tc, d), lambda c, e, f, *_: (c, 0)),
        scratch_shapes=[
            pltpu.VMEM((tc + tr, d), BF16),
            pltpu.VMEM((tc + tr, d), F32),
        ],
    )
    return pl.pallas_call(
        functools.partial(_moe_kernel, tr=tr, sub=sub, win=win),
        grid_spec=grid_spec,
        out_shape=jax.ShapeDtypeStruct((t, d), F32),
        compiler_params=_params("arbitrary", "arbitrary", "arbitrary"),
        name="moe",
    )(nt, toff, cuts, hn, slot_cols, slot_rows, wg, wu, wd)


def _final_kernel(x_ref, y_ref, g_ref, out_ref):
    out_ref[...] = _rms(x_ref[...] + y_ref[...], g_ref[...])


def _final_norm(x, y, g, tm):
    t, d = x.shape
    rows = pl.BlockSpec((tm, d), lambda i: (i, 0))
    return pl.pallas_call(
        _final_kernel,
        grid=(t // tm,),
        in_specs=[rows, rows, pl.BlockSpec((1, d), lambda i: (0, 0))],
        out_specs=rows,
        out_shape=jax.ShapeDtypeStruct((t, d), F32),
        compiler_params=_params("arbitrary"),
        name="final_norm",
    )(x, y, g)


def _routed_swiglu(x, norm_g, router_w, wg, wu, wd, final_g, tc, tr, tf, sub, win):
    t, d = x.shape
    nc, nsub = t // tc, tc // sub
    r_pad = jnp.zeros((d, LANES), F32).at[:, :N_EXPERTS].set(router_w)
    hn, meta, after = _router(x, norm_g, r_pad, tm=sub, blocks_per_chunk=nsub)
    through = after[:, 0, :N_EXPERTS].astype(jnp.int32).reshape(nc, nsub, N_EXPERTS)
    cuts = through.transpose(0, 2, 1)
    total = through[:, -1]
    nt = (total + tr - 1) // tr
    toff = jnp.cumsum(nt, axis=1) - nt
    first_row = jnp.repeat((toff * tr).astype(F32), tc, axis=0)
    experts = jnp.arange(N_EXPERTS, dtype=F32)
    pick = lambda e: jnp.sum(jnp.where(e[:, None] == experts, first_row, 0.0), axis=1)
    row1 = pick(meta[:, META_E1]) + meta[:, META_RANK1]
    row2 = pick(meta[:, META_E2]) + meta[:, META_RANK2]
    slot_cols = jnp.stack([row1, row2, meta[:, META_G1], meta[:, META_G2]], axis=1)
    slot_rows = jnp.zeros((8, t), F32).at[0:4].set(slot_cols.T)
    y = _moe(hn, slot_cols, slot_rows, nt.reshape(-1), toff.reshape(-1), cuts.reshape(-1), wg, wu, wd,
             tc=tc, tr=tr, tf=tf, sub=sub, win=win)
    return _final_norm(x, y, final_g, tm=min(1024, t))


def _permute_w_in(w):
    def perm(block):
        d = block.shape[0]
        return block.reshape(d, N_HEADS, 2, QK_DIM).transpose(0, 2, 1, 3).reshape(d, GW)
    parts = [w[:, :6 * GW], perm(w[:, 6 * GW:7 * GW]), perm(w[:, 7 * GW:8 * GW]), w[:, 8 * GW:]]
    return jnp.concatenate(parts, axis=1)


def _block_diag(pool_w):
    n, p, _ = pool_w.shape
    out = jnp.zeros((n * p, n * p), pool_w.dtype)
    for gi in range(n):
        out = out.at[gi * p:(gi + 1) * p, gi * p:(gi + 1) * p].set(pool_w[gi])
    return out


def _token_mixing(x, layer, lam_init, p, rope_c, rope_d):
    b, s, d = x.shape
    row = lambda v: v.reshape(1, -1)
    w_in = _permute_w_in(p["w_in"][layer]).astype(BF16)
    uab, qkvc, qdt, kd, vdt = _norm_proj(x, row(p["norm1_g"][layer]), w_in, rope_c, rope_d, tm=512, tk=256)
    dww = jnp.zeros((HALO, GW), F32).at[:CONV_WIDTH].set(p["conv_dw_w"][layer])
    yab = _local_mixers(
        uab, dww, row(p["conv_dw_b"][layer]), row(p["conv_ln_g"][layer]), row(p["conv_ln_b"][layer]),
        p["conv_pw_w"][layer].astype(BF16), _block_diag(p["pool_w"][layer]).astype(BF16),
        row(p["pool_scale"][layer]), ts=512)
    os_, lses = [], []
    for dil in DILATIONS:
        o, lse = _band_attn(qkvc, dil)
        os_.append(o.reshape(b * s, GW))
        lses.append(lse.reshape(b * s, GW))
    yd = _diff_attn(qdt, kd, vdt, p["diff_lam"][layer], jnp.tile(p["diff_ln_g"][layer], N_HEADS).reshape(GW, 1),
                    lam_init, tq=512)
    return _out_proj(x.reshape(b * s, d), yab.reshape(b * s, 2 * GW), os_, lses, yd.reshape(b * s, GW),
                     p["w_out"][layer].astype(BF16), tm=512)


def kernel(x, norm1_g, w_in, conv_dw_w, conv_dw_b, conv_ln_g, conv_ln_b, conv_pw_w, pool_w, pool_scale,
           diff_lam, diff_ln_g, w_out, norm2_g, ffn_w_gate, ffn_w_up, ffn_w_down, moe_router, moe_w_gate,
           moe_w_up, moe_w_down, final_g):
    b, s, d = x.shape
    p = dict(norm1_g=norm1_g, w_in=w_in, conv_dw_w=conv_dw_w, conv_dw_b=conv_dw_b, conv_ln_g=conv_ln_g,
             conv_ln_b=conv_ln_b, conv_pw_w=conv_pw_w, pool_w=pool_w, pool_scale=pool_scale,
             diff_lam=diff_lam, diff_ln_g=diff_ln_g, w_out=w_out)
    rope_c = _rope_table(s, HEAD_DIM, HEAD_DIM // 4)
    rope_d = _rope_table(s, QK_DIM, QK_DIM // 4)
    row = lambda v: v.reshape(1, -1)

    x1 = _token_mixing(x, 0, 0.8 - 0.6 * math.exp(-0.3 * 0), p, rope_c, rope_d)
    x2 = _ffn(x1, row(norm2_g[0]), ffn_w_gate[0].astype(BF16), ffn_w_up[0].astype(BF16),
              ffn_w_down[0].astype(BF16), tm=512, tf=1408)

    x3 = _token_mixing(x2.reshape(b, s, d), 1, 0.8 - 0.6 * math.exp(-0.3 * 1), p, rope_c, rope_d)
    out = _routed_swiglu(x3, row(norm2_g[1]), moe_router[0], moe_w_gate[0].astype(BF16),
                         moe_w_up[0].astype(BF16), moe_w_down[0].astype(BF16), row(final_g),
                         tc=2048, tr=256, tf=512, sub=256, win=5)
    return out.reshape(b, s, d)
```

```python
import functools
import math

import jax
import jax.numpy as jnp
from jax import lax
from jax.experimental import pallas as pl
from jax.experimental.pallas import tpu as pltpu

D_MODEL = 1024
GW = 256
N_HEADS = 4
HEAD_DIM = 64
QK_DIM = 32
V_ROWS = HEAD_DIM + 16
CONV_WIDTH = 31
POOL_WINDOWS = (2, 4, 8, 16)
DILATIONS = (1, 4, 16)
BAND = 128
ROPE_THETA = 500000.0
N_EXPERTS = 8
NORM_EPS = 1e-5
HALO = 32
LANES = 128
SUBLANES = 8
VMEM_LIMIT = 56 * 1024 * 1024

BF16 = jnp.bfloat16
F32 = jnp.float32
NT_DIMS = (((1,), (1,)), ((), ()))


def _params(*sem):
    return pltpu.CompilerParams(dimension_semantics=sem, vmem_limit_bytes=VMEM_LIMIT)


def _lanes_in(shape, start, width):
    lane = lax.broadcasted_iota(jnp.int32, shape, len(shape) - 1)
    return (lane >= start) & (lane < start + width)


def _rms(x, g):
    ms = jnp.mean(x * x, axis=-1, keepdims=True)
    return x * lax.rsqrt(ms + NORM_EPS) * g


def _rope_table(seq, width, rot):
    half = rot // 2
    pos = jnp.arange(seq, dtype=F32)
    inv = ROPE_THETA ** (-jnp.arange(0, rot, 2, dtype=F32) / rot)
    ang = pos[:, None] * inv[None, :]
    cos, sin = jnp.cos(ang), jnp.sin(ang)
    lane = jnp.arange(GW) % width
    f = lane % half
    first = lane < half
    second = (lane >= half) & (lane < rot)
    c = jnp.where((lane < rot)[None, :], cos[:, f], 1.0)
    sn = jnp.where(first[None, :], -sin[:, f], 0.0)
    sp = jnp.where(second[None, :], sin[:, f], 0.0)
    return jnp.concatenate([c, sn, sp], axis=1)


def _rope(u, tab_ref, half):
    return (u * tab_ref[:, 0:GW]
            + pltpu.roll(u, GW - half, 1) * tab_ref[:, GW:2 * GW]
            + pltpu.roll(u, half, 1) * tab_ref[:, 2 * GW:3 * GW])


def _norm_proj_kernel(x_ref, g_ref, w_ref, rc_ref, rd_ref, uab_ref, qkvc_ref, qt_ref, kd_ref, vt_ref, v_ref,
                      *, tk):
    h = _rms(x_ref[0], g_ref[...]).astype(BF16)

    def proj(grp):
        return jnp.dot(h, w_ref[:, grp * GW:(grp + 1) * GW], preferred_element_type=F32)

    for grp in range(3):
        uab_ref[0, :, grp * GW:(grp + 1) * GW] = proj(grp)
    c_scale = HEAD_DIM ** -0.5
    d_scale = QK_DIM ** -0.5 * math.log2(math.e)
    qkvc_ref[0, :, 0:GW] = (_rope(proj(3), rc_ref, 8) * c_scale).astype(BF16)
    qkvc_ref[0, :, GW:2 * GW] = _rope(proj(4), rc_ref, 8).astype(BF16)
    qkvc_ref[0, :, 2 * GW:3 * GW] = proj(5).astype(BF16)
    qt_ref[0] = jnp.transpose(_rope(proj(6), rd_ref, 4) * d_scale).astype(BF16)
    kd_ref[0] = _rope(proj(7), rd_ref, 4).astype(BF16)
    v_ref[...] = proj(8)
    vt = jnp.transpose(v_ref[...]).astype(BF16)
    ones = jnp.ones((V_ROWS - HEAD_DIM, tk), BF16)
    for blk in range(vt.shape[1] // tk):
        for hd in range(N_HEADS):
            vt_ref[0, blk, hd * V_ROWS:hd * V_ROWS + HEAD_DIM, :] = (
                vt[hd * HEAD_DIM:(hd + 1) * HEAD_DIM, blk * tk:(blk + 1) * tk])
            vt_ref[0, blk, hd * V_ROWS + HEAD_DIM:(hd + 1) * V_ROWS, :] = ones


def _norm_proj(x, g, w, rope_c, rope_d, tm, tk):
    b, s, d = x.shape
    grid = (s // tm, b)
    return pl.pallas_call(
        functools.partial(_norm_proj_kernel, tk=tk),
        grid=grid,
        in_specs=[
            pl.BlockSpec((1, tm, d), lambda i, j: (j, i, 0)),
            pl.BlockSpec((1, d), lambda i, j: (0, 0)),
            pl.BlockSpec((d, 9 * GW), lambda i, j: (0, 0)),
            pl.BlockSpec((tm, 3 * GW), lambda i, j: (i, 0)),
            pl.BlockSpec((tm, 3 * GW), lambda i, j: (i, 0)),
        ],
        out_specs=[
            pl.BlockSpec((1, tm, 3 * GW), lambda i, j: (j, i, 0)),
            pl.BlockSpec((1, tm, 3 * GW), lambda i, j: (j, i, 0)),
            pl.BlockSpec((1, GW, tm), lambda i, j: (j, 0, i)),
            pl.BlockSpec((1, tm, GW), lambda i, j: (j, i, 0)),
            pl.BlockSpec((1, tm // tk, N_HEADS * V_ROWS, tk), lambda i, j: (j, i, 0, 0)),
        ],
        out_shape=[
            jax.ShapeDtypeStruct((b, s, 3 * GW), F32),
            jax.ShapeDtypeStruct((b, s, 3 * GW), BF16),
            jax.ShapeDtypeStruct((b, GW, s), BF16),
            jax.ShapeDtypeStruct((b, s, GW), BF16),
            jax.ShapeDtypeStruct((b, s // tk, N_HEADS * V_ROWS, tk), BF16),
        ],
        scratch_shapes=[pltpu.VMEM((tm, GW), F32)],
        compiler_params=_params("arbitrary", "arbitrary"),
        name="norm_proj",
    )(x, g, w, rope_c, rope_d)


def _local_kernel(u_ref, dww_ref, dwb_ref, lng_ref, lnb_ref, pww_ref, plw_ref, pls_ref,
                  y_ref, zbuf, pbuf, shift_ref, *, ts):
    si = pl.program_id(1)

    @pl.when(si == 0)
    def _():
        zbuf[0:HALO, :] = jnp.zeros((HALO, GW), F32)
        pbuf[0:HALO, :] = jnp.zeros((HALO, GW), F32)

    a = u_ref[0, :, 0:GW]
    b = u_ref[0, :, GW:2 * GW]
    zbuf[HALO:HALO + ts, :] = a * jax.nn.sigmoid(b)
    pbuf[HALO:HALO + ts, :] = u_ref[0, :, 2 * GW:3 * GW]

    def by_phase(buf, shifted, offsets, term):
        total = None
        for phase in range(SUBLANES):
            offs = [o for o in offsets if o % SUBLANES == phase]
            if not offs:
                continue
            src = buf
            if phase:
                n = max(offs) - phase + ts
                shifted[phase, 0:n, :] = buf[phase:phase + n, :]
                src = shifted.at[phase]
            for o in offs:
                t = term(o, src[o - phase:o - phase + ts, :])
                total = t if total is None else total + t
        return total

    base = HALO - (CONV_WIDTH - 1)
    acc = by_phase(zbuf, shift_ref.at[0], range(base, base + CONV_WIDTH),
                   lambda o, rows: dww_ref[o - base:o - base + 1, :] * rows)
    z = acc + dwb_ref[...]
    mu = jnp.mean(z, axis=-1, keepdims=True)
    zc = z - mu
    var = jnp.mean(zc * zc, axis=-1, keepdims=True)
    zn = zc * lax.rsqrt(var + NORM_EPS) * lng_ref[...] + lnb_ref[...]
    zs = zn * jax.nn.sigmoid(zn)
    y_ref[0, :, 0:GW] = jnp.dot(zs.astype(BF16), pww_ref[...], preferred_element_type=F32).astype(BF16)

    lane = lax.broadcasted_iota(jnp.int32, (ts, GW), 1)
    pool_group = GW // len(POOL_WINDOWS)
    win = jnp.zeros((ts, GW), jnp.int32)
    for gi, w in enumerate(POOL_WINDOWS):
        win = jnp.where(lane >= gi * pool_group, w, win)
    p0 = pbuf[HALO:HALO + ts, :]
    lag = lambda o: HALO - o
    tot = by_phase(pbuf, shift_ref.at[1], range(HALO - max(POOL_WINDOWS) + 1, HALO + 1),
                   lambda o, rows: rows if lag(o) < min(POOL_WINDOWS) else jnp.where(win > lag(o), rows, 0.0))
    pos = si * ts + lax.broadcasted_iota(jnp.int32, (ts, GW), 0)
    cnt = jnp.minimum(pos + 1, win).astype(F32)
    pooled = tot / cnt - p0
    yb = jnp.dot(pooled.astype(BF16), plw_ref[...], preferred_element_type=F32) * pls_ref[...]
    y_ref[0, :, GW:2 * GW] = yb.astype(BF16)

    zbuf[0:HALO, :] = zbuf[ts:ts + HALO, :]
    pbuf[0:HALO, :] = pbuf[ts:ts + HALO, :]


def _local_mixers(uab, dww, dwb, lng, lnb, pww, plw, pls, ts):
    b, s, _ = uab.shape
    vec = lambda: pl.BlockSpec((1, GW), lambda i, j: (0, 0))
    mat = lambda: pl.BlockSpec((GW, GW), lambda i, j: (0, 0))
    return pl.pallas_call(
        functools.partial(_local_kernel, ts=ts),
        grid=(b, s // ts),
        in_specs=[
            pl.BlockSpec((1, ts, 3 * GW), lambda i, j: (i, j, 0)),
            pl.BlockSpec((HALO, GW), lambda i, j: (0, 0)),
            vec(), vec(), vec(), mat(), mat(), vec(),
        ],
        out_specs=pl.BlockSpec((1, ts, 2 * GW), lambda i, j: (i, j, 0)),
        out_shape=jax.ShapeDtypeStruct((b, s, 2 * GW), BF16),
        scratch_shapes=[
            pltpu.VMEM((HALO + ts, GW), F32),
            pltpu.VMEM((HALO + ts, GW), F32),
            pltpu.VMEM((2, SUBLANES, HALO + ts, GW), F32),
        ],
        compiler_params=_params("arbitrary", "arbitrary"),
        name="local_mixers",
    )(uab, dww, dwb, lng, lnb, pww, plw, pls)


def _band_attn_kernel(q_ref, k_ref, v_ref, o_ref, lse_ref, s_ref):
    row = lax.broadcasted_iota(jnp.int32, (BAND, 2 * BAND), 0)
    col = lax.broadcasted_iota(jnp.int32, (BAND, 2 * BAND), 1)
    halves = [_lanes_in((BAND, LANES), sub * HEAD_DIM, HEAD_DIM) for sub in range(LANES // HEAD_DIM)]

    def block(n, carry):
        q0 = pl.multiple_of(n * BAND, BAND)
        k0 = pl.multiple_of(jnp.maximum(n - 1, 0) * BAND, BAND)
        q = q_ref[0, pl.ds(q0, BAND), :].astype(F32)
        kw = k_ref[0, pl.ds(k0, 2 * BAND), :]
        dist = (row - col) + (q0 - k0)
        mask = (dist >= 0) & (dist <= BAND)
        for h in range(N_HEADS):
            qh = jnp.where(_lanes_in((BAND, GW), h * HEAD_DIM, HEAD_DIM), q, 0.0).astype(BF16)
            s = lax.dot_general(qh, kw, NT_DIMS, preferred_element_type=F32)
            s_ref[h] = jnp.where(mask, s, -jnp.inf)
        for tile in range(GW // LANES):
            vw = v_ref[0, pl.ds(k0, 2 * BAND), tile * LANES:(tile + 1) * LANES]
            o = jnp.zeros((BAND, LANES), F32)
            lse = jnp.zeros((BAND, LANES), F32)
            for sub in range(LANES // HEAD_DIM):
                s = s_ref[tile * (LANES // HEAD_DIM) + sub]
                m = jnp.max(s, axis=-1, keepdims=True)
                p = jnp.exp(s - m)
                l = jnp.sum(p, axis=-1, keepdims=True)
                acc = jnp.dot(p.astype(BF16), vw, preferred_element_type=F32)
                mine = halves[sub]
                o = jnp.where(mine, acc / l, o)
                lse = jnp.where(mine, m + jnp.log(l), lse)
            o_ref[0, pl.ds(q0, BAND), tile * LANES:(tile + 1) * LANES] = o
            lse_ref[0, pl.ds(q0, BAND), tile * LANES:(tile + 1) * LANES] = lse
        return carry

    lax.fori_loop(0, q_ref.shape[1] // BAND, block, 0, unroll=2)


def _band_attn(qkv, dil):
    b, s, _ = qkv.shape
    l = s // dil
    assert l >= 2 * BAND and l % BAND == 0
    part = lambda c: pl.BlockSpec((1, l, GW), lambda i, r: (i, 0, 3 * r + c))
    out = pl.BlockSpec((1, l, GW), lambda i, r: (i, 0, r))
    o, lse = pl.pallas_call(
        _band_attn_kernel,
        grid=(b, dil),
        in_specs=[part(0), part(1), part(2)],
        out_specs=[out, out],
        out_shape=[jax.ShapeDtypeStruct((b, l, dil * GW), F32)] * 2,
        scratch_shapes=[pltpu.VMEM((N_HEADS, BAND, 2 * BAND), F32)],
        compiler_params=_params("arbitrary", "arbitrary"),
        name="band_attn",
    )(*[qkv.reshape(b, l, dil * 3 * GW)] * 3)
    return o.reshape(b, s, GW), lse.reshape(b, s, GW)


def _diff_attn_kernel(lam_ref, g_ref, qt_ref, k_ref, vt_ref, o_ref, qm_ref, m_ref, acc_ref, yt_ref,
                      st_ref, mx_ref, *, tq, lam_init):
    n_chain = 2 * N_HEADS
    i = pl.program_id(1)
    @pl.when((pl.program_id(0) == 0) & (i == 0))
    def _():
        qm_ref[...] = jnp.zeros(qm_ref.shape, BF16)

    for c in range(n_chain):
        qm_ref[c, c * QK_DIM:(c + 1) * QK_DIM, :] = qt_ref[0, c * QK_DIM:(c + 1) * QK_DIM, :]
    m_ref[...] = jnp.full(m_ref.shape, -jnp.inf, F32)
    acc_ref[...] = jnp.zeros(acc_ref.shape, F32)
    tk = st_ref.shape[1]
    ratio = tq // tk
    ahead = (lax.broadcasted_iota(jnp.int32, (tk, tq), 0)
             - lax.broadcasted_iota(jnp.int32, (tk, tq), 1))

    def score(c, kb, diag):
        st = jnp.dot(kb, qm_ref[c], preferred_element_type=F32)
        if diag is not None:
            st = jnp.where(ahead <= -diag * tk, st, -jnp.inf)
        st_ref[c] = st
        mx_ref[c] = jnp.max(st, axis=0, keepdims=True)

    def value(c, j):
        h = c % N_HEADS
        m_prev = m_ref[c]
        m_next = jnp.maximum(m_prev, mx_ref[c])
        alpha = jnp.exp2(m_prev - m_next)
        p = jnp.exp2(st_ref[c] - m_next).astype(BF16)
        vt = vt_ref[0, j, h * V_ROWS:(h + 1) * V_ROWS, :]
        acc_ref[c] = alpha * acc_ref[c] + jnp.dot(vt, p, preferred_element_type=F32)
        m_ref[c] = m_next

    def k_block(j):
        return k_ref[0, pl.ds(pl.multiple_of(j * tk, tk), tk), :]

    def scores(j, diag):
        kb = k_block(j)
        for c in range(n_chain):
            score(c, kb, diag)

    def values(j):
        for c in range(n_chain):
            value(c, j)

    def values_then_scores(j, diag):
        kb = k_block(j + 1)
        for c in range(n_chain):
            value(c, j)
            score(c, kb, diag)

    before = ratio * i

    @pl.when(i == 0)
    def _():
        scores(0, 0)

    @pl.when(i > 0)
    def _():
        scores(0, None)

    def body(j, carry):
        values_then_scores(j, None)
        return carry

    lax.fori_loop(0, before - 1, body, 0)

    @pl.when(i > 0)
    def _():
        values_then_scores(before - 1, 0)

    for diag in range(1, ratio):
        values_then_scores(before + diag - 1, diag)
    values(before + ratio - 1)

    lv = lam_ref[...]
    lam = (jnp.exp(jnp.sum(lv[0:1] * lv[1:2], axis=-1, keepdims=True))
           - jnp.exp(jnp.sum(lv[2:3] * lv[3:4], axis=-1, keepdims=True)) + lam_init)
    for h in range(N_HEADS):
        a1, a2 = acc_ref[h], acc_ref[N_HEADS + h]
        oh = (a1[:HEAD_DIM] / a1[HEAD_DIM:HEAD_DIM + 1]
              - lam * (a2[:HEAD_DIM] / a2[HEAD_DIM:HEAD_DIM + 1]))
        ms = jnp.mean(oh * oh, axis=0, keepdims=True)
        g = g_ref[h * HEAD_DIM:(h + 1) * HEAD_DIM, :]
        yt_ref[h * HEAD_DIM:(h + 1) * HEAD_DIM, :] = oh * lax.rsqrt(ms + NORM_EPS) * (g * (1.0 - lam_init))
    o_ref[0] = jnp.transpose(yt_ref[...]).astype(BF16)


def _diff_attn(qt, k, vt, lam_vecs, g_col, lam_init, tq):
    b, s, _ = k.shape
    nk, tk = vt.shape[1], vt.shape[3]
    assert tq % tk == 0 and s % tq == 0
    n_chain = 2 * N_HEADS
    return pl.pallas_call(
        functools.partial(_diff_attn_kernel, tq=tq, lam_init=lam_init),
        grid=(b, s // tq),
        in_specs=[
            pl.BlockSpec((4, QK_DIM), lambda i, j: (0, 0)),
            pl.BlockSpec((GW, 1), lambda i, j: (0, 0)),
            pl.BlockSpec((1, GW, tq), lambda i, j: (i, 0, j)),
            pl.BlockSpec((1, s, GW), lambda i, j: (i, 0, 0)),
            pl.BlockSpec((1, nk, N_HEADS * V_ROWS, tk), lambda i, j: (i, 0, 0, 0)),
        ],
        out_specs=pl.BlockSpec((1, tq, GW), lambda i, j: (i, j, 0)),
        out_shape=jax.ShapeDtypeStruct((b, s, GW), BF16),
        scratch_shapes=[
            pltpu.VMEM((n_chain, GW, tq), BF16),
            pltpu.VMEM((n_chain, 1, tq), F32),
            pltpu.VMEM((n_chain, V_ROWS, tq), F32),
            pltpu.VMEM((GW, tq), F32),
            pltpu.VMEM((n_chain, tk, tq), F32),
            pltpu.VMEM((n_chain, 1, tq), F32),
        ],
        compiler_params=_params("arbitrary", "arbitrary"),
        name="diff_attn",
    )(lam_vecs, g_col, qt, k, vt)


def _out_proj_kernel(x_ref, yab_ref, o1_ref, o2_ref, o3_ref, l1_ref, l2_ref, l3_ref, yd_ref, w_ref, out_ref):
    l1, l2, l3 = l1_ref[...], l2_ref[...], l3_ref[...]
    m = jnp.maximum(jnp.maximum(l1, l2), l3)
    w1, w2, w3 = jnp.exp(l1 - m), jnp.exp(l2 - m), jnp.exp(l3 - m)
    yc = (w1 * o1_ref[...] + w2 * o2_ref[...] + w3 * o3_ref[...]) / (w1 + w2 + w3)
    acc = jnp.dot(yab_ref[...], w_ref[0:2 * GW, :], preferred_element_type=F32)
    acc = acc + jnp.dot(yc.astype(BF16), w_ref[2 * GW:3 * GW, :], preferred_element_type=F32)
    acc = acc + jnp.dot(yd_ref[...], w_ref[3 * GW:4 * GW, :], preferred_element_type=F32)
    out_ref[...] = x_ref[...] + acc


def _out_proj(x, yab, os_, lses, yd, w, tm):
    t, d = x.shape
    rows = lambda c: pl.BlockSpec((tm, c), lambda i: (i, 0))
    return pl.pallas_call(
        _out_proj_kernel,
        grid=(t // tm,),
        in_specs=[rows(d), rows(2 * GW)] + [rows(GW)] * 6 + [rows(GW), pl.BlockSpec((d, d), lambda i: (0, 0))],
        out_specs=rows(d),
        out_shape=jax.ShapeDtypeStruct((t, d), F32),
        compiler_params=_params("arbitrary"),
        name="out_proj",
    )(x, yab, *os_, *lses, yd, w)


def _ffn_kernel(x_ref, g_ref, wg_ref, wu_ref, wd_ref, out_ref, acc_ref, *, tf):
    hn = _rms(x_ref[...], g_ref[...]).astype(BF16)
    acc_ref[...] = x_ref[...]
    for f0 in range(0, wg_ref.shape[1], tf):
        gate = jnp.dot(hn, wg_ref[:, f0:f0 + tf], preferred_element_type=F32)
        up = jnp.dot(hn, wu_ref[:, f0:f0 + tf], preferred_element_type=F32)
        act = (gate * jax.nn.sigmoid(gate) * up).astype(BF16)
        acc_ref[...] += jnp.dot(act, wd_ref[f0:f0 + tf, :], preferred_element_type=F32)
    out_ref[...] = acc_ref[...]


def _ffn(x, g, wg, wu, wd, tm, tf):
    t, d = x.shape
    ff = wg.shape[1]
    assert ff % tf == 0
    whole = lambda shape: pl.BlockSpec(shape, lambda i: (0, 0))
    return pl.pallas_call(
        functools.partial(_ffn_kernel, tf=tf),
        grid=(t // tm,),
        in_specs=[pl.BlockSpec((tm, d), lambda i: (i, 0)), whole((1, d)), whole((d, ff)), whole((d, ff)),
                  whole((ff, d))],
        out_specs=pl.BlockSpec((tm, d), lambda i: (i, 0)),
        out_shape=jax.ShapeDtypeStruct((t, d), F32),
        scratch_shapes=[pltpu.VMEM((tm, d), F32)],
        compiler_params=_params("arbitrary"),
        name="ffn_dense",
    )(x, g, wg, wu, wd)


META_E1, META_E2, META_RANK1, META_RANK2, META_G1, META_G2 = range(6)


def _router_kernel(x_ref, g_ref, r_ref, hn_ref, meta_ref, after_ref, carry_ref, *, blocks_per_chunk):
    @pl.when(pl.program_id(0) % blocks_per_chunk == 0)
    def _():
        carry_ref[...] = jnp.zeros_like(carry_ref)

    h = _rms(x_ref[...], g_ref[...])
    hn_ref[...] = h.astype(BF16)
    logits = jnp.dot(h, r_ref[...], preferred_element_type=F32, precision=lax.Precision.HIGHEST)
    tm = logits.shape[0]
    lane = lax.broadcasted_iota(jnp.int32, logits.shape, 1)
    logits = jnp.where(lane < N_EXPERTS, logits, -jnp.inf)
    m1 = jnp.max(logits, axis=-1, keepdims=True)
    i1 = jnp.min(jnp.where(logits == m1, lane, LANES), axis=-1, keepdims=True)
    rest = jnp.where(lane == i1, -jnp.inf, logits)
    m2 = jnp.max(rest, axis=-1, keepdims=True)
    i2 = jnp.min(jnp.where(rest == m2, lane, LANES), axis=-1, keepdims=True)
    e2 = jnp.exp(m2 - m1)
    g1 = 1.0 / (1.0 + e2)
    g2 = e2 / (1.0 + e2)

    oh1 = lane == i1
    oh2 = lane == i2
    both = jnp.where(oh1, 1.0, jnp.where(oh2, 1.0, 0.0))
    row = lax.broadcasted_iota(jnp.int32, (tm, tm), 0)
    col = lax.broadcasted_iota(jnp.int32, (tm, tm), 1)
    earlier = jnp.where(col < row, 1.0, 0.0).astype(BF16)
    carry = carry_ref[0:1, :]
    prefix = jnp.dot(earlier, both.astype(BF16), preferred_element_type=F32) + carry
    rank1 = jnp.sum(jnp.where(oh1, prefix, 0.0), axis=-1, keepdims=True)
    rank2 = jnp.sum(jnp.where(oh2, prefix, 0.0), axis=-1, keepdims=True)
    total = carry + jnp.sum(both, axis=0, keepdims=True)

    meta = jnp.zeros(logits.shape, F32)
    for k, val in ((META_E1, i1.astype(F32)), (META_E2, i2.astype(F32)), (META_RANK1, rank1),
                   (META_RANK2, rank2), (META_G1, g1), (META_G2, g2)):
        meta = jnp.where(lane == k, val, meta)
    meta_ref[...] = meta
    after_ref[0] = jnp.broadcast_to(total, after_ref.shape[1:])
    carry_ref[...] = jnp.broadcast_to(total, carry_ref.shape)


def _router(x, g, r_pad, tm, blocks_per_chunk):
    t, d = x.shape
    nblk = t // tm
    counts = pl.BlockSpec((1, 8, LANES), lambda i: (i, 0, 0))
    return pl.pallas_call(
        functools.partial(_router_kernel, blocks_per_chunk=blocks_per_chunk),
        grid=(nblk,),
        in_specs=[
            pl.BlockSpec((tm, d), lambda i: (i, 0)),
            pl.BlockSpec((1, d), lambda i: (0, 0)),
            pl.BlockSpec((d, LANES), lambda i: (0, 0)),
        ],
        out_specs=[
            pl.BlockSpec((tm, d), lambda i: (i, 0)),
            pl.BlockSpec((tm, LANES), lambda i: (i, 0)),
            counts,
        ],
        out_shape=[
            jax.ShapeDtypeStruct((t, d), BF16),
            jax.ShapeDtypeStruct((t, LANES), F32),
            jax.ShapeDtypeStruct((nblk, 8, LANES), F32),
        ],
        scratch_shapes=[pltpu.VMEM((8, LANES), F32)],
        compiler_params=_params("arbitrary"),
        name="router",
    )(x, g, r_pad)


SLOT_ROW1, SLOT_ROW2, SLOT_G1, SLOT_G2 = range(4)


def _moe_kernel(nt_ref, toff_ref, cuts_ref, hn_ref, slotc_ref, slotr_ref, wg_ref, wu_ref, wd_ref, out_ref,
                xs_ref, ys_ref, *, tr, sub, win):
    c, e, f = pl.program_id(0), pl.program_id(1), pl.program_id(2)
    last_f = pl.num_programs(2) - 1
    grp = c * N_EXPERTS + e
    n_tiles = nt_ref[grp]
    row0 = toff_ref[grp] * tr
    tc = hn_ref.shape[0]
    nsub = tc // sub

    @pl.when((e == 0) & (f == 0))
    def _():
        out_ref[...] = jnp.zeros_like(out_ref)

    def rows_of(r):
        return pl.ds(pl.multiple_of(r * tr, tr), tr)

    def token_window(r):
        base = grp * nsub
        first = r * tr
        last = jnp.minimum(first + tr, cuts_ref[base + nsub - 1]) - 1
        lo = hi = 0
        for k in range(nsub - 1):
            cut = cuts_ref[base + k]
            lo = lo + (cut <= first).astype(jnp.int32)
            hi = hi + (cut <= last).astype(jnp.int32)
        start = jnp.minimum(lo, nsub - win)
        return hi < start + win, pl.multiple_of(start * sub, sub)

    def spans(r, fn):
        covered, tok0 = token_window(r)

        @pl.when(covered)
        def _():
            fn(r, tok0, win * sub)

        @pl.when(jnp.logical_not(covered))
        def _():
            fn(r, 0, tc)

    def gather_span(r, tok0, ntok):
        toks = pl.ds(tok0, ntok)
        target = (lax.broadcasted_iota(jnp.int32, (tr, ntok), 0) + (row0 + r * tr)).astype(F32)
        hit1 = slotr_ref[SLOT_ROW1:SLOT_ROW1 + 1, toks] == target
        hit2 = slotr_ref[SLOT_ROW2:SLOT_ROW2 + 1, toks] == target
        oh = jnp.where(hit1, 1.0, jnp.where(hit2, 1.0, 0.0)).astype(BF16)
        xs_ref[rows_of(r), :] = jnp.dot(oh, hn_ref[toks, :], preferred_element_type=F32).astype(BF16)

    def gather_tile(r, carry):
        spans(r, gather_span)
        return carry

    def ffn_tile(r, first):
        xt = xs_ref[rows_of(r), :]
        gate = jnp.dot(xt, wg_ref[...], preferred_element_type=F32)
        up = jnp.dot(xt, wu_ref[...], preferred_element_type=F32)
        act = (gate * jax.nn.sigmoid(gate) * up).astype(BF16)
        y = jnp.dot(act, wd_ref[...], preferred_element_type=F32)
        if first:
            ys_ref[rows_of(r), :] = y
        else:
            ys_ref[rows_of(r), :] += y

    def scatter_span(r, tok0, ntok):
        toks = pl.ds(tok0, ntok)
        yb = ys_ref[rows_of(r), :].astype(BF16)
        target = (lax.broadcasted_iota(jnp.int32, (ntok, tr), 1) + (row0 + r * tr)).astype(F32)
        hit1 = slotc_ref[toks, SLOT_ROW1:SLOT_ROW1 + 1] == target
        hit2 = slotc_ref[toks, SLOT_ROW2:SLOT_ROW2 + 1] == target
        gh = jnp.where(hit1, slotc_ref[toks, SLOT_G1:SLOT_G1 + 1],
                       jnp.where(hit2, slotc_ref[toks, SLOT_G2:SLOT_G2 + 1], 0.0)).astype(BF16)
        out_ref[toks, :] += jnp.dot(gh, yb, preferred_element_type=F32)

    def scatter_tile(r, carry):
        spans(r, scatter_span)
        return carry

    def ffn_all(first):
        def pair(t, carry):
            ffn_tile(2 * t, first)
            ffn_tile(2 * t + 1, first)
            return carry

        lax.fori_loop(0, n_tiles // 2, pair, 0)

        @pl.when(n_tiles % 2 == 1)
        def _():
            ffn_tile(n_tiles - 1, first)

    @pl.when(f == 0)
    def _():
        lax.fori_loop(0, n_tiles, gather_tile, 0)
        ffn_all(True)

    @pl.when(f > 0)
    def _():
        ffn_all(False)

    @pl.when(f == last_f)
    def _():
        lax.fori_loop(0, n_tiles, scatter_tile, 0)


def _moe(hn, slot_cols, slot_rows, nt, toff, cuts, wg, wu, wd, tc, tr, sub, win):
    t, d = hn.shape
    ne, nf, _, tf = wg.shape
    grid_spec = pltpu.PrefetchScalarGridSpec(
        num_scalar_prefetch=3,
        grid=(t // tc, ne, nf),
        in_specs=[
            pl.BlockSpec((tc, d), lambda c, e, f, *_: (c, 0)),
            pl.BlockSpec((tc, 4), lambda c, e, f, *_: (c, 0)),
            pl.BlockSpec((8, tc), lambda c, e, f, *_: (0, c)),
            pl.BlockSpec((None, None, d, tf), lambda c, e, f, *_: (e, f, 0, 0)),
            pl.BlockSpec((None, None, d, tf), lambda c, e, f, *_: (e, f, 0, 0)),
            pl.BlockSpec((None, tf, d), lambda c, e, f, *_: (e, f, 0)),
        ],
        out_specs=pl.BlockSpec((tc, d), lambda c, e, f, *_: (c, 0)),
        scratch_shapes=[
            pltpu.VMEM((tc + tr, d), BF16),
            pltpu.VMEM((tc + tr, d), F32),
        ],
    )
    return pl.pallas_call(
        functools.partial(_moe_kernel, tr=tr, sub=sub, win=win),
        grid_spec=grid_spec,
        out_shape=jax.ShapeDtypeStruct((t, d), F32),
        compiler_params=_params("arbitrary", "arbitrary", "arbitrary"),
        name="moe",
    )(nt, toff, cuts, hn, slot_cols, slot_rows, wg, wu, wd)


def _final_kernel(x_ref, y_ref, g_ref, out_ref):
    out_ref[...] = _rms(x_ref[...] + y_ref[...], g_ref[...])


def _final_norm(x, y, g, tm):
    t, d = x.shape
    rows = pl.BlockSpec((tm, d), lambda i: (i, 0))
    return pl.pallas_call(
        _final_kernel,
        grid=(t // tm,),
        in_specs=[rows, rows, pl.BlockSpec((1, d), lambda i: (0, 0))],
        out_specs=rows,
        out_shape=jax.ShapeDtypeStruct((t, d), F32),
        compiler_params=_params("arbitrary"),
        name="final_norm",
    )(x, y, g)


def _routed_swiglu(x, norm_g, router_w, wg, wu, wd, final_g, tc, tr, tf, sub, win):
    t, d = x.shape
    nc, nsub = t // tc, tc // sub
    r_pad = jnp.zeros((d, LANES), F32).at[:, :N_EXPERTS].set(router_w)
    hn, meta, after = _router(x, norm_g, r_pad, tm=sub, blocks_per_chunk=nsub)
    through = after[:, 0, :N_EXPERTS].astype(jnp.int32).reshape(nc, nsub, N_EXPERTS)
    cuts = through.transpose(0, 2, 1)
    total = through[:, -1]
    nt = (total + tr - 1) // tr
    toff = jnp.cumsum(nt, axis=1) - nt
    first_row = jnp.repeat((toff * tr).astype(F32), tc, axis=0)
    experts = jnp.arange(N_EXPERTS, dtype=F32)
    pick = lambda e: jnp.sum(jnp.where(e[:, None] == experts, first_row, 0.0), axis=1)
    row1 = pick(meta[:, META_E1]) + meta[:, META_RANK1]
    row2 = pick(meta[:, META_E2]) + meta[:, META_RANK2]
    slot_cols = jnp.stack([row1, row2, meta[:, META_G1], meta[:, META_G2]], axis=1)
    slot_rows = jnp.zeros((8, t), F32).at[0:4].set(slot_cols.T)
    ne, _, ff = wg.shape
    tiled = lambda w: w.reshape(ne, d, ff // tf, tf).transpose(0, 2, 1, 3)
    y = _moe(hn, slot_cols, slot_rows, nt.reshape(-1), toff.reshape(-1), cuts.reshape(-1), tiled(wg), tiled(wu),
             wd, tc=tc, tr=tr, sub=sub, win=win)
    return _final_norm(x, y, final_g, tm=min(1024, t))


def _permute_w_in(w):
    def perm(block):
        d = block.shape[0]
        return block.reshape(d, N_HEADS, 2, QK_DIM).transpose(0, 2, 1, 3).reshape(d, GW)
    parts = [w[:, :6 * GW], perm(w[:, 6 * GW:7 * GW]), perm(w[:, 7 * GW:8 * GW]), w[:, 8 * GW:]]
    return jnp.concatenate(parts, axis=1)


def _block_diag(pool_w):
    n, p, _ = pool_w.shape
    out = jnp.zeros((n * p, n * p), pool_w.dtype)
    for gi in range(n):
        out = out.at[gi * p:(gi + 1) * p, gi * p:(gi + 1) * p].set(pool_w[gi])
    return out


def _token_mixing(x, layer, lam_init, p, rope_c, rope_d):
    b, s, d = x.shape
    row = lambda v: v.reshape(1, -1)
    w_in = _permute_w_in(p["w_in"][layer]).astype(BF16)
    uab, qkvc, qdt, kd, vdt = _norm_proj(x, row(p["norm1_g"][layer]), w_in, rope_c, rope_d, tm=512, tk=256)
    dww = jnp.zeros((HALO, GW), F32).at[:CONV_WIDTH].set(p["conv_dw_w"][layer])
    yab = _local_mixers(
        uab, dww, row(p["conv_dw_b"][layer]), row(p["conv_ln_g"][layer]), row(p["conv_ln_b"][layer]),
        p["conv_pw_w"][layer].astype(BF16), _block_diag(p["pool_w"][layer]).astype(BF16),
        row(p["pool_scale"][layer]), ts=512)
    os_, lses = [], []
    for dil in DILATIONS:
        o, lse = _band_attn(qkvc, dil)
        os_.append(o.reshape(b * s, GW))
        lses.append(lse.reshape(b * s, GW))
    yd = _diff_attn(qdt, kd, vdt, p["diff_lam"][layer], jnp.tile(p["diff_ln_g"][layer], N_HEADS).reshape(GW, 1),
                    lam_init, tq=512)
    return _out_proj(x.reshape(b * s, d), yab.reshape(b * s, 2 * GW), os_, lses, yd.reshape(b * s, GW),
                     p["w_out"][layer].astype(BF16), tm=512)


def kernel(x, norm1_g, w_in, conv_dw_w, conv_dw_b, conv_ln_g, conv_ln_b, conv_pw_w, pool_w, pool_scale,
           diff_lam, diff_ln_g, w_out, norm2_g, ffn_w_gate, ffn_w_up, ffn_w_down, moe_router, moe_w_gate,
           moe_w_up, moe_w_down, final_g):
    b, s, d = x.shape
    p = dict(norm1_g=norm1_g, w_in=w_in, conv_dw_w=conv_dw_w, conv_dw_b=conv_dw_b, conv_ln_g=conv_ln_g,
             conv_ln_b=conv_ln_b, conv_pw_w=conv_pw_w, pool_w=pool_w, pool_scale=pool_scale,
             diff_lam=diff_lam, diff_ln_g=diff_ln_g, w_out=w_out)
    rope_c = _rope_table(s, HEAD_DIM, HEAD_DIM // 4)
    rope_d = _rope_table(s, QK_DIM, QK_DIM // 4)
    row = lambda v: v.reshape(1, -1)

    x1 = _token_mixing(x, 0, 0.8 - 0.6 * math.exp(-0.3 * 0), p, rope_c, rope_d)
    x2 = _ffn(x1, row(norm2_g[0]), ffn_w_gate[0].astype(BF16), ffn_w_up[0].astype(BF16),
              ffn_w_down[0].astype(BF16), tm=512, tf=256)

    x3 = _token_mixing(x2.reshape(b, s, d), 1, 0.8 - 0.6 * math.exp(-0.3 * 1), p, rope_c, rope_d)
    out = _routed_swiglu(x3, row(norm2_g[1]), moe_router[0], moe_w_gate[0].astype(BF16),
                         moe_w_up[0].astype(BF16), moe_w_down[0].astype(BF16), row(final_g),
                         tc=2048, tr=256, tf=512, sub=256, win=5)
    return out.reshape(b, s, d)
```

```python
import functools
import math

import jax
import jax.numpy as jnp
from jax import lax
from jax.experimental import pallas as pl
from jax.experimental.pallas import tpu as pltpu

D_MODEL = 1024
GW = 256
N_HEADS = 4
HEAD_DIM = 64
QK_DIM = 32
V_ROWS = HEAD_DIM + 16
CONV_WIDTH = 31
POOL_WINDOWS = (2, 4, 8, 16)
DILATIONS = (1, 4, 16)
BAND = 128
ROPE_THETA = 500000.0
N_EXPERTS = 8
NORM_EPS = 1e-5
HALO = 32
LANES = 128
SUBLANES = 8
VMEM_LIMIT = 56 * 1024 * 1024

BF16 = jnp.bfloat16
F32 = jnp.float32
NT_DIMS = (((1,), (1,)), ((), ()))


def _params(*sem):
    return pltpu.CompilerParams(dimension_semantics=sem, vmem_limit_bytes=VMEM_LIMIT)


def _lanes_in(shape, start, width):
    lane = lax.broadcasted_iota(jnp.int32, shape, len(shape) - 1)
    return (lane >= start) & (lane < start + width)


def _rms(x, g):
    ms = jnp.mean(x * x, axis=-1, keepdims=True)
    return x * lax.rsqrt(ms + NORM_EPS) * g


def _rope_table(seq, width, rot):
    half = rot // 2
    pos = jnp.arange(seq, dtype=F32)
    inv = ROPE_THETA ** (-jnp.arange(0, rot, 2, dtype=F32) / rot)
    ang = pos[:, None] * inv[None, :]
    cos, sin = jnp.cos(ang), jnp.sin(ang)
    lane = jnp.arange(GW) % width
    f = lane % half
    first = lane < half
    second = (lane >= half) & (lane < rot)
    c = jnp.where((lane < rot)[None, :], cos[:, f], 1.0)
    sn = jnp.where(first[None, :], -sin[:, f], 0.0)
    sp = jnp.where(second[None, :], sin[:, f], 0.0)
    return jnp.concatenate([c, sn, sp], axis=1)


def _rope(u, tab_ref, half):
    return (u * tab_ref[:, 0:GW]
            + pltpu.roll(u, GW - half, 1) * tab_ref[:, GW:2 * GW]
            + pltpu.roll(u, half, 1) * tab_ref[:, 2 * GW:3 * GW])


def _norm_proj_kernel(x_ref, g_ref, w_ref, rc_ref, rd_ref, uab_ref, qkvc_ref, qt_ref, kd_ref, vt_ref, v_ref,
                      *, tk):
    h = _rms(x_ref[0], g_ref[...]).astype(BF16)

    def proj(grp):
        return jnp.dot(h, w_ref[:, grp * GW:(grp + 1) * GW], preferred_element_type=F32)

    for grp in range(3):
        uab_ref[0, :, grp * GW:(grp + 1) * GW] = proj(grp)
    c_scale = HEAD_DIM ** -0.5
    d_scale = QK_DIM ** -0.5 * math.log2(math.e)
    qkvc_ref[0, :, 0:GW] = (_rope(proj(3), rc_ref, 8) * c_scale).astype(BF16)
    qkvc_ref[0, :, GW:2 * GW] = _rope(proj(4), rc_ref, 8).astype(BF16)
    qkvc_ref[0, :, 2 * GW:3 * GW] = proj(5).astype(BF16)
    qt_ref[0] = jnp.transpose(_rope(proj(6), rd_ref, 4) * d_scale).astype(BF16)
    kd_ref[0] = _rope(proj(7), rd_ref, 4).astype(BF16)
    v_ref[...] = proj(8)
    vt = jnp.transpose(v_ref[...]).astype(BF16)
    ones = jnp.ones((V_ROWS - HEAD_DIM, tk), BF16)
    for blk in range(vt.shape[1] // tk):
        for hd in range(N_HEADS):
            vt_ref[0, blk, hd * V_ROWS:hd * V_ROWS + HEAD_DIM, :] = (
                vt[hd * HEAD_DIM:(hd + 1) * HEAD_DIM, blk * tk:(blk + 1) * tk])
            vt_ref[0, blk, hd * V_ROWS + HEAD_DIM:(hd + 1) * V_ROWS, :] = ones


def _norm_proj(x, g, w, rope_c, rope_d, tm, tk):
    b, s, d = x.shape
    grid = (s // tm, b)
    return pl.pallas_call(
        functools.partial(_norm_proj_kernel, tk=tk),
        grid=grid,
        in_specs=[
            pl.BlockSpec((1, tm, d), lambda i, j: (j, i, 0)),
            pl.BlockSpec((1, d), lambda i, j: (0, 0)),
            pl.BlockSpec((d, 9 * GW), lambda i, j: (0, 0)),
            pl.BlockSpec((tm, 3 * GW), lambda i, j: (i, 0)),
            pl.BlockSpec((tm, 3 * GW), lambda i, j: (i, 0)),
        ],
        out_specs=[
            pl.BlockSpec((1, tm, 3 * GW), lambda i, j: (j, i, 0)),
            pl.BlockSpec((1, tm, 3 * GW), lambda i, j: (j, i, 0)),
            pl.BlockSpec((1, GW, tm), lambda i, j: (j, 0, i)),
            pl.BlockSpec((1, tm, GW), lambda i, j: (j, i, 0)),
            pl.BlockSpec((1, tm // tk, N_HEADS * V_ROWS, tk), lambda i, j: (j, i, 0, 0)),
        ],
        out_shape=[
            jax.ShapeDtypeStruct((b, s, 3 * GW), F32),
            jax.ShapeDtypeStruct((b, s, 3 * GW), BF16),
            jax.ShapeDtypeStruct((b, GW, s), BF16),
            jax.ShapeDtypeStruct((b, s, GW), BF16),
            jax.ShapeDtypeStruct((b, s // tk, N_HEADS * V_ROWS, tk), BF16),
        ],
        scratch_shapes=[pltpu.VMEM((tm, GW), F32)],
        compiler_params=_params("arbitrary", "arbitrary"),
        name="norm_proj",
    )(x, g, w, rope_c, rope_d)


def _local_kernel(u_ref, dww_ref, dwb_ref, lng_ref, lnb_ref, pww_ref, plw_ref, pls_ref,
                  y_ref, zbuf, pbuf, shift_ref, *, ts):
    si = pl.program_id(1)

    @pl.when(si == 0)
    def _():
        zbuf[0:HALO, :] = jnp.zeros((HALO, GW), F32)
        pbuf[0:HALO, :] = jnp.zeros((HALO, GW), F32)

    a = u_ref[0, :, 0:GW]
    b = u_ref[0, :, GW:2 * GW]
    zbuf[HALO:HALO + ts, :] = a * jax.nn.sigmoid(b)
    pbuf[HALO:HALO + ts, :] = u_ref[0, :, 2 * GW:3 * GW]

    def by_phase(buf, shifted, offsets, term):
        total = None
        for phase in range(SUBLANES):
            offs = [o for o in offsets if o % SUBLANES == phase]
            if not offs:
                continue
            src = buf
            if phase:
                n = max(offs) - phase + ts
                shifted[phase, 0:n, :] = buf[phase:phase + n, :]
                src = shifted.at[phase]
            for o in offs:
                t = term(o, src[o - phase:o - phase + ts, :])
                total = t if total is None else total + t
        return total

    base = HALO - (CONV_WIDTH - 1)
    acc = by_phase(zbuf, shift_ref.at[0], range(base, base + CONV_WIDTH),
                   lambda o, rows: dww_ref[o - base:o - base + 1, :] * rows)
    z = acc + dwb_ref[...]
    mu = jnp.mean(z, axis=-1, keepdims=True)
    zc = z - mu
    var = jnp.mean(zc * zc, axis=-1, keepdims=True)
    zn = zc * lax.rsqrt(var + NORM_EPS) * lng_ref[...] + lnb_ref[...]
    zs = zn * jax.nn.sigmoid(zn)
    y_ref[0, :, 0:GW] = jnp.dot(zs.astype(BF16), pww_ref[...], preferred_element_type=F32).astype(BF16)

    lane = lax.broadcasted_iota(jnp.int32, (ts, GW), 1)
    pool_group = GW // len(POOL_WINDOWS)
    win = jnp.zeros((ts, GW), jnp.int32)
    for gi, w in enumerate(POOL_WINDOWS):
        win = jnp.where(lane >= gi * pool_group, w, win)
    p0 = pbuf[HALO:HALO + ts, :]
    lag = lambda o: HALO - o
    tot = by_phase(pbuf, shift_ref.at[1], range(HALO - max(POOL_WINDOWS) + 1, HALO + 1),
                   lambda o, rows: rows if lag(o) < min(POOL_WINDOWS) else jnp.where(win > lag(o), rows, 0.0))
    pos = si * ts + lax.broadcasted_iota(jnp.int32, (ts, GW), 0)
    cnt = jnp.minimum(pos + 1, win).astype(F32)
    pooled = tot / cnt - p0
    yb = jnp.dot(pooled.astype(BF16), plw_ref[...], preferred_element_type=F32) * pls_ref[...]
    y_ref[0, :, GW:2 * GW] = yb.astype(BF16)

    zbuf[0:HALO, :] = zbuf[ts:ts + HALO, :]
    pbuf[0:HALO, :] = pbuf[ts:ts + HALO, :]


def _local_mixers(uab, dww, dwb, lng, lnb, pww, plw, pls, ts):
    b, s, _ = uab.shape
    vec = lambda: pl.BlockSpec((1, GW), lambda i, j: (0, 0))
    mat = lambda: pl.BlockSpec((GW, GW), lambda i, j: (0, 0))
    return pl.pallas_call(
        functools.partial(_local_kernel, ts=ts),
        grid=(b, s // ts),
        in_specs=[
            pl.BlockSpec((1, ts, 3 * GW), lambda i, j: (i, j, 0)),
            pl.BlockSpec((HALO, GW), lambda i, j: (0, 0)),
            vec(), vec(), vec(), mat(), mat(), vec(),
        ],
        out_specs=pl.BlockSpec((1, ts, 2 * GW), lambda i, j: (i, j, 0)),
        out_shape=jax.ShapeDtypeStruct((b, s, 2 * GW), BF16),
        scratch_shapes=[
            pltpu.VMEM((HALO + ts, GW), F32),
            pltpu.VMEM((HALO + ts, GW), F32),
            pltpu.VMEM((2, SUBLANES, HALO + ts, GW), F32),
        ],
        compiler_params=_params("arbitrary", "arbitrary"),
        name="local_mixers",
    )(uab, dww, dwb, lng, lnb, pww, plw, pls)


def _band_attn_kernel(q_ref, k_ref, v_ref, o_ref, lse_ref, s_ref):
    row = lax.broadcasted_iota(jnp.int32, (BAND, 2 * BAND), 0)
    col = lax.broadcasted_iota(jnp.int32, (BAND, 2 * BAND), 1)
    halves = [_lanes_in((BAND, LANES), sub * HEAD_DIM, HEAD_DIM) for sub in range(LANES // HEAD_DIM)]

    def block(n, carry):
        q0 = pl.multiple_of(n * BAND, BAND)
        k0 = pl.multiple_of(jnp.maximum(n - 1, 0) * BAND, BAND)
        q = q_ref[0, pl.ds(q0, BAND), :].astype(F32)
        kw = k_ref[0, pl.ds(k0, 2 * BAND), :]
        dist = (row - col) + (q0 - k0)
        mask = (dist >= 0) & (dist <= BAND)
        for h in range(N_HEADS):
            qh = jnp.where(_lanes_in((BAND, GW), h * HEAD_DIM, HEAD_DIM), q, 0.0).astype(BF16)
            s = lax.dot_general(qh, kw, NT_DIMS, preferred_element_type=F32)
            s_ref[h] = jnp.where(mask, s, -jnp.inf)
        for tile in range(GW // LANES):
            vw = v_ref[0, pl.ds(k0, 2 * BAND), tile * LANES:(tile + 1) * LANES]
            o = jnp.zeros((BAND, LANES), F32)
            lse = jnp.zeros((BAND, LANES), F32)
            for sub in range(LANES // HEAD_DIM):
                s = s_ref[tile * (LANES // HEAD_DIM) + sub]
                m = jnp.max(s, axis=-1, keepdims=True)
                p = jnp.exp(s - m)
                l = jnp.sum(p, axis=-1, keepdims=True)
                acc = jnp.dot(p.astype(BF16), vw, preferred_element_type=F32)
                mine = halves[sub]
                o = jnp.where(mine, acc / l, o)
                lse = jnp.where(mine, m + jnp.log(l), lse)
            o_ref[0, pl.ds(q0, BAND), tile * LANES:(tile + 1) * LANES] = o.astype(o_ref.dtype)
            lse_ref[0, pl.ds(q0, BAND), tile * LANES:(tile + 1) * LANES] = lse
        return carry

    lax.fori_loop(0, q_ref.shape[1] // BAND, block, 0, unroll=2)


def _band_attn(qkv, dil):
    b, s, _ = qkv.shape
    l = s // dil
    assert l >= 2 * BAND and l % BAND == 0
    part = lambda c: pl.BlockSpec((1, l, GW), lambda i, r: (i, 0, 3 * r + c))
    out = pl.BlockSpec((1, l, GW), lambda i, r: (i, 0, r))
    o, lse = pl.pallas_call(
        _band_attn_kernel,
        grid=(b, dil),
        in_specs=[part(0), part(1), part(2)],
        out_specs=[out, out],
        out_shape=[jax.ShapeDtypeStruct((b, l, dil * GW), BF16), jax.ShapeDtypeStruct((b, l, dil * GW), F32)],
        scratch_shapes=[pltpu.VMEM((N_HEADS, BAND, 2 * BAND), F32)],
        compiler_params=_params("arbitrary", "arbitrary"),
        name="band_attn",
    )(*[qkv.reshape(b, l, dil * 3 * GW)] * 3)
    return o.reshape(b, s, GW), lse.reshape(b, s, GW)


def _diff_attn_kernel(lam_ref, g_ref, qt_ref, k_ref, vt_ref, o_ref, qm_ref, m_ref, acc_ref, yt_ref,
                      st_ref, mx_ref, *, tq, lam_init):
    n_chain = 2 * N_HEADS
    i = pl.program_id(1)
    @pl.when((pl.program_id(0) == 0) & (i == 0))
    def _():
        qm_ref[...] = jnp.zeros(qm_ref.shape, BF16)

    for c in range(n_chain):
        qm_ref[c, c * QK_DIM:(c + 1) * QK_DIM, :] = qt_ref[0, c * QK_DIM:(c + 1) * QK_DIM, :]
    m_ref[...] = jnp.full(m_ref.shape, -jnp.inf, F32)
    acc_ref[...] = jnp.zeros(acc_ref.shape, F32)
    tk = st_ref.shape[1]
    ratio = tq // tk
    ahead = (lax.broadcasted_iota(jnp.int32, (tk, tq), 0)
             - lax.broadcasted_iota(jnp.int32, (tk, tq), 1))

    def score(c, kb, diag):
        st = jnp.dot(kb, qm_ref[c], preferred_element_type=F32)
        if diag is not None:
            st = jnp.where(ahead <= -diag * tk, st, -jnp.inf)
        st_ref[c] = st
        mx_ref[c] = jnp.max(st, axis=0, keepdims=True)

    def value(c, j):
        h = c % N_HEADS
        m_prev = m_ref[c]
        m_next = jnp.maximum(m_prev, mx_ref[c])
        alpha = jnp.exp2(m_prev - m_next)
        p = jnp.exp2(st_ref[c] - m_next).astype(BF16)
        vt = vt_ref[0, j, h * V_ROWS:(h + 1) * V_ROWS, :]
        acc_ref[c] = alpha * acc_ref[c] + jnp.dot(vt, p, preferred_element_type=F32)
        m_ref[c] = m_next

    def k_block(j):
        return k_ref[0, pl.ds(pl.multiple_of(j * tk, tk), tk), :]

    def scores(j, diag):
        kb = k_block(j)
        for c in range(n_chain):
            score(c, kb, diag)

    def values(j):
        for c in range(n_chain):
            value(c, j)

    def values_then_scores(j, diag):
        kb = k_block(j + 1)
        for c in range(n_chain):
            value(c, j)
            score(c, kb, diag)

    before = ratio * i

    @pl.when(i == 0)
    def _():
        scores(0, 0)

    @pl.when(i > 0)
    def _():
        scores(0, None)

    def body(j, carry):
        values_then_scores(j, None)
        return carry

    lax.fori_loop(0, before - 1, body, 0)

    @pl.when(i > 0)
    def _():
        values_then_scores(before - 1, 0)

    for diag in range(1, ratio):
        values_then_scores(before + diag - 1, diag)
    values(before + ratio - 1)

    lv = lam_ref[...]
    lam = (jnp.exp(jnp.sum(lv[0:1] * lv[1:2], axis=-1, keepdims=True))
           - jnp.exp(jnp.sum(lv[2:3] * lv[3:4], axis=-1, keepdims=True)) + lam_init)
    for h in range(N_HEADS):
        a1, a2 = acc_ref[h], acc_ref[N_HEADS + h]
        oh = (a1[:HEAD_DIM] / a1[HEAD_DIM:HEAD_DIM + 1]
              - lam * (a2[:HEAD_DIM] / a2[HEAD_DIM:HEAD_DIM + 1]))
        ms = jnp.mean(oh * oh, axis=0, keepdims=True)
        g = g_ref[h * HEAD_DIM:(h + 1) * HEAD_DIM, :]
        yt_ref[h * HEAD_DIM:(h + 1) * HEAD_DIM, :] = oh * lax.rsqrt(ms + NORM_EPS) * (g * (1.0 - lam_init))
    o_ref[0] = jnp.transpose(yt_ref[...]).astype(BF16)


def _diff_attn(qt, k, vt, lam_vecs, g_col, lam_init, tq):
    b, s, _ = k.shape
    nk, tk = vt.shape[1], vt.shape[3]
    assert tq % tk == 0 and s % tq == 0
    n_chain = 2 * N_HEADS
    return pl.pallas_call(
        functools.partial(_diff_attn_kernel, tq=tq, lam_init=lam_init),
        grid=(b, s // tq),
        in_specs=[
            pl.BlockSpec((4, QK_DIM), lambda i, j: (0, 0)),
            pl.BlockSpec((GW, 1), lambda i, j: (0, 0)),
            pl.BlockSpec((1, GW, tq), lambda i, j: (i, 0, j)),
            pl.BlockSpec((1, s, GW), lambda i, j: (i, 0, 0)),
            pl.BlockSpec((1, nk, N_HEADS * V_ROWS, tk), lambda i, j: (i, 0, 0, 0)),
        ],
        out_specs=pl.BlockSpec((1, tq, GW), lambda i, j: (i, j, 0)),
        out_shape=jax.ShapeDtypeStruct((b, s, GW), BF16),
        scratch_shapes=[
            pltpu.VMEM((n_chain, GW, tq), BF16),
            pltpu.VMEM((n_chain, 1, tq), F32),
            pltpu.VMEM((n_chain, V_ROWS, tq), F32),
            pltpu.VMEM((GW, tq), F32),
            pltpu.VMEM((n_chain, tk, tq), F32),
            pltpu.VMEM((n_chain, 1, tq), F32),
        ],
        compiler_params=_params("arbitrary", "arbitrary"),
        name="diff_attn",
    )(lam_vecs, g_col, qt, k, vt)


def _out_proj_kernel(x_ref, yab_ref, o1_ref, o2_ref, o3_ref, l1_ref, l2_ref, l3_ref, yd_ref, w_ref, out_ref):
    l1, l2, l3 = l1_ref[...], l2_ref[...], l3_ref[...]
    m = jnp.maximum(jnp.maximum(l1, l2), l3)
    w1, w2, w3 = jnp.exp(l1 - m), jnp.exp(l2 - m), jnp.exp(l3 - m)
    yc = (w1 * o1_ref[...] + w2 * o2_ref[...] + w3 * o3_ref[...]) / (w1 + w2 + w3)
    acc = jnp.dot(yab_ref[...], w_ref[0:2 * GW, :], preferred_element_type=F32)
    acc = acc + jnp.dot(yc.astype(BF16), w_ref[2 * GW:3 * GW, :], preferred_element_type=F32)
    acc = acc + jnp.dot(yd_ref[...], w_ref[3 * GW:4 * GW, :], preferred_element_type=F32)
    out_ref[...] = x_ref[...] + acc


def _out_proj(x, yab, os_, lses, yd, w, tm):
    t, d = x.shape
    rows = lambda c: pl.BlockSpec((tm, c), lambda i: (i, 0))
    return pl.pallas_call(
        _out_proj_kernel,
        grid=(t // tm,),
        in_specs=[rows(d), rows(2 * GW)] + [rows(GW)] * 6 + [rows(GW), pl.BlockSpec((d, d), lambda i: (0, 0))],
        out_specs=rows(d),
        out_shape=jax.ShapeDtypeStruct((t, d), F32),
        compiler_params=_params("arbitrary"),
        name="out_proj",
    )(x, yab, *os_, *lses, yd, w)


def _ffn_kernel(x_ref, g_ref, wg_ref, wu_ref, wd_ref, out_ref, acc_ref, *, tf):
    hn = _rms(x_ref[...], g_ref[...]).astype(BF16)
    acc_ref[...] = x_ref[...]
    for f0 in range(0, wg_ref.shape[1], tf):
        gate = jnp.dot(hn, wg_ref[:, f0:f0 + tf], preferred_element_type=F32)
        up = jnp.dot(hn, wu_ref[:, f0:f0 + tf], preferred_element_type=F32)
        act = (gate * jax.nn.sigmoid(gate) * up).astype(BF16)
        acc_ref[...] += jnp.dot(act, wd_ref[f0:f0 + tf, :], preferred_element_type=F32)
    out_ref[...] = acc_ref[...]


def _ffn(x, g, wg, wu, wd, tm, tf):
    t, d = x.shape
    ff = wg.shape[1]
    assert ff % tf == 0
    whole = lambda shape: pl.BlockSpec(shape, lambda i: (0, 0))
    return pl.pallas_call(
        functools.partial(_ffn_kernel, tf=tf),
        grid=(t // tm,),
        in_specs=[pl.BlockSpec((tm, d), lambda i: (i, 0)), whole((1, d)), whole((d, ff)), whole((d, ff)),
                  whole((ff, d))],
        out_specs=pl.BlockSpec((tm, d), lambda i: (i, 0)),
        out_shape=jax.ShapeDtypeStruct((t, d), F32),
        scratch_shapes=[pltpu.VMEM((tm, d), F32)],
        compiler_params=_params("arbitrary"),
        name="ffn_dense",
    )(x, g, wg, wu, wd)


META_E1, META_E2, META_RANK1, META_RANK2, META_G1, META_G2 = range(6)


def _router_kernel(x_ref, g_ref, r_ref, hn_ref, meta_ref, after_ref, carry_ref, *, blocks_per_chunk):
    @pl.when(pl.program_id(0) % blocks_per_chunk == 0)
    def _():
        carry_ref[...] = jnp.zeros_like(carry_ref)

    h = _rms(x_ref[...], g_ref[...])
    hn_ref[...] = h.astype(BF16)
    logits = jnp.dot(h, r_ref[...], preferred_element_type=F32, precision=lax.Precision.HIGHEST)
    tm = logits.shape[0]
    lane = lax.broadcasted_iota(jnp.int32, logits.shape, 1)
    logits = jnp.where(lane < N_EXPERTS, logits, -jnp.inf)
    m1 = jnp.max(logits, axis=-1, keepdims=True)
    i1 = jnp.min(jnp.where(logits == m1, lane, LANES), axis=-1, keepdims=True)
    rest = jnp.where(lane == i1, -jnp.inf, logits)
    m2 = jnp.max(rest, axis=-1, keepdims=True)
    i2 = jnp.min(jnp.where(rest == m2, lane, LANES), axis=-1, keepdims=True)
    e2 = jnp.exp(m2 - m1)
    g1 = 1.0 / (1.0 + e2)
    g2 = e2 / (1.0 + e2)

    oh1 = lane == i1
    oh2 = lane == i2
    both = jnp.where(oh1, 1.0, jnp.where(oh2, 1.0, 0.0))
    row = lax.broadcasted_iota(jnp.int32, (tm, tm), 0)
    col = lax.broadcasted_iota(jnp.int32, (tm, tm), 1)
    earlier = jnp.where(col < row, 1.0, 0.0).astype(BF16)
    carry = carry_ref[0:1, :]
    prefix = jnp.dot(earlier, both.astype(BF16), preferred_element_type=F32) + carry
    rank1 = jnp.sum(jnp.where(oh1, prefix, 0.0), axis=-1, keepdims=True)
    rank2 = jnp.sum(jnp.where(oh2, prefix, 0.0), axis=-1, keepdims=True)
    total = carry + jnp.sum(both, axis=0, keepdims=True)

    meta = jnp.zeros(logits.shape, F32)
    for k, val in ((META_E1, i1.astype(F32)), (META_E2, i2.astype(F32)), (META_RANK1, rank1),
                   (META_RANK2, rank2), (META_G1, g1), (META_G2, g2)):
        meta = jnp.where(lane == k, val, meta)
    meta_ref[...] = meta
    after_ref[0] = jnp.broadcast_to(total, after_ref.shape[1:])
    carry_ref[...] = jnp.broadcast_to(total, carry_ref.shape)


def _router(x, g, r_pad, tm, blocks_per_chunk):
    t, d = x.shape
    nblk = t // tm
    counts = pl.BlockSpec((1, 8, LANES), lambda i: (i, 0, 0))
    return pl.pallas_call(
        functools.partial(_router_kernel, blocks_per_chunk=blocks_per_chunk),
        grid=(nblk,),
        in_specs=[
            pl.BlockSpec((tm, d), lambda i: (i, 0)),
            pl.BlockSpec((1, d), lambda i: (0, 0)),
            pl.BlockSpec((d, LANES), lambda i: (0, 0)),
        ],
        out_specs=[
            pl.BlockSpec((tm, d), lambda i: (i, 0)),
            pl.BlockSpec((tm, LANES), lambda i: (i, 0)),
            counts,
        ],
        out_shape=[
            jax.ShapeDtypeStruct((t, d), BF16),
            jax.ShapeDtypeStruct((t, LANES), F32),
            jax.ShapeDtypeStruct((nblk, 8, LANES), F32),
        ],
        scratch_shapes=[pltpu.VMEM((8, LANES), F32)],
        compiler_params=_params("arbitrary"),
        name="router",
    )(x, g, r_pad)


SLOT_ROW1, SLOT_ROW2, SLOT_G1, SLOT_G2 = range(4)


def _moe_kernel(nt_ref, toff_ref, cuts_ref, hn_ref, slotc_ref, slotr_ref, wg_ref, wu_ref, wd_ref, out_ref,
                xs_ref, ys_ref, *, tr, sub, win):
    c, e, f = pl.program_id(0), pl.program_id(1), pl.program_id(2)
    last_f = pl.num_programs(2) - 1
    grp = c * N_EXPERTS + e
    n_tiles = nt_ref[grp]
    row0 = toff_ref[grp] * tr
    tc = hn_ref.shape[0]
    nsub = tc // sub

    @pl.when((e == 0) & (f == 0))
    def _():
        out_ref[...] = jnp.zeros_like(out_ref)

    def rows_of(r):
        return pl.ds(pl.multiple_of(r * tr, tr), tr)

    def token_window(r):
        base = grp * nsub
        first = r * tr
        last = jnp.minimum(first + tr, cuts_ref[base + nsub - 1]) - 1
        lo = hi = 0
        for k in range(nsub - 1):
            cut = cuts_ref[base + k]
            lo = lo + (cut <= first).astype(jnp.int32)
            hi = hi + (cut <= last).astype(jnp.int32)
        start = jnp.minimum(lo, nsub - win)
        return hi < start + win, pl.multiple_of(start * sub, sub)

    def spans(r, fn):
        covered, tok0 = token_window(r)

        @pl.when(covered)
        def _():
            fn(r, tok0, win * sub)

        @pl.when(jnp.logical_not(covered))
        def _():
            fn(r, 0, tc)

    def gather_span(r, tok0, ntok):
        toks = pl.ds(tok0, ntok)
        target = (lax.broadcasted_iota(jnp.int32, (tr, ntok), 0) + (row0 + r * tr)).astype(F32)
        hit1 = slotr_ref[SLOT_ROW1:SLOT_ROW1 + 1, toks] == target
        hit2 = slotr_ref[SLOT_ROW2:SLOT_ROW2 + 1, toks] == target
        oh = jnp.where(hit1, 1.0, jnp.where(hit2, 1.0, 0.0)).astype(BF16)
        xs_ref[rows_of(r), :] = jnp.dot(oh, hn_ref[toks, :], preferred_element_type=F32).astype(BF16)

    def gather_tile(r, carry):
        spans(r, gather_span)
        return carry

    def ffn_tiles(r, count, first):
        rows = pl.ds(pl.multiple_of(r * tr, tr), count * tr)
        xt = xs_ref[rows, :]
        gate = jnp.dot(xt, wg_ref[...], preferred_element_type=F32)
        up = jnp.dot(xt, wu_ref[...], preferred_element_type=F32)
        act = (gate * jax.nn.sigmoid(gate) * up).astype(BF16)
        y = jnp.dot(act, wd_ref[...], preferred_element_type=F32)
        if first:
            ys_ref[rows, :] = y
        else:
            ys_ref[rows, :] += y

    def scatter_span(r, tok0, ntok):
        toks = pl.ds(tok0, ntok)
        yb = ys_ref[rows_of(r), :].astype(BF16)
        target = (lax.broadcasted_iota(jnp.int32, (ntok, tr), 1) + (row0 + r * tr)).astype(F32)
        hit1 = slotc_ref[toks, SLOT_ROW1:SLOT_ROW1 + 1] == target
        hit2 = slotc_ref[toks, SLOT_ROW2:SLOT_ROW2 + 1] == target
        gh = jnp.where(hit1, slotc_ref[toks, SLOT_G1:SLOT_G1 + 1],
                       jnp.where(hit2, slotc_ref[toks, SLOT_G2:SLOT_G2 + 1], 0.0)).astype(BF16)
        out_ref[toks, :] += jnp.dot(gh, yb, preferred_element_type=F32)

    def scatter_tile(r, carry):
        spans(r, scatter_span)
        return carry

    def ffn_all(first):
        def pair(t, carry):
            ffn_tiles(2 * t, 2, first)
            return carry

        lax.fori_loop(0, n_tiles // 2, pair, 0)

        @pl.when(n_tiles % 2 == 1)
        def _():
            ffn_tiles(n_tiles - 1, 1, first)

    @pl.when(f == 0)
    def _():
        lax.fori_loop(0, n_tiles, gather_tile, 0)
        ffn_all(True)

    @pl.when(f > 0)
    def _():
        ffn_all(False)

    @pl.when(f == last_f)
    def _():
        lax.fori_loop(0, n_tiles, scatter_tile, 0)


def _moe(hn, slot_cols, slot_rows, nt, toff, cuts, wg, wu, wd, tc, tr, tf, sub, win):
    t, d = hn.shape
    ne, _, ff = wg.shape
    grid_spec = pltpu.PrefetchScalarGridSpec(
        num_scalar_prefetch=3,
        grid=(t // tc, ne, ff // tf),
        in_specs=[
            pl.BlockSpec((tc, d), lambda c, e, f, *_: (c, 0)),
            pl.BlockSpec((tc, 4), lambda c, e, f, *_: (c, 0)),
            pl.BlockSpec((8, tc), lambda c, e, f, *_: (0, c)),
            pl.BlockSpec((None, d, tf), lambda c, e, f, *_: (e, 0, f)),
            pl.BlockSpec((None, d, tf), lambda c, e, f, *_: (e, 0, f)),
            pl.BlockSpec((None, tf, d), lambda c, e, f, *_: (e, f, 0)),
        ],
        out_specs=pl.BlockSpec((tc, d), lambda c, e, f, *_: (c, 0)),
        scratch_shapes=[
            pltpu.VMEM((tc + tr, d), BF16),
            pltpu.VMEM((tc + tr, d), F32),
        ],
    )
    return pl.pallas_call(
        functools.partial(_moe_kernel, tr=tr, sub=sub, win=win),
        grid_spec=grid_spec,
        out_shape=jax.ShapeDtypeStruct((t, d), F32),
        compiler_params=_params("arbitrary", "arbitrary", "arbitrary"),
        name="moe",
    )(nt, toff, cuts, hn, slot_cols, slot_rows, wg, wu, wd)


def _final_kernel(x_ref, y_ref, g_ref, out_ref):
    out_ref[...] = _rms(x_ref[...] + y_ref[...], g_ref[...])


def _final_norm(x, y, g, tm):
    t, d = x.shape
    rows = pl.BlockSpec((tm, d), lambda i: (i, 0))
    return pl.pallas_call(
        _final_kernel,
        grid=(t // tm,),
        in_specs=[rows, rows, pl.BlockSpec((1, d), lambda i: (0, 0))],
        out_specs=rows,
        out_shape=jax.ShapeDtypeStruct((t, d), F32),
        compiler_params=_params("arbitrary"),
        name="final_norm",
    )(x, y, g)


def _routed_swiglu(x, norm_g, router_w, wg, wu, wd, final_g, tc, tr, tf, sub, win):
    t, d = x.shape
    nc, nsub = t // tc, tc // sub
    r_pad = jnp.zeros((d, LANES), F32).at[:, :N_EXPERTS].set(router_w)
    hn, meta, after = _router(x, norm_g, r_pad, tm=sub, blocks_per_chunk=nsub)
    through = after[:, 0, :N_EXPERTS].astype(jnp.int32).reshape(nc, nsub, N_EXPERTS)
    cuts = through.transpose(0, 2, 1)
    total = through[:, -1]
    nt = (total + tr - 1) // tr
    toff = jnp.cumsum(nt, axis=1) - nt
    first_row = jnp.repeat((toff * tr).astype(F32), tc, axis=0)
    experts = jnp.arange(N_EXPERTS, dtype=F32)
    pick = lambda e: jnp.sum(jnp.where(e[:, None] == experts, first_row, 0.0), axis=1)
    row1 = pick(meta[:, META_E1]) + meta[:, META_RANK1]
    row2 = pick(meta[:, META_E2]) + meta[:, META_RANK2]
    slot_cols = jnp.stack([row1, row2, meta[:, META_G1], meta[:, META_G2]], axis=1)
    slot_rows = jnp.zeros((8, t), F32).at[0:4].set(slot_cols.T)
    y = _moe(hn, slot_cols, slot_rows, nt.reshape(-1), toff.reshape(-1), cuts.reshape(-1), wg, wu, wd,
             tc=tc, tr=tr, tf=tf, sub=sub, win=win)
    return _final_norm(x, y, final_g, tm=min(1024, t))


def _permute_w_in(w):
    def perm(block):
        d = block.shape[0]
        return block.reshape(d, N_HEADS, 2, QK_DIM).transpose(0, 2, 1, 3).reshape(d, GW)
    parts = [w[:, :6 * GW], perm(w[:, 6 * GW:7 * GW]), perm(w[:, 7 * GW:8 * GW]), w[:, 8 * GW:]]
    return jnp.concatenate(parts, axis=1)


def _block_diag(pool_w):
    n, p, _ = pool_w.shape
    out = jnp.zeros((n * p, n * p), pool_w.dtype)
    for gi in range(n):
        out = out.at[gi * p:(gi + 1) * p, gi * p:(gi + 1) * p].set(pool_w[gi])
    return out


def _token_mixing(x, layer, lam_init, p, rope_c, rope_d):
    b, s, d = x.shape
    row = lambda v: v.reshape(1, -1)
    w_in = _permute_w_in(p["w_in"][layer]).astype(BF16)
    uab, qkvc, qdt, kd, vdt = _norm_proj(x, row(p["norm1_g"][layer]), w_in, rope_c, rope_d, tm=512, tk=256)
    dww = jnp.zeros((HALO, GW), F32).at[:CONV_WIDTH].set(p["conv_dw_w"][layer])
    yab = _local_mixers(
        uab, dww, row(p["conv_dw_b"][layer]), row(p["conv_ln_g"][layer]), row(p["conv_ln_b"][layer]),
        p["conv_pw_w"][layer].astype(BF16), _block_diag(p["pool_w"][layer]).astype(BF16),
        row(p["pool_scale"][layer]), ts=512)
    os_, lses = [], []
    for dil in DILATIONS:
        o, lse = _band_attn(qkvc, dil)
        os_.append(o.reshape(b * s, GW))
        lses.append(lse.reshape(b * s, GW))
    yd = _diff_attn(qdt, kd, vdt, p["diff_lam"][layer], jnp.tile(p["diff_ln_g"][layer], N_HEADS).reshape(GW, 1),
                    lam_init, tq=512)
    return _out_proj(x.reshape(b * s, d), yab.reshape(b * s, 2 * GW), os_, lses, yd.reshape(b * s, GW),
                     p["w_out"][layer].astype(BF16), tm=512)


def kernel(x, norm1_g, w_in, conv_dw_w, conv_dw_b, conv_ln_g, conv_ln_b, conv_pw_w, pool_w, pool_scale,
           diff_lam, diff_ln_g, w_out, norm2_g, ffn_w_gate, ffn_w_up, ffn_w_down, moe_router, moe_w_gate,
           moe_w_up, moe_w_down, final_g):
    b, s, d = x.shape
    p = dict(norm1_g=norm1_g, w_in=w_in, conv_dw_w=conv_dw_w, conv_dw_b=conv_dw_b, conv_ln_g=conv_ln_g,
             conv_ln_b=conv_ln_b, conv_pw_w=conv_pw_w, pool_w=pool_w, pool_scale=pool_scale,
             diff_lam=diff_lam, diff_ln_g=diff_ln_g, w_out=w_out)
    rope_c = _rope_table(s, HEAD_DIM, HEAD_DIM // 4)
    rope_d = _rope_table(s, QK_DIM, QK_DIM // 4)
    row = lambda v: v.reshape(1, -1)

    x1 = _token_mixing(x, 0, 0.8 - 0.6 * math.exp(-0.3 * 0), p, rope_c, rope_d)
    x2 = _ffn(x1, row(norm2_g[0]), ffn_w_gate[0].astype(BF16), ffn_w_up[0].astype(BF16),
              ffn_w_down[0].astype(BF16), tm=512, tf=256)

    x3 = _token_mixing(x2.reshape(b, s, d), 1, 0.8 - 0.6 * math.exp(-0.3 * 1), p, rope_c, rope_d)
    out = _routed_swiglu(x3, row(norm2_g[1]), moe_router[0], moe_w_gate[0].astype(BF16),
                         moe_w_up[0].astype(BF16), moe_w_down[0].astype(BF16), row(final_g),
                         tc=2048, tr=256, tf=512, sub=256, win=5)
    return out.reshape(b, s, d)
```

```python
import functools
import math

import jax
import jax.numpy as jnp
from jax import lax
from jax.experimental import pallas as pl
from jax.experimental.pallas import tpu as pltpu

D_MODEL = 1024
GW = 256
N_HEADS = 4
HEAD_DIM = 64
QK_DIM = 32
V_ROWS = HEAD_DIM + 16
CONV_WIDTH = 31
POOL_WINDOWS = (2, 4, 8, 16)
DILATIONS = (1, 4, 16)
BAND = 128
ROPE_THETA = 500000.0
N_EXPERTS = 8
NORM_EPS = 1e-5
HALO = 32
LANES = 128
SUBLANES = 8
VMEM_LIMIT = 56 * 1024 * 1024

BF16 = jnp.bfloat16
F32 = jnp.float32
NT_DIMS = (((1,), (1,)), ((), ()))


def _params(*sem):
    return pltpu.CompilerParams(dimension_semantics=sem, vmem_limit_bytes=VMEM_LIMIT)


def _lanes_in(shape, start, width):
    lane = lax.broadcasted_iota(jnp.int32, shape, len(shape) - 1)
    return (lane >= start) & (lane < start + width)


def _rms(x, g):
    ms = jnp.mean(x * x, axis=-1, keepdims=True)
    return x * lax.rsqrt(ms + NORM_EPS) * g


def _rope_table(seq, width, rot):
    half = rot // 2
    pos = jnp.arange(seq, dtype=F32)
    inv = ROPE_THETA ** (-jnp.arange(0, rot, 2, dtype=F32) / rot)
    ang = pos[:, None] * inv[None, :]
    cos, sin = jnp.cos(ang), jnp.sin(ang)
    lane = jnp.arange(GW) % width
    f = lane % half
    first = lane < half
    second = (lane >= half) & (lane < rot)
    c = jnp.where((lane < rot)[None, :], cos[:, f], 1.0)
    sn = jnp.where(first[None, :], -sin[:, f], 0.0)
    sp = jnp.where(second[None, :], sin[:, f], 0.0)
    return jnp.concatenate([c, sn, sp], axis=1)


def _rope(u, tab_ref, half):
    return (u * tab_ref[:, 0:GW]
            + pltpu.roll(u, GW - half, 1) * tab_ref[:, GW:2 * GW]
            + pltpu.roll(u, half, 1) * tab_ref[:, 2 * GW:3 * GW])


def _norm_proj_kernel(x_ref, g_ref, w_ref, rc_ref, rd_ref, uab_ref, qkvc_ref, qt_ref, kd_ref, vt_ref, v_ref,
                      *, tk):
    h = _rms(x_ref[0], g_ref[...]).astype(BF16)

    def proj(grp):
        return jnp.dot(h, w_ref[:, grp * GW:(grp + 1) * GW], preferred_element_type=F32)

    for grp in range(3):
        uab_ref[0, :, grp * GW:(grp + 1) * GW] = proj(grp)
    c_scale = HEAD_DIM ** -0.5
    d_scale = QK_DIM ** -0.5 * math.log2(math.e)
    qkvc_ref[0, :, 0:GW] = (_rope(proj(3), rc_ref, 8) * c_scale).astype(BF16)
    qkvc_ref[0, :, GW:2 * GW] = _rope(proj(4), rc_ref, 8).astype(BF16)
    qkvc_ref[0, :, 2 * GW:3 * GW] = proj(5).astype(BF16)
    qt_ref[0] = jnp.transpose(_rope(proj(6), rd_ref, 4) * d_scale).astype(BF16)
    kd_ref[0] = _rope(proj(7), rd_ref, 4).astype(BF16)
    v_ref[...] = proj(8)
    vt = jnp.transpose(v_ref[...]).astype(BF16)
    ones = jnp.ones((V_ROWS - HEAD_DIM, tk), BF16)
    for blk in range(vt.shape[1] // tk):
        for hd in range(N_HEADS):
            vt_ref[0, blk, hd * V_ROWS:hd * V_ROWS + HEAD_DIM, :] = (
                vt[hd * HEAD_DIM:(hd + 1) * HEAD_DIM, blk * tk:(blk + 1) * tk])
            vt_ref[0, blk, hd * V_ROWS + HEAD_DIM:(hd + 1) * V_ROWS, :] = ones


def _norm_proj(x, g, w, rope_c, rope_d, tm, tk):
    b, s, d = x.shape
    grid = (s // tm, b)
    return pl.pallas_call(
        functools.partial(_norm_proj_kernel, tk=tk),
        grid=grid,
        in_specs=[
            pl.BlockSpec((1, tm, d), lambda i, j: (j, i, 0)),
            pl.BlockSpec((1, d), lambda i, j: (0, 0)),
            pl.BlockSpec((d, 9 * GW), lambda i, j: (0, 0)),
            pl.BlockSpec((tm, 3 * GW), lambda i, j: (i, 0)),
            pl.BlockSpec((tm, 3 * GW), lambda i, j: (i, 0)),
        ],
        out_specs=[
            pl.BlockSpec((1, tm, 3 * GW), lambda i, j: (j, i, 0)),
            pl.BlockSpec((1, tm, 3 * GW), lambda i, j: (j, i, 0)),
            pl.BlockSpec((1, GW, tm), lambda i, j: (j, 0, i)),
            pl.BlockSpec((1, tm, GW), lambda i, j: (j, i, 0)),
            pl.BlockSpec((1, tm // tk, N_HEADS * V_ROWS, tk), lambda i, j: (j, i, 0, 0)),
        ],
        out_shape=[
            jax.ShapeDtypeStruct((b, s, 3 * GW), F32),
            jax.ShapeDtypeStruct((b, s, 3 * GW), BF16),
            jax.ShapeDtypeStruct((b, GW, s), BF16),
            jax.ShapeDtypeStruct((b, s, GW), BF16),
            jax.ShapeDtypeStruct((b, s // tk, N_HEADS * V_ROWS, tk), BF16),
        ],
        scratch_shapes=[pltpu.VMEM((tm, GW), F32)],
        compiler_params=_params("arbitrary", "arbitrary"),
        name="norm_proj",
    )(x, g, w, rope_c, rope_d)


def _local_kernel(u_ref, dww_ref, dwb_ref, lng_ref, lnb_ref, pww_ref, plw_ref, pls_ref,
                  y_ref, zbuf, pbuf, shift_ref, *, ts):
    si = pl.program_id(1)

    @pl.when(si == 0)
    def _():
        zbuf[0:HALO, :] = jnp.zeros((HALO, GW), F32)
        pbuf[0:HALO, :] = jnp.zeros((HALO, GW), F32)

    a = u_ref[0, :, 0:GW]
    b = u_ref[0, :, GW:2 * GW]
    zbuf[HALO:HALO + ts, :] = a * jax.nn.sigmoid(b)
    pbuf[HALO:HALO + ts, :] = u_ref[0, :, 2 * GW:3 * GW]

    def by_phase(buf, shifted, offsets, term):
        total = None
        for phase in range(SUBLANES):
            offs = [o for o in offsets if o % SUBLANES == phase]
            if not offs:
                continue
            src = buf
            if phase:
                n = max(offs) - phase + ts
                shifted[phase, 0:n, :] = buf[phase:phase + n, :]
                src = shifted.at[phase]
            for o in offs:
                t = term(o, src[o - phase:o - phase + ts, :])
                total = t if total is None else total + t
        return total

    base = HALO - (CONV_WIDTH - 1)
    acc = by_phase(zbuf, shift_ref.at[0], range(base, base + CONV_WIDTH),
                   lambda o, rows: dww_ref[o - base:o - base + 1, :] * rows)
    z = acc + dwb_ref[...]
    mu = jnp.mean(z, axis=-1, keepdims=True)
    zc = z - mu
    var = jnp.mean(zc * zc, axis=-1, keepdims=True)
    zn = zc * lax.rsqrt(var + NORM_EPS) * lng_ref[...] + lnb_ref[...]
    zs = zn * jax.nn.sigmoid(zn)
    y_ref[0, :, 0:GW] = jnp.dot(zs.astype(BF16), pww_ref[...], preferred_element_type=F32).astype(BF16)

    lane = lax.broadcasted_iota(jnp.int32, (ts, GW), 1)
    pool_group = GW // len(POOL_WINDOWS)
    win = jnp.zeros((ts, GW), jnp.int32)
    for gi, w in enumerate(POOL_WINDOWS):
        win = jnp.where(lane >= gi * pool_group, w, win)
    p0 = pbuf[HALO:HALO + ts, :]
    lag = lambda o: HALO - o
    tot = by_phase(pbuf, shift_ref.at[1], range(HALO - max(POOL_WINDOWS) + 1, HALO + 1),
                   lambda o, rows: rows if lag(o) < min(POOL_WINDOWS) else jnp.where(win > lag(o), rows, 0.0))
    pos = si * ts + lax.broadcasted_iota(jnp.int32, (ts, GW), 0)
    cnt = jnp.minimum(pos + 1, win).astype(F32)
    pooled = tot / cnt - p0
    yb = jnp.dot(pooled.astype(BF16), plw_ref[...], preferred_element_type=F32) * pls_ref[...]
    y_ref[0, :, GW:2 * GW] = yb.astype(BF16)

    zbuf[0:HALO, :] = zbuf[ts:ts + HALO, :]
    pbuf[0:HALO, :] = pbuf[ts:ts + HALO, :]


def _local_mixers(uab, dww, dwb, lng, lnb, pww, plw, pls, ts):
    b, s, _ = uab.shape
    vec = lambda: pl.BlockSpec((1, GW), lambda i, j: (0, 0))
    mat = lambda: pl.BlockSpec((GW, GW), lambda i, j: (0, 0))
    return pl.pallas_call(
        functools.partial(_local_kernel, ts=ts),
        grid=(b, s // ts),
        in_specs=[
            pl.BlockSpec((1, ts, 3 * GW), lambda i, j: (i, j, 0)),
            pl.BlockSpec((HALO, GW), lambda i, j: (0, 0)),
            vec(), vec(), vec(), mat(), mat(), vec(),
        ],
        out_specs=pl.BlockSpec((1, ts, 2 * GW), lambda i, j: (i, j, 0)),
        out_shape=jax.ShapeDtypeStruct((b, s, 2 * GW), BF16),
        scratch_shapes=[
            pltpu.VMEM((HALO + ts, GW), F32),
            pltpu.VMEM((HALO + ts, GW), F32),
            pltpu.VMEM((2, SUBLANES, HALO + ts, GW), F32),
        ],
        compiler_params=_params("arbitrary", "arbitrary"),
        name="local_mixers",
    )(uab, dww, dwb, lng, lnb, pww, plw, pls)


def _band_attn_kernel(q_ref, k_ref, v_ref, o_ref, lse_ref, s_ref):
    row = lax.broadcasted_iota(jnp.int32, (BAND, 2 * BAND), 0)
    col = lax.broadcasted_iota(jnp.int32, (BAND, 2 * BAND), 1)
    halves = [_lanes_in((BAND, LANES), sub * HEAD_DIM, HEAD_DIM) for sub in range(LANES // HEAD_DIM)]

    def starts(n):
        return pl.multiple_of(n * BAND, BAND), pl.multiple_of(jnp.maximum(n - 1, 0) * BAND, BAND)

    def scores(n, slot):
        q0, k0 = starts(n)
        q = q_ref[0, pl.ds(q0, BAND), :].astype(F32)
        kw = k_ref[0, pl.ds(k0, 2 * BAND), :]
        dist = (row - col) + (q0 - k0)
        mask = (dist >= 0) & (dist <= BAND)
        for h in range(N_HEADS):
            qh = jnp.where(_lanes_in((BAND, GW), h * HEAD_DIM, HEAD_DIM), q, 0.0).astype(BF16)
            s = lax.dot_general(qh, kw, NT_DIMS, preferred_element_type=F32)
            s_ref[slot * N_HEADS + h] = jnp.where(mask, s, -jnp.inf)

    def values(n, slot):
        q0, k0 = starts(n)
        for tile in range(GW // LANES):
            vw = v_ref[0, pl.ds(k0, 2 * BAND), tile * LANES:(tile + 1) * LANES]
            o = jnp.zeros((BAND, LANES), F32)
            lse = jnp.zeros((BAND, LANES), F32)
            for sub in range(LANES // HEAD_DIM):
                s = s_ref[slot * N_HEADS + tile * (LANES // HEAD_DIM) + sub]
                m = jnp.max(s, axis=-1, keepdims=True)
                p = jnp.exp(s - m)
                l = jnp.sum(p, axis=-1, keepdims=True)
                acc = jnp.dot(p.astype(BF16), vw, preferred_element_type=F32)
                mine = halves[sub]
                o = jnp.where(mine, acc / l, o)
                lse = jnp.where(mine, m + jnp.log(l), lse)
            o_ref[0, pl.ds(q0, BAND), tile * LANES:(tile + 1) * LANES] = o.astype(o_ref.dtype)
            lse_ref[0, pl.ds(q0, BAND), tile * LANES:(tile + 1) * LANES] = lse

    def pair(t, carry):
        scores(2 * t, 0)
        scores(2 * t + 1, 1)
        values(2 * t, 0)
        values(2 * t + 1, 1)
        return carry

    lax.fori_loop(0, q_ref.shape[1] // (2 * BAND), pair, 0)


def _band_attn(qkv, dil):
    b, s, _ = qkv.shape
    l = s // dil
    assert l % (2 * BAND) == 0
    part = lambda c: pl.BlockSpec((1, l, GW), lambda i, r: (i, 0, 3 * r + c))
    out = pl.BlockSpec((1, l, GW), lambda i, r: (i, 0, r))
    o, lse = pl.pallas_call(
        _band_attn_kernel,
        grid=(b, dil),
        in_specs=[part(0), part(1), part(2)],
        out_specs=[out, out],
        out_shape=[jax.ShapeDtypeStruct((b, l, dil * GW), BF16), jax.ShapeDtypeStruct((b, l, dil * GW), F32)],
        scratch_shapes=[pltpu.VMEM((2 * N_HEADS, BAND, 2 * BAND), F32)],
        compiler_params=_params("arbitrary", "arbitrary"),
        name="band_attn",
    )(*[qkv.reshape(b, l, dil * 3 * GW)] * 3)
    return o.reshape(b, s, GW), lse.reshape(b, s, GW)


def _diff_attn_kernel(lam_ref, g_ref, qt_ref, k_ref, vt_ref, o_ref, qm_ref, m_ref, acc_ref, yt_ref,
                      st_ref, mx_ref, *, tq, lam_init):
    n_chain = 2 * N_HEADS
    i = pl.program_id(1)
    @pl.when((pl.program_id(0) == 0) & (i == 0))
    def _():
        qm_ref[...] = jnp.zeros(qm_ref.shape, BF16)

    for c in range(n_chain):
        qm_ref[c, c * QK_DIM:(c + 1) * QK_DIM, :] = qt_ref[0, c * QK_DIM:(c + 1) * QK_DIM, :]
    m_ref[...] = jnp.full(m_ref.shape, -jnp.inf, F32)
    acc_ref[...] = jnp.zeros(acc_ref.shape, F32)
    tk = st_ref.shape[1]
    ratio = tq // tk
    ahead = (lax.broadcasted_iota(jnp.int32, (tk, tq), 0)
             - lax.broadcasted_iota(jnp.int32, (tk, tq), 1))

    def score(c, kb, diag):
        st = jnp.dot(kb, qm_ref[c], preferred_element_type=F32)
        if diag is not None:
            st = jnp.where(ahead <= -diag * tk, st, -jnp.inf)
        st_ref[c] = st
        mx_ref[c] = jnp.max(st, axis=0, keepdims=True)

    def value(c, j):
        h = c % N_HEADS
        m_prev = m_ref[c]
        m_next = jnp.maximum(m_prev, mx_ref[c])
        alpha = jnp.exp2(m_prev - m_next)
        p = jnp.exp2(st_ref[c] - m_next).astype(BF16)
        vt = vt_ref[0, j, h * V_ROWS:(h + 1) * V_ROWS, :]
        acc_ref[c] = alpha * acc_ref[c] + jnp.dot(vt, p, preferred_element_type=F32)
        m_ref[c] = m_next

    def k_block(j):
        return k_ref[0, pl.ds(pl.multiple_of(j * tk, tk), tk), :]

    def scores(j, diag):
        kb = k_block(j)
        for c in range(n_chain):
            score(c, kb, diag)

    def values(j):
        for c in range(n_chain):
            value(c, j)

    def values_then_scores(j, diag):
        kb = k_block(j + 1)
        for c in range(n_chain):
            value(c, j)
            score(c, kb, diag)

    before = ratio * i

    @pl.when(i == 0)
    def _():
        scores(0, 0)

    @pl.when(i > 0)
    def _():
        scores(0, None)

    def body(j, carry):
        values_then_scores(j, None)
        return carry

    lax.fori_loop(0, before - 1, body, 0)

    @pl.when(i > 0)
    def _():
        values_then_scores(before - 1, 0)

    for diag in range(1, ratio):
        values_then_scores(before + diag - 1, diag)
    values(before + ratio - 1)

    lv = lam_ref[...]
    lam = (jnp.exp(jnp.sum(lv[0:1] * lv[1:2], axis=-1, keepdims=True))
           - jnp.exp(jnp.sum(lv[2:3] * lv[3:4], axis=-1, keepdims=True)) + lam_init)
    for h in range(N_HEADS):
        a1, a2 = acc_ref[h], acc_ref[N_HEADS + h]
        oh = (a1[:HEAD_DIM] / a1[HEAD_DIM:HEAD_DIM + 1]
              - lam * (a2[:HEAD_DIM] / a2[HEAD_DIM:HEAD_DIM + 1]))
        ms = jnp.mean(oh * oh, axis=0, keepdims=True)
        g = g_ref[h * HEAD_DIM:(h + 1) * HEAD_DIM, :]
        yt_ref[h * HEAD_DIM:(h + 1) * HEAD_DIM, :] = oh * lax.rsqrt(ms + NORM_EPS) * (g * (1.0 - lam_init))
    o_ref[0] = jnp.transpose(yt_ref[...]).astype(BF16)


def _diff_attn(qt, k, vt, lam_vecs, g_col, lam_init, tq):
    b, s, _ = k.shape
    nk, tk = vt.shape[1], vt.shape[3]
    assert tq % tk == 0 and s % tq == 0
    n_chain = 2 * N_HEADS
    return pl.pallas_call(
        functools.partial(_diff_attn_kernel, tq=tq, lam_init=lam_init),
        grid=(b, s // tq),
        in_specs=[
            pl.BlockSpec((4, QK_DIM), lambda i, j: (0, 0)),
            pl.BlockSpec((GW, 1), lambda i, j: (0, 0)),
            pl.BlockSpec((1, GW, tq), lambda i, j: (i, 0, j)),
            pl.BlockSpec((1, s, GW), lambda i, j: (i, 0, 0)),
            pl.BlockSpec((1, nk, N_HEADS * V_ROWS, tk), lambda i, j: (i, 0, 0, 0)),
        ],
        out_specs=pl.BlockSpec((1, tq, GW), lambda i, j: (i, j, 0)),
        out_shape=jax.ShapeDtypeStruct((b, s, GW), BF16),
        scratch_shapes=[
            pltpu.VMEM((n_chain, GW, tq), BF16),
            pltpu.VMEM((n_chain, 1, tq), F32),
            pltpu.VMEM((n_chain, V_ROWS, tq), F32),
            pltpu.VMEM((GW, tq), F32),
            pltpu.VMEM((n_chain, tk, tq), F32),
            pltpu.VMEM((n_chain, 1, tq), F32),
        ],
        compiler_params=_params("arbitrary", "arbitrary"),
        name="diff_attn",
    )(lam_vecs, g_col, qt, k, vt)


def _out_proj_kernel(x_ref, yab_ref, o1_ref, o2_ref, o3_ref, l1_ref, l2_ref, l3_ref, yd_ref, w_ref, out_ref):
    l1, l2, l3 = l1_ref[...], l2_ref[...], l3_ref[...]
    m = jnp.maximum(jnp.maximum(l1, l2), l3)
    w1, w2, w3 = jnp.exp(l1 - m), jnp.exp(l2 - m), jnp.exp(l3 - m)
    yc = (w1 * o1_ref[...] + w2 * o2_ref[...] + w3 * o3_ref[...]) / (w1 + w2 + w3)
    acc = jnp.dot(yab_ref[...], w_ref[0:2 * GW, :], preferred_element_type=F32)
    acc = acc + jnp.dot(yc.astype(BF16), w_ref[2 * GW:3 * GW, :], preferred_element_type=F32)
    acc = acc + jnp.dot(yd_ref[...], w_ref[3 * GW:4 * GW, :], preferred_element_type=F32)
    out_ref[...] = x_ref[...] + acc


def _out_proj(x, yab, os_, lses, yd, w, tm):
    t, d = x.shape
    rows = lambda c: pl.BlockSpec((tm, c), lambda i: (i, 0))
    return pl.pallas_call(
        _out_proj_kernel,
        grid=(t // tm,),
        in_specs=[rows(d), rows(2 * GW)] + [rows(GW)] * 6 + [rows(GW), pl.BlockSpec((d, d), lambda i: (0, 0))],
        out_specs=rows(d),
        out_shape=jax.ShapeDtypeStruct((t, d), F32),
        compiler_params=_params("arbitrary"),
        name="out_proj",
    )(x, yab, *os_, *lses, yd, w)


def _ffn_kernel(x_ref, g_ref, wg_ref, wu_ref, wd_ref, out_ref, acc_ref, *, tf):
    hn = _rms(x_ref[...], g_ref[...]).astype(BF16)
    acc_ref[...] = x_ref[...]
    for f0 in range(0, wg_ref.shape[1], tf):
        gate = jnp.dot(hn, wg_ref[:, f0:f0 + tf], preferred_element_type=F32)
        up = jnp.dot(hn, wu_ref[:, f0:f0 + tf], preferred_element_type=F32)
        act = (gate * jax.nn.sigmoid(gate) * up).astype(BF16)
        acc_ref[...] += jnp.dot(act, wd_ref[f0:f0 + tf, :], preferred_element_type=F32)
    out_ref[...] = acc_ref[...]


def _ffn(x, g, wg, wu, wd, tm, tf):
    t, d = x.shape
    ff = wg.shape[1]
    assert ff % tf == 0
    whole = lambda shape: pl.BlockSpec(shape, lambda i: (0, 0))
    return pl.pallas_call(
        functools.partial(_ffn_kernel, tf=tf),
        grid=(t // tm,),
        in_specs=[pl.BlockSpec((tm, d), lambda i: (i, 0)), whole((1, d)), whole((d, ff)), whole((d, ff)),
                  whole((ff, d))],
        out_specs=pl.BlockSpec((tm, d), lambda i: (i, 0)),
        out_shape=jax.ShapeDtypeStruct((t, d), F32),
        scratch_shapes=[pltpu.VMEM((tm, d), F32)],
        compiler_params=_params("arbitrary"),
        name="ffn_dense",
    )(x, g, wg, wu, wd)


META_E1, META_E2, META_RANK1, META_RANK2, META_G1, META_G2 = range(6)


def _router_kernel(x_ref, g_ref, rhi_ref, rlo_ref, hn_ref, meta_ref, after_ref, carry_ref, *, blocks_per_chunk):
    @pl.when(pl.program_id(0) % blocks_per_chunk == 0)
    def _():
        carry_ref[...] = jnp.zeros_like(carry_ref)

    h = _rms(x_ref[...], g_ref[...])
    hn_ref[...] = h.astype(BF16)
    h_hi = h.astype(BF16)
    h_lo = (h - h_hi.astype(F32)).astype(BF16)
    logits = (jnp.dot(h_hi, rhi_ref[...], preferred_element_type=F32)
              + jnp.dot(h_hi, rlo_ref[...], preferred_element_type=F32)
              + jnp.dot(h_lo, rhi_ref[...], preferred_element_type=F32))
    tm = logits.shape[0]
    lane = lax.broadcasted_iota(jnp.int32, logits.shape, 1)
    logits = jnp.where(lane < N_EXPERTS, logits, -jnp.inf)
    m1 = jnp.max(logits, axis=-1, keepdims=True)
    i1 = jnp.min(jnp.where(logits == m1, lane, LANES), axis=-1, keepdims=True)
    rest = jnp.where(lane == i1, -jnp.inf, logits)
    m2 = jnp.max(rest, axis=-1, keepdims=True)
    i2 = jnp.min(jnp.where(rest == m2, lane, LANES), axis=-1, keepdims=True)
    e2 = jnp.exp(m2 - m1)
    g1 = 1.0 / (1.0 + e2)
    g2 = e2 / (1.0 + e2)

    oh1 = lane == i1
    oh2 = lane == i2
    both = jnp.where(oh1, 1.0, jnp.where(oh2, 1.0, 0.0))
    row = lax.broadcasted_iota(jnp.int32, (tm, tm), 0)
    col = lax.broadcasted_iota(jnp.int32, (tm, tm), 1)
    earlier = jnp.where(col < row, 1.0, 0.0).astype(BF16)
    carry = carry_ref[0:1, :]
    prefix = jnp.dot(earlier, both.astype(BF16), preferred_element_type=F32) + carry
    rank1 = jnp.sum(jnp.where(oh1, prefix, 0.0), axis=-1, keepdims=True)
    rank2 = jnp.sum(jnp.where(oh2, prefix, 0.0), axis=-1, keepdims=True)
    total = carry + jnp.sum(both, axis=0, keepdims=True)

    meta = jnp.zeros(logits.shape, F32)
    for k, val in ((META_E1, i1.astype(F32)), (META_E2, i2.astype(F32)), (META_RANK1, rank1),
                   (META_RANK2, rank2), (META_G1, g1), (META_G2, g2)):
        meta = jnp.where(lane == k, val, meta)
    meta_ref[...] = meta
    after_ref[0] = jnp.broadcast_to(total, after_ref.shape[1:])
    carry_ref[...] = jnp.broadcast_to(total, carry_ref.shape)


def _router(x, g, r_pad, tm, blocks_per_chunk):
    t, d = x.shape
    nblk = t // tm
    r_hi = r_pad.astype(BF16)
    counts = pl.BlockSpec((1, 8, LANES), lambda i: (i, 0, 0))
    return pl.pallas_call(
        functools.partial(_router_kernel, blocks_per_chunk=blocks_per_chunk),
        grid=(nblk,),
        in_specs=[
            pl.BlockSpec((tm, d), lambda i: (i, 0)),
            pl.BlockSpec((1, d), lambda i: (0, 0)),
            pl.BlockSpec((d, LANES), lambda i: (0, 0)),
            pl.BlockSpec((d, LANES), lambda i: (0, 0)),
        ],
        out_specs=[
            pl.BlockSpec((tm, d), lambda i: (i, 0)),
            pl.BlockSpec((tm, LANES), lambda i: (i, 0)),
            counts,
        ],
        out_shape=[
            jax.ShapeDtypeStruct((t, d), BF16),
            jax.ShapeDtypeStruct((t, LANES), F32),
            jax.ShapeDtypeStruct((nblk, 8, LANES), F32),
        ],
        scratch_shapes=[pltpu.VMEM((8, LANES), F32)],
        compiler_params=_params("arbitrary"),
        name="router",
    )(x, g, r_hi, (r_pad - r_hi.astype(F32)).astype(BF16))


SLOT_ROW1, SLOT_ROW2, SLOT_G1, SLOT_G2 = range(4)


def _moe_kernel(nt_ref, toff_ref, cuts_ref, hn_ref, slotc_ref, slotr_ref, wg_ref, wu_ref, wd_ref, out_ref,
                xs_ref, ys_ref, *, tr, sub, win):
    c, e, f = pl.program_id(0), pl.program_id(1), pl.program_id(2)
    last_f = pl.num_programs(2) - 1
    grp = c * N_EXPERTS + e
    n_tiles = nt_ref[grp]
    row0 = toff_ref[grp] * tr
    tc = hn_ref.shape[0]
    nsub = tc // sub

    @pl.when((e == 0) & (f == 0))
    def _():
        out_ref[...] = jnp.zeros_like(out_ref)

    def rows_of(r):
        return pl.ds(pl.multiple_of(r * tr, tr), tr)

    def token_window(r):
        base = grp * nsub
        first = r * tr
        last = jnp.minimum(first + tr, cuts_ref[base + nsub - 1]) - 1
        lo = hi = 0
        for k in range(nsub - 1):
            cut = cuts_ref[base + k]
            lo = lo + (cut <= first).astype(jnp.int32)
            hi = hi + (cut <= last).astype(jnp.int32)
        start = jnp.minimum(lo, nsub - win)
        return hi < start + win, pl.multiple_of(start * sub, sub)

    def spans(r, fn):
        covered, tok0 = token_window(r)

        @pl.when(covered)
        def _():
            fn(r, tok0, win * sub)

        @pl.when(jnp.logical_not(covered))
        def _():
            fn(r, 0, tc)

    def gather_span(r, tok0, ntok):
        toks = pl.ds(tok0, ntok)
        target = (lax.broadcasted_iota(jnp.int32, (tr, ntok), 0) + (row0 + r * tr)).astype(F32)
        hit1 = slotr_ref[SLOT_ROW1:SLOT_ROW1 + 1, toks] == target
        hit2 = slotr_ref[SLOT_ROW2:SLOT_ROW2 + 1, toks] == target
        oh = jnp.where(hit1, 1.0, jnp.where(hit2, 1.0, 0.0)).astype(BF16)
        xs_ref[rows_of(r), :] = jnp.dot(oh, hn_ref[toks, :], preferred_element_type=F32).astype(BF16)

    def gather_tile(r, carry):
        spans(r, gather_span)
        return carry

    def ffn_tiles(r, count, first):
        rows = pl.ds(pl.multiple_of(r * tr, tr), count * tr)
        xt = xs_ref[rows, :]
        gate = jnp.dot(xt, wg_ref[...], preferred_element_type=F32)
        up = jnp.dot(xt, wu_ref[...], preferred_element_type=F32)
        act = (gate * jax.nn.sigmoid(gate) * up).astype(BF16)
        y = jnp.dot(act, wd_ref[...], preferred_element_type=F32)
        if first:
            ys_ref[rows, :] = y
        else:
            ys_ref[rows, :] += y

    def scatter_span(r, tok0, ntok):
        toks = pl.ds(tok0, ntok)
        yb = ys_ref[rows_of(r), :].astype(BF16)
        target = (lax.broadcasted_iota(jnp.int32, (ntok, tr), 1) + (row0 + r * tr)).astype(F32)
        hit1 = slotc_ref[toks, SLOT_ROW1:SLOT_ROW1 + 1] == target
        hit2 = slotc_ref[toks, SLOT_ROW2:SLOT_ROW2 + 1] == target
        gh = jnp.where(hit1, slotc_ref[toks, SLOT_G1:SLOT_G1 + 1],
                       jnp.where(hit2, slotc_ref[toks, SLOT_G2:SLOT_G2 + 1], 0.0)).astype(BF16)
        out_ref[toks, :] += jnp.dot(gh, yb, preferred_element_type=F32)

    def scatter_tile(r, carry):
        spans(r, scatter_span)
        return carry

    def ffn_all(first):
        def pair(t, carry):
            ffn_tiles(2 * t, 2, first)
            return carry

        lax.fori_loop(0, n_tiles // 2, pair, 0)

        @pl.when(n_tiles % 2 == 1)
        def _():
            ffn_tiles(n_tiles - 1, 1, first)

    @pl.when(f == 0)
    def _():
        lax.fori_loop(0, n_tiles, gather_tile, 0)
        ffn_all(True)

    @pl.when(f > 0)
    def _():
        ffn_all(False)

    @pl.when(f == last_f)
    def _():
        lax.fori_loop(0, n_tiles, scatter_tile, 0)


def _moe(hn, slot_cols, slot_rows, nt, toff, cuts, wg, wu, wd, tc, tr, tf, sub, win):
    t, d = hn.shape
    ne, _, ff = wg.shape
    grid_spec = pltpu.PrefetchScalarGridSpec(
        num_scalar_prefetch=3,
        grid=(t // tc, ne, ff // tf),
        in_specs=[
            pl.BlockSpec((tc, d), lambda c, e, f, *_: (c, 0)),
            pl.BlockSpec((tc, 4), lambda c, e, f, *_: (c, 0)),
            pl.BlockSpec((8, tc), lambda c, e, f, *_: (0, c)),
            pl.BlockSpec((None, d, tf), lambda c, e, f, *_: (e, 0, f)),
            pl.BlockSpec((None, d, tf), lambda c, e, f, *_: (e, 0, f)),
            pl.BlockSpec((None, tf, d), lambda c, e, f, *_: (e, f, 0)),
        ],
        out_specs=pl.BlockSpec((tc, d), lambda c, e, f, *_: (c, 0)),
        scratch_shapes=[
            pltpu.VMEM((tc + tr, d), BF16),
            pltpu.VMEM((tc + tr, d), F32),
        ],
    )
    return pl.pallas_call(
        functools.partial(_moe_kernel, tr=tr, sub=sub, win=win),
        grid_spec=grid_spec,
        out_shape=jax.ShapeDtypeStruct((t, d), F32),
        compiler_params=_params("arbitrary", "arbitrary", "arbitrary"),
        name="moe",
    )(nt, toff, cuts, hn, slot_cols, slot_rows, wg, wu, wd)


def _final_kernel(x_ref, y_ref, g_ref, out_ref):
    out_ref[...] = _rms(x_ref[...] + y_ref[...], g_ref[...])


def _final_norm(x, y, g, tm):
    t, d = x.shape
    rows = pl.BlockSpec((tm, d), lambda i: (i, 0))
    return pl.pallas_call(
        _final_kernel,
        grid=(t // tm,),
        in_specs=[rows, rows, pl.BlockSpec((1, d), lambda i: (0, 0))],
        out_specs=rows,
        out_shape=jax.ShapeDtypeStruct((t, d), F32),
        compiler_params=_params("arbitrary"),
        name="final_norm",
    )(x, y, g)


def _routed_swiglu(x, norm_g, router_w, wg, wu, wd, final_g, tc, tr, tf, sub, win):
    t, d = x.shape
    nc, nsub = t // tc, tc // sub
    r_pad = jnp.zeros((d, LANES), F32).at[:, :N_EXPERTS].set(router_w)
    hn, meta, after = _router(x, norm_g, r_pad, tm=sub, blocks_per_chunk=nsub)
    through = after[:, 0, :N_EXPERTS].astype(jnp.int32).reshape(nc, nsub, N_EXPERTS)
    cuts = through.transpose(0, 2, 1)
    total = through[:, -1]
    nt = (total + tr - 1) // tr
    toff = jnp.cumsum(nt, axis=1) - nt
    first_row = jnp.repeat((toff * tr).astype(F32), tc, axis=0)
    experts = jnp.arange(N_EXPERTS, dtype=F32)
    pick = lambda e: jnp.sum(jnp.where(e[:, None] == experts, first_row, 0.0), axis=1)
    row1 = pick(meta[:, META_E1]) + meta[:, META_RANK1]
    row2 = pick(meta[:, META_E2]) + meta[:, META_RANK2]
    slot_cols = jnp.stack([row1, row2, meta[:, META_G1], meta[:, META_G2]], axis=1)
    slot_rows = jnp.zeros((8, t), F32).at[0:4].set(slot_cols.T)
    y = _moe(hn, slot_cols, slot_rows, nt.reshape(-1), toff.reshape(-1), cuts.reshape(-1), wg, wu, wd,
             tc=tc, tr=tr, tf=tf, sub=sub, win=win)
    return _final_norm(x, y, final_g, tm=min(1024, t))


def _permute_w_in(w):
    def perm(block):
        d = block.shape[0]
        return block.reshape(d, N_HEADS, 2, QK_DIM).transpose(0, 2, 1, 3).reshape(d, GW)
    parts = [w[:, :6 * GW], perm(w[:, 6 * GW:7 * GW]), perm(w[:, 7 * GW:8 * GW]), w[:, 8 * GW:]]
    return jnp.concatenate(parts, axis=1)


def _block_diag(pool_w):
    n, p, _ = pool_w.shape
    out = jnp.zeros((n * p, n * p), pool_w.dtype)
    for gi in range(n):
        out = out.at[gi * p:(gi + 1) * p, gi * p:(gi + 1) * p].set(pool_w[gi])
    return out


def _token_mixing(x, layer, lam_init, p, rope_c, rope_d):
    b, s, d = x.shape
    row = lambda v: v.reshape(1, -1)
    w_in = _permute_w_in(p["w_in"][layer]).astype(BF16)
    uab, qkvc, qdt, kd, vdt = _norm_proj(x, row(p["norm1_g"][layer]), w_in, rope_c, rope_d, tm=512, tk=256)
    dww = jnp.zeros((HALO, GW), F32).at[:CONV_WIDTH].set(p["conv_dw_w"][layer])
    yab = _local_mixers(
        uab, dww, row(p["conv_dw_b"][layer]), row(p["conv_ln_g"][layer]), row(p["conv_ln_b"][layer]),
        p["conv_pw_w"][layer].astype(BF16), _block_diag(p["pool_w"][layer]).astype(BF16),
        row(p["pool_scale"][layer]), ts=512)
    os_, lses = [], []
    for dil in DILATIONS:
        o, lse = _band_attn(qkvc, dil)
        os_.append(o.reshape(b * s, GW))
        lses.append(lse.reshape(b * s, GW))
    yd = _diff_attn(qdt, kd, vdt, p["diff_lam"][layer], jnp.tile(p["diff_ln_g"][layer], N_HEADS).reshape(GW, 1),
                    lam_init, tq=512)
    return _out_proj(x.reshape(b * s, d), yab.reshape(b * s, 2 * GW), os_, lses, yd.reshape(b * s, GW),
                     p["w_out"][layer].astype(BF16), tm=512)


def kernel(x, norm1_g, w_in, conv_dw_w, conv_dw_b, conv_ln_g, conv_ln_b, conv_pw_w, pool_w, pool_scale,
           diff_lam, diff_ln_g, w_out, norm2_g, ffn_w_gate, ffn_w_up, ffn_w_down, moe_router, moe_w_gate,
           moe_w_up, moe_w_down, final_g):
    b, s, d = x.shape
    p = dict(norm1_g=norm1_g, w_in=w_in, conv_dw_w=conv_dw_w, conv_dw_b=conv_dw_b, conv_ln_g=conv_ln_g,
             conv_ln_b=conv_ln_b, conv_pw_w=conv_pw_w, pool_w=pool_w, pool_scale=pool_scale,
             diff_lam=diff_lam, diff_ln_g=diff_ln_g, w_out=w_out)
    rope_c = _rope_table(s, HEAD_DIM, HEAD_DIM // 4)
    rope_d = _rope_table(s, QK_DIM, QK_DIM // 4)
    row = lambda v: v.reshape(1, -1)

    x1 = _token_mixing(x, 0, 0.8 - 0.6 * math.exp(-0.3 * 0), p, rope_c, rope_d)
    x2 = _ffn(x1, row(norm2_g[0]), ffn_w_gate[0].astype(BF16), ffn_w_up[0].astype(BF16),
              ffn_w_down[0].astype(BF16), tm=512, tf=256)

    x3 = _token_mixing(x2.reshape(b, s, d), 1, 0.8 - 0.6 * math.exp(-0.3 * 1), p, rope_c, rope_d)
    out = _routed_swiglu(x3, row(norm2_g[1]), moe_router[0], moe_w_gate[0].astype(BF16),
                         moe_w_up[0].astype(BF16), moe_w_down[0].astype(BF16), row(final_g),
                         tc=2048, tr=256, tf=512, sub=256, win=5)
    return out.reshape(b, s, d)
```

```python
import functools
import math

import jax
import jax.numpy as jnp
from jax import lax
from jax.experimental import pallas as pl
from jax.experimental.pallas import tpu as pltpu

D_MODEL = 1024
GW = 256
N_HEADS = 4
HEAD_DIM = 64
QK_DIM = 32
V_ROWS = HEAD_DIM + 16
CONV_WIDTH = 31
POOL_WINDOWS = (2, 4, 8, 16)
DILATIONS = (1, 4, 16)
BAND = 128
ROPE_THETA = 500000.0
N_EXPERTS = 8
NORM_EPS = 1e-5
HALO = 32
LANES = 128
SUBLANES = 8
VMEM_LIMIT = 56 * 1024 * 1024

BF16 = jnp.bfloat16
F32 = jnp.float32
NT_DIMS = (((1,), (1,)), ((), ()))


def _params(*sem):
    return pltpu.CompilerParams(dimension_semantics=sem, vmem_limit_bytes=VMEM_LIMIT)


def _lanes_in(shape, start, width):
    lane = lax.broadcasted_iota(jnp.int32, shape, len(shape) - 1)
    return (lane >= start) & (lane < start + width)


def _rms(x, g):
    ms = jnp.mean(x * x, axis=-1, keepdims=True)
    return x * lax.rsqrt(ms + NORM_EPS) * g


def _rope_table(seq, width, rot):
    half = rot // 2
    pos = jnp.arange(seq, dtype=F32)
    inv = ROPE_THETA ** (-jnp.arange(0, rot, 2, dtype=F32) / rot)
    ang = pos[:, None] * inv[None, :]
    cos, sin = jnp.cos(ang), jnp.sin(ang)
    lane = jnp.arange(GW) % width
    f = lane % half
    first = lane < half
    second = (lane >= half) & (lane < rot)
    c = jnp.where((lane < rot)[None, :], cos[:, f], 1.0)
    sn = jnp.where(first[None, :], -sin[:, f], 0.0)
    sp = jnp.where(second[None, :], sin[:, f], 0.0)
    return jnp.concatenate([c, sn, sp], axis=1)


def _rope(u, tab_ref, half):
    return (u * tab_ref[:, 0:GW]
            + pltpu.roll(u, GW - half, 1) * tab_ref[:, GW:2 * GW]
            + pltpu.roll(u, half, 1) * tab_ref[:, 2 * GW:3 * GW])


def _norm_proj_kernel(x_ref, g_ref, w_ref, rc_ref, rd_ref, uab_ref, qkvc_ref, qt_ref, kd_ref, vt_ref, v_ref,
                      *, tk):
    h = _rms(x_ref[0], g_ref[...]).astype(BF16)

    def proj(grp):
        return jnp.dot(h, w_ref[:, grp * GW:(grp + 1) * GW], preferred_element_type=F32)

    for grp in range(3):
        uab_ref[0, :, grp * GW:(grp + 1) * GW] = proj(grp)
    c_scale = HEAD_DIM ** -0.5
    d_scale = QK_DIM ** -0.5 * math.log2(math.e)
    qkvc_ref[0, :, 0:GW] = (_rope(proj(3), rc_ref, 8) * c_scale).astype(BF16)
    qkvc_ref[0, :, GW:2 * GW] = _rope(proj(4), rc_ref, 8).astype(BF16)
    qkvc_ref[0, :, 2 * GW:3 * GW] = proj(5).astype(BF16)
    qt_ref[0] = jnp.transpose(_rope(proj(6), rd_ref, 4) * d_scale).astype(BF16)
    kd_ref[0] = _rope(proj(7), rd_ref, 4).astype(BF16)
    v_ref[...] = proj(8)
    vt = jnp.transpose(v_ref[...]).astype(BF16)
    ones = jnp.ones((V_ROWS - HEAD_DIM, tk), BF16)
    for blk in range(vt.shape[1] // tk):
        for hd in range(N_HEADS):
            vt_ref[0, blk, hd * V_ROWS:hd * V_ROWS + HEAD_DIM, :] = (
                vt[hd * HEAD_DIM:(hd + 1) * HEAD_DIM, blk * tk:(blk + 1) * tk])
            vt_ref[0, blk, hd * V_ROWS + HEAD_DIM:(hd + 1) * V_ROWS, :] = ones


def _norm_proj(x, g, w, rope_c, rope_d, tm, tk):
    b, s, d = x.shape
    grid = (s // tm, b)
    return pl.pallas_call(
        functools.partial(_norm_proj_kernel, tk=tk),
        grid=grid,
        in_specs=[
            pl.BlockSpec((1, tm, d), lambda i, j: (j, i, 0)),
            pl.BlockSpec((1, d), lambda i, j: (0, 0)),
            pl.BlockSpec((d, 9 * GW), lambda i, j: (0, 0)),
            pl.BlockSpec((tm, 3 * GW), lambda i, j: (i, 0)),
            pl.BlockSpec((tm, 3 * GW), lambda i, j: (i, 0)),
        ],
        out_specs=[
            pl.BlockSpec((1, tm, 3 * GW), lambda i, j: (j, i, 0)),
            pl.BlockSpec((1, tm, 3 * GW), lambda i, j: (j, i, 0)),
            pl.BlockSpec((1, GW, tm), lambda i, j: (j, 0, i)),
            pl.BlockSpec((1, tm, GW), lambda i, j: (j, i, 0)),
            pl.BlockSpec((1, tm // tk, N_HEADS * V_ROWS, tk), lambda i, j: (j, i, 0, 0)),
        ],
        out_shape=[
            jax.ShapeDtypeStruct((b, s, 3 * GW), F32),
            jax.ShapeDtypeStruct((b, s, 3 * GW), BF16),
            jax.ShapeDtypeStruct((b, GW, s), BF16),
            jax.ShapeDtypeStruct((b, s, GW), BF16),
            jax.ShapeDtypeStruct((b, s // tk, N_HEADS * V_ROWS, tk), BF16),
        ],
        scratch_shapes=[pltpu.VMEM((tm, GW), F32)],
        compiler_params=_params("arbitrary", "arbitrary"),
        name="norm_proj",
    )(x, g, w, rope_c, rope_d)


def _local_kernel(u_ref, dww_ref, dwb_ref, lng_ref, lnb_ref, pww_ref, plw_ref, pls_ref,
                  y_ref, zbuf, pbuf, shift_ref, *, ts):
    si = pl.program_id(1)

    @pl.when(si == 0)
    def _():
        zbuf[0:HALO, :] = jnp.zeros((HALO, GW), F32)
        pbuf[0:HALO, :] = jnp.zeros((HALO, GW), F32)

    a = u_ref[0, :, 0:GW]
    b = u_ref[0, :, GW:2 * GW]
    zbuf[HALO:HALO + ts, :] = a * jax.nn.sigmoid(b)
    pbuf[HALO:HALO + ts, :] = u_ref[0, :, 2 * GW:3 * GW]

    def by_phase(buf, shifted, offsets, term):
        total = None
        for phase in range(SUBLANES):
            offs = [o for o in offsets if o % SUBLANES == phase]
            if not offs:
                continue
            src = buf
            if phase:
                n = max(offs) - phase + ts
                shifted[phase, 0:n, :] = buf[phase:phase + n, :]
                src = shifted.at[phase]
            for o in offs:
                t = term(o, src[o - phase:o - phase + ts, :])
                total = t if total is None else total + t
        return total

    base = HALO - (CONV_WIDTH - 1)
    acc = by_phase(zbuf, shift_ref.at[0], range(base, base + CONV_WIDTH),
                   lambda o, rows: dww_ref[o - base:o - base + 1, :] * rows)
    z = acc + dwb_ref[...]
    mu = jnp.mean(z, axis=-1, keepdims=True)
    zc = z - mu
    var = jnp.mean(zc * zc, axis=-1, keepdims=True)
    zn = zc * lax.rsqrt(var + NORM_EPS) * lng_ref[...] + lnb_ref[...]
    zs = zn * jax.nn.sigmoid(zn)
    y_ref[0, :, 0:GW] = jnp.dot(zs.astype(BF16), pww_ref[...], preferred_element_type=F32).astype(BF16)

    lane = lax.broadcasted_iota(jnp.int32, (ts, GW), 1)
    pool_group = GW // len(POOL_WINDOWS)
    win = jnp.zeros((ts, GW), jnp.int32)
    for gi, w in enumerate(POOL_WINDOWS):
        win = jnp.where(lane >= gi * pool_group, w, win)
    p0 = pbuf[HALO:HALO + ts, :]
    lag = lambda o: HALO - o
    tot = by_phase(pbuf, shift_ref.at[1], range(HALO - max(POOL_WINDOWS) + 1, HALO + 1),
                   lambda o, rows: rows if lag(o) < min(POOL_WINDOWS) else jnp.where(win > lag(o), rows, 0.0))
    pos = si * ts + lax.broadcasted_iota(jnp.int32, (ts, GW), 0)
    cnt = jnp.minimum(pos + 1, win).astype(F32)
    pooled = tot / cnt - p0
    yb = jnp.dot(pooled.astype(BF16), plw_ref[...], preferred_element_type=F32) * pls_ref[...]
    y_ref[0, :, GW:2 * GW] = yb.astype(BF16)

    zbuf[0:HALO, :] = zbuf[ts:ts + HALO, :]
    pbuf[0:HALO, :] = pbuf[ts:ts + HALO, :]


def _local_mixers(uab, dww, dwb, lng, lnb, pww, plw, pls, ts):
    b, s, _ = uab.shape
    vec = lambda: pl.BlockSpec((1, GW), lambda i, j: (0, 0))
    mat = lambda: pl.BlockSpec((GW, GW), lambda i, j: (0, 0))
    return pl.pallas_call(
        functools.partial(_local_kernel, ts=ts),
        grid=(b, s // ts),
        in_specs=[
            pl.BlockSpec((1, ts, 3 * GW), lambda i, j: (i, j, 0)),
            pl.BlockSpec((HALO, GW), lambda i, j: (0, 0)),
            vec(), vec(), vec(), mat(), mat(), vec(),
        ],
        out_specs=pl.BlockSpec((1, ts, 2 * GW), lambda i, j: (i, j, 0)),
        out_shape=jax.ShapeDtypeStruct((b, s, 2 * GW), BF16),
        scratch_shapes=[
            pltpu.VMEM((HALO + ts, GW), F32),
            pltpu.VMEM((HALO + ts, GW), F32),
            pltpu.VMEM((2, SUBLANES, HALO + ts, GW), F32),
        ],
        compiler_params=_params("arbitrary", "arbitrary"),
        name="local_mixers",
    )(uab, dww, dwb, lng, lnb, pww, plw, pls)


def _band_attn_kernel(q_ref, k_ref, v_ref, o_ref, lse_ref, s_ref):
    row = lax.broadcasted_iota(jnp.int32, (BAND, 2 * BAND), 0)
    col = lax.broadcasted_iota(jnp.int32, (BAND, 2 * BAND), 1)
    halves = [_lanes_in((BAND, LANES), sub * HEAD_DIM, HEAD_DIM) for sub in range(LANES // HEAD_DIM)]

    def starts(n):
        return pl.multiple_of(n * BAND, BAND), pl.multiple_of(jnp.maximum(n - 1, 0) * BAND, BAND)

    def scores(n, slot):
        q0, k0 = starts(n)
        q = q_ref[0, pl.ds(q0, BAND), :].astype(F32)
        kw = k_ref[0, pl.ds(k0, 2 * BAND), :]
        dist = (row - col) + (q0 - k0)
        mask = (dist >= 0) & (dist <= BAND)
        for h in range(N_HEADS):
            qh = jnp.where(_lanes_in((BAND, GW), h * HEAD_DIM, HEAD_DIM), q, 0.0).astype(BF16)
            s = lax.dot_general(qh, kw, NT_DIMS, preferred_element_type=F32)
            s_ref[slot * N_HEADS + h] = jnp.where(mask, s, -jnp.inf)

    def values(n, slot):
        q0, k0 = starts(n)
        for tile in range(GW // LANES):
            vw = v_ref[0, pl.ds(k0, 2 * BAND), tile * LANES:(tile + 1) * LANES]
            o = jnp.zeros((BAND, LANES), F32)
            lse = jnp.zeros((BAND, LANES), F32)
            for sub in range(LANES // HEAD_DIM):
                s = s_ref[slot * N_HEADS + tile * (LANES // HEAD_DIM) + sub]
                m = jnp.max(s, axis=-1, keepdims=True)
                p = jnp.exp(s - m)
                l = jnp.sum(p, axis=-1, keepdims=True)
                acc = jnp.dot(p.astype(BF16), vw, preferred_element_type=F32)
                mine = halves[sub]
                o = jnp.where(mine, acc / l, o)
                lse = jnp.where(mine, m + jnp.log(l), lse)
            o_ref[0, pl.ds(q0, BAND), tile * LANES:(tile + 1) * LANES] = o.astype(o_ref.dtype)
            lse_ref[0, pl.ds(q0, BAND), tile * LANES:(tile + 1) * LANES] = lse

    def pair(t, carry):
        scores(2 * t, 0)
        scores(2 * t + 1, 1)
        values(2 * t, 0)
        values(2 * t + 1, 1)
        return carry

    lax.fori_loop(0, q_ref.shape[1] // (2 * BAND), pair, 0)


def _band_attn(qkv, dil):
    b, s, _ = qkv.shape
    l = s // dil
    assert l % (2 * BAND) == 0
    part = lambda c: pl.BlockSpec((1, l, GW), lambda i, r: (i, 0, 3 * r + c))
    out = pl.BlockSpec((1, l, GW), lambda i, r: (i, 0, r))
    o, lse = pl.pallas_call(
        _band_attn_kernel,
        grid=(b, dil),
        in_specs=[part(0), part(1), part(2)],
        out_specs=[out, out],
        out_shape=[jax.ShapeDtypeStruct((b, l, dil * GW), BF16), jax.ShapeDtypeStruct((b, l, dil * GW), F32)],
        scratch_shapes=[pltpu.VMEM((2 * N_HEADS, BAND, 2 * BAND), F32)],
        compiler_params=_params("arbitrary", "arbitrary"),
        name="band_attn",
    )(*[qkv.reshape(b, l, dil * 3 * GW)] * 3)
    return o.reshape(b, s, GW), lse.reshape(b, s, GW)


def _diff_attn_kernel(lam_ref, g_ref, qt_ref, k_ref, vt_ref, o_ref, qm_ref, m_ref, acc_ref, yt_ref,
                      st_ref, mx_ref, *, tq, lam_init):
    n_chain = 2 * N_HEADS
    i = pl.program_id(1)
    @pl.when((pl.program_id(0) == 0) & (i == 0))
    def _():
        qm_ref[...] = jnp.zeros(qm_ref.shape, BF16)

    for c in range(n_chain):
        qm_ref[c, c * QK_DIM:(c + 1) * QK_DIM, :] = qt_ref[0, c * QK_DIM:(c + 1) * QK_DIM, :]
    m_ref[...] = jnp.full(m_ref.shape, -jnp.inf, F32)
    acc_ref[...] = jnp.zeros(acc_ref.shape, F32)
    tk = st_ref.shape[1]
    ratio = tq // tk
    ahead = (lax.broadcasted_iota(jnp.int32, (tk, tq), 0)
             - lax.broadcasted_iota(jnp.int32, (tk, tq), 1))

    def score(c, kb, diag):
        st = jnp.dot(kb, qm_ref[c], preferred_element_type=F32)
        if diag is not None:
            st = jnp.where(ahead <= -diag * tk, st, -jnp.inf)
        st_ref[c] = st
        mx_ref[c] = jnp.max(st, axis=0, keepdims=True)

    def value(c, j):
        h = c % N_HEADS
        m_prev = m_ref[c]
        m_next = jnp.maximum(m_prev, mx_ref[c])
        alpha = jnp.exp2(m_prev - m_next)
        p = jnp.exp2(st_ref[c] - m_next).astype(BF16)
        vt = vt_ref[0, j, h * V_ROWS:(h + 1) * V_ROWS, :]
        acc_ref[c] = alpha * acc_ref[c] + jnp.dot(vt, p, preferred_element_type=F32)
        m_ref[c] = m_next

    def k_block(j):
        return k_ref[0, pl.ds(pl.multiple_of(j * tk, tk), tk), :]

    def scores(j, diag):
        kb = k_block(j)
        for c in range(n_chain):
            score(c, kb, diag)

    def values(j):
        for c in range(n_chain):
            value(c, j)

    def values_then_scores(j, diag):
        kb = k_block(j + 1)
        for c in range(n_chain):
            value(c, j)
            score(c, kb, diag)

    before = ratio * i

    @pl.when(i == 0)
    def _():
        scores(0, 0)

    @pl.when(i > 0)
    def _():
        scores(0, None)

    def body(j, carry):
        values_then_scores(j, None)
        return carry

    lax.fori_loop(0, before - 1, body, 0)

    @pl.when(i > 0)
    def _():
        values_then_scores(before - 1, 0)

    for diag in range(1, ratio):
        values_then_scores(before + diag - 1, diag)
    values(before + ratio - 1)

    lv = lam_ref[...]
    lam = (jnp.exp(jnp.sum(lv[0:1] * lv[1:2], axis=-1, keepdims=True))
           - jnp.exp(jnp.sum(lv[2:3] * lv[3:4], axis=-1, keepdims=True)) + lam_init)
    for h in range(N_HEADS):
        a1, a2 = acc_ref[h], acc_ref[N_HEADS + h]
        oh = (a1[:HEAD_DIM] / a1[HEAD_DIM:HEAD_DIM + 1]
              - lam * (a2[:HEAD_DIM] / a2[HEAD_DIM:HEAD_DIM + 1]))
        ms = jnp.mean(oh * oh, axis=0, keepdims=True)
        g = g_ref[h * HEAD_DIM:(h + 1) * HEAD_DIM, :]
        yt_ref[h * HEAD_DIM:(h + 1) * HEAD_DIM, :] = oh * lax.rsqrt(ms + NORM_EPS) * (g * (1.0 - lam_init))
    o_ref[0] = jnp.transpose(yt_ref[...]).astype(BF16)


def _diff_attn(qt, k, vt, lam_vecs, g_col, lam_init, tq):
    b, s, _ = k.shape
    nk, tk = vt.shape[1], vt.shape[3]
    assert tq % tk == 0 and s % tq == 0
    n_chain = 2 * N_HEADS
    return pl.pallas_call(
        functools.partial(_diff_attn_kernel, tq=tq, lam_init=lam_init),
        grid=(b, s // tq),
        in_specs=[
            pl.BlockSpec((4, QK_DIM), lambda i, j: (0, 0)),
            pl.BlockSpec((GW, 1), lambda i, j: (0, 0)),
            pl.BlockSpec((1, GW, tq), lambda i, j: (i, 0, j)),
            pl.BlockSpec((1, s, GW), lambda i, j: (i, 0, 0)),
            pl.BlockSpec((1, nk, N_HEADS * V_ROWS, tk), lambda i, j: (i, 0, 0, 0)),
        ],
        out_specs=pl.BlockSpec((1, tq, GW), lambda i, j: (i, j, 0)),
        out_shape=jax.ShapeDtypeStruct((b, s, GW), BF16),
        scratch_shapes=[
            pltpu.VMEM((n_chain, GW, tq), BF16),
            pltpu.VMEM((n_chain, 1, tq), F32),
            pltpu.VMEM((n_chain, V_ROWS, tq), F32),
            pltpu.VMEM((GW, tq), F32),
            pltpu.VMEM((n_chain, tk, tq), F32),
            pltpu.VMEM((n_chain, 1, tq), F32),
        ],
        compiler_params=_params("arbitrary", "arbitrary"),
        name="diff_attn",
    )(lam_vecs, g_col, qt, k, vt)


def _out_proj_kernel(x_ref, yab_ref, o1_ref, o2_ref, o3_ref, l1_ref, l2_ref, l3_ref, yd_ref, w_ref, out_ref):
    l1, l2, l3 = l1_ref[...], l2_ref[...], l3_ref[...]
    m = jnp.maximum(jnp.maximum(l1, l2), l3)
    w1, w2, w3 = jnp.exp(l1 - m), jnp.exp(l2 - m), jnp.exp(l3 - m)
    yc = (w1 * o1_ref[...] + w2 * o2_ref[...] + w3 * o3_ref[...]) / (w1 + w2 + w3)
    acc = jnp.dot(yab_ref[...], w_ref[0:2 * GW, :], preferred_element_type=F32)
    acc = acc + jnp.dot(yc.astype(BF16), w_ref[2 * GW:3 * GW, :], preferred_element_type=F32)
    acc = acc + jnp.dot(yd_ref[...], w_ref[3 * GW:4 * GW, :], preferred_element_type=F32)
    out_ref[...] = x_ref[...] + acc


def _out_proj(x, yab, os_, lses, yd, w, tm):
    t, d = x.shape
    rows = lambda c: pl.BlockSpec((tm, c), lambda i: (i, 0))
    return pl.pallas_call(
        _out_proj_kernel,
        grid=(t // tm,),
        in_specs=[rows(d), rows(2 * GW)] + [rows(GW)] * 6 + [rows(GW), pl.BlockSpec((d, d), lambda i: (0, 0))],
        out_specs=rows(d),
        out_shape=jax.ShapeDtypeStruct((t, d), F32),
        compiler_params=_params("arbitrary"),
        name="out_proj",
    )(x, yab, *os_, *lses, yd, w)


def _ffn_kernel(x_ref, g_ref, wg_ref, wu_ref, wd_ref, out_ref, acc_ref, *, tf):
    hn = _rms(x_ref[...], g_ref[...]).astype(BF16)
    acc_ref[...] = x_ref[...]
    for f0 in range(0, wg_ref.shape[1], tf):
        gate = jnp.dot(hn, wg_ref[:, f0:f0 + tf], preferred_element_type=F32)
        up = jnp.dot(hn, wu_ref[:, f0:f0 + tf], preferred_element_type=F32)
        act = (gate * jax.nn.sigmoid(gate) * up).astype(BF16)
        acc_ref[...] += jnp.dot(act, wd_ref[f0:f0 + tf, :], preferred_element_type=F32)
    out_ref[...] = acc_ref[...]


def _ffn(x, g, wg, wu, wd, tm, tf):
    t, d = x.shape
    ff = wg.shape[1]
    assert ff % tf == 0
    whole = lambda shape: pl.BlockSpec(shape, lambda i: (0, 0))
    return pl.pallas_call(
        functools.partial(_ffn_kernel, tf=tf),
        grid=(t // tm,),
        in_specs=[pl.BlockSpec((tm, d), lambda i: (i, 0)), whole((1, d)), whole((d, ff)), whole((d, ff)),
                  whole((ff, d))],
        out_specs=pl.BlockSpec((tm, d), lambda i: (i, 0)),
        out_shape=jax.ShapeDtypeStruct((t, d), F32),
        scratch_shapes=[pltpu.VMEM((tm, d), F32)],
        compiler_params=_params("arbitrary"),
        name="ffn_dense",
    )(x, g, wg, wu, wd)


META_E1, META_E2, META_RANK1, META_RANK2, META_G1, META_G2 = range(6)


def _router_kernel(x_ref, g_ref, rhi_ref, rlo_ref, hn_ref, meta_ref, after_ref, carry_ref, *, blocks_per_chunk):
    @pl.when(pl.program_id(0) % blocks_per_chunk == 0)
    def _():
        carry_ref[...] = jnp.zeros_like(carry_ref)

    h = _rms(x_ref[...], g_ref[...])
    hn_ref[...] = h.astype(BF16)
    h_hi = h.astype(BF16)
    h_lo = (h - h_hi.astype(F32)).astype(BF16)
    logits = (jnp.dot(h_hi, rhi_ref[...], preferred_element_type=F32)
              + jnp.dot(h_hi, rlo_ref[...], preferred_element_type=F32)
              + jnp.dot(h_lo, rhi_ref[...], preferred_element_type=F32))
    tm = logits.shape[0]
    lane = lax.broadcasted_iota(jnp.int32, logits.shape, 1)
    logits = jnp.where(lane < N_EXPERTS, logits, -jnp.inf)
    m1 = jnp.max(logits, axis=-1, keepdims=True)
    i1 = jnp.min(jnp.where(logits == m1, lane, LANES), axis=-1, keepdims=True)
    rest = jnp.where(lane == i1, -jnp.inf, logits)
    m2 = jnp.max(rest, axis=-1, keepdims=True)
    i2 = jnp.min(jnp.where(rest == m2, lane, LANES), axis=-1, keepdims=True)
    e2 = jnp.exp(m2 - m1)
    g1 = 1.0 / (1.0 + e2)
    g2 = e2 / (1.0 + e2)

    oh1 = lane == i1
    oh2 = lane == i2
    both = jnp.where(oh1, 1.0, jnp.where(oh2, 1.0, 0.0))
    row = lax.broadcasted_iota(jnp.int32, (tm, tm), 0)
    col = lax.broadcasted_iota(jnp.int32, (tm, tm), 1)
    earlier = jnp.where(col < row, 1.0, 0.0).astype(BF16)
    carry = carry_ref[0:1, :]
    prefix = jnp.dot(earlier, both.astype(BF16), preferred_element_type=F32) + carry
    rank1 = jnp.sum(jnp.where(oh1, prefix, 0.0), axis=-1, keepdims=True)
    rank2 = jnp.sum(jnp.where(oh2, prefix, 0.0), axis=-1, keepdims=True)
    total = carry + jnp.sum(both, axis=0, keepdims=True)

    meta = jnp.zeros(logits.shape, F32)
    for k, val in ((META_E1, i1.astype(F32)), (META_E2, i2.astype(F32)), (META_RANK1, rank1),
                   (META_RANK2, rank2), (META_G1, g1), (META_G2, g2)):
        meta = jnp.where(lane == k, val, meta)
    meta_ref[...] = meta
    after_ref[0] = jnp.broadcast_to(total, after_ref.shape[1:])
    carry_ref[...] = jnp.broadcast_to(total, carry_ref.shape)


def _router(x, g, r_pad, tm, blocks_per_chunk):
    t, d = x.shape
    nblk = t // tm
    r_hi = r_pad.astype(BF16)
    counts = pl.BlockSpec((1, 8, LANES), lambda i: (i, 0, 0))
    return pl.pallas_call(
        functools.partial(_router_kernel, blocks_per_chunk=blocks_per_chunk),
        grid=(nblk,),
        in_specs=[
            pl.BlockSpec((tm, d), lambda i: (i, 0)),
            pl.BlockSpec((1, d), lambda i: (0, 0)),
            pl.BlockSpec((d, LANES), lambda i: (0, 0)),
            pl.BlockSpec((d, LANES), lambda i: (0, 0)),
        ],
        out_specs=[
            pl.BlockSpec((tm, d), lambda i: (i, 0)),
            pl.BlockSpec((tm, LANES), lambda i: (i, 0)),
            counts,
        ],
        out_shape=[
            jax.ShapeDtypeStruct((t, d), BF16),
            jax.ShapeDtypeStruct((t, LANES), F32),
            jax.ShapeDtypeStruct((nblk, 8, LANES), F32),
        ],
        scratch_shapes=[pltpu.VMEM((8, LANES), F32)],
        compiler_params=_params("arbitrary"),
        name="router",
    )(x, g, r_hi, (r_pad - r_hi.astype(F32)).astype(BF16))


SLOT_ROW1, SLOT_ROW2, SLOT_G1, SLOT_G2 = range(4)


def _moe_kernel(nt_ref, toff_ref, cuts_ref, hn_ref, slotc_ref, slotr_ref, wg_ref, wu_ref, wd_ref, out_ref,
                xs_ref, ys_ref, *, tr, sub, win):
    c, e, f = pl.program_id(0), pl.program_id(1), pl.program_id(2)
    last_f = pl.num_programs(2) - 1
    grp = c * N_EXPERTS + e
    n_tiles = nt_ref[grp]
    row0 = toff_ref[grp] * tr
    tc = hn_ref.shape[0]
    nsub = tc // sub

    @pl.when((e == 0) & (f == 0))
    def _():
        out_ref[...] = jnp.zeros_like(out_ref)

    def rows_of(r):
        return pl.ds(pl.multiple_of(r * tr, tr), tr)

    def token_window(r):
        base = grp * nsub
        first = r * tr
        last = jnp.minimum(first + tr, cuts_ref[base + nsub - 1]) - 1
        lo = hi = 0
        for k in range(nsub - 1):
            cut = cuts_ref[base + k]
            lo = lo + (cut <= first).astype(jnp.int32)
            hi = hi + (cut <= last).astype(jnp.int32)
        start = jnp.minimum(lo, nsub - win)
        return hi < start + win, pl.multiple_of(start * sub, sub)

    def spans(r, fn):
        covered, tok0 = token_window(r)

        @pl.when(covered)
        def _():
            fn(r, tok0, win * sub)

        @pl.when(jnp.logical_not(covered))
        def _():
            fn(r, 0, tc)

    def gather_span(r, tok0, ntok):
        toks = pl.ds(tok0, ntok)
        target = (lax.broadcasted_iota(jnp.int32, (tr, ntok), 0) + (row0 + r * tr)).astype(F32)
        hit1 = slotr_ref[SLOT_ROW1:SLOT_ROW1 + 1, toks] == target
        hit2 = slotr_ref[SLOT_ROW2:SLOT_ROW2 + 1, toks] == target
        oh = jnp.where(hit1, 1.0, jnp.where(hit2, 1.0, 0.0)).astype(BF16)
        xs_ref[rows_of(r), :] = jnp.dot(oh, hn_ref[toks, :], preferred_element_type=F32).astype(BF16)

    def gather_tile(r, carry):
        spans(r, gather_span)
        return carry

    def ffn_tiles(r, count, first):
        rows = pl.ds(pl.multiple_of(r * tr, tr), count * tr)
        xt = xs_ref[rows, :]
        gate = jnp.dot(xt, wg_ref[...], preferred_element_type=F32)
        up = jnp.dot(xt, wu_ref[...], preferred_element_type=F32)
        act = (gate * jax.nn.sigmoid(gate) * up).astype(BF16)
        y = jnp.dot(act, wd_ref[...], preferred_element_type=F32)
        if first:
            ys_ref[rows, :] = y
        else:
            ys_ref[rows, :] += y

    def scatter_span(r, tok0, ntok):
        toks = pl.ds(tok0, ntok)
        yb = ys_ref[rows_of(r), :].astype(BF16)
        target = (lax.broadcasted_iota(jnp.int32, (ntok, tr), 1) + (row0 + r * tr)).astype(F32)
        hit1 = slotc_ref[toks, SLOT_ROW1:SLOT_ROW1 + 1] == target
        hit2 = slotc_ref[toks, SLOT_ROW2:SLOT_ROW2 + 1] == target
        gh = jnp.where(hit1, slotc_ref[toks, SLOT_G1:SLOT_G1 + 1],
                       jnp.where(hit2, slotc_ref[toks, SLOT_G2:SLOT_G2 + 1], 0.0)).astype(BF16)
        out_ref[toks, :] += jnp.dot(gh, yb, preferred_element_type=F32)

    def scatter_tile(r, carry):
        spans(r, scatter_span)
        return carry

    def ffn_all(first):
        def pair(t, carry):
            ffn_tiles(2 * t, 2, first)
            return carry

        lax.fori_loop(0, n_tiles // 2, pair, 0)

        @pl.when(n_tiles % 2 == 1)
        def _():
            ffn_tiles(n_tiles - 1, 1, first)

    @pl.when(f == 0)
    def _():
        lax.fori_loop(0, n_tiles, gather_tile, 0)
        ffn_all(True)

    @pl.when(f > 0)
    def _():
        ffn_all(False)

    @pl.when(f == last_f)
    def _():
        lax.fori_loop(0, n_tiles, scatter_tile, 0)


def _moe(hn, slot_cols, slot_rows, nt, toff, cuts, wg, wu, wd, tc, tr, tf, sub, win):
    t, d = hn.shape
    ne, _, ff = wg.shape
    grid_spec = pltpu.PrefetchScalarGridSpec(
        num_scalar_prefetch=3,
        grid=(t // tc, ne, ff // tf),
        in_specs=[
            pl.BlockSpec((tc, d), lambda c, e, f, *_: (c, 0)),
            pl.BlockSpec((tc, 4), lambda c, e, f, *_: (c, 0)),
            pl.BlockSpec((8, tc), lambda c, e, f, *_: (0, c)),
            pl.BlockSpec((None, d, tf), lambda c, e, f, *_: (e, 0, f)),
            pl.BlockSpec((None, d, tf), lambda c, e, f, *_: (e, 0, f)),
            pl.BlockSpec((None, tf, d), lambda c, e, f, *_: (e, f, 0)),
        ],
        out_specs=pl.BlockSpec((tc, d), lambda c, e, f, *_: (c, 0)),
        scratch_shapes=[
            pltpu.VMEM((tc + tr, d), BF16),
            pltpu.VMEM((tc + tr, d), F32),
        ],
    )
    return pl.pallas_call(
        functools.partial(_moe_kernel, tr=tr, sub=sub, win=win),
        grid_spec=grid_spec,
        out_shape=jax.ShapeDtypeStruct((t, d), F32),
        compiler_params=_params("arbitrary", "arbitrary", "arbitrary"),
        name="moe",
    )(nt, toff, cuts, hn, slot_cols, slot_rows, wg, wu, wd)


def _final_kernel(x_ref, y_ref, g_ref, out_ref):
    out_ref[...] = _rms(x_ref[...] + y_ref[...], g_ref[...])


def _final_norm(x, y, g, tm):
    t, d = x.shape
    rows = pl.BlockSpec((tm, d), lambda i: (i, 0))
    return pl.pallas_call(
        _final_kernel,
        grid=(t // tm,),
        in_specs=[rows, rows, pl.BlockSpec((1, d), lambda i: (0, 0))],
        out_specs=rows,
        out_shape=jax.ShapeDtypeStruct((t, d), F32),
        compiler_params=_params("arbitrary"),
        name="final_norm",
    )(x, y, g)


def _routed_swiglu(x, norm_g, router_w, wg, wu, wd, final_g, tc, tr, tf, sub, win):
    t, d = x.shape
    nc, nsub = t // tc, tc // sub
    r_pad = jnp.zeros((d, LANES), F32).at[:, :N_EXPERTS].set(router_w)
    hn, meta, after = _router(x, norm_g, r_pad, tm=sub, blocks_per_chunk=nsub)
    through = after[:, 0, :N_EXPERTS].astype(jnp.int32).reshape(nc, nsub, N_EXPERTS)
    cuts = through.transpose(0, 2, 1)
    total = through[:, -1]
    nt = (total + tr - 1) // tr
    toff = jnp.cumsum(nt, axis=1) - nt
    first_row = jnp.repeat((toff * tr).astype(F32), tc, axis=0)
    experts = jnp.arange(N_EXPERTS, dtype=F32)
    pick = lambda e: jnp.sum(jnp.where(e[:, None] == experts, first_row, 0.0), axis=1)
    row1 = pick(meta[:, META_E1]) + meta[:, META_RANK1]
    row2 = pick(meta[:, META_E2]) + meta[:, META_RANK2]
    slot_cols = jnp.stack([row1, row2, meta[:, META_G1], meta[:, META_G2]], axis=1)
    slot_rows = jnp.zeros((8, t), F32).at[0:4].set(slot_cols.T)
    y = _moe(hn, slot_cols, slot_rows, nt.reshape(-1), toff.reshape(-1), cuts.reshape(-1), wg, wu, wd,
             tc=tc, tr=tr, tf=tf, sub=sub, win=win)
    return _final_norm(x, y, final_g, tm=min(1024, t))


def _permute_w_in(w):
    def perm(block):
        d = block.shape[0]
        return block.reshape(d, N_HEADS, 2, QK_DIM).transpose(0, 2, 1, 3).reshape(d, GW)
    parts = [w[:, :6 * GW], perm(w[:, 6 * GW:7 * GW]), perm(w[:, 7 * GW:8 * GW]), w[:, 8 * GW:]]
    return jnp.concatenate(parts, axis=1)


def _block_diag(pool_w):
    n, p, _ = pool_w.shape
    out = jnp.zeros((n * p, n * p), pool_w.dtype)
    for gi in range(n):
        out = out.at[gi * p:(gi + 1) * p, gi * p:(gi + 1) * p].set(pool_w[gi])
    return out


def _token_mixing(x, layer, lam_init, p, rope_c, rope_d):
    b, s, d = x.shape
    row = lambda v: v.reshape(1, -1)
    w_in = _permute_w_in(p["w_in"][layer]).astype(BF16)
    uab, qkvc, qdt, kd, vdt = _norm_proj(x, row(p["norm1_g"][layer]), w_in, rope_c, rope_d, tm=512, tk=256)
    dww = jnp.zeros((HALO, GW), F32).at[:CONV_WIDTH].set(p["conv_dw_w"][layer])
    yab = _local_mixers(
        uab, dww, row(p["conv_dw_b"][layer]), row(p["conv_ln_g"][layer]), row(p["conv_ln_b"][layer]),
        p["conv_pw_w"][layer].astype(BF16), _block_diag(p["pool_w"][layer]).astype(BF16),
        row(p["pool_scale"][layer]), ts=512)
    os_, lses = [], []
    for dil in DILATIONS:
        o, lse = _band_attn(qkvc, dil)
        os_.append(o.reshape(b * s, GW))
        lses.append(lse.reshape(b * s, GW))
    yd = _diff_attn(qdt, kd, vdt, p["diff_lam"][layer], jnp.tile(p["diff_ln_g"][layer], N_HEADS).reshape(GW, 1),
                    lam_init, tq=512)
    return _out_proj(x.reshape(b * s, d), yab.reshape(b * s, 2 * GW), os_, lses, yd.reshape(b * s, GW),
                     p["w_out"][layer].astype(BF16), tm=512)


def kernel(x, norm1_g, w_in, conv_dw_w, conv_dw_b, conv_ln_g, conv_ln_b, conv_pw_w, pool_w, pool_scale,
           diff_lam, diff_ln_g, w_out, norm2_g, ffn_w_gate, ffn_w_up, ffn_w_down, moe_router, moe_w_gate,
           moe_w_up, moe_w_down, final_g):
    b, s, d = x.shape
    p = dict(norm1_g=norm1_g, w_in=w_in, conv_dw_w=conv_dw_w, conv_dw_b=conv_dw_b, conv_ln_g=conv_ln_g,
             conv_ln_b=conv_ln_b, conv_pw_w=conv_pw_w, pool_w=pool_w, pool_scale=pool_scale,
             diff_lam=diff_lam, diff_ln_g=diff_ln_g, w_out=w_out)
    rope_c = _rope_table(s, HEAD_DIM, HEAD_DIM // 4)
    rope_d = _rope_table(s, QK_DIM, QK_DIM // 4)
    row = lambda v: v.reshape(1, -1)

    x1 = _token_mixing(x, 0, 0.8 - 0.6 * math.exp(-0.3 * 0), p, rope_c, rope_d)
    x2 = _ffn(x1, row(norm2_g[0]), ffn_w_gate[0].astype(BF16), ffn_w_up[0].astype(BF16),
              ffn_w_down[0].astype(BF16), tm=512, tf=256)

    x3 = _token_mixing(x2.reshape(b, s, d), 1, 0.8 - 0.6 * math.exp(-0.3 * 1), p, rope_c, rope_d)
    out = _routed_swiglu(x3, row(norm2_g[1]), moe_router[0], moe_w_gate[0].astype(BF16),
                         moe_w_up[0].astype(BF16), moe_w_down[0].astype(BF16), row(final_g),
                         tc=2048, tr=272, tf=512, sub=256, win=5)
    return out.reshape(b, s, d)
```

```python
import functools
import math

import jax
import jax.numpy as jnp
from jax import lax
from jax.experimental import pallas as pl
from jax.experimental.pallas import tpu as pltpu

D_MODEL = 1024
GW = 256
N_HEADS = 4
HEAD_DIM = 64
QK_DIM = 32
V_ROWS = HEAD_DIM + 16
CONV_WIDTH = 31
POOL_WINDOWS = (2, 4, 8, 16)
DILATIONS = (1, 4, 16)
BAND = 128
ROPE_THETA = 500000.0
N_EXPERTS = 8
NORM_EPS = 1e-5
HALO = 32
LANES = 128
SUBLANES = 8
VMEM_LIMIT = 56 * 1024 * 1024

BF16 = jnp.bfloat16
F32 = jnp.float32
NT_DIMS = (((1,), (1,)), ((), ()))


def _params(*sem):
    return pltpu.CompilerParams(dimension_semantics=sem, vmem_limit_bytes=VMEM_LIMIT)


def _lanes_in(shape, start, width):
    lane = lax.broadcasted_iota(jnp.int32, shape, len(shape) - 1)
    return (lane >= start) & (lane < start + width)


def _rms(x, g):
    ms = jnp.mean(x * x, axis=-1, keepdims=True)
    return x * lax.rsqrt(ms + NORM_EPS) * g


def _rope_table(seq, width, rot):
    half = rot // 2
    pos = jnp.arange(seq, dtype=F32)
    inv = ROPE_THETA ** (-jnp.arange(0, rot, 2, dtype=F32) / rot)
    ang = pos[:, None] * inv[None, :]
    cos, sin = jnp.cos(ang), jnp.sin(ang)
    lane = jnp.arange(GW) % width
    f = lane % half
    first = lane < half
    second = (lane >= half) & (lane < rot)
    c = jnp.where((lane < rot)[None, :], cos[:, f], 1.0)
    sn = jnp.where(first[None, :], -sin[:, f], 0.0)
    sp = jnp.where(second[None, :], sin[:, f], 0.0)
    return jnp.concatenate([c, sn, sp], axis=1)


def _rope(u, tab_ref, half):
    return (u * tab_ref[:, 0:GW]
            + pltpu.roll(u, GW - half, 1) * tab_ref[:, GW:2 * GW]
            + pltpu.roll(u, half, 1) * tab_ref[:, 2 * GW:3 * GW])


def _norm_proj_kernel(x_ref, g_ref, w_ref, rc_ref, rd_ref, uab_ref, qkvc_ref, qkv4_ref, qkv16_ref, qt_ref, kd_ref,
                      vt_ref, v_ref, cs_ref, *, tk):
    h = _rms(x_ref[0], g_ref[...]).astype(BF16)

    def proj(grp):
        return jnp.dot(h, w_ref[:, grp * GW:(grp + 1) * GW], preferred_element_type=F32)

    for grp in range(3):
        uab_ref[0, :, grp * GW:(grp + 1) * GW] = proj(grp)
    c_scale = HEAD_DIM ** -0.5
    d_scale = QK_DIM ** -0.5 * math.log2(math.e)
    parts = (_rope(proj(3), rc_ref, 8) * c_scale, _rope(proj(4), rc_ref, 8), proj(5))
    for k, val in enumerate(parts):
        qkvc_ref[0, :, k * GW:(k + 1) * GW] = val.astype(BF16)
        for half in range(GW // LANES):
            cs_ref[2 * k + half] = val[:, half * LANES:(half + 1) * LANES]
    tm = cs_ref.shape[1]
    for dil, out_ref in zip(DILATIONS[1:], (qkv4_ref, qkv16_ref)):
        for r in range(dil):
            for slab in range(cs_ref.shape[0]):
                rows = cs_ref[slab, pl.ds(r, tm // dil, stride=dil), :]
                col = r * 3 * GW + slab * LANES
                out_ref[0, :, col:col + LANES] = rows.astype(BF16)
    qt_ref[0] = jnp.transpose(_rope(proj(6), rd_ref, 4) * d_scale).astype(BF16)
    kd_ref[0] = _rope(proj(7), rd_ref, 4).astype(BF16)
    v_ref[...] = proj(8)
    vt = jnp.transpose(v_ref[...]).astype(BF16)
    ones = jnp.ones((V_ROWS - HEAD_DIM, tk), BF16)
    for blk in range(vt.shape[1] // tk):
        for hd in range(N_HEADS):
            vt_ref[0, blk, hd * V_ROWS:hd * V_ROWS + HEAD_DIM, :] = (
                vt[hd * HEAD_DIM:(hd + 1) * HEAD_DIM, blk * tk:(blk + 1) * tk])
            vt_ref[0, blk, hd * V_ROWS + HEAD_DIM:(hd + 1) * V_ROWS, :] = ones


def _norm_proj(x, g, w, rope_c, rope_d, tm, tk):
    b, s, d = x.shape
    grid = (s // tm, b)
    return pl.pallas_call(
        functools.partial(_norm_proj_kernel, tk=tk),
        grid=grid,
        in_specs=[
            pl.BlockSpec((1, tm, d), lambda i, j: (j, i, 0)),
            pl.BlockSpec((1, d), lambda i, j: (0, 0)),
            pl.BlockSpec((d, 9 * GW), lambda i, j: (0, 0)),
            pl.BlockSpec((tm, 3 * GW), lambda i, j: (i, 0)),
            pl.BlockSpec((tm, 3 * GW), lambda i, j: (i, 0)),
        ],
        out_specs=[
            pl.BlockSpec((1, tm, 3 * GW), lambda i, j: (j, i, 0)),
            pl.BlockSpec((1, tm, 3 * GW), lambda i, j: (j, i, 0)),
            *[pl.BlockSpec((1, tm // dil, dil * 3 * GW), lambda i, j: (j, i, 0)) for dil in DILATIONS[1:]],
            pl.BlockSpec((1, GW, tm), lambda i, j: (j, 0, i)),
            pl.BlockSpec((1, tm, GW), lambda i, j: (j, i, 0)),
            pl.BlockSpec((1, tm // tk, N_HEADS * V_ROWS, tk), lambda i, j: (j, i, 0, 0)),
        ],
        out_shape=[
            jax.ShapeDtypeStruct((b, s, 3 * GW), F32),
            jax.ShapeDtypeStruct((b, s, 3 * GW), BF16),
            *[jax.ShapeDtypeStruct((b, s // dil, dil * 3 * GW), BF16) for dil in DILATIONS[1:]],
            jax.ShapeDtypeStruct((b, GW, s), BF16),
            jax.ShapeDtypeStruct((b, s, GW), BF16),
            jax.ShapeDtypeStruct((b, s // tk, N_HEADS * V_ROWS, tk), BF16),
        ],
        scratch_shapes=[
            pltpu.VMEM((tm, GW), F32),
            pltpu.VMEM((3 * GW // LANES, tm, LANES), F32),
        ],
        compiler_params=_params("arbitrary", "arbitrary"),
        name="norm_proj",
    )(x, g, w, rope_c, rope_d)


def _local_kernel(u_ref, dww_ref, dwb_ref, lng_ref, lnb_ref, pww_ref, plw_ref, pls_ref,
                  y_ref, zbuf, pbuf, shift_ref, *, ts):
    si = pl.program_id(1)

    @pl.when(si == 0)
    def _():
        zbuf[0:HALO, :] = jnp.zeros((HALO, GW), F32)
        pbuf[0:HALO, :] = jnp.zeros((HALO, GW), F32)

    a = u_ref[0, :, 0:GW]
    b = u_ref[0, :, GW:2 * GW]
    zbuf[HALO:HALO + ts, :] = a * jax.nn.sigmoid(b)
    pbuf[HALO:HALO + ts, :] = u_ref[0, :, 2 * GW:3 * GW]

    def by_phase(buf, shifted, offsets, term):
        total = None
        for phase in range(SUBLANES):
            offs = [o for o in offsets if o % SUBLANES == phase]
            if not offs:
                continue
            src = buf
            if phase:
                n = max(offs) - phase + ts
                shifted[phase, 0:n, :] = buf[phase:phase + n, :]
                src = shifted.at[phase]
            for o in offs:
                t = term(o, src[o - phase:o - phase + ts, :])
                total = t if total is None else total + t
        return total

    base = HALO - (CONV_WIDTH - 1)
    acc = by_phase(zbuf, shift_ref.at[0], range(base, base + CONV_WIDTH),
                   lambda o, rows: dww_ref[o - base:o - base + 1, :] * rows)
    z = acc + dwb_ref[...]
    mu = jnp.mean(z, axis=-1, keepdims=True)
    zc = z - mu
    var = jnp.mean(zc * zc, axis=-1, keepdims=True)
    zn = zc * lax.rsqrt(var + NORM_EPS) * lng_ref[...] + lnb_ref[...]
    zs = zn * jax.nn.sigmoid(zn)
    y_ref[0, :, 0:GW] = jnp.dot(zs.astype(BF16), pww_ref[...], preferred_element_type=F32).astype(BF16)

    lane = lax.broadcasted_iota(jnp.int32, (ts, GW), 1)
    pool_group = GW // len(POOL_WINDOWS)
    win = jnp.zeros((ts, GW), jnp.int32)
    for gi, w in enumerate(POOL_WINDOWS):
        win = jnp.where(lane >= gi * pool_group, w, win)
    p0 = pbuf[HALO:HALO + ts, :]
    lag = lambda o: HALO - o
    tot = by_phase(pbuf, shift_ref.at[1], range(HALO - max(POOL_WINDOWS) + 1, HALO + 1),
                   lambda o, rows: rows if lag(o) < min(POOL_WINDOWS) else jnp.where(win > lag(o), rows, 0.0))
    pos = si * ts + lax.broadcasted_iota(jnp.int32, (ts, GW), 0)
    cnt = jnp.minimum(pos + 1, win).astype(F32)
    pooled = tot / cnt - p0
    yb = jnp.dot(pooled.astype(BF16), plw_ref[...], preferred_element_type=F32) * pls_ref[...]
    y_ref[0, :, GW:2 * GW] = yb.astype(BF16)

    zbuf[0:HALO, :] = zbuf[ts:ts + HALO, :]
    pbuf[0:HALO, :] = pbuf[ts:ts + HALO, :]


def _local_mixers(uab, dww, dwb, lng, lnb, pww, plw, pls, ts):
    b, s, _ = uab.shape
    vec = lambda: pl.BlockSpec((1, GW), lambda i, j: (0, 0))
    mat = lambda: pl.BlockSpec((GW, GW), lambda i, j: (0, 0))
    return pl.pallas_call(
        functools.partial(_local_kernel, ts=ts),
        grid=(b, s // ts),
        in_specs=[
            pl.BlockSpec((1, ts, 3 * GW), lambda i, j: (i, j, 0)),
            pl.BlockSpec((HALO, GW), lambda i, j: (0, 0)),
            vec(), vec(), vec(), mat(), mat(), vec(),
        ],
        out_specs=pl.BlockSpec((1, ts, 2 * GW), lambda i, j: (i, j, 0)),
        out_shape=jax.ShapeDtypeStruct((b, s, 2 * GW), BF16),
        scratch_shapes=[
            pltpu.VMEM((HALO + ts, GW), F32),
            pltpu.VMEM((HALO + ts, GW), F32),
            pltpu.VMEM((2, SUBLANES, HALO + ts, GW), F32),
        ],
        compiler_params=_params("arbitrary", "arbitrary"),
        name="local_mixers",
    )(uab, dww, dwb, lng, lnb, pww, plw, pls)


def _band_attn_kernel(q_ref, k_ref, v_ref, o_ref, lse_ref, s_ref):
    row = lax.broadcasted_iota(jnp.int32, (BAND, 2 * BAND), 0)
    col = lax.broadcasted_iota(jnp.int32, (BAND, 2 * BAND), 1)
    halves = [_lanes_in((BAND, LANES), sub * HEAD_DIM, HEAD_DIM) for sub in range(LANES // HEAD_DIM)]

    def starts(n):
        return pl.multiple_of(n * BAND, BAND), pl.multiple_of(jnp.maximum(n - 1, 0) * BAND, BAND)

    def scores(n, slot):
        q0, k0 = starts(n)
        q = q_ref[0, pl.ds(q0, BAND), :].astype(F32)
        kw = k_ref[0, pl.ds(k0, 2 * BAND), :]
        dist = (row - col) + (q0 - k0)
        mask = (dist >= 0) & (dist <= BAND)
        for h in range(N_HEADS):
            qh = jnp.where(_lanes_in((BAND, GW), h * HEAD_DIM, HEAD_DIM), q, 0.0).astype(BF16)
            s = lax.dot_general(qh, kw, NT_DIMS, preferred_element_type=F32)
            s_ref[slot * N_HEADS + h] = jnp.where(mask, s, -jnp.inf)

    def values(n, slot):
        q0, k0 = starts(n)
        for tile in range(GW // LANES):
            vw = v_ref[0, pl.ds(k0, 2 * BAND), tile * LANES:(tile + 1) * LANES]
            o = jnp.zeros((BAND, LANES), F32)
            lse = jnp.zeros((BAND, LANES), F32)
            for sub in range(LANES // HEAD_DIM):
                s = s_ref[slot * N_HEADS + tile * (LANES // HEAD_DIM) + sub]
                m = jnp.max(s, axis=-1, keepdims=True)
                p = jnp.exp(s - m)
                l = jnp.sum(p, axis=-1, keepdims=True)
                acc = jnp.dot(p.astype(BF16), vw, preferred_element_type=F32)
                mine = halves[sub]
                o = jnp.where(mine, acc / l, o)
                lse = jnp.where(mine, m + jnp.log(l), lse)
            o_ref[0, pl.ds(q0, BAND), tile * LANES:(tile + 1) * LANES] = o.astype(o_ref.dtype)
            lse_ref[0, pl.ds(q0, BAND), tile * LANES:(tile + 1) * LANES] = lse

    def pair(t, carry):
        scores(2 * t, 0)
        scores(2 * t + 1, 1)
        values(2 * t, 0)
        values(2 * t + 1, 1)
        return carry

    lax.fori_loop(0, q_ref.shape[1] // (2 * BAND), pair, 0)


def _band_attn(qkv, dil):
    b, l, _ = qkv.shape
    assert l % (2 * BAND) == 0
    part = lambda c: pl.BlockSpec((1, l, GW), lambda i, r: (i, 0, 3 * r + c))
    out = pl.BlockSpec((1, l, GW), lambda i, r: (i, 0, r))
    return pl.pallas_call(
        _band_attn_kernel,
        grid=(b, dil),
        in_specs=[part(0), part(1), part(2)],
        out_specs=[out, out],
        out_shape=[jax.ShapeDtypeStruct((b, l, dil * GW), BF16), jax.ShapeDtypeStruct((b, l, dil * GW), F32)],
        scratch_shapes=[pltpu.VMEM((2 * N_HEADS, BAND, 2 * BAND), F32)],
        compiler_params=_params("arbitrary", "arbitrary"),
        name="band_attn",
    )(qkv, qkv, qkv)


def _diff_attn_kernel(lam_ref, g_ref, qt_ref, k_ref, vt_ref, o_ref, qm_ref, m_ref, acc_ref, yt_ref,
                      st_ref, mx_ref, *, tq, lam_init):
    n_chain = 2 * N_HEADS
    i = pl.program_id(1)
    @pl.when((pl.program_id(0) == 0) & (i == 0))
    def _():
        qm_ref[...] = jnp.zeros(qm_ref.shape, BF16)

    for c in range(n_chain):
        qm_ref[c, c * QK_DIM:(c + 1) * QK_DIM, :] = qt_ref[0, c * QK_DIM:(c + 1) * QK_DIM, :]
    m_ref[...] = jnp.full(m_ref.shape, -jnp.inf, F32)
    acc_ref[...] = jnp.zeros(acc_ref.shape, F32)
    tk = st_ref.shape[1]
    ratio = tq // tk
    ahead = (lax.broadcasted_iota(jnp.int32, (tk, tq), 0)
             - lax.broadcasted_iota(jnp.int32, (tk, tq), 1))

    def score(c, kb, diag):
        st = jnp.dot(kb, qm_ref[c], preferred_element_type=F32)
        if diag is not None:
            st = jnp.where(ahead <= -diag * tk, st, -jnp.inf)
        st_ref[c] = st
        mx_ref[c] = jnp.max(st, axis=0, keepdims=True)

    def value(c, j):
        h = c % N_HEADS
        m_prev = m_ref[c]
        m_next = jnp.maximum(m_prev, mx_ref[c])
        alpha = jnp.exp2(m_prev - m_next)
        p = jnp.exp2(st_ref[c] - m_next).astype(BF16)
        vt = vt_ref[0, j, h * V_ROWS:(h + 1) * V_ROWS, :]
        acc_ref[c] = alpha * acc_ref[c] + jnp.dot(vt, p, preferred_element_type=F32)
        m_ref[c] = m_next

    def k_block(j):
        return k_ref[0, pl.ds(pl.multiple_of(j * tk, tk), tk), :]

    def scores(j, diag):
        kb = k_block(j)
        for c in range(n_chain):
            score(c, kb, diag)

    def values(j):
        for c in range(n_chain):
            value(c, j)

    def values_then_scores(j, diag):
        kb = k_block(j + 1)
        for c in range(n_chain):
            value(c, j)
            score(c, kb, diag)

    before = ratio * i

    @pl.when(i == 0)
    def _():
        scores(0, 0)

    @pl.when(i > 0)
    def _():
        scores(0, None)

    def body(j, carry):
        values_then_scores(j, None)
        return carry

    lax.fori_loop(0, before - 1, body, 0)

    @pl.when(i > 0)
    def _():
        values_then_scores(before - 1, 0)

    for diag in range(1, ratio):
        values_then_scores(before + diag - 1, diag)
    values(before + ratio - 1)

    lv = lam_ref[...]
    lam = (jnp.exp(jnp.sum(lv[0:1] * lv[1:2], axis=-1, keepdims=True))
           - jnp.exp(jnp.sum(lv[2:3] * lv[3:4], axis=-1, keepdims=True)) + lam_init)
    for h in range(N_HEADS):
        a1, a2 = acc_ref[h], acc_ref[N_HEADS + h]
        oh = (a1[:HEAD_DIM] / a1[HEAD_DIM:HEAD_DIM + 1]
              - lam * (a2[:HEAD_DIM] / a2[HEAD_DIM:HEAD_DIM + 1]))
        ms = jnp.mean(oh * oh, axis=0, keepdims=True)
        g = g_ref[h * HEAD_DIM:(h + 1) * HEAD_DIM, :]
        yt_ref[h * HEAD_DIM:(h + 1) * HEAD_DIM, :] = oh * lax.rsqrt(ms + NORM_EPS) * (g * (1.0 - lam_init))
    o_ref[0] = jnp.transpose(yt_ref[...]).astype(BF16)


def _diff_attn(qt, k, vt, lam_vecs, g_col, lam_init, tq):
    b, s, _ = k.shape
    nk, tk = vt.shape[1], vt.shape[3]
    assert tq % tk == 0 and s % tq == 0
    n_chain = 2 * N_HEADS
    return pl.pallas_call(
        functools.partial(_diff_attn_kernel, tq=tq, lam_init=lam_init),
        grid=(b, s // tq),
        in_specs=[
            pl.BlockSpec((4, QK_DIM), lambda i, j: (0, 0)),
            pl.BlockSpec((GW, 1), lambda i, j: (0, 0)),
            pl.BlockSpec((1, GW, tq), lambda i, j: (i, 0, j)),
            pl.BlockSpec((1, s, GW), lambda i, j: (i, 0, 0)),
            pl.BlockSpec((1, nk, N_HEADS * V_ROWS, tk), lambda i, j: (i, 0, 0, 0)),
        ],
        out_specs=pl.BlockSpec((1, tq, GW), lambda i, j: (i, j, 0)),
        out_shape=jax.ShapeDtypeStruct((b, s, GW), BF16),
        scratch_shapes=[
            pltpu.VMEM((n_chain, GW, tq), BF16),
            pltpu.VMEM((n_chain, 1, tq), F32),
            pltpu.VMEM((n_chain, V_ROWS, tq), F32),
            pltpu.VMEM((GW, tq), F32),
            pltpu.VMEM((n_chain, tk, tq), F32),
            pltpu.VMEM((n_chain, 1, tq), F32),
        ],
        compiler_params=_params("arbitrary", "arbitrary"),
        name="diff_attn",
    )(lam_vecs, g_col, qt, k, vt)


def _out_proj_kernel(x_ref, yab_ref, o1_ref, o2_ref, o3_ref, l1_ref, l2_ref, l3_ref, yd_ref, w_ref, out_ref,
                     nat_ref):
    tm = x_ref.shape[0]

    def natural(ref, dil, slot):
        for r in range(dil):
            for half in range(GW // LANES):
                col = r * GW + half * LANES
                nat_ref[2 * slot + half, pl.ds(r, tm // dil, stride=dil), :] = ref[:, col:col + LANES].astype(F32)
        return jnp.concatenate([nat_ref[2 * slot], nat_ref[2 * slot + 1]], axis=1)

    d2, d3 = DILATIONS[1:]
    l1, l2, l3 = l1_ref[...], natural(l2_ref, d2, 0), natural(l3_ref, d3, 1)
    o2, o3 = natural(o2_ref, d2, 2), natural(o3_ref, d3, 3)
    m = jnp.maximum(jnp.maximum(l1, l2), l3)
    w1, w2, w3 = jnp.exp(l1 - m), jnp.exp(l2 - m), jnp.exp(l3 - m)
    yc = (w1 * o1_ref[...] + w2 * o2 + w3 * o3) / (w1 + w2 + w3)
    acc = jnp.dot(yab_ref[...], w_ref[0:2 * GW, :], preferred_element_type=F32)
    acc = acc + jnp.dot(yc.astype(BF16), w_ref[2 * GW:3 * GW, :], preferred_element_type=F32)
    acc = acc + jnp.dot(yd_ref[...], w_ref[3 * GW:4 * GW, :], preferred_element_type=F32)
    out_ref[...] = x_ref[...] + acc


def _out_proj(x, yab, os_, lses, yd, w, tm):
    t, d = x.shape
    rows = lambda c: pl.BlockSpec((tm, c), lambda i: (i, 0))
    branch = [pl.BlockSpec((tm // dil, dil * GW), lambda i: (i, 0)) for dil in DILATIONS]
    return pl.pallas_call(
        _out_proj_kernel,
        grid=(t // tm,),
        in_specs=[rows(d), rows(2 * GW)] + branch * 2 + [rows(GW), pl.BlockSpec((d, d), lambda i: (0, 0))],
        out_specs=rows(d),
        out_shape=jax.ShapeDtypeStruct((t, d), F32),
        scratch_shapes=[pltpu.VMEM((4 * GW // LANES, tm, LANES), F32)],
        compiler_params=_params("arbitrary"),
        name="out_proj",
    )(x, yab, *os_, *lses, yd, w)


def _ffn_kernel(x_ref, g_ref, wg_ref, wu_ref, wd_ref, out_ref, acc_ref, *, tf):
    hn = _rms(x_ref[...], g_ref[...]).astype(BF16)
    acc_ref[...] = x_ref[...]
    for f0 in range(0, wg_ref.shape[1], tf):
        gate = jnp.dot(hn, wg_ref[:, f0:f0 + tf], preferred_element_type=F32)
        up = jnp.dot(hn, wu_ref[:, f0:f0 + tf], preferred_element_type=F32)
        act = (gate * jax.nn.sigmoid(gate) * up).astype(BF16)
        acc_ref[...] += jnp.dot(act, wd_ref[f0:f0 + tf, :], preferred_element_type=F32)
    out_ref[...] = acc_ref[...]


def _ffn(x, g, wg, wu, wd, tm, tf):
    t, d = x.shape
    ff = wg.shape[1]
    assert ff % tf == 0
    whole = lambda shape: pl.BlockSpec(shape, lambda i: (0, 0))
    return pl.pallas_call(
        functools.partial(_ffn_kernel, tf=tf),
        grid=(t // tm,),
        in_specs=[pl.BlockSpec((tm, d), lambda i: (i, 0)), whole((1, d)), whole((d, ff)), whole((d, ff)),
                  whole((ff, d))],
        out_specs=pl.BlockSpec((tm, d), lambda i: (i, 0)),
        out_shape=jax.ShapeDtypeStruct((t, d), F32),
        scratch_shapes=[pltpu.VMEM((tm, d), F32)],
        compiler_params=_params("arbitrary"),
        name="ffn_dense",
    )(x, g, wg, wu, wd)


META_E1, META_E2, META_RANK1, META_RANK2, META_G1, META_G2 = range(6)


def _router_kernel(x_ref, g_ref, rhi_ref, rlo_ref, hn_ref, meta_ref, after_ref, carry_ref, *, blocks_per_chunk):
    @pl.when(pl.program_id(0) % blocks_per_chunk == 0)
    def _():
        carry_ref[...] = jnp.zeros_like(carry_ref)

    h = _rms(x_ref[...], g_ref[...])
    hn_ref[...] = h.astype(BF16)
    h_hi = h.astype(BF16)
    h_lo = (h - h_hi.astype(F32)).astype(BF16)
    logits = (jnp.dot(h_hi, rhi_ref[...], preferred_element_type=F32)
              + jnp.dot(h_hi, rlo_ref[...], preferred_element_type=F32)
              + jnp.dot(h_lo, rhi_ref[...], preferred_element_type=F32))
    tm = logits.shape[0]
    lane = lax.broadcasted_iota(jnp.int32, logits.shape, 1)
    logits = jnp.where(lane < N_EXPERTS, logits, -jnp.inf)
    m1 = jnp.max(logits, axis=-1, keepdims=True)
    i1 = jnp.min(jnp.where(logits == m1, lane, LANES), axis=-1, keepdims=True)
    rest = jnp.where(lane == i1, -jnp.inf, logits)
    m2 = jnp.max(rest, axis=-1, keepdims=True)
    i2 = jnp.min(jnp.where(rest == m2, lane, LANES), axis=-1, keepdims=True)
    e2 = jnp.exp(m2 - m1)
    g1 = 1.0 / (1.0 + e2)
    g2 = e2 / (1.0 + e2)

    oh1 = lane == i1
    oh2 = lane == i2
    both = jnp.where(oh1, 1.0, jnp.where(oh2, 1.0, 0.0))
    row = lax.broadcasted_iota(jnp.int32, (tm, tm), 0)
    col = lax.broadcasted_iota(jnp.int32, (tm, tm), 1)
    earlier = jnp.where(col < row, 1.0, 0.0).astype(BF16)
    carry = carry_ref[0:1, :]
    prefix = jnp.dot(earlier, both.astype(BF16), preferred_element_type=F32) + carry
    rank1 = jnp.sum(jnp.where(oh1, prefix, 0.0), axis=-1, keepdims=True)
    rank2 = jnp.sum(jnp.where(oh2, prefix, 0.0), axis=-1, keepdims=True)
    total = carry + jnp.sum(both, axis=0, keepdims=True)

    meta = jnp.zeros(logits.shape, F32)
    for k, val in ((META_E1, i1.astype(F32)), (META_E2, i2.astype(F32)), (META_RANK1, rank1),
                   (META_RANK2, rank2), (META_G1, g1), (META_G2, g2)):
        meta = jnp.where(lane == k, val, meta)
    meta_ref[...] = meta
    after_ref[0] = jnp.broadcast_to(total, after_ref.shape[1:])
    carry_ref[...] = jnp.broadcast_to(total, carry_ref.shape)


def _router(x, g, r_pad, tm, blocks_per_chunk):
    t, d = x.shape
    nblk = t // tm
    r_hi = r_pad.astype(BF16)
    counts = pl.BlockSpec((1, 8, LANES), lambda i: (i, 0, 0))
    return pl.pallas_call(
        functools.partial(_router_kernel, blocks_per_chunk=blocks_per_chunk),
        grid=(nblk,),
        in_specs=[
            pl.BlockSpec((tm, d), lambda i: (i, 0)),
            pl.BlockSpec((1, d), lambda i: (0, 0)),
            pl.BlockSpec((d, LANES), lambda i: (0, 0)),
            pl.BlockSpec((d, LANES), lambda i: (0, 0)),
        ],
        out_specs=[
            pl.BlockSpec((tm, d), lambda i: (i, 0)),
            pl.BlockSpec((tm, LANES), lambda i: (i, 0)),
            counts,
        ],
        out_shape=[
            jax.ShapeDtypeStruct((t, d), BF16),
            jax.ShapeDtypeStruct((t, LANES), F32),
            jax.ShapeDtypeStruct((nblk, 8, LANES), F32),
        ],
        scratch_shapes=[pltpu.VMEM((8, LANES), F32)],
        compiler_params=_params("arbitrary"),
        name="router",
    )(x, g, r_hi, (r_pad - r_hi.astype(F32)).astype(BF16))


SLOT_ROW1, SLOT_ROW2, SLOT_G1, SLOT_G2 = range(4)


def _moe_kernel(nt_ref, toff_ref, cuts_ref, hn_ref, slotc_ref, slotr_ref, wg_ref, wu_ref, wd_ref, out_ref,
                xs_ref, ys_ref, *, tr, sub, win):
    c, e, f = pl.program_id(0), pl.program_id(1), pl.program_id(2)
    last_f = pl.num_programs(2) - 1
    grp = c * N_EXPERTS + e
    n_tiles = nt_ref[grp]
    row0 = toff_ref[grp] * tr
    tc = hn_ref.shape[0]
    nsub = tc // sub

    @pl.when((e == 0) & (f == 0))
    def _():
        out_ref[...] = jnp.zeros_like(out_ref)

    def rows_of(r):
        return pl.ds(pl.multiple_of(r * tr, tr), tr)

    def token_window(r):
        base = grp * nsub
        first = r * tr
        last = jnp.minimum(first + tr, cuts_ref[base + nsub - 1]) - 1
        lo = hi = 0
        for k in range(nsub - 1):
            cut = cuts_ref[base + k]
            lo = lo + (cut <= first).astype(jnp.int32)
            hi = hi + (cut <= last).astype(jnp.int32)
        start = jnp.minimum(lo, nsub - win)
        return hi < start + win, pl.multiple_of(start * sub, sub)

    def spans(r, fn):
        covered, tok0 = token_window(r)

        @pl.when(covered)
        def _():
            fn(r, tok0, win * sub)

        @pl.when(jnp.logical_not(covered))
        def _():
            fn(r, 0, tc)

    def gather_span(r, tok0, ntok):
        toks = pl.ds(tok0, ntok)
        target = (lax.broadcasted_iota(jnp.int32, (tr, ntok), 0) + (row0 + r * tr)).astype(F32)
        hit1 = slotr_ref[SLOT_ROW1:SLOT_ROW1 + 1, toks] == target
        hit2 = slotr_ref[SLOT_ROW2:SLOT_ROW2 + 1, toks] == target
        oh = jnp.where(hit1, 1.0, jnp.where(hit2, 1.0, 0.0)).astype(BF16)
        xs_ref[rows_of(r), :] = jnp.dot(oh, hn_ref[toks, :], preferred_element_type=F32).astype(BF16)

    def gather_tile(r, carry):
        spans(r, gather_span)
        return carry

    def ffn_tiles(r, count, first):
        rows = pl.ds(pl.multiple_of(r * tr, tr), count * tr)
        xt = xs_ref[rows, :]
        gate = jnp.dot(xt, wg_ref[...], preferred_element_type=F32)
        up = jnp.dot(xt, wu_ref[...], preferred_element_type=F32)
        act = (gate * jax.nn.sigmoid(gate) * up).astype(BF16)
        y = jnp.dot(act, wd_ref[...], preferred_element_type=F32)
        if first:
            ys_ref[rows, :] = y
        else:
            ys_ref[rows, :] += y

    def scatter_span(r, tok0, ntok):
        toks = pl.ds(tok0, ntok)
        yb = ys_ref[rows_of(r), :].astype(BF16)
        target = (lax.broadcasted_iota(jnp.int32, (ntok, tr), 1) + (row0 + r * tr)).astype(F32)
        hit1 = slotc_ref[toks, SLOT_ROW1:SLOT_ROW1 + 1] == target
        hit2 = slotc_ref[toks, SLOT_ROW2:SLOT_ROW2 + 1] == target
        gh = jnp.where(hit1, slotc_ref[toks, SLOT_G1:SLOT_G1 + 1],
                       jnp.where(hit2, slotc_ref[toks, SLOT_G2:SLOT_G2 + 1], 0.0)).astype(BF16)
        out_ref[toks, :] += jnp.dot(gh, yb, preferred_element_type=F32)

    def scatter_tile(r, carry):
        spans(r, scatter_span)
        return carry

    def ffn_all(first):
        def pair(t, carry):
            ffn_tiles(2 * t, 2, first)
            return carry

        lax.fori_loop(0, n_tiles // 2, pair, 0)

        @pl.when(n_tiles % 2 == 1)
        def _():
            ffn_tiles(n_tiles - 1, 1, first)

    @pl.when(f == 0)
    def _():
        lax.fori_loop(0, n_tiles, gather_tile, 0)
        ffn_all(True)

    @pl.when(f > 0)
    def _():
        ffn_all(False)

    @pl.when(f == last_f)
    def _():
        lax.fori_loop(0, n_tiles, scatter_tile, 0)


def _moe(hn, slot_cols, slot_rows, nt, toff, cuts, wg, wu, wd, tc, tr, tf, sub, win):
    t, d = hn.shape
    ne, _, ff = wg.shape
    grid_spec = pltpu.PrefetchScalarGridSpec(
        num_scalar_prefetch=3,
        grid=(t // tc, ne, ff // tf),
        in_specs=[
            pl.BlockSpec((tc, d), lambda c, e, f, *_: (c, 0)),
            pl.BlockSpec((tc, 4), lambda c, e, f, *_: (c, 0)),
            pl.BlockSpec((8, tc), lambda c, e, f, *_: (0, c)),
            pl.BlockSpec((None, d, tf), lambda c, e, f, *_: (e, 0, f)),
            pl.BlockSpec((None, d, tf), lambda c, e, f, *_: (e, 0, f)),
            pl.BlockSpec((None, tf, d), lambda c, e, f, *_: (e, f, 0)),
        ],
        out_specs=pl.BlockSpec((tc, d), lambda c, e, f, *_: (c, 0)),
        scratch_shapes=[
            pltpu.VMEM((tc + tr, d), BF16),
            pltpu.VMEM((tc + tr, d), F32),
        ],
    )
    return pl.pallas_call(
        functools.partial(_moe_kernel, tr=tr, sub=sub, win=win),
        grid_spec=grid_spec,
        out_shape=jax.ShapeDtypeStruct((t, d), F32),
        compiler_params=_params("arbitrary", "arbitrary", "arbitrary"),
        name="moe",
    )(nt, toff, cuts, hn, slot_cols, slot_rows, wg, wu, wd)


def _final_kernel(x_ref, y_ref, g_ref, out_ref):
    out_ref[...] = _rms(x_ref[...] + y_ref[...], g_ref[...])


def _final_norm(x, y, g, tm):
    t, d = x.shape
    rows = pl.BlockSpec((tm, d), lambda i: (i, 0))
    return pl.pallas_call(
        _final_kernel,
        grid=(t // tm,),
        in_specs=[rows, rows, pl.BlockSpec((1, d), lambda i: (0, 0))],
        out_specs=rows,
        out_shape=jax.ShapeDtypeStruct((t, d), F32),
        compiler_params=_params("arbitrary"),
        name="final_norm",
    )(x, y, g)


def _routed_swiglu(x, norm_g, router_w, wg, wu, wd, final_g, tc, tr, tf, sub, win):
    t, d = x.shape
    nc, nsub = t // tc, tc // sub
    r_pad = jnp.zeros((d, LANES), F32).at[:, :N_EXPERTS].set(router_w)
    hn, meta, after = _router(x, norm_g, r_pad, tm=sub, blocks_per_chunk=nsub)
    through = after[:, 0, :N_EXPERTS].astype(jnp.int32).reshape(nc, nsub, N_EXPERTS)
    cuts = through.transpose(0, 2, 1)
    total = through[:, -1]
    nt = (total + tr - 1) // tr
    toff = jnp.cumsum(nt, axis=1) - nt
    first_row = jnp.repeat((toff * tr).astype(F32), tc, axis=0)
    experts = jnp.arange(N_EXPERTS, dtype=F32)
    pick = lambda e: jnp.sum(jnp.where(e[:, None] == experts, first_row, 0.0), axis=1)
    row1 = pick(meta[:, META_E1]) + meta[:, META_RANK1]
    row2 = pick(meta[:, META_E2]) + meta[:, META_RANK2]
    slot_cols = jnp.stack([row1, row2, meta[:, META_G1], meta[:, META_G2]], axis=1)
    slot_rows = jnp.zeros((8, t), F32).at[0:4].set(slot_cols.T)
    y = _moe(hn, slot_cols, slot_rows, nt.reshape(-1), toff.reshape(-1), cuts.reshape(-1), wg, wu, wd,
             tc=tc, tr=tr, tf=tf, sub=sub, win=win)
    return _final_norm(x, y, final_g, tm=min(1024, t))


def _permute_w_in(w):
    def perm(block):
        d = block.shape[0]
        return block.reshape(d, N_HEADS, 2, QK_DIM).transpose(0, 2, 1, 3).reshape(d, GW)
    parts = [w[:, :6 * GW], perm(w[:, 6 * GW:7 * GW]), perm(w[:, 7 * GW:8 * GW]), w[:, 8 * GW:]]
    return jnp.concatenate(parts, axis=1)


def _block_diag(pool_w):
    n, p, _ = pool_w.shape
    out = jnp.zeros((n * p, n * p), pool_w.dtype)
    for gi in range(n):
        out = out.at[gi * p:(gi + 1) * p, gi * p:(gi + 1) * p].set(pool_w[gi])
    return out


def _token_mixing(x, layer, lam_init, p, rope_c, rope_d):
    b, s, d = x.shape
    row = lambda v: v.reshape(1, -1)
    w_in = _permute_w_in(p["w_in"][layer]).astype(BF16)
    uab, qkvc, qkvc4, qkvc16, qdt, kd, vdt = _norm_proj(x, row(p["norm1_g"][layer]), w_in, rope_c, rope_d,
                                                        tm=512, tk=256)
    dww = jnp.zeros((HALO, GW), F32).at[:CONV_WIDTH].set(p["conv_dw_w"][layer])
    yab = _local_mixers(
        uab, dww, row(p["conv_dw_b"][layer]), row(p["conv_ln_g"][layer]), row(p["conv_ln_b"][layer]),
        p["conv_pw_w"][layer].astype(BF16), _block_diag(p["pool_w"][layer]).astype(BF16),
        row(p["pool_scale"][layer]), ts=512)
    os_, lses = [], []
    for dil, qkv in zip(DILATIONS, (qkvc, qkvc4, qkvc16)):
        o, lse = _band_attn(qkv, dil)
        os_.append(o.reshape(b * s // dil, dil * GW))
        lses.append(lse.reshape(b * s // dil, dil * GW))
    yd = _diff_attn(qdt, kd, vdt, p["diff_lam"][layer], jnp.tile(p["diff_ln_g"][layer], N_HEADS).reshape(GW, 1),
                    lam_init, tq=512)
    return _out_proj(x.reshape(b * s, d), yab.reshape(b * s, 2 * GW), os_, lses, yd.reshape(b * s, GW),
                     p["w_out"][layer].astype(BF16), tm=512)


def kernel(x, norm1_g, w_in, conv_dw_w, conv_dw_b, conv_ln_g, conv_ln_b, conv_pw_w, pool_w, pool_scale,
           diff_lam, diff_ln_g, w_out, norm2_g, ffn_w_gate, ffn_w_up, ffn_w_down, moe_router, moe_w_gate,
           moe_w_up, moe_w_down, final_g):
    b, s, d = x.shape
    p = dict(norm1_g=norm1_g, w_in=w_in, conv_dw_w=conv_dw_w, conv_dw_b=conv_dw_b, conv_ln_g=conv_ln_g,
             conv_ln_b=conv_ln_b, conv_pw_w=conv_pw_w, pool_w=pool_w, pool_scale=pool_scale,
             diff_lam=diff_lam, diff_ln_g=diff_ln_g, w_out=w_out)
    rope_c = _rope_table(s, HEAD_DIM, HEAD_DIM // 4)
    rope_d = _rope_table(s, QK_DIM, QK_DIM // 4)
    row = lambda v: v.reshape(1, -1)

    x1 = _token_mixing(x, 0, 0.8 - 0.6 * math.exp(-0.3 * 0), p, rope_c, rope_d)
    x2 = _ffn(x1, row(norm2_g[0]), ffn_w_gate[0].astype(BF16), ffn_w_up[0].astype(BF16),
              ffn_w_down[0].astype(BF16), tm=512, tf=256)

    x3 = _token_mixing(x2.reshape(b, s, d), 1, 0.8 - 0.6 * math.exp(-0.3 * 1), p, rope_c, rope_d)
    out = _routed_swiglu(x3, row(norm2_g[1]), moe_router[0], moe_w_gate[0].astype(BF16),
                         moe_w_up[0].astype(BF16), moe_w_down[0].astype(BF16), row(final_g),
                         tc=2048, tr=256, tf=512, sub=256, win=5)
    return out.reshape(b, s, d)
```

```python
import functools
import math

import jax
import jax.numpy as jnp
from jax import lax
from jax.experimental import pallas as pl
from jax.experimental.pallas import tpu as pltpu

D_MODEL = 1024
GW = 256
N_HEADS = 4
HEAD_DIM = 64
QK_DIM = 32
V_ROWS = HEAD_DIM + 16
CONV_WIDTH = 31
POOL_WINDOWS = (2, 4, 8, 16)
DILATIONS = (1, 4, 16)
BAND = 128
ROPE_THETA = 500000.0
N_EXPERTS = 8
NORM_EPS = 1e-5
HALO = 32
LANES = 128
SUBLANES = 8
FFN_SUBTILE = 256
VMEM_LIMIT = 56 * 1024 * 1024

BF16 = jnp.bfloat16
F32 = jnp.float32
NT_DIMS = (((1,), (1,)), ((), ()))


def _params(*sem):
    return pltpu.CompilerParams(dimension_semantics=sem, vmem_limit_bytes=VMEM_LIMIT)


def _lanes_in(shape, start, width):
    lane = lax.broadcasted_iota(jnp.int32, shape, len(shape) - 1)
    return (lane >= start) & (lane < start + width)


def _rms(x, g):
    ms = jnp.mean(x * x, axis=-1, keepdims=True)
    return x * lax.rsqrt(ms + NORM_EPS) * g


def _rope_table(seq, width, rot):
    half = rot // 2
    pos = jnp.arange(seq, dtype=F32)
    inv = ROPE_THETA ** (-jnp.arange(0, rot, 2, dtype=F32) / rot)
    ang = pos[:, None] * inv[None, :]
    cos, sin = jnp.cos(ang), jnp.sin(ang)
    lane = jnp.arange(GW) % width
    f = lane % half
    first = lane < half
    second = (lane >= half) & (lane < rot)
    c = jnp.where((lane < rot)[None, :], cos[:, f], 1.0)
    sn = jnp.where(first[None, :], -sin[:, f], 0.0)
    sp = jnp.where(second[None, :], sin[:, f], 0.0)
    return jnp.concatenate([c, sn, sp], axis=1)


def _rope(u, tab_ref, half):
    return (u * tab_ref[:, 0:GW]
            + pltpu.roll(u, GW - half, 1) * tab_ref[:, GW:2 * GW]
            + pltpu.roll(u, half, 1) * tab_ref[:, 2 * GW:3 * GW])


def _norm_proj_kernel(x_ref, g_ref, w_ref, rc_ref, rd_ref, uab_ref, qkvc_ref, qkv4_ref, qkv16_ref, qt_ref, kd_ref,
                      vt_ref, v_ref, cs_ref, *, tk):
    h = _rms(x_ref[0], g_ref[...]).astype(BF16)

    def proj(grp):
        return jnp.dot(h, w_ref[:, grp * GW:(grp + 1) * GW], preferred_element_type=F32)

    for grp in range(3):
        uab_ref[0, :, grp * GW:(grp + 1) * GW] = proj(grp)
    c_scale = HEAD_DIM ** -0.5
    d_scale = QK_DIM ** -0.5 * math.log2(math.e)
    parts = (_rope(proj(3), rc_ref, 8) * c_scale, _rope(proj(4), rc_ref, 8), proj(5))
    for k, val in enumerate(parts):
        qkvc_ref[0, :, k * GW:(k + 1) * GW] = val.astype(BF16)
        for half in range(GW // LANES):
            cs_ref[2 * k + half] = val[:, half * LANES:(half + 1) * LANES]
    tm = cs_ref.shape[1]
    for dil, out_ref in zip(DILATIONS[1:], (qkv4_ref, qkv16_ref)):
        for r in range(dil):
            for slab in range(cs_ref.shape[0]):
                rows = cs_ref[slab, pl.ds(r, tm // dil, stride=dil), :]
                col = r * 3 * GW + slab * LANES
                out_ref[0, :, col:col + LANES] = rows.astype(BF16)
    qt_ref[0] = jnp.transpose(_rope(proj(6), rd_ref, 4) * d_scale).astype(BF16)
    kd_ref[0] = _rope(proj(7), rd_ref, 4).astype(BF16)
    v_ref[...] = proj(8)
    vt = jnp.transpose(v_ref[...]).astype(BF16)
    ones = jnp.ones((V_ROWS - HEAD_DIM, tk), BF16)
    for blk in range(vt.shape[1] // tk):
        for hd in range(N_HEADS):
            vt_ref[0, blk, hd * V_ROWS:hd * V_ROWS + HEAD_DIM, :] = (
                vt[hd * HEAD_DIM:(hd + 1) * HEAD_DIM, blk * tk:(blk + 1) * tk])
            vt_ref[0, blk, hd * V_ROWS + HEAD_DIM:(hd + 1) * V_ROWS, :] = ones


def _norm_proj(x, g, w, rope_c, rope_d, tm, tk):
    b, s, d = x.shape
    grid = (s // tm, b)
    return pl.pallas_call(
        functools.partial(_norm_proj_kernel, tk=tk),
        grid=grid,
        in_specs=[
            pl.BlockSpec((1, tm, d), lambda i, j: (j, i, 0)),
            pl.BlockSpec((1, d), lambda i, j: (0, 0)),
            pl.BlockSpec((d, 9 * GW), lambda i, j: (0, 0)),
            pl.BlockSpec((tm, 3 * GW), lambda i, j: (i, 0)),
            pl.BlockSpec((tm, 3 * GW), lambda i, j: (i, 0)),
        ],
        out_specs=[
            pl.BlockSpec((1, tm, 3 * GW), lambda i, j: (j, i, 0)),
            pl.BlockSpec((1, tm, 3 * GW), lambda i, j: (j, i, 0)),
            *[pl.BlockSpec((1, tm // dil, dil * 3 * GW), lambda i, j: (j, i, 0)) for dil in DILATIONS[1:]],
            pl.BlockSpec((1, GW, tm), lambda i, j: (j, 0, i)),
            pl.BlockSpec((1, tm, GW), lambda i, j: (j, i, 0)),
            pl.BlockSpec((1, tm // tk, N_HEADS * V_ROWS, tk), lambda i, j: (j, i, 0, 0)),
        ],
        out_shape=[
            jax.ShapeDtypeStruct((b, s, 3 * GW), F32),
            jax.ShapeDtypeStruct((b, s, 3 * GW), BF16),
            *[jax.ShapeDtypeStruct((b, s // dil, dil * 3 * GW), BF16) for dil in DILATIONS[1:]],
            jax.ShapeDtypeStruct((b, GW, s), BF16),
            jax.ShapeDtypeStruct((b, s, GW), BF16),
            jax.ShapeDtypeStruct((b, s // tk, N_HEADS * V_ROWS, tk), BF16),
        ],
        scratch_shapes=[
            pltpu.VMEM((tm, GW), F32),
            pltpu.VMEM((3 * GW // LANES, tm, LANES), F32),
        ],
        compiler_params=_params("arbitrary", "arbitrary"),
        name="norm_proj",
    )(x, g, w, rope_c, rope_d)


def _local_kernel(u_ref, dww_ref, dwb_ref, lng_ref, lnb_ref, pww_ref, plw_ref, pls_ref,
                  y_ref, zbuf, pbuf, shift_ref, *, ts):
    si = pl.program_id(1)

    @pl.when(si == 0)
    def _():
        zbuf[0:HALO, :] = jnp.zeros((HALO, GW), F32)
        pbuf[0:HALO, :] = jnp.zeros((HALO, GW), F32)

    a = u_ref[0, :, 0:GW]
    b = u_ref[0, :, GW:2 * GW]
    zbuf[HALO:HALO + ts, :] = a * jax.nn.sigmoid(b)
    pbuf[HALO:HALO + ts, :] = u_ref[0, :, 2 * GW:3 * GW]

    def by_phase(buf, shifted, offsets, term):
        total = None
        for phase in range(SUBLANES):
            offs = [o for o in offsets if o % SUBLANES == phase]
            if not offs:
                continue
            src = buf
            if phase:
                n = max(offs) - phase + ts
                shifted[phase, 0:n, :] = buf[phase:phase + n, :]
                src = shifted.at[phase]
            for o in offs:
                t = term(o, src[o - phase:o - phase + ts, :])
                total = t if total is None else total + t
        return total

    base = HALO - (CONV_WIDTH - 1)
    acc = by_phase(zbuf, shift_ref.at[0], range(base, base + CONV_WIDTH),
                   lambda o, rows: dww_ref[o - base:o - base + 1, :] * rows)
    z = acc + dwb_ref[...]
    mu = jnp.mean(z, axis=-1, keepdims=True)
    zc = z - mu
    var = jnp.mean(zc * zc, axis=-1, keepdims=True)
    zn = zc * lax.rsqrt(var + NORM_EPS) * lng_ref[...] + lnb_ref[...]
    zs = zn * jax.nn.sigmoid(zn)
    y_ref[0, :, 0:GW] = jnp.dot(zs.astype(BF16), pww_ref[...], preferred_element_type=F32).astype(BF16)

    lane = lax.broadcasted_iota(jnp.int32, (ts, GW), 1)
    pool_group = GW // len(POOL_WINDOWS)
    win = jnp.zeros((ts, GW), jnp.int32)
    for gi, w in enumerate(POOL_WINDOWS):
        win = jnp.where(lane >= gi * pool_group, w, win)
    p0 = pbuf[HALO:HALO + ts, :]
    lag = lambda o: HALO - o
    tot = by_phase(pbuf, shift_ref.at[1], range(HALO - max(POOL_WINDOWS) + 1, HALO + 1),
                   lambda o, rows: rows if lag(o) < min(POOL_WINDOWS) else jnp.where(win > lag(o), rows, 0.0))
    pos = si * ts + lax.broadcasted_iota(jnp.int32, (ts, GW), 0)
    cnt = jnp.minimum(pos + 1, win).astype(F32)
    pooled = tot / cnt - p0
    yb = jnp.dot(pooled.astype(BF16), plw_ref[...], preferred_element_type=F32) * pls_ref[...]
    y_ref[0, :, GW:2 * GW] = yb.astype(BF16)

    zbuf[0:HALO, :] = zbuf[ts:ts + HALO, :]
    pbuf[0:HALO, :] = pbuf[ts:ts + HALO, :]


def _local_mixers(uab, dww, dwb, lng, lnb, pww, plw, pls, ts):
    b, s, _ = uab.shape
    vec = lambda: pl.BlockSpec((1, GW), lambda i, j: (0, 0))
    mat = lambda: pl.BlockSpec((GW, GW), lambda i, j: (0, 0))
    return pl.pallas_call(
        functools.partial(_local_kernel, ts=ts),
        grid=(b, s // ts),
        in_specs=[
            pl.BlockSpec((1, ts, 3 * GW), lambda i, j: (i, j, 0)),
            pl.BlockSpec((HALO, GW), lambda i, j: (0, 0)),
            vec(), vec(), vec(), mat(), mat(), vec(),
        ],
        out_specs=pl.BlockSpec((1, ts, 2 * GW), lambda i, j: (i, j, 0)),
        out_shape=jax.ShapeDtypeStruct((b, s, 2 * GW), BF16),
        scratch_shapes=[
            pltpu.VMEM((HALO + ts, GW), F32),
            pltpu.VMEM((HALO + ts, GW), F32),
            pltpu.VMEM((2, SUBLANES, HALO + ts, GW), F32),
        ],
        compiler_params=_params("arbitrary", "arbitrary"),
        name="local_mixers",
    )(uab, dww, dwb, lng, lnb, pww, plw, pls)


def _band_attn_kernel(q_ref, k_ref, v_ref, o_ref, lse_ref, s_ref):
    row = lax.broadcasted_iota(jnp.int32, (BAND, 2 * BAND), 0)
    col = lax.broadcasted_iota(jnp.int32, (BAND, 2 * BAND), 1)
    halves = [_lanes_in((BAND, LANES), sub * HEAD_DIM, HEAD_DIM) for sub in range(LANES // HEAD_DIM)]

    def starts(n):
        return pl.multiple_of(n * BAND, BAND), pl.multiple_of(jnp.maximum(n - 1, 0) * BAND, BAND)

    def scores(n, slot):
        q0, k0 = starts(n)
        q = q_ref[0, pl.ds(q0, BAND), :].astype(F32)
        kw = k_ref[0, pl.ds(k0, 2 * BAND), :]
        dist = (row - col) + (q0 - k0)
        mask = (dist >= 0) & (dist <= BAND)
        for h in range(N_HEADS):
            qh = jnp.where(_lanes_in((BAND, GW), h * HEAD_DIM, HEAD_DIM), q, 0.0).astype(BF16)
            s = lax.dot_general(qh, kw, NT_DIMS, preferred_element_type=F32)
            s_ref[slot * N_HEADS + h] = jnp.where(mask, s, -jnp.inf)

    def values(n, slot):
        q0, k0 = starts(n)
        for tile in range(GW // LANES):
            vw = v_ref[0, pl.ds(k0, 2 * BAND), tile * LANES:(tile + 1) * LANES]
            o = jnp.zeros((BAND, LANES), F32)
            lse = jnp.zeros((BAND, LANES), F32)
            for sub in range(LANES // HEAD_DIM):
                s = s_ref[slot * N_HEADS + tile * (LANES // HEAD_DIM) + sub]
                m = jnp.max(s, axis=-1, keepdims=True)
                p = jnp.exp(s - m)
                l = jnp.sum(p, axis=-1, keepdims=True)
                acc = jnp.dot(p.astype(BF16), vw, preferred_element_type=F32)
                mine = halves[sub]
                o = jnp.where(mine, acc / l, o)
                lse = jnp.where(mine, m + jnp.log(l), lse)
            o_ref[0, pl.ds(q0, BAND), tile * LANES:(tile + 1) * LANES] = o.astype(o_ref.dtype)
            lse_ref[0, pl.ds(q0, BAND), tile * LANES:(tile + 1) * LANES] = lse

    def pair(t, carry):
        scores(2 * t, 0)
        scores(2 * t + 1, 1)
        values(2 * t, 0)
        values(2 * t + 1, 1)
        return carry

    lax.fori_loop(0, q_ref.shape[1] // (2 * BAND), pair, 0)


def _band_attn(qkv, dil):
    b, l, _ = qkv.shape
    assert l % (2 * BAND) == 0
    part = lambda c: pl.BlockSpec((1, l, GW), lambda i, r: (i, 0, 3 * r + c))
    out = pl.BlockSpec((1, l, GW), lambda i, r: (i, 0, r))
    return pl.pallas_call(
        _band_attn_kernel,
        grid=(b, dil),
        in_specs=[part(0), part(1), part(2)],
        out_specs=[out, out],
        out_shape=[jax.ShapeDtypeStruct((b, l, dil * GW), BF16), jax.ShapeDtypeStruct((b, l, dil * GW), F32)],
        scratch_shapes=[pltpu.VMEM((2 * N_HEADS, BAND, 2 * BAND), F32)],
        compiler_params=_params("arbitrary", "arbitrary"),
        name="band_attn",
    )(qkv, qkv, qkv)


def _diff_attn_kernel(lam_ref, g_ref, qt_ref, k_ref, vt_ref, o_ref, qm_ref, m_ref, acc_ref, yt_ref,
                      st_ref, mx_ref, *, tq, lam_init):
    n_chain = 2 * N_HEADS
    i = pl.program_id(1)
    @pl.when((pl.program_id(0) == 0) & (i == 0))
    def _():
        qm_ref[...] = jnp.zeros(qm_ref.shape, BF16)

    for c in range(n_chain):
        qm_ref[c, c * QK_DIM:(c + 1) * QK_DIM, :] = qt_ref[0, c * QK_DIM:(c + 1) * QK_DIM, :]
    m_ref[...] = jnp.full(m_ref.shape, -jnp.inf, F32)
    acc_ref[...] = jnp.zeros(acc_ref.shape, F32)
    tk = st_ref.shape[1]
    ratio = tq // tk
    ahead = (lax.broadcasted_iota(jnp.int32, (tk, tq), 0)
             - lax.broadcasted_iota(jnp.int32, (tk, tq), 1))

    def score(c, kb, diag):
        st = jnp.dot(kb, qm_ref[c], preferred_element_type=F32)
        if diag is not None:
            st = jnp.where(ahead <= -diag * tk, st, -jnp.inf)
        st_ref[c] = st
        mx_ref[c] = jnp.max(st, axis=0, keepdims=True)

    def value(c, j):
        h = c % N_HEADS
        m_prev = m_ref[c]
        m_next = jnp.maximum(m_prev, mx_ref[c])
        alpha = jnp.exp2(m_prev - m_next)
        p = jnp.exp2(st_ref[c] - m_next).astype(BF16)
        vt = vt_ref[0, j, h * V_ROWS:(h + 1) * V_ROWS, :]
        acc_ref[c] = alpha * acc_ref[c] + jnp.dot(vt, p, preferred_element_type=F32)
        m_ref[c] = m_next

    def k_block(j):
        return k_ref[0, pl.ds(pl.multiple_of(j * tk, tk), tk), :]

    def scores(j, diag):
        kb = k_block(j)
        for c in range(n_chain):
            score(c, kb, diag)

    def values(j):
        for c in range(n_chain):
            value(c, j)

    def values_then_scores(j, diag):
        kb = k_block(j + 1)
        for c in range(n_chain):
            value(c, j)
            score(c, kb, diag)

    before = ratio * i

    @pl.when(i == 0)
    def _():
        scores(0, 0)

    @pl.when(i > 0)
    def _():
        scores(0, None)

    def body(j, carry):
        values_then_scores(j, None)
        return carry

    lax.fori_loop(0, before - 1, body, 0)

    @pl.when(i > 0)
    def _():
        values_then_scores(before - 1, 0)

    for diag in range(1, ratio):
        values_then_scores(before + diag - 1, diag)
    values(before + ratio - 1)

    lv = lam_ref[...]
    lam = (jnp.exp(jnp.sum(lv[0:1] * lv[1:2], axis=-1, keepdims=True))
           - jnp.exp(jnp.sum(lv[2:3] * lv[3:4], axis=-1, keepdims=True)) + lam_init)
    for h in range(N_HEADS):
        a1, a2 = acc_ref[h], acc_ref[N_HEADS + h]
        oh = (a1[:HEAD_DIM] / a1[HEAD_DIM:HEAD_DIM + 1]
              - lam * (a2[:HEAD_DIM] / a2[HEAD_DIM:HEAD_DIM + 1]))
        ms = jnp.mean(oh * oh, axis=0, keepdims=True)
        g = g_ref[h * HEAD_DIM:(h + 1) * HEAD_DIM, :]
        yt_ref[h * HEAD_DIM:(h + 1) * HEAD_DIM, :] = oh * lax.rsqrt(ms + NORM_EPS) * (g * (1.0 - lam_init))
    o_ref[0] = jnp.transpose(yt_ref[...]).astype(BF16)


def _diff_attn(qt, k, vt, lam_vecs, g_col, lam_init, tq):
    b, s, _ = k.shape
    nk, tk = vt.shape[1], vt.shape[3]
    assert tq % tk == 0 and s % tq == 0
    n_chain = 2 * N_HEADS
    return pl.pallas_call(
        functools.partial(_diff_attn_kernel, tq=tq, lam_init=lam_init),
        grid=(b, s // tq),
        in_specs=[
            pl.BlockSpec((4, QK_DIM), lambda i, j: (0, 0)),
            pl.BlockSpec((GW, 1), lambda i, j: (0, 0)),
            pl.BlockSpec((1, GW, tq), lambda i, j: (i, 0, j)),
            pl.BlockSpec((1, s, GW), lambda i, j: (i, 0, 0)),
            pl.BlockSpec((1, nk, N_HEADS * V_ROWS, tk), lambda i, j: (i, 0, 0, 0)),
        ],
        out_specs=pl.BlockSpec((1, tq, GW), lambda i, j: (i, j, 0)),
        out_shape=jax.ShapeDtypeStruct((b, s, GW), BF16),
        scratch_shapes=[
            pltpu.VMEM((n_chain, GW, tq), BF16),
            pltpu.VMEM((n_chain, 1, tq), F32),
            pltpu.VMEM((n_chain, V_ROWS, tq), F32),
            pltpu.VMEM((GW, tq), F32),
            pltpu.VMEM((n_chain, tk, tq), F32),
            pltpu.VMEM((n_chain, 1, tq), F32),
        ],
        compiler_params=_params("arbitrary", "arbitrary"),
        name="diff_attn",
    )(lam_vecs, g_col, qt, k, vt)


def _out_proj_kernel(x_ref, yab_ref, o1_ref, o2_ref, o3_ref, l1_ref, l2_ref, l3_ref, yd_ref, w_ref, out_ref,
                     nat_ref):
    tm = x_ref.shape[0]

    def natural(ref, dil, slot):
        for r in range(dil):
            for half in range(GW // LANES):
                col = r * GW + half * LANES
                nat_ref[2 * slot + half, pl.ds(r, tm // dil, stride=dil), :] = ref[:, col:col + LANES].astype(F32)
        return jnp.concatenate([nat_ref[2 * slot], nat_ref[2 * slot + 1]], axis=1)

    d2, d3 = DILATIONS[1:]
    l1, l2, l3 = l1_ref[...], natural(l2_ref, d2, 0), natural(l3_ref, d3, 1)
    o2, o3 = natural(o2_ref, d2, 2), natural(o3_ref, d3, 3)
    m = jnp.maximum(jnp.maximum(l1, l2), l3)
    w1, w2, w3 = jnp.exp(l1 - m), jnp.exp(l2 - m), jnp.exp(l3 - m)
    yc = (w1 * o1_ref[...] + w2 * o2 + w3 * o3) / (w1 + w2 + w3)
    acc = jnp.dot(yab_ref[...], w_ref[0:2 * GW, :], preferred_element_type=F32)
    acc = acc + jnp.dot(yc.astype(BF16), w_ref[2 * GW:3 * GW, :], preferred_element_type=F32)
    acc = acc + jnp.dot(yd_ref[...], w_ref[3 * GW:4 * GW, :], preferred_element_type=F32)
    out_ref[...] = x_ref[...] + acc


def _out_proj(x, yab, os_, lses, yd, w, tm):
    t, d = x.shape
    rows = lambda c: pl.BlockSpec((tm, c), lambda i: (i, 0))
    branch = [pl.BlockSpec((tm // dil, dil * GW), lambda i: (i, 0)) for dil in DILATIONS]
    return pl.pallas_call(
        _out_proj_kernel,
        grid=(t // tm,),
        in_specs=[rows(d), rows(2 * GW)] + branch * 2 + [rows(GW), pl.BlockSpec((d, d), lambda i: (0, 0))],
        out_specs=rows(d),
        out_shape=jax.ShapeDtypeStruct((t, d), F32),
        scratch_shapes=[pltpu.VMEM((4 * GW // LANES, tm, LANES), F32)],
        compiler_params=_params("arbitrary"),
        name="out_proj",
    )(x, yab, *os_, *lses, yd, w)


def _ffn_kernel(x_ref, g_ref, wg_ref, wu_ref, wd_ref, out_ref, acc_ref, *, tf):
    hn = _rms(x_ref[...], g_ref[...]).astype(BF16)
    acc_ref[...] = x_ref[...]
    for f0 in range(0, wg_ref.shape[1], tf):
        gate = jnp.dot(hn, wg_ref[:, f0:f0 + tf], preferred_element_type=F32)
        up = jnp.dot(hn, wu_ref[:, f0:f0 + tf], preferred_element_type=F32)
        act = (gate * jax.nn.sigmoid(gate) * up).astype(BF16)
        acc_ref[...] += jnp.dot(act, wd_ref[f0:f0 + tf, :], preferred_element_type=F32)
    out_ref[...] = acc_ref[...]


def _ffn(x, g, wg, wu, wd, tm, tf):
    t, d = x.shape
    ff = wg.shape[1]
    assert ff % tf == 0
    whole = lambda shape: pl.BlockSpec(shape, lambda i: (0, 0))
    return pl.pallas_call(
        functools.partial(_ffn_kernel, tf=tf),
        grid=(t // tm,),
        in_specs=[pl.BlockSpec((tm, d), lambda i: (i, 0)), whole((1, d)), whole((d, ff)), whole((d, ff)),
                  whole((ff, d))],
        out_specs=pl.BlockSpec((tm, d), lambda i: (i, 0)),
        out_shape=jax.ShapeDtypeStruct((t, d), F32),
        scratch_shapes=[pltpu.VMEM((tm, d), F32)],
        compiler_params=_params("arbitrary"),
        name="ffn_dense",
    )(x, g, wg, wu, wd)


META_E1, META_E2, META_RANK1, META_RANK2, META_G1, META_G2 = range(6)


def _router_kernel(x_ref, g_ref, rhi_ref, rlo_ref, hn_ref, meta_ref, after_ref, carry_ref, *, blocks_per_chunk):
    @pl.when(pl.program_id(0) % blocks_per_chunk == 0)
    def _():
        carry_ref[...] = jnp.zeros_like(carry_ref)

    h = _rms(x_ref[...], g_ref[...])
    hn_ref[...] = h.astype(BF16)
    h_hi = h.astype(BF16)
    h_lo = (h - h_hi.astype(F32)).astype(BF16)
    logits = (jnp.dot(h_hi, rhi_ref[...], preferred_element_type=F32)
              + jnp.dot(h_hi, rlo_ref[...], preferred_element_type=F32)
              + jnp.dot(h_lo, rhi_ref[...], preferred_element_type=F32))
    tm = logits.shape[0]
    lane = lax.broadcasted_iota(jnp.int32, logits.shape, 1)
    logits = jnp.where(lane < N_EXPERTS, logits, -jnp.inf)
    m1 = jnp.max(logits, axis=-1, keepdims=True)
    i1 = jnp.min(jnp.where(logits == m1, lane, LANES), axis=-1, keepdims=True)
    rest = jnp.where(lane == i1, -jnp.inf, logits)
    m2 = jnp.max(rest, axis=-1, keepdims=True)
    i2 = jnp.min(jnp.where(rest == m2, lane, LANES), axis=-1, keepdims=True)
    e2 = jnp.exp(m2 - m1)
    g1 = 1.0 / (1.0 + e2)
    g2 = e2 / (1.0 + e2)

    oh1 = lane == i1
    oh2 = lane == i2
    both = jnp.where(oh1, 1.0, jnp.where(oh2, 1.0, 0.0))
    row = lax.broadcasted_iota(jnp.int32, (tm, tm), 0)
    col = lax.broadcasted_iota(jnp.int32, (tm, tm), 1)
    earlier = jnp.where(col < row, 1.0, 0.0).astype(BF16)
    carry = carry_ref[0:1, :]
    prefix = jnp.dot(earlier, both.astype(BF16), preferred_element_type=F32) + carry
    rank1 = jnp.sum(jnp.where(oh1, prefix, 0.0), axis=-1, keepdims=True)
    rank2 = jnp.sum(jnp.where(oh2, prefix, 0.0), axis=-1, keepdims=True)
    total = carry + jnp.sum(both, axis=0, keepdims=True)

    meta = jnp.zeros(logits.shape, F32)
    for k, val in ((META_E1, i1.astype(F32)), (META_E2, i2.astype(F32)), (META_RANK1, rank1),
                   (META_RANK2, rank2), (META_G1, g1), (META_G2, g2)):
        meta = jnp.where(lane == k, val, meta)
    meta_ref[...] = meta
    after_ref[0] = jnp.broadcast_to(total, after_ref.shape[1:])
    carry_ref[...] = jnp.broadcast_to(total, carry_ref.shape)


def _router(x, g, r_pad, tm, blocks_per_chunk):
    t, d = x.shape
    nblk = t // tm
    r_hi = r_pad.astype(BF16)
    counts = pl.BlockSpec((1, 8, LANES), lambda i: (i, 0, 0))
    return pl.pallas_call(
        functools.partial(_router_kernel, blocks_per_chunk=blocks_per_chunk),
        grid=(nblk,),
        in_specs=[
            pl.BlockSpec((tm, d), lambda i: (i, 0)),
            pl.BlockSpec((1, d), lambda i: (0, 0)),
            pl.BlockSpec((d, LANES), lambda i: (0, 0)),
            pl.BlockSpec((d, LANES), lambda i: (0, 0)),
        ],
        out_specs=[
            pl.BlockSpec((tm, d), lambda i: (i, 0)),
            pl.BlockSpec((tm, LANES), lambda i: (i, 0)),
            counts,
        ],
        out_shape=[
            jax.ShapeDtypeStruct((t, d), BF16),
            jax.ShapeDtypeStruct((t, LANES), F32),
            jax.ShapeDtypeStruct((nblk, 8, LANES), F32),
        ],
        scratch_shapes=[pltpu.VMEM((8, LANES), F32)],
        compiler_params=_params("arbitrary"),
        name="router",
    )(x, g, r_hi, (r_pad - r_hi.astype(F32)).astype(BF16))


SLOT_ROW1, SLOT_ROW2, SLOT_G1, SLOT_G2 = range(4)


def _moe_kernel(nt_ref, toff_ref, cuts_ref, hn_ref, slotc_ref, slotr_ref, wg_ref, wu_ref, wd_ref, out_ref,
                xs_ref, ys_ref, *, tr, sub, win):
    c, e, f = pl.program_id(0), pl.program_id(1), pl.program_id(2)
    last_f = pl.num_programs(2) - 1
    grp = c * N_EXPERTS + e
    n_tiles = nt_ref[grp]
    row0 = toff_ref[grp] * tr
    tc = hn_ref.shape[0]
    nsub = tc // sub

    @pl.when((e == 0) & (f == 0))
    def _():
        out_ref[...] = jnp.zeros_like(out_ref)

    def rows_of(r):
        return pl.ds(pl.multiple_of(r * tr, tr), tr)

    def token_window(r):
        base = grp * nsub
        first = r * tr
        last = jnp.minimum(first + tr, cuts_ref[base + nsub - 1]) - 1
        lo = hi = 0
        for k in range(nsub - 1):
            cut = cuts_ref[base + k]
            lo = lo + (cut <= first).astype(jnp.int32)
            hi = hi + (cut <= last).astype(jnp.int32)
        start = jnp.minimum(lo, nsub - win)
        return hi < start + win, pl.multiple_of(start * sub, sub)

    def spans(r, fn):
        covered, tok0 = token_window(r)

        @pl.when(covered)
        def _():
            fn(r, tok0, win * sub)

        @pl.when(jnp.logical_not(covered))
        def _():
            fn(r, 0, tc)

    def gather_span(r, tok0, ntok):
        toks = pl.ds(tok0, ntok)
        target = (lax.broadcasted_iota(jnp.int32, (tr, ntok), 0) + (row0 + r * tr)).astype(F32)
        hit1 = slotr_ref[SLOT_ROW1:SLOT_ROW1 + 1, toks] == target
        hit2 = slotr_ref[SLOT_ROW2:SLOT_ROW2 + 1, toks] == target
        oh = jnp.where(hit1, 1.0, jnp.where(hit2, 1.0, 0.0)).astype(BF16)
        xs_ref[rows_of(r), :] = jnp.dot(oh, hn_ref[toks, :], preferred_element_type=F32).astype(BF16)

    def gather_tile(r, carry):
        spans(r, gather_span)
        return carry

    def ffn_tiles(r, count, first):
        rows = pl.ds(pl.multiple_of(r * tr, tr), count * tr)
        xt = xs_ref[rows, :]
        for f0 in range(0, wg_ref.shape[1], FFN_SUBTILE):
            gate = jnp.dot(xt, wg_ref[:, f0:f0 + FFN_SUBTILE], preferred_element_type=F32)
            up = jnp.dot(xt, wu_ref[:, f0:f0 + FFN_SUBTILE], preferred_element_type=F32)
            act = (gate * jax.nn.sigmoid(gate) * up).astype(BF16)
            y = jnp.dot(act, wd_ref[f0:f0 + FFN_SUBTILE, :], preferred_element_type=F32)
            if first and f0 == 0:
                ys_ref[rows, :] = y
            else:
                ys_ref[rows, :] += y

    def scatter_span(r, tok0, ntok):
        toks = pl.ds(tok0, ntok)
        yb = ys_ref[rows_of(r), :].astype(BF16)
        target = (lax.broadcasted_iota(jnp.int32, (ntok, tr), 1) + (row0 + r * tr)).astype(F32)
        hit1 = slotc_ref[toks, SLOT_ROW1:SLOT_ROW1 + 1] == target
        hit2 = slotc_ref[toks, SLOT_ROW2:SLOT_ROW2 + 1] == target
        gh = jnp.where(hit1, slotc_ref[toks, SLOT_G1:SLOT_G1 + 1],
                       jnp.where(hit2, slotc_ref[toks, SLOT_G2:SLOT_G2 + 1], 0.0)).astype(BF16)
        out_ref[toks, :] += jnp.dot(gh, yb, preferred_element_type=F32)

    def scatter_tile(r, carry):
        spans(r, scatter_span)
        return carry

    def ffn_all(first):
        def pair(t, carry):
            ffn_tiles(2 * t, 2, first)
            return carry

        lax.fori_loop(0, n_tiles // 2, pair, 0)

        @pl.when(n_tiles % 2 == 1)
        def _():
            ffn_tiles(n_tiles - 1, 1, first)

    @pl.when(f == 0)
    def _():
        lax.fori_loop(0, n_tiles, gather_tile, 0)
        ffn_all(True)

    @pl.when(f > 0)
    def _():
        ffn_all(False)

    @pl.when(f == last_f)
    def _():
        lax.fori_loop(0, n_tiles, scatter_tile, 0)


def _moe(hn, slot_cols, slot_rows, nt, toff, cuts, wg, wu, wd, tc, tr, tf, sub, win):
    t, d = hn.shape
    ne, _, ff = wg.shape
    grid_spec = pltpu.PrefetchScalarGridSpec(
        num_scalar_prefetch=3,
        grid=(t // tc, ne, ff // tf),
        in_specs=[
            pl.BlockSpec((tc, d), lambda c, e, f, *_: (c, 0), pipeline_mode=pl.Buffered(1)),
            pl.BlockSpec((tc, 4), lambda c, e, f, *_: (c, 0), pipeline_mode=pl.Buffered(1)),
            pl.BlockSpec((8, tc), lambda c, e, f, *_: (0, c), pipeline_mode=pl.Buffered(1)),
            pl.BlockSpec((None, d, tf), lambda c, e, f, *_: (e, 0, f)),
            pl.BlockSpec((None, d, tf), lambda c, e, f, *_: (e, 0, f)),
            pl.BlockSpec((None, tf, d), lambda c, e, f, *_: (e, f, 0)),
        ],
        out_specs=pl.BlockSpec((tc, d), lambda c, e, f, *_: (c, 0), pipeline_mode=pl.Buffered(1)),
        scratch_shapes=[
            pltpu.VMEM((tc + tr, d), BF16),
            pltpu.VMEM((tc + tr, d), F32),
        ],
    )
    return pl.pallas_call(
        functools.partial(_moe_kernel, tr=tr, sub=sub, win=win),
        grid_spec=grid_spec,
        out_shape=jax.ShapeDtypeStruct((t, d), F32),
        compiler_params=_params("arbitrary", "arbitrary", "arbitrary"),
        name="moe",
    )(nt, toff, cuts, hn, slot_cols, slot_rows, wg, wu, wd)


def _final_kernel(x_ref, y_ref, g_ref, out_ref):
    out_ref[...] = _rms(x_ref[...] + y_ref[...], g_ref[...])


def _final_norm(x, y, g, tm):
    t, d = x.shape
    rows = pl.BlockSpec((tm, d), lambda i: (i, 0))
    return pl.pallas_call(
        _final_kernel,
        grid=(t // tm,),
        in_specs=[rows, rows, pl.BlockSpec((1, d), lambda i: (0, 0))],
        out_specs=rows,
        out_shape=jax.ShapeDtypeStruct((t, d), F32),
        compiler_params=_params("arbitrary"),
        name="final_norm",
    )(x, y, g)


def _routed_swiglu(x, norm_g, router_w, wg, wu, wd, final_g, tc, tr, tf, sub, win):
    t, d = x.shape
    nc, nsub = t // tc, tc // sub
    r_pad = jnp.zeros((d, LANES), F32).at[:, :N_EXPERTS].set(router_w)
    hn, meta, after = _router(x, norm_g, r_pad, tm=sub, blocks_per_chunk=nsub)
    through = after[:, 0, :N_EXPERTS].astype(jnp.int32).reshape(nc, nsub, N_EXPERTS)
    cuts = through.transpose(0, 2, 1)
    total = through[:, -1]
    nt = (total + tr - 1) // tr
    toff = jnp.cumsum(nt, axis=1) - nt
    first_row = jnp.repeat((toff * tr).astype(F32), tc, axis=0)
    experts = jnp.arange(N_EXPERTS, dtype=F32)
    pick = lambda e: jnp.sum(jnp.where(e[:, None] == experts, first_row, 0.0), axis=1)
    row1 = pick(meta[:, META_E1]) + meta[:, META_RANK1]
    row2 = pick(meta[:, META_E2]) + meta[:, META_RANK2]
    slot_cols = jnp.stack([row1, row2, meta[:, META_G1], meta[:, META_G2]], axis=1)
    slot_rows = jnp.zeros((8, t), F32).at[0:4].set(slot_cols.T)
    y = _moe(hn, slot_cols, slot_rows, nt.reshape(-1), toff.reshape(-1), cuts.reshape(-1), wg, wu, wd,
             tc=tc, tr=tr, tf=tf, sub=sub, win=win)
    return _final_norm(x, y, final_g, tm=min(1024, t))


def _permute_w_in(w):
    def perm(block):
        d = block.shape[0]
        return block.reshape(d, N_HEADS, 2, QK_DIM).transpose(0, 2, 1, 3).reshape(d, GW)
    parts = [w[:, :6 * GW], perm(w[:, 6 * GW:7 * GW]), perm(w[:, 7 * GW:8 * GW]), w[:, 8 * GW:]]
    return jnp.concatenate(parts, axis=1)


def _block_diag(pool_w):
    n, p, _ = pool_w.shape
    out = jnp.zeros((n * p, n * p), pool_w.dtype)
    for gi in range(n):
        out = out.at[gi * p:(gi + 1) * p, gi * p:(gi + 1) * p].set(pool_w[gi])
    return out


def _token_mixing(x, layer, lam_init, p, rope_c, rope_d):
    b, s, d = x.shape
    row = lambda v: v.reshape(1, -1)
    w_in = _permute_w_in(p["w_in"][layer]).astype(BF16)
    uab, qkvc, qkvc4, qkvc16, qdt, kd, vdt = _norm_proj(x, row(p["norm1_g"][layer]), w_in, rope_c, rope_d,
                                                        tm=512, tk=256)
    dww = jnp.zeros((HALO, GW), F32).at[:CONV_WIDTH].set(p["conv_dw_w"][layer])
    yab = _local_mixers(
        uab, dww, row(p["conv_dw_b"][layer]), row(p["conv_ln_g"][layer]), row(p["conv_ln_b"][layer]),
        p["conv_pw_w"][layer].astype(BF16), _block_diag(p["pool_w"][layer]).astype(BF16),
        row(p["pool_scale"][layer]), ts=512)
    os_, lses = [], []
    for dil, qkv in zip(DILATIONS, (qkvc, qkvc4, qkvc16)):
        o, lse = _band_attn(qkv, dil)
        os_.append(o.reshape(b * s // dil, dil * GW))
        lses.append(lse.reshape(b * s // dil, dil * GW))
    yd = _diff_attn(qdt, kd, vdt, p["diff_lam"][layer], jnp.tile(p["diff_ln_g"][layer], N_HEADS).reshape(GW, 1),
                    lam_init, tq=512)
    return _out_proj(x.reshape(b * s, d), yab.reshape(b * s, 2 * GW), os_, lses, yd.reshape(b * s, GW),
                     p["w_out"][layer].astype(BF16), tm=512)


def kernel(x, norm1_g, w_in, conv_dw_w, conv_dw_b, conv_ln_g, conv_ln_b, conv_pw_w, pool_w, pool_scale,
           diff_lam, diff_ln_g, w_out, norm2_g, ffn_w_gate, ffn_w_up, ffn_w_down, moe_router, moe_w_gate,
           moe_w_up, moe_w_down, final_g):
    b, s, d = x.shape
    p = dict(norm1_g=norm1_g, w_in=w_in, conv_dw_w=conv_dw_w, conv_dw_b=conv_dw_b, conv_ln_g=conv_ln_g,
             conv_ln_b=conv_ln_b, conv_pw_w=conv_pw_w, pool_w=pool_w, pool_scale=pool_scale,
             diff_lam=diff_lam, diff_ln_g=diff_ln_g, w_out=w_out)
    rope_c = _rope_table(s, HEAD_DIM, HEAD_DIM // 4)
    rope_d = _rope_table(s, QK_DIM, QK_DIM // 4)
    row = lambda v: v.reshape(1, -1)

    x1 = _token_mixing(x, 0, 0.8 - 0.6 * math.exp(-0.3 * 0), p, rope_c, rope_d)
    x2 = _ffn(x1, row(norm2_g[0]), ffn_w_gate[0].astype(BF16), ffn_w_up[0].astype(BF16),
              ffn_w_down[0].astype(BF16), tm=512, tf=256)

    x3 = _token_mixing(x2.reshape(b, s, d), 1, 0.8 - 0.6 * math.exp(-0.3 * 1), p, rope_c, rope_d)
    out = _routed_swiglu(x3, row(norm2_g[1]), moe_router[0], moe_w_gate[0].astype(BF16),
                         moe_w_up[0].astype(BF16), moe_w_down[0].astype(BF16), row(final_g),
                         tc=2048, tr=256, tf=1792, sub=256, win=5)
    return out.reshape(b, s, d)
```

```python
import functools
import math

import jax
import jax.numpy as jnp
from jax import lax
from jax.experimental import pallas as pl
from jax.experimental.pallas import tpu as pltpu

D_MODEL = 1024
GW = 256
N_HEADS = 4
HEAD_DIM = 64
QK_DIM = 32
V_ROWS = HEAD_DIM + 16
CONV_WIDTH = 31
POOL_WINDOWS = (2, 4, 8, 16)
DILATIONS = (1, 4, 16)
BAND = 128
ROPE_THETA = 500000.0
N_EXPERTS = 8
NORM_EPS = 1e-5
HALO = 32
LANES = 128
SUBLANES = 8
FFN_SUBTILE = 256
VMEM_LIMIT = 56 * 1024 * 1024

BF16 = jnp.bfloat16
F32 = jnp.float32
NT_DIMS = (((1,), (1,)), ((), ()))


def _params(*sem):
    return pltpu.CompilerParams(dimension_semantics=sem, vmem_limit_bytes=VMEM_LIMIT)


def _lanes_in(shape, start, width):
    lane = lax.broadcasted_iota(jnp.int32, shape, len(shape) - 1)
    return (lane >= start) & (lane < start + width)


def _rms(x, g):
    ms = jnp.mean(x * x, axis=-1, keepdims=True)
    return x * lax.rsqrt(ms + NORM_EPS) * g


def _rope_table(seq, width, rot):
    half = rot // 2
    pos = jnp.arange(seq, dtype=F32)
    inv = ROPE_THETA ** (-jnp.arange(0, rot, 2, dtype=F32) / rot)
    ang = pos[:, None] * inv[None, :]
    cos, sin = jnp.cos(ang), jnp.sin(ang)
    lane = jnp.arange(GW) % width
    f = lane % half
    first = lane < half
    second = (lane >= half) & (lane < rot)
    c = jnp.where((lane < rot)[None, :], cos[:, f], 1.0)
    sn = jnp.where(first[None, :], -sin[:, f], 0.0)
    sp = jnp.where(second[None, :], sin[:, f], 0.0)
    return jnp.concatenate([c, sn, sp], axis=1)


def _rope(u, tab_ref, half):
    return (u * tab_ref[:, 0:GW]
            + pltpu.roll(u, GW - half, 1) * tab_ref[:, GW:2 * GW]
            + pltpu.roll(u, half, 1) * tab_ref[:, 2 * GW:3 * GW])


def _norm_proj_kernel(x_ref, g_ref, w_ref, rc_ref, rd_ref, uab_ref, qkvc_ref, qkv4_ref, qkv16_ref, qt_ref, kd_ref,
                      vt_ref, v_ref, cs_ref, *, tk):
    h = _rms(x_ref[0], g_ref[...]).astype(BF16)

    def proj(grp):
        return jnp.dot(h, w_ref[:, grp * GW:(grp + 1) * GW], preferred_element_type=F32)

    for grp in range(3):
        uab_ref[0, :, grp * GW:(grp + 1) * GW] = proj(grp)
    c_scale = HEAD_DIM ** -0.5
    d_scale = QK_DIM ** -0.5 * math.log2(math.e)
    parts = (_rope(proj(3), rc_ref, 8) * c_scale, _rope(proj(4), rc_ref, 8), proj(5))
    for k, val in enumerate(parts):
        qkvc_ref[0, :, k * GW:(k + 1) * GW] = val.astype(BF16)
        for half in range(GW // LANES):
            cs_ref[2 * k + half] = val[:, half * LANES:(half + 1) * LANES]
    tm = cs_ref.shape[1]
    for dil, out_ref in zip(DILATIONS[1:], (qkv4_ref, qkv16_ref)):
        for r in range(dil):
            for slab in range(cs_ref.shape[0]):
                rows = cs_ref[slab, pl.ds(r, tm // dil, stride=dil), :]
                col = r * 3 * GW + slab * LANES
                out_ref[0, :, col:col + LANES] = rows.astype(BF16)
    qt_ref[0] = jnp.transpose(_rope(proj(6), rd_ref, 4) * d_scale).astype(BF16)
    kd_ref[0] = _rope(proj(7), rd_ref, 4).astype(BF16)
    v_ref[...] = proj(8)
    vt = jnp.transpose(v_ref[...]).astype(BF16)
    ones = jnp.ones((V_ROWS - HEAD_DIM, tk), BF16)
    for blk in range(vt.shape[1] // tk):
        for hd in range(N_HEADS):
            vt_ref[0, blk, hd * V_ROWS:hd * V_ROWS + HEAD_DIM, :] = (
                vt[hd * HEAD_DIM:(hd + 1) * HEAD_DIM, blk * tk:(blk + 1) * tk])
            vt_ref[0, blk, hd * V_ROWS + HEAD_DIM:(hd + 1) * V_ROWS, :] = ones


def _norm_proj(x, g, w, rope_c, rope_d, tm, tk):
    b, s, d = x.shape
    grid = (s // tm, b)
    return pl.pallas_call(
        functools.partial(_norm_proj_kernel, tk=tk),
        grid=grid,
        in_specs=[
            pl.BlockSpec((1, tm, d), lambda i, j: (j, i, 0)),
            pl.BlockSpec((1, d), lambda i, j: (0, 0)),
            pl.BlockSpec((d, 9 * GW), lambda i, j: (0, 0)),
            pl.BlockSpec((tm, 3 * GW), lambda i, j: (i, 0)),
            pl.BlockSpec((tm, 3 * GW), lambda i, j: (i, 0)),
        ],
        out_specs=[
            pl.BlockSpec((1, tm, 3 * GW), lambda i, j: (j, i, 0)),
            pl.BlockSpec((1, tm, 3 * GW), lambda i, j: (j, i, 0)),
            *[pl.BlockSpec((1, tm // dil, dil * 3 * GW), lambda i, j: (j, i, 0)) for dil in DILATIONS[1:]],
            pl.BlockSpec((1, GW, tm), lambda i, j: (j, 0, i)),
            pl.BlockSpec((1, tm, GW), lambda i, j: (j, i, 0)),
            pl.BlockSpec((1, tm // tk, N_HEADS * V_ROWS, tk), lambda i, j: (j, i, 0, 0)),
        ],
        out_shape=[
            jax.ShapeDtypeStruct((b, s, 3 * GW), F32),
            jax.ShapeDtypeStruct((b, s, 3 * GW), BF16),
            *[jax.ShapeDtypeStruct((b, s // dil, dil * 3 * GW), BF16) for dil in DILATIONS[1:]],
            jax.ShapeDtypeStruct((b, GW, s), BF16),
            jax.ShapeDtypeStruct((b, s, GW), BF16),
            jax.ShapeDtypeStruct((b, s // tk, N_HEADS * V_ROWS, tk), BF16),
        ],
        scratch_shapes=[
            pltpu.VMEM((tm, GW), F32),
            pltpu.VMEM((3 * GW // LANES, tm, LANES), F32),
        ],
        compiler_params=_params("arbitrary", "arbitrary"),
        name="norm_proj",
    )(x, g, w, rope_c, rope_d)


def _local_kernel(u_ref, dww_ref, dwb_ref, lng_ref, lnb_ref, pww_ref, plw_ref, pls_ref,
                  y_ref, zbuf, pbuf, shift_ref, *, ts):
    si = pl.program_id(1)

    @pl.when(si == 0)
    def _():
        zbuf[0:HALO, :] = jnp.zeros((HALO, GW), F32)
        pbuf[0:HALO, :] = jnp.zeros((HALO, GW), F32)

    a = u_ref[0, :, 0:GW]
    b = u_ref[0, :, GW:2 * GW]
    zbuf[HALO:HALO + ts, :] = a * jax.nn.sigmoid(b)
    pbuf[HALO:HALO + ts, :] = u_ref[0, :, 2 * GW:3 * GW]

    def by_phase(buf, shifted, offsets, term):
        total = None
        for phase in range(SUBLANES):
            offs = [o for o in offsets if o % SUBLANES == phase]
            if not offs:
                continue
            src = buf
            if phase:
                n = max(offs) - phase + ts
                shifted[phase, 0:n, :] = buf[phase:phase + n, :]
                src = shifted.at[phase]
            for o in offs:
                t = term(o, src[o - phase:o - phase + ts, :])
                total = t if total is None else total + t
        return total

    base = HALO - (CONV_WIDTH - 1)
    acc = by_phase(zbuf, shift_ref.at[0], range(base, base + CONV_WIDTH),
                   lambda o, rows: dww_ref[o - base:o - base + 1, :] * rows)
    z = acc + dwb_ref[...]
    mu = jnp.mean(z, axis=-1, keepdims=True)
    zc = z - mu
    var = jnp.mean(zc * zc, axis=-1, keepdims=True)
    zn = zc * lax.rsqrt(var + NORM_EPS) * lng_ref[...] + lnb_ref[...]
    zs = zn * jax.nn.sigmoid(zn)
    y_ref[0, :, 0:GW] = jnp.dot(zs.astype(BF16), pww_ref[...], preferred_element_type=F32).astype(BF16)

    lane = lax.broadcasted_iota(jnp.int32, (ts, GW), 1)
    pool_group = GW // len(POOL_WINDOWS)
    win = jnp.zeros((ts, GW), jnp.int32)
    for gi, w in enumerate(POOL_WINDOWS):
        win = jnp.where(lane >= gi * pool_group, w, win)
    p0 = pbuf[HALO:HALO + ts, :]
    n = HALO + ts
    sums = {1: pbuf}
    level, start = 1, 0
    while level < max(POOL_WINDOWS):
        prev = sums[level]
        start = -(-(start + level) // SUBLANES) * SUBLANES
        dst = shift_ref.at[1, len(sums) - 1]
        dst[start:n, :] = prev[start:n, :] + prev[start - level:n - level, :]
        level *= 2
        sums[level] = dst
    assert start <= HALO
    tot = sums[POOL_WINDOWS[0]][HALO:n, :]
    for gi, w in enumerate(POOL_WINDOWS[1:], start=1):
        tot = jnp.where(lane >= gi * pool_group, sums[w][HALO:n, :], tot)
    pos = si * ts + lax.broadcasted_iota(jnp.int32, (ts, GW), 0)
    cnt = jnp.minimum(pos + 1, win).astype(F32)
    pooled = tot / cnt - p0
    yb = jnp.dot(pooled.astype(BF16), plw_ref[...], preferred_element_type=F32) * pls_ref[...]
    y_ref[0, :, GW:2 * GW] = yb.astype(BF16)

    zbuf[0:HALO, :] = zbuf[ts:ts + HALO, :]
    pbuf[0:HALO, :] = pbuf[ts:ts + HALO, :]


def _local_mixers(uab, dww, dwb, lng, lnb, pww, plw, pls, ts):
    b, s, _ = uab.shape
    vec = lambda: pl.BlockSpec((1, GW), lambda i, j: (0, 0))
    mat = lambda: pl.BlockSpec((GW, GW), lambda i, j: (0, 0))
    return pl.pallas_call(
        functools.partial(_local_kernel, ts=ts),
        grid=(b, s // ts),
        in_specs=[
            pl.BlockSpec((1, ts, 3 * GW), lambda i, j: (i, j, 0)),
            pl.BlockSpec((HALO, GW), lambda i, j: (0, 0)),
            vec(), vec(), vec(), mat(), mat(), vec(),
        ],
        out_specs=pl.BlockSpec((1, ts, 2 * GW), lambda i, j: (i, j, 0)),
        out_shape=jax.ShapeDtypeStruct((b, s, 2 * GW), BF16),
        scratch_shapes=[
            pltpu.VMEM((HALO + ts, GW), F32),
            pltpu.VMEM((HALO + ts, GW), F32),
            pltpu.VMEM((2, SUBLANES, HALO + ts, GW), F32),
        ],
        compiler_params=_params("arbitrary", "arbitrary"),
        name="local_mixers",
    )(uab, dww, dwb, lng, lnb, pww, plw, pls)


def _band_attn_kernel(q_ref, k_ref, v_ref, o_ref, lse_ref, s_ref):
    row = lax.broadcasted_iota(jnp.int32, (BAND, 2 * BAND), 0)
    col = lax.broadcasted_iota(jnp.int32, (BAND, 2 * BAND), 1)
    halves = [_lanes_in((BAND, LANES), sub * HEAD_DIM, HEAD_DIM) for sub in range(LANES // HEAD_DIM)]

    def starts(n):
        return pl.multiple_of(n * BAND, BAND), pl.multiple_of(jnp.maximum(n - 1, 0) * BAND, BAND)

    def scores(n, slot):
        q0, k0 = starts(n)
        q = q_ref[0, pl.ds(q0, BAND), :].astype(F32)
        kw = k_ref[0, pl.ds(k0, 2 * BAND), :]
        dist = (row - col) + (q0 - k0)
        mask = (dist >= 0) & (dist <= BAND)
        for h in range(N_HEADS):
            qh = jnp.where(_lanes_in((BAND, GW), h * HEAD_DIM, HEAD_DIM), q, 0.0).astype(BF16)
            s = lax.dot_general(qh, kw, NT_DIMS, preferred_element_type=F32)
            s_ref[slot * N_HEADS + h] = jnp.where(mask, s, -jnp.inf)

    def values(n, slot):
        q0, k0 = starts(n)
        for tile in range(GW // LANES):
            vw = v_ref[0, pl.ds(k0, 2 * BAND), tile * LANES:(tile + 1) * LANES]
            o = jnp.zeros((BAND, LANES), F32)
            lse = jnp.zeros((BAND, LANES), F32)
            for sub in range(LANES // HEAD_DIM):
                s = s_ref[slot * N_HEADS + tile * (LANES // HEAD_DIM) + sub]
                m = jnp.max(s, axis=-1, keepdims=True)
                p = jnp.exp(s - m)
                l = jnp.sum(p, axis=-1, keepdims=True)
                acc = jnp.dot(p.astype(BF16), vw, preferred_element_type=F32)
                mine = halves[sub]
                o = jnp.where(mine, acc / l, o)
                lse = jnp.where(mine, m + jnp.log(l), lse)
            o_ref[0, pl.ds(q0, BAND), tile * LANES:(tile + 1) * LANES] = o.astype(o_ref.dtype)
            lse_ref[0, pl.ds(q0, BAND), tile * LANES:(tile + 1) * LANES] = lse

    def pair(t, carry):
        scores(2 * t, 0)
        scores(2 * t + 1, 1)
        values(2 * t, 0)
        values(2 * t + 1, 1)
        return carry

    lax.fori_loop(0, q_ref.shape[1] // (2 * BAND), pair, 0)


def _band_attn(qkv, dil):
    b, l, _ = qkv.shape
    assert l % (2 * BAND) == 0
    part = lambda c: pl.BlockSpec((1, l, GW), lambda i, r: (i, 0, 3 * r + c))
    out = pl.BlockSpec((1, l, GW), lambda i, r: (i, 0, r))
    return pl.pallas_call(
        _band_attn_kernel,
        grid=(b, dil),
        in_specs=[part(0), part(1), part(2)],
        out_specs=[out, out],
        out_shape=[jax.ShapeDtypeStruct((b, l, dil * GW), BF16), jax.ShapeDtypeStruct((b, l, dil * GW), F32)],
        scratch_shapes=[pltpu.VMEM((2 * N_HEADS, BAND, 2 * BAND), F32)],
        compiler_params=_params("arbitrary", "arbitrary"),
        name="band_attn",
    )(qkv, qkv, qkv)


def _diff_attn_kernel(lam_ref, g_ref, qt_ref, k_ref, vt_ref, o_ref, qm_ref, m_ref, acc_ref, yt_ref,
                      st_ref, mx_ref, *, tq, lam_init):
    n_chain = 2 * N_HEADS
    i = pl.program_id(1)
    @pl.when((pl.program_id(0) == 0) & (i == 0))
    def _():
        qm_ref[...] = jnp.zeros(qm_ref.shape, BF16)

    for c in range(n_chain):
        qm_ref[c, c * QK_DIM:(c + 1) * QK_DIM, :] = qt_ref[0, c * QK_DIM:(c + 1) * QK_DIM, :]
    m_ref[...] = jnp.full(m_ref.shape, -jnp.inf, F32)
    acc_ref[...] = jnp.zeros(acc_ref.shape, F32)
    tk = st_ref.shape[1]
    ratio = tq // tk
    ahead = (lax.broadcasted_iota(jnp.int32, (tk, tq), 0)
             - lax.broadcasted_iota(jnp.int32, (tk, tq), 1))

    def score(c, kb, diag):
        st = jnp.dot(kb, qm_ref[c], preferred_element_type=F32)
        if diag is not None:
            st = jnp.where(ahead <= -diag * tk, st, -jnp.inf)
        st_ref[c] = st
        mx_ref[c] = jnp.max(st, axis=0, keepdims=True)

    def value(c, j):
        h = c % N_HEADS
        m_prev = m_ref[c]
        m_next = jnp.maximum(m_prev, mx_ref[c])
        alpha = jnp.exp2(m_prev - m_next)
        p = jnp.exp2(st_ref[c] - m_next).astype(BF16)
        vt = vt_ref[0, j, h * V_ROWS:(h + 1) * V_ROWS, :]
        acc_ref[c] = alpha * acc_ref[c] + jnp.dot(vt, p, preferred_element_type=F32)
        m_ref[c] = m_next

    def k_block(j):
        return k_ref[0, pl.ds(pl.multiple_of(j * tk, tk), tk), :]

    def scores(j, diag):
        kb = k_block(j)
        for c in range(n_chain):
            score(c, kb, diag)

    def values(j):
        for c in range(n_chain):
            value(c, j)

    def values_then_scores(j, diag):
        kb = k_block(j + 1)
        for c in range(n_chain):
            value(c, j)
            score(c, kb, diag)

    before = ratio * i

    @pl.when(i == 0)
    def _():
        scores(0, 0)

    @pl.when(i > 0)
    def _():
        scores(0, None)

    def body(t, carry):
        values_then_scores(2 * t, None)
        values_then_scores(2 * t + 1, None)
        return carry

    plain = jnp.maximum(before - 1, 0)
    lax.fori_loop(0, plain // 2, body, 0)

    @pl.when(plain % 2 == 1)
    def _():
        values_then_scores(plain - 1, None)

    @pl.when(i > 0)
    def _():
        values_then_scores(before - 1, 0)

    for diag in range(1, ratio):
        values_then_scores(before + diag - 1, diag)
    values(before + ratio - 1)

    lv = lam_ref[...]
    lam = (jnp.exp(jnp.sum(lv[0:1] * lv[1:2], axis=-1, keepdims=True))
           - jnp.exp(jnp.sum(lv[2:3] * lv[3:4], axis=-1, keepdims=True)) + lam_init)
    for h in range(N_HEADS):
        a1, a2 = acc_ref[h], acc_ref[N_HEADS + h]
        oh = (a1[:HEAD_DIM] / a1[HEAD_DIM:HEAD_DIM + 1]
              - lam * (a2[:HEAD_DIM] / a2[HEAD_DIM:HEAD_DIM + 1]))
        ms = jnp.mean(oh * oh, axis=0, keepdims=True)
        g = g_ref[h * HEAD_DIM:(h + 1) * HEAD_DIM, :]
        yt_ref[h * HEAD_DIM:(h + 1) * HEAD_DIM, :] = oh * lax.rsqrt(ms + NORM_EPS) * (g * (1.0 - lam_init))
    o_ref[0] = jnp.transpose(yt_ref[...]).astype(BF16)


def _diff_attn(qt, k, vt, lam_vecs, g_col, lam_init, tq):
    b, s, _ = k.shape
    nk, tk = vt.shape[1], vt.shape[3]
    assert tq % tk == 0 and s % tq == 0
    n_chain = 2 * N_HEADS
    return pl.pallas_call(
        functools.partial(_diff_attn_kernel, tq=tq, lam_init=lam_init),
        grid=(b, s // tq),
        in_specs=[
            pl.BlockSpec((4, QK_DIM), lambda i, j: (0, 0)),
            pl.BlockSpec((GW, 1), lambda i, j: (0, 0)),
            pl.BlockSpec((1, GW, tq), lambda i, j: (i, 0, j)),
            pl.BlockSpec((1, s, GW), lambda i, j: (i, 0, 0)),
            pl.BlockSpec((1, nk, N_HEADS * V_ROWS, tk), lambda i, j: (i, 0, 0, 0)),
        ],
        out_specs=pl.BlockSpec((1, tq, GW), lambda i, j: (i, j, 0)),
        out_shape=jax.ShapeDtypeStruct((b, s, GW), BF16),
        scratch_shapes=[
            pltpu.VMEM((n_chain, GW, tq), BF16),
            pltpu.VMEM((n_chain, 1, tq), F32),
            pltpu.VMEM((n_chain, V_ROWS, tq), F32),
            pltpu.VMEM((GW, tq), F32),
            pltpu.VMEM((n_chain, tk, tq), F32),
            pltpu.VMEM((n_chain, 1, tq), F32),
        ],
        compiler_params=_params("arbitrary", "arbitrary"),
        name="diff_attn",
    )(lam_vecs, g_col, qt, k, vt)


def _out_proj_kernel(x_ref, yab_ref, o1_ref, o2_ref, o3_ref, l1_ref, l2_ref, l3_ref, yd_ref, w_ref, out_ref,
                     nat_ref):
    tm = x_ref.shape[0]

    def natural(ref, dil, slot):
        for r in range(dil):
            for half in range(GW // LANES):
                col = r * GW + half * LANES
                nat_ref[2 * slot + half, pl.ds(r, tm // dil, stride=dil), :] = ref[:, col:col + LANES].astype(F32)
        return jnp.concatenate([nat_ref[2 * slot], nat_ref[2 * slot + 1]], axis=1)

    d2, d3 = DILATIONS[1:]
    l1, l2, l3 = l1_ref[...], natural(l2_ref, d2, 0), natural(l3_ref, d3, 1)
    o2, o3 = natural(o2_ref, d2, 2), natural(o3_ref, d3, 3)
    m = jnp.maximum(jnp.maximum(l1, l2), l3)
    w1, w2, w3 = jnp.exp(l1 - m), jnp.exp(l2 - m), jnp.exp(l3 - m)
    yc = (w1 * o1_ref[...] + w2 * o2 + w3 * o3) / (w1 + w2 + w3)
    acc = jnp.dot(yab_ref[...], w_ref[0:2 * GW, :], preferred_element_type=F32)
    acc = acc + jnp.dot(yc.astype(BF16), w_ref[2 * GW:3 * GW, :], preferred_element_type=F32)
    acc = acc + jnp.dot(yd_ref[...], w_ref[3 * GW:4 * GW, :], preferred_element_type=F32)
    out_ref[...] = x_ref[...] + acc


def _out_proj(x, yab, os_, lses, yd, w, tm):
    t, d = x.shape
    rows = lambda c: pl.BlockSpec((tm, c), lambda i: (i, 0))
    branch = [pl.BlockSpec((tm // dil, dil * GW), lambda i: (i, 0)) for dil in DILATIONS]
    return pl.pallas_call(
        _out_proj_kernel,
        grid=(t // tm,),
        in_specs=[rows(d), rows(2 * GW)] + branch * 2 + [rows(GW), pl.BlockSpec((d, d), lambda i: (0, 0))],
        out_specs=rows(d),
        out_shape=jax.ShapeDtypeStruct((t, d), F32),
        scratch_shapes=[pltpu.VMEM((4 * GW // LANES, tm, LANES), F32)],
        compiler_params=_params("arbitrary"),
        name="out_proj",
    )(x, yab, *os_, *lses, yd, w)


def _ffn_kernel(x_ref, g_ref, wg_ref, wu_ref, wd_ref, out_ref, acc_ref, *, tf):
    hn = _rms(x_ref[...], g_ref[...]).astype(BF16)
    acc_ref[...] = x_ref[...]
    for f0 in range(0, wg_ref.shape[1], tf):
        gate = jnp.dot(hn, wg_ref[:, f0:f0 + tf], preferred_element_type=F32)
        up = jnp.dot(hn, wu_ref[:, f0:f0 + tf], preferred_element_type=F32)
        act = (gate * jax.nn.sigmoid(gate) * up).astype(BF16)
        acc_ref[...] += jnp.dot(act, wd_ref[f0:f0 + tf, :], preferred_element_type=F32)
    out_ref[...] = acc_ref[...]


def _ffn(x, g, wg, wu, wd, tm, tf):
    t, d = x.shape
    ff = wg.shape[1]
    assert ff % tf == 0
    whole = lambda shape: pl.BlockSpec(shape, lambda i: (0, 0))
    return pl.pallas_call(
        functools.partial(_ffn_kernel, tf=tf),
        grid=(t // tm,),
        in_specs=[pl.BlockSpec((tm, d), lambda i: (i, 0)), whole((1, d)), whole((d, ff)), whole((d, ff)),
                  whole((ff, d))],
        out_specs=pl.BlockSpec((tm, d), lambda i: (i, 0)),
        out_shape=jax.ShapeDtypeStruct((t, d), F32),
        scratch_shapes=[pltpu.VMEM((tm, d), F32)],
        compiler_params=_params("arbitrary"),
        name="ffn_dense",
    )(x, g, wg, wu, wd)


META_E1, META_E2, META_RANK1, META_RANK2, META_G1, META_G2 = range(6)


def _router_kernel(x_ref, g_ref, rhi_ref, rlo_ref, hn_ref, meta_ref, after_ref, carry_ref, *, blocks_per_chunk):
    @pl.when(pl.program_id(0) % blocks_per_chunk == 0)
    def _():
        carry_ref[...] = jnp.zeros_like(carry_ref)

    h = _rms(x_ref[...], g_ref[...])
    hn_ref[...] = h.astype(BF16)
    h_hi = h.astype(BF16)
    h_lo = (h - h_hi.astype(F32)).astype(BF16)
    logits = (jnp.dot(h_hi, rhi_ref[...], preferred_element_type=F32)
              + jnp.dot(h_hi, rlo_ref[...], preferred_element_type=F32)
              + jnp.dot(h_lo, rhi_ref[...], preferred_element_type=F32))
    tm = logits.shape[0]
    lane = lax.broadcasted_iota(jnp.int32, logits.shape, 1)
    logits = jnp.where(lane < N_EXPERTS, logits, -jnp.inf)
    m1 = jnp.max(logits, axis=-1, keepdims=True)
    i1 = jnp.min(jnp.where(logits == m1, lane, LANES), axis=-1, keepdims=True)
    rest = jnp.where(lane == i1, -jnp.inf, logits)
    m2 = jnp.max(rest, axis=-1, keepdims=True)
    i2 = jnp.min(jnp.where(rest == m2, lane, LANES), axis=-1, keepdims=True)
    e2 = jnp.exp(m2 - m1)
    g1 = 1.0 / (1.0 + e2)
    g2 = e2 / (1.0 + e2)

    oh1 = lane == i1
    oh2 = lane == i2
    both = jnp.where(oh1, 1.0, jnp.where(oh2, 1.0, 0.0))
    row = lax.broadcasted_iota(jnp.int32, (tm, tm), 0)
    col = lax.broadcasted_iota(jnp.int32, (tm, tm), 1)
    earlier = jnp.where(col < row, 1.0, 0.0).astype(BF16)
    carry = carry_ref[0:1, :]
    prefix = jnp.dot(earlier, both.astype(BF16), preferred_element_type=F32) + carry
    rank1 = jnp.sum(jnp.where(oh1, prefix, 0.0), axis=-1, keepdims=True)
    rank2 = jnp.sum(jnp.where(oh2, prefix, 0.0), axis=-1, keepdims=True)
    total = carry + jnp.sum(both, axis=0, keepdims=True)

    meta = jnp.zeros(logits.shape, F32)
    for k, val in ((META_E1, i1.astype(F32)), (META_E2, i2.astype(F32)), (META_RANK1, rank1),
                   (META_RANK2, rank2), (META_G1, g1), (META_G2, g2)):
        meta = jnp.where(lane == k, val, meta)
    meta_ref[...] = meta
    after_ref[0] = jnp.broadcast_to(total, after_ref.shape[1:])
    carry_ref[...] = jnp.broadcast_to(total, carry_ref.shape)


def _router(x, g, r_pad, tm, blocks_per_chunk):
    t, d = x.shape
    nblk = t // tm
    r_hi = r_pad.astype(BF16)
    counts = pl.BlockSpec((1, 8, LANES), lambda i: (i, 0, 0))
    return pl.pallas_call(
        functools.partial(_router_kernel, blocks_per_chunk=blocks_per_chunk),
        grid=(nblk,),
        in_specs=[
            pl.BlockSpec((tm, d), lambda i: (i, 0)),
            pl.BlockSpec((1, d), lambda i: (0, 0)),
            pl.BlockSpec((d, LANES), lambda i: (0, 0)),
            pl.BlockSpec((d, LANES), lambda i: (0, 0)),
        ],
        out_specs=[
            pl.BlockSpec((tm, d), lambda i: (i, 0)),
            pl.BlockSpec((tm, LANES), lambda i: (i, 0)),
            counts,
        ],
        out_shape=[
            jax.ShapeDtypeStruct((t, d), BF16),
            jax.ShapeDtypeStruct((t, LANES), F32),
            jax.ShapeDtypeStruct((nblk, 8, LANES), F32),
        ],
        scratch_shapes=[pltpu.VMEM((8, LANES), F32)],
        compiler_params=_params("arbitrary"),
        name="router",
    )(x, g, r_hi, (r_pad - r_hi.astype(F32)).astype(BF16))


SLOT_ROW1, SLOT_ROW2, SLOT_G1, SLOT_G2 = range(4)


def _moe_kernel(nt_ref, toff_ref, cuts_ref, hn_ref, slotc_ref, slotr_ref, wg_ref, wu_ref, wd_ref, out_ref,
                xs_ref, ys_ref, *, tr, sub, win):
    c, e, f = pl.program_id(0), pl.program_id(1), pl.program_id(2)
    last_f = pl.num_programs(2) - 1
    grp = c * N_EXPERTS + e
    n_tiles = nt_ref[grp]
    row0 = toff_ref[grp] * tr
    tc = hn_ref.shape[0]
    nsub = tc // sub

    @pl.when((e == 0) & (f == 0))
    def _():
        out_ref[...] = jnp.zeros_like(out_ref)

    def rows_of(r):
        return pl.ds(pl.multiple_of(r * tr, tr), tr)

    def token_window(r):
        base = grp * nsub
        first = r * tr
        last = jnp.minimum(first + tr, cuts_ref[base + nsub - 1]) - 1
        lo = hi = 0
        for k in range(nsub - 1):
            cut = cuts_ref[base + k]
            lo = lo + (cut <= first).astype(jnp.int32)
            hi = hi + (cut <= last).astype(jnp.int32)
        start = jnp.minimum(lo, nsub - win)
        return hi < start + win, pl.multiple_of(start * sub, sub)

    def spans(r, fn):
        covered, tok0 = token_window(r)

        @pl.when(covered)
        def _():
            fn(r, tok0, win * sub)

        @pl.when(jnp.logical_not(covered))
        def _():
            fn(r, 0, tc)

    def gather_span(r, tok0, ntok):
        toks = pl.ds(tok0, ntok)
        target = (lax.broadcasted_iota(jnp.int32, (tr, ntok), 0) + (row0 + r * tr)).astype(F32)
        hit1 = slotr_ref[SLOT_ROW1:SLOT_ROW1 + 1, toks] == target
        hit2 = slotr_ref[SLOT_ROW2:SLOT_ROW2 + 1, toks] == target
        oh = jnp.where(hit1, 1.0, jnp.where(hit2, 1.0, 0.0)).astype(BF16)
        xs_ref[rows_of(r), :] = jnp.dot(oh, hn_ref[toks, :], preferred_element_type=F32).astype(BF16)

    def gather_tile(r, carry):
        spans(r, gather_span)
        return carry

    def ffn_tiles(r, count, first):
        rows = pl.ds(pl.multiple_of(r * tr, tr), count * tr)
        xt = xs_ref[rows, :]
        for f0 in range(0, wg_ref.shape[1], FFN_SUBTILE):
            gate = jnp.dot(xt, wg_ref[:, f0:f0 + FFN_SUBTILE], preferred_element_type=F32)
            up = jnp.dot(xt, wu_ref[:, f0:f0 + FFN_SUBTILE], preferred_element_type=F32)
            act = (gate * jax.nn.sigmoid(gate) * up).astype(BF16)
            y = jnp.dot(act, wd_ref[f0:f0 + FFN_SUBTILE, :], preferred_element_type=F32)
            if first and f0 == 0:
                ys_ref[rows, :] = y
            else:
                ys_ref[rows, :] += y

    def scatter_span(r, tok0, ntok):
        toks = pl.ds(tok0, ntok)
        yb = ys_ref[rows_of(r), :].astype(BF16)
        target = (lax.broadcasted_iota(jnp.int32, (ntok, tr), 1) + (row0 + r * tr)).astype(F32)
        hit1 = slotc_ref[toks, SLOT_ROW1:SLOT_ROW1 + 1] == target
        hit2 = slotc_ref[toks, SLOT_ROW2:SLOT_ROW2 + 1] == target
        gh = jnp.where(hit1, slotc_ref[toks, SLOT_G1:SLOT_G1 + 1],
                       jnp.where(hit2, slotc_ref[toks, SLOT_G2:SLOT_G2 + 1], 0.0)).astype(BF16)
        out_ref[toks, :] += jnp.dot(gh, yb, preferred_element_type=F32)

    def scatter_tile(r, carry):
        spans(r, scatter_span)
        return carry

    def ffn_all(first):
        def pair(t, carry):
            ffn_tiles(2 * t, 2, first)
            return carry

        lax.fori_loop(0, n_tiles // 2, pair, 0)

        @pl.when(n_tiles % 2 == 1)
        def _():
            ffn_tiles(n_tiles - 1, 1, first)

    @pl.when(f == 0)
    def _():
        lax.fori_loop(0, n_tiles, gather_tile, 0)
        ffn_all(True)

    @pl.when(f > 0)
    def _():
        ffn_all(False)

    @pl.when(f == last_f)
    def _():
        lax.fori_loop(0, n_tiles, scatter_tile, 0)


def _moe(hn, slot_cols, slot_rows, nt, toff, cuts, wg, wu, wd, tc, tr, tf, sub, win):
    t, d = hn.shape
    ne, _, ff = wg.shape
    grid_spec = pltpu.PrefetchScalarGridSpec(
        num_scalar_prefetch=3,
        grid=(t // tc, ne, ff // tf),
        in_specs=[
            pl.BlockSpec((tc, d), lambda c, e, f, *_: (c, 0), pipeline_mode=pl.Buffered(1)),
            pl.BlockSpec((tc, 4), lambda c, e, f, *_: (c, 0), pipeline_mode=pl.Buffered(1)),
            pl.BlockSpec((8, tc), lambda c, e, f, *_: (0, c), pipeline_mode=pl.Buffered(1)),
            pl.BlockSpec((None, d, tf), lambda c, e, f, *_: (e, 0, f)),
            pl.BlockSpec((None, d, tf), lambda c, e, f, *_: (e, 0, f)),
            pl.BlockSpec((None, tf, d), lambda c, e, f, *_: (e, f, 0)),
        ],
        out_specs=pl.BlockSpec((tc, d), lambda c, e, f, *_: (c, 0), pipeline_mode=pl.Buffered(1)),
        scratch_shapes=[
            pltpu.VMEM((tc + tr, d), BF16),
            pltpu.VMEM((tc + tr, d), F32),
        ],
    )
    return pl.pallas_call(
        functools.partial(_moe_kernel, tr=tr, sub=sub, win=win),
        grid_spec=grid_spec,
        out_shape=jax.ShapeDtypeStruct((t, d), F32),
        compiler_params=_params("arbitrary", "arbitrary", "arbitrary"),
        name="moe",
    )(nt, toff, cuts, hn, slot_cols, slot_rows, wg, wu, wd)


def _final_kernel(x_ref, y_ref, g_ref, out_ref):
    out_ref[...] = _rms(x_ref[...] + y_ref[...], g_ref[...])


def _final_norm(x, y, g, tm):
    t, d = x.shape
    rows = pl.BlockSpec((tm, d), lambda i: (i, 0))
    return pl.pallas_call(
        _final_kernel,
        grid=(t // tm,),
        in_specs=[rows, rows, pl.BlockSpec((1, d), lambda i: (0, 0))],
        out_specs=rows,
        out_shape=jax.ShapeDtypeStruct((t, d), F32),
        compiler_params=_params("arbitrary"),
        name="final_norm",
    )(x, y, g)


def _routed_swiglu(x, norm_g, router_w, wg, wu, wd, final_g, tc, tr, tf, sub, win):
    t, d = x.shape
    nc, nsub = t // tc, tc // sub
    r_pad = jnp.zeros((d, LANES), F32).at[:, :N_EXPERTS].set(router_w)
    hn, meta, after = _router(x, norm_g, r_pad, tm=sub, blocks_per_chunk=nsub)
    through = after[:, 0, :N_EXPERTS].astype(jnp.int32).reshape(nc, nsub, N_EXPERTS)
    cuts = through.transpose(0, 2, 1)
    total = through[:, -1]
    nt = (total + tr - 1) // tr
    toff = jnp.cumsum(nt, axis=1) - nt
    first_row = jnp.repeat((toff * tr).astype(F32), tc, axis=0)
    experts = jnp.arange(N_EXPERTS, dtype=F32)
    pick = lambda e: jnp.sum(jnp.where(e[:, None] == experts, first_row, 0.0), axis=1)
    row1 = pick(meta[:, META_E1]) + meta[:, META_RANK1]
    row2 = pick(meta[:, META_E2]) + meta[:, META_RANK2]
    slot_cols = jnp.stack([row1, row2, meta[:, META_G1], meta[:, META_G2]], axis=1)
    slot_rows = jnp.zeros((8, t), F32).at[0:4].set(slot_cols.T)
    y = _moe(hn, slot_cols, slot_rows, nt.reshape(-1), toff.reshape(-1), cuts.reshape(-1), wg, wu, wd,
             tc=tc, tr=tr, tf=tf, sub=sub, win=win)
    return _final_norm(x, y, final_g, tm=min(1024, t))


def _permute_w_in(w):
    def perm(block):
        d = block.shape[0]
        return block.reshape(d, N_HEADS, 2, QK_DIM).transpose(0, 2, 1, 3).reshape(d, GW)
    parts = [w[:, :6 * GW], perm(w[:, 6 * GW:7 * GW]), perm(w[:, 7 * GW:8 * GW]), w[:, 8 * GW:]]
    return jnp.concatenate(parts, axis=1)


def _block_diag(pool_w):
    n, p, _ = pool_w.shape
    out = jnp.zeros((n * p, n * p), pool_w.dtype)
    for gi in range(n):
        out = out.at[gi * p:(gi + 1) * p, gi * p:(gi + 1) * p].set(pool_w[gi])
    return out


def _token_mixing(x, layer, lam_init, p, rope_c, rope_d):
    b, s, d = x.shape
    row = lambda v: v.reshape(1, -1)
    w_in = _permute_w_in(p["w_in"][layer]).astype(BF16)
    uab, qkvc, qkvc4, qkvc16, qdt, kd, vdt = _norm_proj(x, row(p["norm1_g"][layer]), w_in, rope_c, rope_d,
                                                        tm=512, tk=256)
    dww = jnp.zeros((HALO, GW), F32).at[:CONV_WIDTH].set(p["conv_dw_w"][layer])
    yab = _local_mixers(
        uab, dww, row(p["conv_dw_b"][layer]), row(p["conv_ln_g"][layer]), row(p["conv_ln_b"][layer]),
        p["conv_pw_w"][layer].astype(BF16), _block_diag(p["pool_w"][layer]).astype(BF16),
        row(p["pool_scale"][layer]), ts=512)
    os_, lses = [], []
    for dil, qkv in zip(DILATIONS, (qkvc, qkvc4, qkvc16)):
        o, lse = _band_attn(qkv, dil)
        os_.append(o.reshape(b * s // dil, dil * GW))
        lses.append(lse.reshape(b * s // dil, dil * GW))
    yd = _diff_attn(qdt, kd, vdt, p["diff_lam"][layer], jnp.tile(p["diff_ln_g"][layer], N_HEADS).reshape(GW, 1),
                    lam_init, tq=512)
    return _out_proj(x.reshape(b * s, d), yab.reshape(b * s, 2 * GW), os_, lses, yd.reshape(b * s, GW),
                     p["w_out"][layer].astype(BF16), tm=512)


def kernel(x, norm1_g, w_in, conv_dw_w, conv_dw_b, conv_ln_g, conv_ln_b, conv_pw_w, pool_w, pool_scale,
           diff_lam, diff_ln_g, w_out, norm2_g, ffn_w_gate, ffn_w_up, ffn_w_down, moe_router, moe_w_gate,
           moe_w_up, moe_w_down, final_g):
    b, s, d = x.shape
    p = dict(norm1_g=norm1_g, w_in=w_in, conv_dw_w=conv_dw_w, conv_dw_b=conv_dw_b, conv_ln_g=conv_ln_g,
             conv_ln_b=conv_ln_b, conv_pw_w=conv_pw_w, pool_w=pool_w, pool_scale=pool_scale,
             diff_lam=diff_lam, diff_ln_g=diff_ln_g, w_out=w_out)
    rope_c = _rope_table(s, HEAD_DIM, HEAD_DIM // 4)
    rope_d = _rope_table(s, QK_DIM, QK_DIM // 4)
    row = lambda v: v.reshape(1, -1)

    x1 = _token_mixing(x, 0, 0.8 - 0.6 * math.exp(-0.3 * 0), p, rope_c, rope_d)
    x2 = _ffn(x1, row(norm2_g[0]), ffn_w_gate[0].astype(BF16), ffn_w_up[0].astype(BF16),
              ffn_w_down[0].astype(BF16), tm=512, tf=256)

    x3 = _token_mixing(x2.reshape(b, s, d), 1, 0.8 - 0.6 * math.exp(-0.3 * 1), p, rope_c, rope_d)
    out = _routed_swiglu(x3, row(norm2_g[1]), moe_router[0], moe_w_gate[0].astype(BF16),
                         moe_w_up[0].astype(BF16), moe_w_down[0].astype(BF16), row(final_g),
                         tc=2048, tr=256, tf=1792, sub=256, win=5)
    return out.reshape(b, s, d)
```

```python
import functools
import math

import jax
import jax.numpy as jnp
from jax import lax
from jax.experimental import pallas as pl
from jax.experimental.pallas import tpu as pltpu

D_MODEL = 1024
GW = 256
N_HEADS = 4
HEAD_DIM = 64
QK_DIM = 32
V_ROWS = HEAD_DIM + 16
CONV_WIDTH = 31
POOL_WINDOWS = (2, 4, 8, 16)
DILATIONS = (1, 4, 16)
BAND = 128
ROPE_THETA = 500000.0
N_EXPERTS = 8
NORM_EPS = 1e-5
HALO = 32
LANES = 128
SUBLANES = 8
FFN_SUBTILE = 256
VMEM_LIMIT = 56 * 1024 * 1024

BF16 = jnp.bfloat16
F32 = jnp.float32
NT_DIMS = (((1,), (1,)), ((), ()))


def _params(*sem):
    return pltpu.CompilerParams(dimension_semantics=sem, vmem_limit_bytes=VMEM_LIMIT)


def _lanes_in(shape, start, width):
    lane = lax.broadcasted_iota(jnp.int32, shape, len(shape) - 1)
    return (lane >= start) & (lane < start + width)


def _rms(x, g):
    ms = jnp.mean(x * x, axis=-1, keepdims=True)
    return x * lax.rsqrt(ms + NORM_EPS) * g


def _rope_table(seq, width, rot):
    half = rot // 2
    pos = jnp.arange(seq, dtype=F32)
    inv = ROPE_THETA ** (-jnp.arange(0, rot, 2, dtype=F32) / rot)
    ang = pos[:, None] * inv[None, :]
    cos, sin = jnp.cos(ang), jnp.sin(ang)
    lane = jnp.arange(GW) % width
    f = lane % half
    first = lane < half
    second = (lane >= half) & (lane < rot)
    c = jnp.where((lane < rot)[None, :], cos[:, f], 1.0)
    sn = jnp.where(first[None, :], -sin[:, f], 0.0)
    sp = jnp.where(second[None, :], sin[:, f], 0.0)
    return jnp.concatenate([c, sn, sp], axis=1)


def _rope(u, tab_ref, half):
    return (u * tab_ref[:, 0:GW]
            + pltpu.roll(u, GW - half, 1) * tab_ref[:, GW:2 * GW]
            + pltpu.roll(u, half, 1) * tab_ref[:, 2 * GW:3 * GW])


def _norm_proj_kernel(x_ref, g_ref, w_ref, rc_ref, rd_ref, uab_ref, qkvc_ref, qkv4_ref, qkv16_ref, qt_ref, kd_ref,
                      vt_ref, v_ref, cs_ref, *, tk):
    h = _rms(x_ref[0], g_ref[...]).astype(BF16)

    def proj(grp):
        return jnp.dot(h, w_ref[:, grp * GW:(grp + 1) * GW], preferred_element_type=F32)

    for grp in range(3):
        uab_ref[0, :, grp * GW:(grp + 1) * GW] = proj(grp)
    c_scale = HEAD_DIM ** -0.5
    d_scale = QK_DIM ** -0.5 * math.log2(math.e)
    parts = (_rope(proj(3), rc_ref, 8) * c_scale, _rope(proj(4), rc_ref, 8), proj(5))
    for k, val in enumerate(parts):
        qkvc_ref[0, :, k * GW:(k + 1) * GW] = val.astype(BF16)
        for half in range(GW // LANES):
            cs_ref[2 * k + half] = val[:, half * LANES:(half + 1) * LANES]
    tm = cs_ref.shape[1]
    for dil, out_ref in zip(DILATIONS[1:], (qkv4_ref, qkv16_ref)):
        for r in range(dil):
            for slab in range(cs_ref.shape[0]):
                rows = cs_ref[slab, pl.ds(r, tm // dil, stride=dil), :]
                col = r * 3 * GW + slab * LANES
                out_ref[0, :, col:col + LANES] = rows.astype(BF16)
    qt_ref[0] = jnp.transpose(_rope(proj(6), rd_ref, 4) * d_scale).astype(BF16)
    kd_ref[0] = _rope(proj(7), rd_ref, 4).astype(BF16)
    v_ref[...] = proj(8)
    vt = jnp.transpose(v_ref[...]).astype(BF16)
    ones = jnp.ones((V_ROWS - HEAD_DIM, tk), BF16)
    for blk in range(vt.shape[1] // tk):
        for hd in range(N_HEADS):
            vt_ref[0, blk, hd * V_ROWS:hd * V_ROWS + HEAD_DIM, :] = (
                vt[hd * HEAD_DIM:(hd + 1) * HEAD_DIM, blk * tk:(blk + 1) * tk])
            vt_ref[0, blk, hd * V_ROWS + HEAD_DIM:(hd + 1) * V_ROWS, :] = ones


def _norm_proj(x, g, w, rope_c, rope_d, tm, tk):
    b, s, d = x.shape
    grid = (s // tm, b)
    return pl.pallas_call(
        functools.partial(_norm_proj_kernel, tk=tk),
        grid=grid,
        in_specs=[
            pl.BlockSpec((1, tm, d), lambda i, j: (j, i, 0)),
            pl.BlockSpec((1, d), lambda i, j: (0, 0)),
            pl.BlockSpec((d, 9 * GW), lambda i, j: (0, 0)),
            pl.BlockSpec((tm, 3 * GW), lambda i, j: (i, 0)),
            pl.BlockSpec((tm, 3 * GW), lambda i, j: (i, 0)),
        ],
        out_specs=[
            pl.BlockSpec((1, tm, 3 * GW), lambda i, j: (j, i, 0)),
            pl.BlockSpec((1, tm, 3 * GW), lambda i, j: (j, i, 0)),
            *[pl.BlockSpec((1, tm // dil, dil * 3 * GW), lambda i, j: (j, i, 0)) for dil in DILATIONS[1:]],
            pl.BlockSpec((1, GW, tm), lambda i, j: (j, 0, i)),
            pl.BlockSpec((1, tm, GW), lambda i, j: (j, i, 0)),
            pl.BlockSpec((1, tm // tk, N_HEADS * V_ROWS, tk), lambda i, j: (j, i, 0, 0)),
        ],
        out_shape=[
            jax.ShapeDtypeStruct((b, s, 3 * GW), F32),
            jax.ShapeDtypeStruct((b, s, 3 * GW), BF16),
            *[jax.ShapeDtypeStruct((b, s // dil, dil * 3 * GW), BF16) for dil in DILATIONS[1:]],
            jax.ShapeDtypeStruct((b, GW, s), BF16),
            jax.ShapeDtypeStruct((b, s, GW), BF16),
            jax.ShapeDtypeStruct((b, s // tk, N_HEADS * V_ROWS, tk), BF16),
        ],
        scratch_shapes=[
            pltpu.VMEM((tm, GW), F32),
            pltpu.VMEM((3 * GW // LANES, tm, LANES), F32),
        ],
        compiler_params=_params("arbitrary", "arbitrary"),
        name="norm_proj",
    )(x, g, w, rope_c, rope_d)


def _local_kernel(u_ref, dww_ref, dwb_ref, lng_ref, lnb_ref, pww_ref, plw_ref, pls_ref,
                  y_ref, zbuf, pbuf, shift_ref, *, ts):
    si = pl.program_id(1)

    @pl.when(si == 0)
    def _():
        zbuf[0:HALO, :] = jnp.zeros((HALO, GW), F32)
        pbuf[0:HALO, :] = jnp.zeros((HALO, GW), F32)

    a = u_ref[0, :, 0:GW]
    b = u_ref[0, :, GW:2 * GW]
    zbuf[HALO:HALO + ts, :] = a * jax.nn.sigmoid(b)
    pbuf[HALO:HALO + ts, :] = u_ref[0, :, 2 * GW:3 * GW]

    def by_phase(buf, shifted, offsets, term):
        total = None
        for phase in range(SUBLANES):
            offs = [o for o in offsets if o % SUBLANES == phase]
            if not offs:
                continue
            src = buf
            if phase:
                n = max(offs) - phase + ts
                shifted[phase, 0:n, :] = buf[phase:phase + n, :]
                src = shifted.at[phase]
            for o in offs:
                t = term(o, src[o - phase:o - phase + ts, :])
                total = t if total is None else total + t
        return total

    base = HALO - (CONV_WIDTH - 1)
    acc = by_phase(zbuf, shift_ref.at[0], range(base, base + CONV_WIDTH),
                   lambda o, rows: dww_ref[o - base:o - base + 1, :] * rows)
    z = acc + dwb_ref[...]
    mu = jnp.mean(z, axis=-1, keepdims=True)
    zc = z - mu
    var = jnp.mean(zc * zc, axis=-1, keepdims=True)
    zn = zc * lax.rsqrt(var + NORM_EPS) * lng_ref[...] + lnb_ref[...]
    zs = zn * jax.nn.sigmoid(zn)
    y_ref[0, :, 0:GW] = jnp.dot(zs.astype(BF16), pww_ref[...], preferred_element_type=F32).astype(BF16)

    lane = lax.broadcasted_iota(jnp.int32, (ts, GW), 1)
    pool_group = GW // len(POOL_WINDOWS)
    win = jnp.zeros((ts, GW), jnp.int32)
    for gi, w in enumerate(POOL_WINDOWS):
        win = jnp.where(lane >= gi * pool_group, w, win)
    p0 = pbuf[HALO:HALO + ts, :]
    n = HALO + ts
    sums = {1: pbuf}
    level, start = 1, 0
    while level < max(POOL_WINDOWS):
        prev = sums[level]
        start = -(-(start + level) // SUBLANES) * SUBLANES
        dst = shift_ref.at[1, len(sums) - 1]
        dst[start:n, :] = prev[start:n, :] + prev[start - level:n - level, :]
        level *= 2
        sums[level] = dst
    assert start <= HALO
    tot = sums[POOL_WINDOWS[0]][HALO:n, :]
    for gi, w in enumerate(POOL_WINDOWS[1:], start=1):
        tot = jnp.where(lane >= gi * pool_group, sums[w][HALO:n, :], tot)
    pos = si * ts + lax.broadcasted_iota(jnp.int32, (ts, GW), 0)
    cnt = jnp.minimum(pos + 1, win).astype(F32)
    pooled = tot / cnt - p0
    yb = jnp.dot(pooled.astype(BF16), plw_ref[...], preferred_element_type=F32) * pls_ref[...]
    y_ref[0, :, GW:2 * GW] = yb.astype(BF16)

    zbuf[0:HALO, :] = zbuf[ts:ts + HALO, :]
    pbuf[0:HALO, :] = pbuf[ts:ts + HALO, :]


def _local_mixers(uab, dww, dwb, lng, lnb, pww, plw, pls, ts):
    b, s, _ = uab.shape
    vec = lambda: pl.BlockSpec((1, GW), lambda i, j: (0, 0))
    mat = lambda: pl.BlockSpec((GW, GW), lambda i, j: (0, 0))
    return pl.pallas_call(
        functools.partial(_local_kernel, ts=ts),
        grid=(b, s // ts),
        in_specs=[
            pl.BlockSpec((1, ts, 3 * GW), lambda i, j: (i, j, 0)),
            pl.BlockSpec((HALO, GW), lambda i, j: (0, 0)),
            vec(), vec(), vec(), mat(), mat(), vec(),
        ],
        out_specs=pl.BlockSpec((1, ts, 2 * GW), lambda i, j: (i, j, 0)),
        out_shape=jax.ShapeDtypeStruct((b, s, 2 * GW), BF16),
        scratch_shapes=[
            pltpu.VMEM((HALO + ts, GW), F32),
            pltpu.VMEM((HALO + ts, GW), F32),
            pltpu.VMEM((2, SUBLANES, HALO + ts, GW), F32),
        ],
        compiler_params=_params("arbitrary", "arbitrary"),
        name="local_mixers",
    )(uab, dww, dwb, lng, lnb, pww, plw, pls)


def _band_attn_kernel(q_ref, k_ref, v_ref, o_ref, lse_ref, s_ref):
    row = lax.broadcasted_iota(jnp.int32, (BAND, 2 * BAND), 0)
    col = lax.broadcasted_iota(jnp.int32, (BAND, 2 * BAND), 1)
    halves = [_lanes_in((BAND, LANES), sub * HEAD_DIM, HEAD_DIM) for sub in range(LANES // HEAD_DIM)]

    def starts(n):
        return pl.multiple_of(n * BAND, BAND), pl.multiple_of(jnp.maximum(n - 1, 0) * BAND, BAND)

    def scores(n, slot):
        q0, k0 = starts(n)
        q = q_ref[0, pl.ds(q0, BAND), :].astype(F32)
        kw = k_ref[0, pl.ds(k0, 2 * BAND), :]
        dist = (row - col) + (q0 - k0)
        mask = (dist >= 0) & (dist <= BAND)
        for h in range(N_HEADS):
            qh = jnp.where(_lanes_in((BAND, GW), h * HEAD_DIM, HEAD_DIM), q, 0.0).astype(BF16)
            s = lax.dot_general(qh, kw, NT_DIMS, preferred_element_type=F32)
            s_ref[slot * N_HEADS + h] = jnp.where(mask, s, -jnp.inf)

    def values(n, slot):
        q0, k0 = starts(n)
        for tile in range(GW // LANES):
            vw = v_ref[0, pl.ds(k0, 2 * BAND), tile * LANES:(tile + 1) * LANES]
            o = jnp.zeros((BAND, LANES), F32)
            lse = jnp.zeros((BAND, LANES), F32)
            for sub in range(LANES // HEAD_DIM):
                s = s_ref[slot * N_HEADS + tile * (LANES // HEAD_DIM) + sub]
                m = jnp.max(s, axis=-1, keepdims=True)
                p = jnp.exp(s - m)
                l = jnp.sum(p, axis=-1, keepdims=True)
                acc = jnp.dot(p.astype(BF16), vw, preferred_element_type=F32)
                mine = halves[sub]
                o = jnp.where(mine, acc / l, o)
                lse = jnp.where(mine, m + jnp.log(l), lse)
            o_ref[0, pl.ds(q0, BAND), tile * LANES:(tile + 1) * LANES] = o.astype(o_ref.dtype)
            lse_ref[0, pl.ds(q0, BAND), tile * LANES:(tile + 1) * LANES] = lse

    def pair(t, carry):
        scores(2 * t, 0)
        scores(2 * t + 1, 1)
        values(2 * t, 0)
        values(2 * t + 1, 1)
        return carry

    lax.fori_loop(0, q_ref.shape[1] // (2 * BAND), pair, 0)


def _band_attn(qkv, dil):
    b, l, _ = qkv.shape
    assert l % (2 * BAND) == 0
    part = lambda c: pl.BlockSpec((1, l, GW), lambda i, r: (i, 0, 3 * r + c))
    out = pl.BlockSpec((1, l, GW), lambda i, r: (i, 0, r))
    return pl.pallas_call(
        _band_attn_kernel,
        grid=(b, dil),
        in_specs=[part(0), part(1), part(2)],
        out_specs=[out, out],
        out_shape=[jax.ShapeDtypeStruct((b, l, dil * GW), BF16), jax.ShapeDtypeStruct((b, l, dil * GW), F32)],
        scratch_shapes=[pltpu.VMEM((2 * N_HEADS, BAND, 2 * BAND), F32)],
        compiler_params=_params("arbitrary", "arbitrary"),
        name="band_attn",
    )(qkv, qkv, qkv)


def _diff_attn_kernel(lam_ref, g_ref, qt_ref, k_ref, vt_ref, o_ref, qm_ref, m_ref, acc_ref, yt_ref,
                      st_ref, mx_ref, *, tq, lam_init):
    n_chain = 2 * N_HEADS
    i = pl.program_id(1)
    @pl.when((pl.program_id(0) == 0) & (i == 0))
    def _():
        qm_ref[...] = jnp.zeros(qm_ref.shape, BF16)

    for c in range(n_chain):
        qm_ref[c, c * QK_DIM:(c + 1) * QK_DIM, :] = qt_ref[0, c * QK_DIM:(c + 1) * QK_DIM, :]
    m_ref[...] = jnp.full(m_ref.shape, -jnp.inf, F32)
    acc_ref[...] = jnp.zeros(acc_ref.shape, F32)
    tk = st_ref.shape[1]
    ratio = tq // tk
    ahead = (lax.broadcasted_iota(jnp.int32, (tk, tq), 0)
             - lax.broadcasted_iota(jnp.int32, (tk, tq), 1))

    def score(c, kb, diag):
        st = jnp.dot(kb, qm_ref[c], preferred_element_type=F32)
        if diag is not None:
            st = jnp.where(ahead <= -diag * tk, st, -jnp.inf)
        st_ref[c] = st
        mx_ref[c] = jnp.max(st, axis=0, keepdims=True)

    def value(c, j):
        h = c % N_HEADS
        m_prev = m_ref[c]
        m_next = jnp.maximum(m_prev, mx_ref[c])
        alpha = jnp.exp2(m_prev - m_next)
        p = jnp.exp2(st_ref[c] - m_next).astype(BF16)
        vt = vt_ref[0, j, h * V_ROWS:(h + 1) * V_ROWS, :]
        acc_ref[c] = alpha * acc_ref[c] + jnp.dot(vt, p, preferred_element_type=F32)
        m_ref[c] = m_next

    def k_block(j):
        return k_ref[0, pl.ds(pl.multiple_of(j * tk, tk), tk), :]

    def scores(j, diag):
        kb = k_block(j)
        for c in range(n_chain):
            score(c, kb, diag)

    def values(j):
        for c in range(n_chain):
            value(c, j)

    def values_then_scores(j, diag):
        kb = k_block(j + 1)
        for c in range(n_chain):
            value(c, j)
            score(c, kb, diag)

    before = ratio * i

    @pl.when(i == 0)
    def _():
        scores(0, 0)

    @pl.when(i > 0)
    def _():
        scores(0, None)

    def body(t, carry):
        values_then_scores(2 * t, None)
        values_then_scores(2 * t + 1, None)
        return carry

    plain = jnp.maximum(before - 1, 0)
    lax.fori_loop(0, plain // 2, body, 0)

    @pl.when(plain % 2 == 1)
    def _():
        values_then_scores(plain - 1, None)

    @pl.when(i > 0)
    def _():
        values_then_scores(before - 1, 0)

    for diag in range(1, ratio):
        values_then_scores(before + diag - 1, diag)
    values(before + ratio - 1)

    lv = lam_ref[...]
    lam = (jnp.exp(jnp.sum(lv[0:1] * lv[1:2], axis=-1, keepdims=True))
           - jnp.exp(jnp.sum(lv[2:3] * lv[3:4], axis=-1, keepdims=True)) + lam_init)
    for h in range(N_HEADS):
        a1, a2 = acc_ref[h], acc_ref[N_HEADS + h]
        oh = (a1[:HEAD_DIM] / a1[HEAD_DIM:HEAD_DIM + 1]
              - lam * (a2[:HEAD_DIM] / a2[HEAD_DIM:HEAD_DIM + 1]))
        ms = jnp.mean(oh * oh, axis=0, keepdims=True)
        g = g_ref[h * HEAD_DIM:(h + 1) * HEAD_DIM, :]
        yt_ref[h * HEAD_DIM:(h + 1) * HEAD_DIM, :] = oh * lax.rsqrt(ms + NORM_EPS) * (g * (1.0 - lam_init))
    o_ref[0] = jnp.transpose(yt_ref[...]).astype(BF16)


def _diff_attn(qt, k, vt, lam_vecs, g_col, lam_init, tq):
    b, s, _ = k.shape
    nk, tk = vt.shape[1], vt.shape[3]
    assert tq % tk == 0 and s % tq == 0
    n_chain = 2 * N_HEADS
    return pl.pallas_call(
        functools.partial(_diff_attn_kernel, tq=tq, lam_init=lam_init),
        grid=(b, s // tq),
        in_specs=[
            pl.BlockSpec((4, QK_DIM), lambda i, j: (0, 0)),
            pl.BlockSpec((GW, 1), lambda i, j: (0, 0)),
            pl.BlockSpec((1, GW, tq), lambda i, j: (i, 0, j)),
            pl.BlockSpec((1, s, GW), lambda i, j: (i, 0, 0)),
            pl.BlockSpec((1, nk, N_HEADS * V_ROWS, tk), lambda i, j: (i, 0, 0, 0)),
        ],
        out_specs=pl.BlockSpec((1, tq, GW), lambda i, j: (i, j, 0)),
        out_shape=jax.ShapeDtypeStruct((b, s, GW), BF16),
        scratch_shapes=[
            pltpu.VMEM((n_chain, GW, tq), BF16),
            pltpu.VMEM((n_chain, 1, tq), F32),
            pltpu.VMEM((n_chain, V_ROWS, tq), F32),
            pltpu.VMEM((GW, tq), F32),
            pltpu.VMEM((n_chain, tk, tq), F32),
            pltpu.VMEM((n_chain, 1, tq), F32),
        ],
        compiler_params=_params("arbitrary", "arbitrary"),
        name="diff_attn",
    )(lam_vecs, g_col, qt, k, vt)


def _out_proj_kernel(x_ref, yab_ref, o1_ref, o2_ref, o3_ref, l1_ref, l2_ref, l3_ref, yd_ref, w_ref, out_ref,
                     nat_ref):
    tm = x_ref.shape[0]

    def natural(ref, dil, slot):
        for r in range(dil):
            for half in range(GW // LANES):
                col = r * GW + half * LANES
                nat_ref[2 * slot + half, pl.ds(r, tm // dil, stride=dil), :] = ref[:, col:col + LANES].astype(F32)
        return jnp.concatenate([nat_ref[2 * slot], nat_ref[2 * slot + 1]], axis=1)

    d2, d3 = DILATIONS[1:]
    l1, l2, l3 = l1_ref[...], natural(l2_ref, d2, 0), natural(l3_ref, d3, 1)
    o2, o3 = natural(o2_ref, d2, 2), natural(o3_ref, d3, 3)
    m = jnp.maximum(jnp.maximum(l1, l2), l3)
    w1, w2, w3 = jnp.exp(l1 - m), jnp.exp(l2 - m), jnp.exp(l3 - m)
    yc = (w1 * o1_ref[...] + w2 * o2 + w3 * o3) / (w1 + w2 + w3)
    acc = jnp.dot(yab_ref[...], w_ref[0:2 * GW, :], preferred_element_type=F32)
    acc = acc + jnp.dot(yc.astype(BF16), w_ref[2 * GW:3 * GW, :], preferred_element_type=F32)
    acc = acc + jnp.dot(yd_ref[...], w_ref[3 * GW:4 * GW, :], preferred_element_type=F32)
    out_ref[...] = x_ref[...] + acc


def _out_proj(x, yab, os_, lses, yd, w, tm):
    t, d = x.shape
    rows = lambda c: pl.BlockSpec((tm, c), lambda i: (i, 0))
    branch = [pl.BlockSpec((tm // dil, dil * GW), lambda i: (i, 0)) for dil in DILATIONS]
    return pl.pallas_call(
        _out_proj_kernel,
        grid=(t // tm,),
        in_specs=[rows(d), rows(2 * GW)] + branch * 2 + [rows(GW), pl.BlockSpec((d, d), lambda i: (0, 0))],
        out_specs=rows(d),
        out_shape=jax.ShapeDtypeStruct((t, d), F32),
        scratch_shapes=[pltpu.VMEM((4 * GW // LANES, tm, LANES), F32)],
        compiler_params=_params("arbitrary"),
        name="out_proj",
    )(x, yab, *os_, *lses, yd, w)


def _ffn_kernel(x_ref, g_ref, wg_ref, wu_ref, wd_ref, out_ref, acc_ref, *, tf):
    hn = _rms(x_ref[...], g_ref[...]).astype(BF16)
    acc_ref[...] = x_ref[...]
    for f0 in range(0, wg_ref.shape[1], tf):
        gate = jnp.dot(hn, wg_ref[:, f0:f0 + tf], preferred_element_type=F32)
        up = jnp.dot(hn, wu_ref[:, f0:f0 + tf], preferred_element_type=F32)
        act = (gate * jax.nn.sigmoid(gate) * up).astype(BF16)
        acc_ref[...] += jnp.dot(act, wd_ref[f0:f0 + tf, :], preferred_element_type=F32)
    out_ref[...] = acc_ref[...]


def _ffn(x, g, wg, wu, wd, tm, tf):
    t, d = x.shape
    ff = wg.shape[1]
    assert ff % tf == 0
    whole = lambda shape: pl.BlockSpec(shape, lambda i: (0, 0))
    return pl.pallas_call(
        functools.partial(_ffn_kernel, tf=tf),
        grid=(t // tm,),
        in_specs=[pl.BlockSpec((tm, d), lambda i: (i, 0)), whole((1, d)), whole((d, ff)), whole((d, ff)),
                  whole((ff, d))],
        out_specs=pl.BlockSpec((tm, d), lambda i: (i, 0)),
        out_shape=jax.ShapeDtypeStruct((t, d), F32),
        scratch_shapes=[pltpu.VMEM((tm, d), F32)],
        compiler_params=_params("arbitrary"),
        name="ffn_dense",
    )(x, g, wg, wu, wd)


META_E1, META_E2, META_RANK1, META_RANK2, META_G1, META_G2 = range(6)


def _router_kernel(x_ref, g_ref, rhi_ref, rlo_ref, hn_ref, meta_ref, after_ref, carry_ref, *, blocks_per_chunk):
    @pl.when(pl.program_id(0) % blocks_per_chunk == 0)
    def _():
        carry_ref[...] = jnp.zeros_like(carry_ref)

    h = _rms(x_ref[...], g_ref[...])
    hn_ref[...] = h.astype(BF16)
    h_hi = h.astype(BF16)
    h_lo = (h - h_hi.astype(F32)).astype(BF16)
    nt = lambda a, b: lax.dot_general(a, b, NT_DIMS, preferred_element_type=F32)
    logits = nt(rhi_ref[...], h_hi) + nt(rlo_ref[...], h_hi) + nt(rhi_ref[...], h_lo)
    tm = logits.shape[1]
    expert = lax.broadcasted_iota(jnp.int32, logits.shape, 0)
    m1 = jnp.max(logits, axis=0, keepdims=True)
    i1 = jnp.min(jnp.where(logits == m1, expert, N_EXPERTS), axis=0, keepdims=True)
    rest = jnp.where(expert == i1, -jnp.inf, logits)
    m2 = jnp.max(rest, axis=0, keepdims=True)
    i2 = jnp.min(jnp.where(rest == m2, expert, N_EXPERTS), axis=0, keepdims=True)
    e2 = jnp.exp(m2 - m1)
    g1 = 1.0 / (1.0 + e2)
    g2 = e2 / (1.0 + e2)

    oh1 = expert == i1
    oh2 = expert == i2
    both = jnp.where(oh1, 1.0, jnp.where(oh2, 1.0, 0.0))
    row = lax.broadcasted_iota(jnp.int32, (tm, tm), 0)
    col = lax.broadcasted_iota(jnp.int32, (tm, tm), 1)
    earlier = jnp.where(row < col, 1.0, 0.0).astype(BF16)
    carry = carry_ref[:, 0:1]
    prefix = jnp.dot(both.astype(BF16), earlier, preferred_element_type=F32) + carry
    rank1 = jnp.sum(jnp.where(oh1, prefix, 0.0), axis=0, keepdims=True)
    rank2 = jnp.sum(jnp.where(oh2, prefix, 0.0), axis=0, keepdims=True)
    total = carry + jnp.sum(both, axis=1, keepdims=True)

    meta = jnp.zeros(logits.shape, F32)
    for k, val in ((META_E1, i1.astype(F32)), (META_E2, i2.astype(F32)), (META_RANK1, rank1),
                   (META_RANK2, rank2), (META_G1, g1), (META_G2, g2)):
        meta = jnp.where(expert == k, val, meta)
    meta_ref[...] = meta
    after_ref[0] = jnp.broadcast_to(total, after_ref.shape[1:])
    carry_ref[...] = jnp.broadcast_to(total, carry_ref.shape)


def _router(x, g, router_w, tm, blocks_per_chunk):
    t, d = x.shape
    nblk = t // tm
    assert router_w.shape == (d, N_EXPERTS) and N_EXPERTS == SUBLANES
    r_t = router_w.T
    r_hi = r_t.astype(BF16)
    counts = pl.BlockSpec((1, N_EXPERTS, LANES), lambda i: (i, 0, 0))
    return pl.pallas_call(
        functools.partial(_router_kernel, blocks_per_chunk=blocks_per_chunk),
        grid=(nblk,),
        in_specs=[
            pl.BlockSpec((tm, d), lambda i: (i, 0)),
            pl.BlockSpec((1, d), lambda i: (0, 0)),
            pl.BlockSpec((N_EXPERTS, d), lambda i: (0, 0)),
            pl.BlockSpec((N_EXPERTS, d), lambda i: (0, 0)),
        ],
        out_specs=[
            pl.BlockSpec((tm, d), lambda i: (i, 0)),
            pl.BlockSpec((N_EXPERTS, tm), lambda i: (0, i)),
            counts,
        ],
        out_shape=[
            jax.ShapeDtypeStruct((t, d), BF16),
            jax.ShapeDtypeStruct((N_EXPERTS, t), F32),
            jax.ShapeDtypeStruct((nblk, N_EXPERTS, LANES), F32),
        ],
        scratch_shapes=[pltpu.VMEM((N_EXPERTS, LANES), F32)],
        compiler_params=_params("arbitrary"),
        name="router",
    )(x, g, r_hi, (r_t - r_hi.astype(F32)).astype(BF16))


SLOT_ROW1, SLOT_ROW2, SLOT_G1, SLOT_G2 = range(4)


def _moe_kernel(nt_ref, toff_ref, cuts_ref, hn_ref, slotc_ref, slotr_ref, wg_ref, wu_ref, wd_ref, out_ref,
                xs_ref, ys_ref, *, tr, sub, win):
    c, e, f = pl.program_id(0), pl.program_id(1), pl.program_id(2)
    last_f = pl.num_programs(2) - 1
    grp = c * N_EXPERTS + e
    n_tiles = nt_ref[grp]
    row0 = toff_ref[grp] * tr
    tc = hn_ref.shape[0]
    nsub = tc // sub

    @pl.when((e == 0) & (f == 0))
    def _():
        out_ref[...] = jnp.zeros_like(out_ref)

    def rows_of(r):
        return pl.ds(pl.multiple_of(r * tr, tr), tr)

    def token_window(r):
        base = grp * nsub
        first = r * tr
        last = jnp.minimum(first + tr, cuts_ref[base + nsub - 1]) - 1
        lo = hi = 0
        for k in range(nsub - 1):
            cut = cuts_ref[base + k]
            lo = lo + (cut <= first).astype(jnp.int32)
            hi = hi + (cut <= last).astype(jnp.int32)
        start = jnp.minimum(lo, nsub - win)
        return hi < start + win, pl.multiple_of(start * sub, sub)

    def spans(r, fn):
        covered, tok0 = token_window(r)

        @pl.when(covered)
        def _():
            fn(r, tok0, win * sub)

        @pl.when(jnp.logical_not(covered))
        def _():
            fn(r, 0, tc)

    def gather_span(r, tok0, ntok):
        toks = pl.ds(tok0, ntok)
        target = (lax.broadcasted_iota(jnp.int32, (tr, ntok), 0) + (row0 + r * tr)).astype(F32)
        hit1 = slotr_ref[SLOT_ROW1:SLOT_ROW1 + 1, toks] == target
        hit2 = slotr_ref[SLOT_ROW2:SLOT_ROW2 + 1, toks] == target
        oh = jnp.where(hit1, 1.0, jnp.where(hit2, 1.0, 0.0)).astype(BF16)
        xs_ref[rows_of(r), :] = jnp.dot(oh, hn_ref[toks, :], preferred_element_type=F32).astype(BF16)

    def gather_tile(r, carry):
        spans(r, gather_span)
        return carry

    def ffn_tiles(r, count, first):
        rows = pl.ds(pl.multiple_of(r * tr, tr), count * tr)
        xt = xs_ref[rows, :]
        for f0 in range(0, wg_ref.shape[1], FFN_SUBTILE):
            gate = jnp.dot(xt, wg_ref[:, f0:f0 + FFN_SUBTILE], preferred_element_type=F32)
            up = jnp.dot(xt, wu_ref[:, f0:f0 + FFN_SUBTILE], preferred_element_type=F32)
            act = (gate * jax.nn.sigmoid(gate) * up).astype(BF16)
            y = jnp.dot(act, wd_ref[f0:f0 + FFN_SUBTILE, :], preferred_element_type=F32)
            if first and f0 == 0:
                ys_ref[rows, :] = y
            else:
                ys_ref[rows, :] += y

    def scatter_span(r, tok0, ntok):
        toks = pl.ds(tok0, ntok)
        yb = ys_ref[rows_of(r), :].astype(BF16)
        target = (lax.broadcasted_iota(jnp.int32, (ntok, tr), 1) + (row0 + r * tr)).astype(F32)
        hit1 = slotc_ref[toks, SLOT_ROW1:SLOT_ROW1 + 1] == target
        hit2 = slotc_ref[toks, SLOT_ROW2:SLOT_ROW2 + 1] == target
        gh = jnp.where(hit1, slotc_ref[toks, SLOT_G1:SLOT_G1 + 1],
                       jnp.where(hit2, slotc_ref[toks, SLOT_G2:SLOT_G2 + 1], 0.0)).astype(BF16)
        out_ref[toks, :] += jnp.dot(gh, yb, preferred_element_type=F32)

    def scatter_tile(r, carry):
        spans(r, scatter_span)
        return carry

    def ffn_all(first):
        def pair(t, carry):
            ffn_tiles(2 * t, 2, first)
            return carry

        lax.fori_loop(0, n_tiles // 2, pair, 0)

        @pl.when(n_tiles % 2 == 1)
        def _():
            ffn_tiles(n_tiles - 1, 1, first)

    @pl.when(f == 0)
    def _():
        lax.fori_loop(0, n_tiles, gather_tile, 0)
        ffn_all(True)

    @pl.when(f > 0)
    def _():
        ffn_all(False)

    @pl.when(f == last_f)
    def _():
        lax.fori_loop(0, n_tiles, scatter_tile, 0)


def _moe(hn, slot_cols, slot_rows, nt, toff, cuts, wg, wu, wd, tc, tr, tf, sub, win):
    t, d = hn.shape
    ne, _, ff = wg.shape
    grid_spec = pltpu.PrefetchScalarGridSpec(
        num_scalar_prefetch=3,
        grid=(t // tc, ne, ff // tf),
        in_specs=[
            pl.BlockSpec((tc, d), lambda c, e, f, *_: (c, 0), pipeline_mode=pl.Buffered(1)),
            pl.BlockSpec((tc, 4), lambda c, e, f, *_: (c, 0), pipeline_mode=pl.Buffered(1)),
            pl.BlockSpec((8, tc), lambda c, e, f, *_: (0, c), pipeline_mode=pl.Buffered(1)),
            pl.BlockSpec((None, d, tf), lambda c, e, f, *_: (e, 0, f)),
            pl.BlockSpec((None, d, tf), lambda c, e, f, *_: (e, 0, f)),
            pl.BlockSpec((None, tf, d), lambda c, e, f, *_: (e, f, 0)),
        ],
        out_specs=pl.BlockSpec((tc, d), lambda c, e, f, *_: (c, 0), pipeline_mode=pl.Buffered(1)),
        scratch_shapes=[
            pltpu.VMEM((tc + tr, d), BF16),
            pltpu.VMEM((tc + tr, d), F32),
        ],
    )
    return pl.pallas_call(
        functools.partial(_moe_kernel, tr=tr, sub=sub, win=win),
        grid_spec=grid_spec,
        out_shape=jax.ShapeDtypeStruct((t, d), F32),
        compiler_params=_params("arbitrary", "arbitrary", "arbitrary"),
        name="moe",
    )(nt, toff, cuts, hn, slot_cols, slot_rows, wg, wu, wd)


def _final_kernel(x_ref, y_ref, g_ref, out_ref):
    out_ref[...] = _rms(x_ref[...] + y_ref[...], g_ref[...])


def _final_norm(x, y, g, tm):
    t, d = x.shape
    rows = pl.BlockSpec((tm, d), lambda i: (i, 0))
    return pl.pallas_call(
        _final_kernel,
        grid=(t // tm,),
        in_specs=[rows, rows, pl.BlockSpec((1, d), lambda i: (0, 0))],
        out_specs=rows,
        out_shape=jax.ShapeDtypeStruct((t, d), F32),
        compiler_params=_params("arbitrary"),
        name="final_norm",
    )(x, y, g)


def _routed_swiglu(x, norm_g, router_w, wg, wu, wd, final_g, tc, tr, tf, sub, win):
    t, d = x.shape
    nc, nsub = t // tc, tc // sub
    hn, meta, after = _router(x, norm_g, router_w, tm=sub, blocks_per_chunk=nsub)
    through = after[:, :, 0].astype(jnp.int32).reshape(nc, nsub, N_EXPERTS)
    cuts = through.transpose(0, 2, 1)
    total = through[:, -1]
    nt = (total + tr - 1) // tr
    toff = jnp.cumsum(nt, axis=1) - nt
    first_row = jnp.repeat((toff * tr).astype(F32), tc, axis=0)
    experts = jnp.arange(N_EXPERTS, dtype=F32)
    pick = lambda e: jnp.sum(jnp.where(e[:, None] == experts, first_row, 0.0), axis=1)
    row1 = pick(meta[META_E1]) + meta[META_RANK1]
    row2 = pick(meta[META_E2]) + meta[META_RANK2]
    slot_rows = jnp.zeros((8, t), F32).at[0:4].set(jnp.stack([row1, row2, meta[META_G1], meta[META_G2]]))
    slot_cols = slot_rows[0:4].T
    y = _moe(hn, slot_cols, slot_rows, nt.reshape(-1), toff.reshape(-1), cuts.reshape(-1), wg, wu, wd,
             tc=tc, tr=tr, tf=tf, sub=sub, win=win)
    return _final_norm(x, y, final_g, tm=min(1024, t))


def _permute_w_in(w):
    def perm(block):
        d = block.shape[0]
        return block.reshape(d, N_HEADS, 2, QK_DIM).transpose(0, 2, 1, 3).reshape(d, GW)
    parts = [w[:, :6 * GW], perm(w[:, 6 * GW:7 * GW]), perm(w[:, 7 * GW:8 * GW]), w[:, 8 * GW:]]
    return jnp.concatenate(parts, axis=1)


def _block_diag(pool_w):
    n, p, _ = pool_w.shape
    out = jnp.zeros((n * p, n * p), pool_w.dtype)
    for gi in range(n):
        out = out.at[gi * p:(gi + 1) * p, gi * p:(gi + 1) * p].set(pool_w[gi])
    return out


def _token_mixing(x, layer, lam_init, p, rope_c, rope_d):
    b, s, d = x.shape
    row = lambda v: v.reshape(1, -1)
    w_in = _permute_w_in(p["w_in"][layer]).astype(BF16)
    uab, qkvc, qkvc4, qkvc16, qdt, kd, vdt = _norm_proj(x, row(p["norm1_g"][layer]), w_in, rope_c, rope_d,
                                                        tm=512, tk=256)
    dww = jnp.zeros((HALO, GW), F32).at[:CONV_WIDTH].set(p["conv_dw_w"][layer])
    yab = _local_mixers(
        uab, dww, row(p["conv_dw_b"][layer]), row(p["conv_ln_g"][layer]), row(p["conv_ln_b"][layer]),
        p["conv_pw_w"][layer].astype(BF16), _block_diag(p["pool_w"][layer]).astype(BF16),
        row(p["pool_scale"][layer]), ts=512)
    os_, lses = [], []
    for dil, qkv in zip(DILATIONS, (qkvc, qkvc4, qkvc16)):
        o, lse = _band_attn(qkv, dil)
        os_.append(o.reshape(b * s // dil, dil * GW))
        lses.append(lse.reshape(b * s // dil, dil * GW))
    yd = _diff_attn(qdt, kd, vdt, p["diff_lam"][layer], jnp.tile(p["diff_ln_g"][layer], N_HEADS).reshape(GW, 1),
                    lam_init, tq=512)
    return _out_proj(x.reshape(b * s, d), yab.reshape(b * s, 2 * GW), os_, lses, yd.reshape(b * s, GW),
                     p["w_out"][layer].astype(BF16), tm=512)


def kernel(x, norm1_g, w_in, conv_dw_w, conv_dw_b, conv_ln_g, conv_ln_b, conv_pw_w, pool_w, pool_scale,
           diff_lam, diff_ln_g, w_out, norm2_g, ffn_w_gate, ffn_w_up, ffn_w_down, moe_router, moe_w_gate,
           moe_w_up, moe_w_down, final_g):
    b, s, d = x.shape
    p = dict(norm1_g=norm1_g, w_in=w_in, conv_dw_w=conv_dw_w, conv_dw_b=conv_dw_b, conv_ln_g=conv_ln_g,
             conv_ln_b=conv_ln_b, conv_pw_w=conv_pw_w, pool_w=pool_w, pool_scale=pool_scale,
             diff_lam=diff_lam, diff_ln_g=diff_ln_g, w_out=w_out)
    rope_c = _rope_table(s, HEAD_DIM, HEAD_DIM // 4)
    rope_d = _rope_table(s, QK_DIM, QK_DIM // 4)
    row = lambda v: v.reshape(1, -1)

    x1 = _token_mixing(x, 0, 0.8 - 0.6 * math.exp(-0.3 * 0), p, rope_c, rope_d)
    x2 = _ffn(x1, row(norm2_g[0]), ffn_w_gate[0].astype(BF16), ffn_w_up[0].astype(BF16),
              ffn_w_down[0].astype(BF16), tm=512, tf=256)

    x3 = _token_mixing(x2.reshape(b, s, d), 1, 0.8 - 0.6 * math.exp(-0.3 * 1), p, rope_c, rope_d)
    out = _routed_swiglu(x3, row(norm2_g[1]), moe_router[0], moe_w_gate[0].astype(BF16),
                         moe_w_up[0].astype(BF16), moe_w_down[0].astype(BF16), row(final_g),
                         tc=2048, tr=256, tf=1792, sub=256, win=5)
    return out.reshape(b, s, d)
```

```python
import functools
import math

import jax
import jax.numpy as jnp
from jax import lax
from jax.experimental import pallas as pl
from jax.experimental.pallas import tpu as pltpu

D_MODEL = 1024
GW = 256
N_HEADS = 4
HEAD_DIM = 64
QK_DIM = 32
V_ROWS = HEAD_DIM + 16
CONV_WIDTH = 31
POOL_WINDOWS = (2, 4, 8, 16)
DILATIONS = (1, 4, 16)
BAND = 128
ROPE_THETA = 500000.0
N_EXPERTS = 8
NORM_EPS = 1e-5
HALO = 32
LANES = 128
SUBLANES = 8
FFN_SUBTILE = 256
VMEM_LIMIT = 56 * 1024 * 1024

BF16 = jnp.bfloat16
F32 = jnp.float32
NT_DIMS = (((1,), (1,)), ((), ()))


def _params(*sem):
    return pltpu.CompilerParams(dimension_semantics=sem, vmem_limit_bytes=VMEM_LIMIT)


def _lanes_in(shape, start, width):
    lane = lax.broadcasted_iota(jnp.int32, shape, len(shape) - 1)
    return (lane >= start) & (lane < start + width)


def _rms(x, g):
    ms = jnp.mean(x * x, axis=-1, keepdims=True)
    return x * lax.rsqrt(ms + NORM_EPS) * g


def _rope_table(seq, width, rot):
    half = rot // 2
    pos = jnp.arange(seq, dtype=F32)
    inv = ROPE_THETA ** (-jnp.arange(0, rot, 2, dtype=F32) / rot)
    ang = pos[:, None] * inv[None, :]
    cos, sin = jnp.cos(ang), jnp.sin(ang)
    lane = jnp.arange(GW) % width
    f = lane % half
    first = lane < half
    second = (lane >= half) & (lane < rot)
    c = jnp.where((lane < rot)[None, :], cos[:, f], 1.0)
    sn = jnp.where(first[None, :], -sin[:, f], 0.0)
    sp = jnp.where(second[None, :], sin[:, f], 0.0)
    return jnp.concatenate([c, sn, sp], axis=1)


def _rope(u, tab_ref, half):
    return (u * tab_ref[:, 0:GW]
            + pltpu.roll(u, GW - half, 1) * tab_ref[:, GW:2 * GW]
            + pltpu.roll(u, half, 1) * tab_ref[:, 2 * GW:3 * GW])


def _norm_proj_kernel(x_ref, g_ref, w_ref, rc_ref, rd_ref, uab_ref, qkvc_ref, qkv4_ref, qkv16_ref, qt_ref, kd_ref,
                      vt_ref, v_ref, cs_ref, *, tk):
    h = _rms(x_ref[0], g_ref[...]).astype(BF16)

    def proj(grp):
        return jnp.dot(h, w_ref[:, grp * GW:(grp + 1) * GW], preferred_element_type=F32)

    for grp in range(3):
        uab_ref[0, :, grp * GW:(grp + 1) * GW] = proj(grp)
    c_scale = HEAD_DIM ** -0.5
    d_scale = QK_DIM ** -0.5 * math.log2(math.e)
    parts = (_rope(proj(3), rc_ref, 8) * c_scale, _rope(proj(4), rc_ref, 8), proj(5))
    for k, val in enumerate(parts):
        qkvc_ref[0, :, k * GW:(k + 1) * GW] = val.astype(BF16)
        for half in range(GW // LANES):
            cs_ref[2 * k + half] = val[:, half * LANES:(half + 1) * LANES]
    tm = cs_ref.shape[1]
    for dil, out_ref in zip(DILATIONS[1:], (qkv4_ref, qkv16_ref)):
        for r in range(dil):
            for slab in range(cs_ref.shape[0]):
                rows = cs_ref[slab, pl.ds(r, tm // dil, stride=dil), :]
                col = r * 3 * GW + slab * LANES
                out_ref[0, :, col:col + LANES] = rows.astype(BF16)
    qt_ref[0] = jnp.transpose(_rope(proj(6), rd_ref, 4) * d_scale).astype(BF16)
    kd_ref[0] = _rope(proj(7), rd_ref, 4).astype(BF16)
    v_ref[...] = proj(8)
    vt = jnp.transpose(v_ref[...]).astype(BF16)
    ones = jnp.ones((V_ROWS - HEAD_DIM, tk), BF16)
    for blk in range(vt.shape[1] // tk):
        for hd in range(N_HEADS):
            vt_ref[0, blk, hd * V_ROWS:hd * V_ROWS + HEAD_DIM, :] = (
                vt[hd * HEAD_DIM:(hd + 1) * HEAD_DIM, blk * tk:(blk + 1) * tk])
            vt_ref[0, blk, hd * V_ROWS + HEAD_DIM:(hd + 1) * V_ROWS, :] = ones


def _norm_proj(x, g, w, rope_c, rope_d, tm, tk):
    b, s, d = x.shape
    grid = (s // tm, b)
    return pl.pallas_call(
        functools.partial(_norm_proj_kernel, tk=tk),
        grid=grid,
        in_specs=[
            pl.BlockSpec((1, tm, d), lambda i, j: (j, i, 0)),
            pl.BlockSpec((1, d), lambda i, j: (0, 0)),
            pl.BlockSpec((d, 9 * GW), lambda i, j: (0, 0)),
            pl.BlockSpec((tm, 3 * GW), lambda i, j: (i, 0)),
            pl.BlockSpec((tm, 3 * GW), lambda i, j: (i, 0)),
        ],
        out_specs=[
            pl.BlockSpec((1, tm, 3 * GW), lambda i, j: (j, i, 0)),
            pl.BlockSpec((1, tm, 3 * GW), lambda i, j: (j, i, 0)),
            *[pl.BlockSpec((1, tm // dil, dil * 3 * GW), lambda i, j: (j, i, 0)) for dil in DILATIONS[1:]],
            pl.BlockSpec((1, GW, tm), lambda i, j: (j, 0, i)),
            pl.BlockSpec((1, tm, GW), lambda i, j: (j, i, 0)),
            pl.BlockSpec((1, tm // tk, N_HEADS * V_ROWS, tk), lambda i, j: (j, i, 0, 0)),
        ],
        out_shape=[
            jax.ShapeDtypeStruct((b, s, 3 * GW), F32),
            jax.ShapeDtypeStruct((b, s, 3 * GW), BF16),
            *[jax.ShapeDtypeStruct((b, s // dil, dil * 3 * GW), BF16) for dil in DILATIONS[1:]],
            jax.ShapeDtypeStruct((b, GW, s), BF16),
            jax.ShapeDtypeStruct((b, s, GW), BF16),
            jax.ShapeDtypeStruct((b, s // tk, N_HEADS * V_ROWS, tk), BF16),
        ],
        scratch_shapes=[
            pltpu.VMEM((tm, GW), F32),
            pltpu.VMEM((3 * GW // LANES, tm, LANES), F32),
        ],
        compiler_params=_params("arbitrary", "arbitrary"),
        name="norm_proj",
    )(x, g, w, rope_c, rope_d)


def _local_kernel(u_ref, dww_ref, dwb_ref, lng_ref, lnb_ref, pww_ref, plw_ref, pls_ref,
                  y_ref, zbuf, pbuf, shift_ref, *, ts):
    si = pl.program_id(1)

    @pl.when(si == 0)
    def _():
        zbuf[0:HALO, :] = jnp.zeros((HALO, GW), F32)
        pbuf[0:HALO, :] = jnp.zeros((HALO, GW), F32)

    a = u_ref[0, :, 0:GW]
    b = u_ref[0, :, GW:2 * GW]
    zbuf[HALO:HALO + ts, :] = a * jax.nn.sigmoid(b)
    pbuf[HALO:HALO + ts, :] = u_ref[0, :, 2 * GW:3 * GW]

    def by_phase(buf, shifted, offsets, term):
        total = None
        for phase in range(SUBLANES):
            offs = [o for o in offsets if o % SUBLANES == phase]
            if not offs:
                continue
            src = buf
            if phase:
                n = max(offs) - phase + ts
                shifted[phase, 0:n, :] = buf[phase:phase + n, :]
                src = shifted.at[phase]
            for o in offs:
                t = term(o, src[o - phase:o - phase + ts, :])
                total = t if total is None else total + t
        return total

    base = HALO - (CONV_WIDTH - 1)
    acc = by_phase(zbuf, shift_ref.at[0], range(base, base + CONV_WIDTH),
                   lambda o, rows: dww_ref[o - base:o - base + 1, :] * rows)
    z = acc + dwb_ref[...]
    mu = jnp.mean(z, axis=-1, keepdims=True)
    zc = z - mu
    var = jnp.mean(zc * zc, axis=-1, keepdims=True)
    zn = zc * lax.rsqrt(var + NORM_EPS) * lng_ref[...] + lnb_ref[...]
    zs = zn * jax.nn.sigmoid(zn)
    y_ref[0, :, 0:GW] = jnp.dot(zs.astype(BF16), pww_ref[...], preferred_element_type=F32).astype(BF16)

    lane = lax.broadcasted_iota(jnp.int32, (ts, GW), 1)
    pool_group = GW // len(POOL_WINDOWS)
    win = jnp.zeros((ts, GW), jnp.int32)
    for gi, w in enumerate(POOL_WINDOWS):
        win = jnp.where(lane >= gi * pool_group, w, win)
    p0 = pbuf[HALO:HALO + ts, :]
    n = HALO + ts
    sums = {1: pbuf}
    level, start = 1, 0
    while level < max(POOL_WINDOWS):
        prev = sums[level]
        start = -(-(start + level) // SUBLANES) * SUBLANES
        dst = shift_ref.at[1, len(sums) - 1]
        dst[start:n, :] = prev[start:n, :] + prev[start - level:n - level, :]
        level *= 2
        sums[level] = dst
    assert start <= HALO
    tot = sums[POOL_WINDOWS[0]][HALO:n, :]
    for gi, w in enumerate(POOL_WINDOWS[1:], start=1):
        tot = jnp.where(lane >= gi * pool_group, sums[w][HALO:n, :], tot)
    pos = si * ts + lax.broadcasted_iota(jnp.int32, (ts, GW), 0)
    cnt = jnp.minimum(pos + 1, win).astype(F32)
    pooled = tot / cnt - p0
    yb = jnp.dot(pooled.astype(BF16), plw_ref[...], preferred_element_type=F32) * pls_ref[...]
    y_ref[0, :, GW:2 * GW] = yb.astype(BF16)

    zbuf[0:HALO, :] = zbuf[ts:ts + HALO, :]
    pbuf[0:HALO, :] = pbuf[ts:ts + HALO, :]


def _local_mixers(uab, dww, dwb, lng, lnb, pww, plw, pls, ts):
    b, s, _ = uab.shape
    vec = lambda: pl.BlockSpec((1, GW), lambda i, j: (0, 0))
    mat = lambda: pl.BlockSpec((GW, GW), lambda i, j: (0, 0))
    return pl.pallas_call(
        functools.partial(_local_kernel, ts=ts),
        grid=(b, s // ts),
        in_specs=[
            pl.BlockSpec((1, ts, 3 * GW), lambda i, j: (i, j, 0)),
            pl.BlockSpec((HALO, GW), lambda i, j: (0, 0)),
            vec(), vec(), vec(), mat(), mat(), vec(),
        ],
        out_specs=pl.BlockSpec((1, ts, 2 * GW), lambda i, j: (i, j, 0)),
        out_shape=jax.ShapeDtypeStruct((b, s, 2 * GW), BF16),
        scratch_shapes=[
            pltpu.VMEM((HALO + ts, GW), F32),
            pltpu.VMEM((HALO + ts, GW), F32),
            pltpu.VMEM((2, SUBLANES, HALO + ts, GW), F32),
        ],
        compiler_params=_params("arbitrary", "arbitrary"),
        name="local_mixers",
    )(uab, dww, dwb, lng, lnb, pww, plw, pls)


def _band_attn_kernel(q_ref, k_ref, v_ref, o_ref, lse_ref, s_ref):
    row = lax.broadcasted_iota(jnp.int32, (BAND, 2 * BAND), 0)
    col = lax.broadcasted_iota(jnp.int32, (BAND, 2 * BAND), 1)
    halves = [_lanes_in((BAND, LANES), sub * HEAD_DIM, HEAD_DIM) for sub in range(LANES // HEAD_DIM)]

    def starts(n):
        return pl.multiple_of(n * BAND, BAND), pl.multiple_of(jnp.maximum(n - 1, 0) * BAND, BAND)

    def scores(n, slot):
        q0, k0 = starts(n)
        q = q_ref[0, pl.ds(q0, BAND), :].astype(F32)
        kw = k_ref[0, pl.ds(k0, 2 * BAND), :]
        dist = (row - col) + (q0 - k0)
        mask = (dist >= 0) & (dist <= BAND)
        for h in range(N_HEADS):
            qh = jnp.where(_lanes_in((BAND, GW), h * HEAD_DIM, HEAD_DIM), q, 0.0).astype(BF16)
            s = lax.dot_general(qh, kw, NT_DIMS, preferred_element_type=F32)
            s_ref[slot * N_HEADS + h] = jnp.where(mask, s, -jnp.inf)

    def values(n, slot):
        q0, k0 = starts(n)
        for tile in range(GW // LANES):
            vw = v_ref[0, pl.ds(k0, 2 * BAND), tile * LANES:(tile + 1) * LANES]
            o = jnp.zeros((BAND, LANES), F32)
            lse = jnp.zeros((BAND, LANES), F32)
            for sub in range(LANES // HEAD_DIM):
                s = s_ref[slot * N_HEADS + tile * (LANES // HEAD_DIM) + sub]
                m = jnp.max(s, axis=-1, keepdims=True)
                p = jnp.exp(s - m)
                l = jnp.sum(p, axis=-1, keepdims=True)
                acc = jnp.dot(p.astype(BF16), vw, preferred_element_type=F32)
                mine = halves[sub]
                o = jnp.where(mine, acc / l, o)
                lse = jnp.where(mine, m + jnp.log(l), lse)
            o_ref[0, pl.ds(q0, BAND), tile * LANES:(tile + 1) * LANES] = o.astype(o_ref.dtype)
            lse_ref[0, pl.ds(q0, BAND), tile * LANES:(tile + 1) * LANES] = lse

    def pair(t, carry):
        scores(2 * t, 0)
        scores(2 * t + 1, 1)
        values(2 * t, 0)
        values(2 * t + 1, 1)
        return carry

    lax.fori_loop(0, q_ref.shape[1] // (2 * BAND), pair, 0)


def _band_attn(qkv, dil):
    b, l, _ = qkv.shape
    assert l % (2 * BAND) == 0
    part = lambda c: pl.BlockSpec((1, l, GW), lambda i, r: (i, 0, 3 * r + c))
    out = pl.BlockSpec((1, l, GW), lambda i, r: (i, 0, r))
    return pl.pallas_call(
        _band_attn_kernel,
        grid=(b, dil),
        in_specs=[part(0), part(1), part(2)],
        out_specs=[out, out],
        out_shape=[jax.ShapeDtypeStruct((b, l, dil * GW), BF16), jax.ShapeDtypeStruct((b, l, dil * GW), F32)],
        scratch_shapes=[pltpu.VMEM((2 * N_HEADS, BAND, 2 * BAND), F32)],
        compiler_params=_params("arbitrary", "arbitrary"),
        name="band_attn",
    )(qkv, qkv, qkv)


def _diff_attn_kernel(lam_ref, g_ref, qt_ref, k_ref, vt_ref, o_ref, qm_ref, m_ref, acc_ref, yt_ref,
                      st_ref, mx_ref, *, tq, lam_init):
    n_chain = 2 * N_HEADS
    i = pl.program_id(1)
    @pl.when((pl.program_id(0) == 0) & (i == 0))
    def _():
        qm_ref[...] = jnp.zeros(qm_ref.shape, BF16)

    for c in range(n_chain):
        qm_ref[c, c * QK_DIM:(c + 1) * QK_DIM, :] = qt_ref[0, c * QK_DIM:(c + 1) * QK_DIM, :]
    m_ref[...] = jnp.full(m_ref.shape, -jnp.inf, F32)
    acc_ref[...] = jnp.zeros(acc_ref.shape, F32)
    tk = st_ref.shape[1]
    ratio = tq // tk
    ahead = (lax.broadcasted_iota(jnp.int32, (tk, tq), 0)
             - lax.broadcasted_iota(jnp.int32, (tk, tq), 1))

    def first_query(diag):
        return 0 if diag is None else diag * tk

    def score(c, kb, diag):
        lo = first_query(diag)
        st = jnp.dot(kb, qm_ref[c, :, lo:], preferred_element_type=F32)
        if diag is not None:
            st = jnp.where(ahead[:, lo:] <= -diag * tk, st, -jnp.inf)
        st_ref[c, :, lo:] = st
        mx_ref[c, :, lo:] = jnp.max(st, axis=0, keepdims=True)

    def value(c, j, diag=None):
        h = c % N_HEADS
        lo = first_query(diag)
        m_prev = m_ref[c, :, lo:]
        m_next = jnp.maximum(m_prev, mx_ref[c, :, lo:])
        alpha = jnp.exp2(m_prev - m_next)
        p = jnp.exp2(st_ref[c, :, lo:] - m_next).astype(BF16)
        vt = vt_ref[0, j, h * V_ROWS:(h + 1) * V_ROWS, :]
        acc_ref[c, :, lo:] = alpha * acc_ref[c, :, lo:] + jnp.dot(vt, p, preferred_element_type=F32)
        m_ref[c, :, lo:] = m_next

    def k_block(j):
        return k_ref[0, pl.ds(pl.multiple_of(j * tk, tk), tk), :]

    def scores(j, diag):
        kb = k_block(j)
        for c in range(n_chain):
            score(c, kb, diag)

    def values(j, diag):
        for c in range(n_chain):
            value(c, j, diag)

    def values_then_scores(j, diag):
        kb = k_block(j + 1)
        for c in range(n_chain):
            value(c, j, diag - 1 if diag else None)
            score(c, kb, diag)

    before = ratio * i

    @pl.when(i == 0)
    def _():
        scores(0, 0)

    @pl.when(i > 0)
    def _():
        scores(0, None)

    def body(t, carry):
        values_then_scores(2 * t, None)
        values_then_scores(2 * t + 1, None)
        return carry

    plain = jnp.maximum(before - 1, 0)
    lax.fori_loop(0, plain // 2, body, 0)

    @pl.when(plain % 2 == 1)
    def _():
        values_then_scores(plain - 1, None)

    @pl.when(i > 0)
    def _():
        values_then_scores(before - 1, 0)

    for diag in range(1, ratio):
        values_then_scores(before + diag - 1, diag)
    values(before + ratio - 1, ratio - 1)

    lv = lam_ref[...]
    lam = (jnp.exp(jnp.sum(lv[0:1] * lv[1:2], axis=-1, keepdims=True))
           - jnp.exp(jnp.sum(lv[2:3] * lv[3:4], axis=-1, keepdims=True)) + lam_init)
    for h in range(N_HEADS):
        a1, a2 = acc_ref[h], acc_ref[N_HEADS + h]
        oh = (a1[:HEAD_DIM] / a1[HEAD_DIM:HEAD_DIM + 1]
              - lam * (a2[:HEAD_DIM] / a2[HEAD_DIM:HEAD_DIM + 1]))
        ms = jnp.mean(oh * oh, axis=0, keepdims=True)
        g = g_ref[h * HEAD_DIM:(h + 1) * HEAD_DIM, :]
        yt_ref[h * HEAD_DIM:(h + 1) * HEAD_DIM, :] = oh * lax.rsqrt(ms + NORM_EPS) * (g * (1.0 - lam_init))
    o_ref[0] = jnp.transpose(yt_ref[...]).astype(BF16)


def _diff_attn(qt, k, vt, lam_vecs, g_col, lam_init, tq):
    b, s, _ = k.shape
    nk, tk = vt.shape[1], vt.shape[3]
    assert tq % tk == 0 and s % tq == 0
    n_chain = 2 * N_HEADS
    return pl.pallas_call(
        functools.partial(_diff_attn_kernel, tq=tq, lam_init=lam_init),
        grid=(b, s // tq),
        in_specs=[
            pl.BlockSpec((4, QK_DIM), lambda i, j: (0, 0)),
            pl.BlockSpec((GW, 1), lambda i, j: (0, 0)),
            pl.BlockSpec((1, GW, tq), lambda i, j: (i, 0, j)),
            pl.BlockSpec((1, s, GW), lambda i, j: (i, 0, 0)),
            pl.BlockSpec((1, nk, N_HEADS * V_ROWS, tk), lambda i, j: (i, 0, 0, 0)),
        ],
        out_specs=pl.BlockSpec((1, tq, GW), lambda i, j: (i, j, 0)),
        out_shape=jax.ShapeDtypeStruct((b, s, GW), BF16),
        scratch_shapes=[
            pltpu.VMEM((n_chain, GW, tq), BF16),
            pltpu.VMEM((n_chain, 1, tq), F32),
            pltpu.VMEM((n_chain, V_ROWS, tq), F32),
            pltpu.VMEM((GW, tq), F32),
            pltpu.VMEM((n_chain, tk, tq), F32),
            pltpu.VMEM((n_chain, 1, tq), F32),
        ],
        compiler_params=_params("arbitrary", "arbitrary"),
        name="diff_attn",
    )(lam_vecs, g_col, qt, k, vt)


def _out_proj_kernel(x_ref, yab_ref, o1_ref, o2_ref, o3_ref, l1_ref, l2_ref, l3_ref, yd_ref, w_ref, out_ref,
                     nat_ref):
    tm = x_ref.shape[0]

    def natural(ref, dil, slot):
        for r in range(dil):
            for half in range(GW // LANES):
                col = r * GW + half * LANES
                nat_ref[2 * slot + half, pl.ds(r, tm // dil, stride=dil), :] = ref[:, col:col + LANES].astype(F32)
        return jnp.concatenate([nat_ref[2 * slot], nat_ref[2 * slot + 1]], axis=1)

    d2, d3 = DILATIONS[1:]
    l1, l2, l3 = l1_ref[...], natural(l2_ref, d2, 0), natural(l3_ref, d3, 1)
    o2, o3 = natural(o2_ref, d2, 2), natural(o3_ref, d3, 3)
    m = jnp.maximum(jnp.maximum(l1, l2), l3)
    w1, w2, w3 = jnp.exp(l1 - m), jnp.exp(l2 - m), jnp.exp(l3 - m)
    yc = (w1 * o1_ref[...] + w2 * o2 + w3 * o3) / (w1 + w2 + w3)
    acc = jnp.dot(yab_ref[...], w_ref[0:2 * GW, :], preferred_element_type=F32)
    acc = acc + jnp.dot(yc.astype(BF16), w_ref[2 * GW:3 * GW, :], preferred_element_type=F32)
    acc = acc + jnp.dot(yd_ref[...], w_ref[3 * GW:4 * GW, :], preferred_element_type=F32)
    out_ref[...] = x_ref[...] + acc


def _out_proj(x, yab, os_, lses, yd, w, tm):
    t, d = x.shape
    rows = lambda c: pl.BlockSpec((tm, c), lambda i: (i, 0))
    branch = [pl.BlockSpec((tm // dil, dil * GW), lambda i: (i, 0)) for dil in DILATIONS]
    return pl.pallas_call(
        _out_proj_kernel,
        grid=(t // tm,),
        in_specs=[rows(d), rows(2 * GW)] + branch * 2 + [rows(GW), pl.BlockSpec((d, d), lambda i: (0, 0))],
        out_specs=rows(d),
        out_shape=jax.ShapeDtypeStruct((t, d), F32),
        scratch_shapes=[pltpu.VMEM((4 * GW // LANES, tm, LANES), F32)],
        compiler_params=_params("arbitrary"),
        name="out_proj",
    )(x, yab, *os_, *lses, yd, w)


def _ffn_kernel(x_ref, g_ref, wg_ref, wu_ref, wd_ref, out_ref, acc_ref, *, tf):
    hn = _rms(x_ref[...], g_ref[...]).astype(BF16)
    acc_ref[...] = x_ref[...]
    for f0 in range(0, wg_ref.shape[1], tf):
        gate = jnp.dot(hn, wg_ref[:, f0:f0 + tf], preferred_element_type=F32)
        up = jnp.dot(hn, wu_ref[:, f0:f0 + tf], preferred_element_type=F32)
        act = (gate * jax.nn.sigmoid(gate) * up).astype(BF16)
        acc_ref[...] += jnp.dot(act, wd_ref[f0:f0 + tf, :], preferred_element_type=F32)
    out_ref[...] = acc_ref[...]


def _ffn(x, g, wg, wu, wd, tm, tf):
    t, d = x.shape
    ff = wg.shape[1]
    assert ff % tf == 0
    whole = lambda shape: pl.BlockSpec(shape, lambda i: (0, 0))
    return pl.pallas_call(
        functools.partial(_ffn_kernel, tf=tf),
        grid=(t // tm,),
        in_specs=[pl.BlockSpec((tm, d), lambda i: (i, 0)), whole((1, d)), whole((d, ff)), whole((d, ff)),
                  whole((ff, d))],
        out_specs=pl.BlockSpec((tm, d), lambda i: (i, 0)),
        out_shape=jax.ShapeDtypeStruct((t, d), F32),
        scratch_shapes=[pltpu.VMEM((tm, d), F32)],
        compiler_params=_params("arbitrary"),
        name="ffn_dense",
    )(x, g, wg, wu, wd)


META_E1, META_E2, META_RANK1, META_RANK2, META_G1, META_G2 = range(6)


def _router_kernel(x_ref, g_ref, rhi_ref, rlo_ref, hn_ref, meta_ref, after_ref, carry_ref, *, blocks_per_chunk):
    @pl.when(pl.program_id(0) % blocks_per_chunk == 0)
    def _():
        carry_ref[...] = jnp.zeros_like(carry_ref)

    h = _rms(x_ref[...], g_ref[...])
    hn_ref[...] = h.astype(BF16)
    h_hi = h.astype(BF16)
    h_lo = (h - h_hi.astype(F32)).astype(BF16)
    nt = lambda a, b: lax.dot_general(a, b, NT_DIMS, preferred_element_type=F32)
    logits = nt(rhi_ref[...], h_hi) + nt(rlo_ref[...], h_hi) + nt(rhi_ref[...], h_lo)
    tm = logits.shape[1]
    expert = lax.broadcasted_iota(jnp.int32, logits.shape, 0)
    m1 = jnp.max(logits, axis=0, keepdims=True)
    i1 = jnp.min(jnp.where(logits == m1, expert, N_EXPERTS), axis=0, keepdims=True)
    rest = jnp.where(expert == i1, -jnp.inf, logits)
    m2 = jnp.max(rest, axis=0, keepdims=True)
    i2 = jnp.min(jnp.where(rest == m2, expert, N_EXPERTS), axis=0, keepdims=True)
    e2 = jnp.exp(m2 - m1)
    g1 = 1.0 / (1.0 + e2)
    g2 = e2 / (1.0 + e2)

    oh1 = expert == i1
    oh2 = expert == i2
    both = jnp.where(oh1, 1.0, jnp.where(oh2, 1.0, 0.0))
    row = lax.broadcasted_iota(jnp.int32, (tm, tm), 0)
    col = lax.broadcasted_iota(jnp.int32, (tm, tm), 1)
    earlier = jnp.where(row < col, 1.0, 0.0).astype(BF16)
    carry = carry_ref[:, 0:1]
    prefix = jnp.dot(both.astype(BF16), earlier, preferred_element_type=F32) + carry
    rank1 = jnp.sum(jnp.where(oh1, prefix, 0.0), axis=0, keepdims=True)
    rank2 = jnp.sum(jnp.where(oh2, prefix, 0.0), axis=0, keepdims=True)
    total = carry + jnp.sum(both, axis=1, keepdims=True)

    meta = jnp.zeros(logits.shape, F32)
    for k, val in ((META_E1, i1.astype(F32)), (META_E2, i2.astype(F32)), (META_RANK1, rank1),
                   (META_RANK2, rank2), (META_G1, g1), (META_G2, g2)):
        meta = jnp.where(expert == k, val, meta)
    meta_ref[...] = meta
    after_ref[0] = jnp.broadcast_to(total, after_ref.shape[1:])
    carry_ref[...] = jnp.broadcast_to(total, carry_ref.shape)


def _router(x, g, router_w, tm, blocks_per_chunk):
    t, d = x.shape
    nblk = t // tm
    assert router_w.shape == (d, N_EXPERTS) and N_EXPERTS == SUBLANES
    r_t = router_w.T
    r_hi = r_t.astype(BF16)
    counts = pl.BlockSpec((1, N_EXPERTS, LANES), lambda i: (i, 0, 0))
    return pl.pallas_call(
        functools.partial(_router_kernel, blocks_per_chunk=blocks_per_chunk),
        grid=(nblk,),
        in_specs=[
            pl.BlockSpec((tm, d), lambda i: (i, 0)),
            pl.BlockSpec((1, d), lambda i: (0, 0)),
            pl.BlockSpec((N_EXPERTS, d), lambda i: (0, 0)),
            pl.BlockSpec((N_EXPERTS, d), lambda i: (0, 0)),
        ],
        out_specs=[
            pl.BlockSpec((tm, d), lambda i: (i, 0)),
            pl.BlockSpec((N_EXPERTS, tm), lambda i: (0, i)),
            counts,
        ],
        out_shape=[
            jax.ShapeDtypeStruct((t, d), BF16),
            jax.ShapeDtypeStruct((N_EXPERTS, t), F32),
            jax.ShapeDtypeStruct((nblk, N_EXPERTS, LANES), F32),
        ],
        scratch_shapes=[pltpu.VMEM((N_EXPERTS, LANES), F32)],
        compiler_params=_params("arbitrary"),
        name="router",
    )(x, g, r_hi, (r_t - r_hi.astype(F32)).astype(BF16))


SLOT_ROW1, SLOT_ROW2, SLOT_G1, SLOT_G2 = range(4)


def _moe_kernel(nt_ref, toff_ref, cuts_ref, hn_ref, slotc_ref, slotr_ref, wg_ref, wu_ref, wd_ref, out_ref,
                xs_ref, ys_ref, *, tr, sub, win):
    c, e, f = pl.program_id(0), pl.program_id(1), pl.program_id(2)
    last_f = pl.num_programs(2) - 1
    grp = c * N_EXPERTS + e
    n_tiles = nt_ref[grp]
    row0 = toff_ref[grp] * tr
    tc = hn_ref.shape[0]
    nsub = tc // sub

    @pl.when((e == 0) & (f == 0))
    def _():
        out_ref[...] = jnp.zeros_like(out_ref)

    def rows_of(r):
        return pl.ds(pl.multiple_of(r * tr, tr), tr)

    def token_window(r):
        base = grp * nsub
        first = r * tr
        last = jnp.minimum(first + tr, cuts_ref[base + nsub - 1]) - 1
        lo = hi = 0
        for k in range(nsub - 1):
            cut = cuts_ref[base + k]
            lo = lo + (cut <= first).astype(jnp.int32)
            hi = hi + (cut <= last).astype(jnp.int32)
        start = jnp.minimum(lo, nsub - win)
        return hi < start + win, pl.multiple_of(start * sub, sub)

    def spans(r, fn):
        covered, tok0 = token_window(r)

        @pl.when(covered)
        def _():
            fn(r, tok0, win * sub)

        @pl.when(jnp.logical_not(covered))
        def _():
            fn(r, 0, tc)

    def gather_span(r, tok0, ntok):
        toks = pl.ds(tok0, ntok)
        target = (lax.broadcasted_iota(jnp.int32, (tr, ntok), 0) + (row0 + r * tr)).astype(F32)
        hit1 = slotr_ref[SLOT_ROW1:SLOT_ROW1 + 1, toks] == target
        hit2 = slotr_ref[SLOT_ROW2:SLOT_ROW2 + 1, toks] == target
        oh = jnp.where(hit1, 1.0, jnp.where(hit2, 1.0, 0.0)).astype(BF16)
        xs_ref[rows_of(r), :] = jnp.dot(oh, hn_ref[toks, :], preferred_element_type=F32).astype(BF16)

    def gather_tile(r, carry):
        spans(r, gather_span)
        return carry

    def ffn_tiles(r, count, first):
        rows = pl.ds(pl.multiple_of(r * tr, tr), count * tr)
        xt = xs_ref[rows, :]
        for f0 in range(0, wg_ref.shape[1], FFN_SUBTILE):
            gate = jnp.dot(xt, wg_ref[:, f0:f0 + FFN_SUBTILE], preferred_element_type=F32)
            up = jnp.dot(xt, wu_ref[:, f0:f0 + FFN_SUBTILE], preferred_element_type=F32)
            act = (gate * jax.nn.sigmoid(gate) * up).astype(BF16)
            y = jnp.dot(act, wd_ref[f0:f0 + FFN_SUBTILE, :], preferred_element_type=F32)
            if first and f0 == 0:
                ys_ref[rows, :] = y
            else:
                ys_ref[rows, :] += y

    def scatter_span(r, tok0, ntok):
        toks = pl.ds(tok0, ntok)
        yb = ys_ref[rows_of(r), :].astype(BF16)
        target = (lax.broadcasted_iota(jnp.int32, (ntok, tr), 1) + (row0 + r * tr)).astype(F32)
        hit1 = slotc_ref[toks, SLOT_ROW1:SLOT_ROW1 + 1] == target
        hit2 = slotc_ref[toks, SLOT_ROW2:SLOT_ROW2 + 1] == target
        gh = jnp.where(hit1, slotc_ref[toks, SLOT_G1:SLOT_G1 + 1],
                       jnp.where(hit2, slotc_ref[toks, SLOT_G2:SLOT_G2 + 1], 0.0)).astype(BF16)
        out_ref[toks, :] += jnp.dot(gh, yb, preferred_element_type=F32)

    def scatter_tile(r, carry):
        spans(r, scatter_span)
        return carry

    def ffn_all(first):
        def pair(t, carry):
            ffn_tiles(2 * t, 2, first)
            return carry

        lax.fori_loop(0, n_tiles // 2, pair, 0)

        @pl.when(n_tiles % 2 == 1)
        def _():
            ffn_tiles(n_tiles - 1, 1, first)

    @pl.when(f == 0)
    def _():
        lax.fori_loop(0, n_tiles, gather_tile, 0)
        ffn_all(True)

    @pl.when(f > 0)
    def _():
        ffn_all(False)

    @pl.when(f == last_f)
    def _():
        lax.fori_loop(0, n_tiles, scatter_tile, 0)


def _moe(hn, slot_cols, slot_rows, nt, toff, cuts, wg, wu, wd, tc, tr, tf, sub, win):
    t, d = hn.shape
    ne, _, ff = wg.shape
    grid_spec = pltpu.PrefetchScalarGridSpec(
        num_scalar_prefetch=3,
        grid=(t // tc, ne, ff // tf),
        in_specs=[
            pl.BlockSpec((tc, d), lambda c, e, f, *_: (c, 0), pipeline_mode=pl.Buffered(1)),
            pl.BlockSpec((tc, 4), lambda c, e, f, *_: (c, 0), pipeline_mode=pl.Buffered(1)),
            pl.BlockSpec((8, tc), lambda c, e, f, *_: (0, c), pipeline_mode=pl.Buffered(1)),
            pl.BlockSpec((None, d, tf), lambda c, e, f, *_: (e, 0, f)),
            pl.BlockSpec((None, d, tf), lambda c, e, f, *_: (e, 0, f)),
            pl.BlockSpec((None, tf, d), lambda c, e, f, *_: (e, f, 0)),
        ],
        out_specs=pl.BlockSpec((tc, d), lambda c, e, f, *_: (c, 0), pipeline_mode=pl.Buffered(1)),
        scratch_shapes=[
            pltpu.VMEM((tc + tr, d), BF16),
            pltpu.VMEM((tc + tr, d), F32),
        ],
    )
    return pl.pallas_call(
        functools.partial(_moe_kernel, tr=tr, sub=sub, win=win),
        grid_spec=grid_spec,
        out_shape=jax.ShapeDtypeStruct((t, d), F32),
        compiler_params=_params("arbitrary", "arbitrary", "arbitrary"),
        name="moe",
    )(nt, toff, cuts, hn, slot_cols, slot_rows, wg, wu, wd)


def _final_kernel(x_ref, y_ref, g_ref, out_ref):
    out_ref[...] = _rms(x_ref[...] + y_ref[...], g_ref[...])


def _final_norm(x, y, g, tm):
    t, d = x.shape
    rows = pl.BlockSpec((tm, d), lambda i: (i, 0))
    return pl.pallas_call(
        _final_kernel,
        grid=(t // tm,),
        in_specs=[rows, rows, pl.BlockSpec((1, d), lambda i: (0, 0))],
        out_specs=rows,
        out_shape=jax.ShapeDtypeStruct((t, d), F32),
        compiler_params=_params("arbitrary"),
        name="final_norm",
    )(x, y, g)


def _routed_swiglu(x, norm_g, router_w, wg, wu, wd, final_g, tc, tr, tf, sub, win):
    t, d = x.shape
    nc, nsub = t // tc, tc // sub
    hn, meta, after = _router(x, norm_g, router_w, tm=sub, blocks_per_chunk=nsub)
    through = after[:, :, 0].astype(jnp.int32).reshape(nc, nsub, N_EXPERTS)
    cuts = through.transpose(0, 2, 1)
    total = through[:, -1]
    nt = (total + tr - 1) // tr
    toff = jnp.cumsum(nt, axis=1) - nt
    first_row = jnp.repeat((toff * tr).astype(F32), tc, axis=0)
    experts = jnp.arange(N_EXPERTS, dtype=F32)
    pick = lambda e: jnp.sum(jnp.where(e[:, None] == experts, first_row, 0.0), axis=1)
    row1 = pick(meta[META_E1]) + meta[META_RANK1]
    row2 = pick(meta[META_E2]) + meta[META_RANK2]
    slot_rows = jnp.zeros((8, t), F32).at[0:4].set(jnp.stack([row1, row2, meta[META_G1], meta[META_G2]]))
    slot_cols = slot_rows[0:4].T
    y = _moe(hn, slot_cols, slot_rows, nt.reshape(-1), toff.reshape(-1), cuts.reshape(-1), wg, wu, wd,
             tc=tc, tr=tr, tf=tf, sub=sub, win=win)
    return _final_norm(x, y, final_g, tm=min(1024, t))


def _permute_w_in(w):
    def perm(block):
        d = block.shape[0]
        return block.reshape(d, N_HEADS, 2, QK_DIM).transpose(0, 2, 1, 3).reshape(d, GW)
    parts = [w[:, :6 * GW], perm(w[:, 6 * GW:7 * GW]), perm(w[:, 7 * GW:8 * GW]), w[:, 8 * GW:]]
    return jnp.concatenate(parts, axis=1)


def _block_diag(pool_w):
    n, p, _ = pool_w.shape
    out = jnp.zeros((n * p, n * p), pool_w.dtype)
    for gi in range(n):
        out = out.at[gi * p:(gi + 1) * p, gi * p:(gi + 1) * p].set(pool_w[gi])
    return out


def _token_mixing(x, layer, lam_init, p, rope_c, rope_d):
    b, s, d = x.shape
    row = lambda v: v.reshape(1, -1)
    w_in = _permute_w_in(p["w_in"][layer]).astype(BF16)
    uab, qkvc, qkvc4, qkvc16, qdt, kd, vdt = _norm_proj(x, row(p["norm1_g"][layer]), w_in, rope_c, rope_d,
                                                        tm=512, tk=256)
    dww = jnp.zeros((HALO, GW), F32).at[:CONV_WIDTH].set(p["conv_dw_w"][layer])
    yab = _local_mixers(
        uab, dww, row(p["conv_dw_b"][layer]), row(p["conv_ln_g"][layer]), row(p["conv_ln_b"][layer]),
        p["conv_pw_w"][layer].astype(BF16), _block_diag(p["pool_w"][layer]).astype(BF16),
        row(p["pool_scale"][layer]), ts=512)
    os_, lses = [], []
    for dil, qkv in zip(DILATIONS, (qkvc, qkvc4, qkvc16)):
        o, lse = _band_attn(qkv, dil)
        os_.append(o.reshape(b * s // dil, dil * GW))
        lses.append(lse.reshape(b * s // dil, dil * GW))
    yd = _diff_attn(qdt, kd, vdt, p["diff_lam"][layer], jnp.tile(p["diff_ln_g"][layer], N_HEADS).reshape(GW, 1),
                    lam_init, tq=512)
    return _out_proj(x.reshape(b * s, d), yab.reshape(b * s, 2 * GW), os_, lses, yd.reshape(b * s, GW),
                     p["w_out"][layer].astype(BF16), tm=512)


def kernel(x, norm1_g, w_in, conv_dw_w, conv_dw_b, conv_ln_g, conv_ln_b, conv_pw_w, pool_w, pool_scale,
           diff_lam, diff_ln_g, w_out, norm2_g, ffn_w_gate, ffn_w_up, ffn_w_down, moe_router, moe_w_gate,
           moe_w_up, moe_w_down, final_g):
    b, s, d = x.shape
    p = dict(norm1_g=norm1_g, w_in=w_in, conv_dw_w=conv_dw_w, conv_dw_b=conv_dw_b, conv_ln_g=conv_ln_g,
             conv_ln_b=conv_ln_b, conv_pw_w=conv_pw_w, pool_w=pool_w, pool_scale=pool_scale,
             diff_lam=diff_lam, diff_ln_g=diff_ln_g, w_out=w_out)
    rope_c = _rope_table(s, HEAD_DIM, HEAD_DIM // 4)
    rope_d = _rope_table(s, QK_DIM, QK_DIM // 4)
    row = lambda v: v.reshape(1, -1)

    x1 = _token_mixing(x, 0, 0.8 - 0.6 * math.exp(-0.3 * 0), p, rope_c, rope_d)
    x2 = _ffn(x1, row(norm2_g[0]), ffn_w_gate[0].astype(BF16), ffn_w_up[0].astype(BF16),
              ffn_w_down[0].astype(BF16), tm=512, tf=256)

    x3 = _token_mixing(x2.reshape(b, s, d), 1, 0.8 - 0.6 * math.exp(-0.3 * 1), p, rope_c, rope_d)
    out = _routed_swiglu(x3, row(norm2_g[1]), moe_router[0], moe_w_gate[0].astype(BF16),
                         moe_w_up[0].astype(BF16), moe_w_down[0].astype(BF16), row(final_g),
                         tc=2048, tr=256, tf=1792, sub=256, win=5)
    return out.reshape(b, s, d)
```

```python
import functools
import math

import jax
import jax.numpy as jnp
from jax import lax
from jax.experimental import pallas as pl
from jax.experimental.pallas import tpu as pltpu

D_MODEL = 1024
GW = 256
N_HEADS = 4
HEAD_DIM = 64
QK_DIM = 32
V_ROWS = HEAD_DIM + 16
CONV_WIDTH = 31
POOL_WINDOWS = (2, 4, 8, 16)
DILATIONS = (1, 4, 16)
BAND = 128
ROPE_THETA = 500000.0
N_EXPERTS = 8
NORM_EPS = 1e-5
HALO = 32
LANES = 128
SUBLANES = 8
FFN_SUBTILE = 256
VMEM_LIMIT = 56 * 1024 * 1024

BF16 = jnp.bfloat16
F32 = jnp.float32
NT_DIMS = (((1,), (1,)), ((), ()))


def _params(*sem):
    return pltpu.CompilerParams(dimension_semantics=sem, vmem_limit_bytes=VMEM_LIMIT)


def _lanes_in(shape, start, width):
    lane = lax.broadcasted_iota(jnp.int32, shape, len(shape) - 1)
    return (lane >= start) & (lane < start + width)


def _rms(x, g):
    ms = jnp.mean(x * x, axis=-1, keepdims=True)
    return x * lax.rsqrt(ms + NORM_EPS) * g


def _rope_table(seq, width, rot):
    half = rot // 2
    pos = jnp.arange(seq, dtype=F32)
    inv = ROPE_THETA ** (-jnp.arange(0, rot, 2, dtype=F32) / rot)
    ang = pos[:, None] * inv[None, :]
    cos, sin = jnp.cos(ang), jnp.sin(ang)
    lane = jnp.arange(GW) % width
    f = lane % half
    first = lane < half
    second = (lane >= half) & (lane < rot)
    c = jnp.where((lane < rot)[None, :], cos[:, f], 1.0)
    sn = jnp.where(first[None, :], -sin[:, f], 0.0)
    sp = jnp.where(second[None, :], sin[:, f], 0.0)
    return jnp.concatenate([c, sn, sp], axis=1)


def _rope(u, tab_ref, half):
    return (u * tab_ref[:, 0:GW]
            + pltpu.roll(u, GW - half, 1) * tab_ref[:, GW:2 * GW]
            + pltpu.roll(u, half, 1) * tab_ref[:, 2 * GW:3 * GW])


def _norm_proj_kernel(x_ref, g_ref, w_ref, rc_ref, rd_ref, uab_ref, qkvc_ref, qkv4_ref, qkv16_ref, qt_ref, kd_ref,
                      vt_ref, v_ref, cs_ref, *, tk):
    h = _rms(x_ref[0], g_ref[...]).astype(BF16)

    def proj(grp):
        return jnp.dot(h, w_ref[:, grp * GW:(grp + 1) * GW], preferred_element_type=F32)

    for grp in range(3):
        uab_ref[0, :, grp * GW:(grp + 1) * GW] = proj(grp)
    c_scale = HEAD_DIM ** -0.5
    d_scale = QK_DIM ** -0.5 * math.log2(math.e)
    parts = (_rope(proj(3), rc_ref, 8) * c_scale, _rope(proj(4), rc_ref, 8), proj(5))
    for k, val in enumerate(parts):
        qkvc_ref[0, :, k * GW:(k + 1) * GW] = val.astype(BF16)
        for half in range(GW // LANES):
            cs_ref[2 * k + half] = val[:, half * LANES:(half + 1) * LANES]
    tm = cs_ref.shape[1]
    for dil, out_ref in zip(DILATIONS[1:], (qkv4_ref, qkv16_ref)):
        for r in range(dil):
            for slab in range(cs_ref.shape[0]):
                rows = cs_ref[slab, pl.ds(r, tm // dil, stride=dil), :]
                col = r * 3 * GW + slab * LANES
                out_ref[0, :, col:col + LANES] = rows.astype(BF16)
    qt_ref[0] = jnp.transpose(_rope(proj(6), rd_ref, 4) * d_scale).astype(BF16)
    kd_ref[0] = _rope(proj(7), rd_ref, 4).astype(BF16)
    v_ref[...] = proj(8)
    vt = jnp.transpose(v_ref[...]).astype(BF16)
    ones = jnp.ones((V_ROWS - HEAD_DIM, tk), BF16)
    for blk in range(vt.shape[1] // tk):
        for hd in range(N_HEADS):
            vt_ref[0, blk, hd * V_ROWS:hd * V_ROWS + HEAD_DIM, :] = (
                vt[hd * HEAD_DIM:(hd + 1) * HEAD_DIM, blk * tk:(blk + 1) * tk])
            vt_ref[0, blk, hd * V_ROWS + HEAD_DIM:(hd + 1) * V_ROWS, :] = ones


def _norm_proj(x, g, w, rope_c, rope_d, tm, tk):
    b, s, d = x.shape
    grid = (s // tm, b)
    return pl.pallas_call(
        functools.partial(_norm_proj_kernel, tk=tk),
        grid=grid,
        in_specs=[
            pl.BlockSpec((1, tm, d), lambda i, j: (j, i, 0)),
            pl.BlockSpec((1, d), lambda i, j: (0, 0)),
            pl.BlockSpec((d, 9 * GW), lambda i, j: (0, 0)),
            pl.BlockSpec((tm, 3 * GW), lambda i, j: (i, 0)),
            pl.BlockSpec((tm, 3 * GW), lambda i, j: (i, 0)),
        ],
        out_specs=[
            pl.BlockSpec((1, tm, 3 * GW), lambda i, j: (j, i, 0)),
            pl.BlockSpec((1, tm, 3 * GW), lambda i, j: (j, i, 0)),
            *[pl.BlockSpec((1, tm // dil, dil * 3 * GW), lambda i, j: (j, i, 0)) for dil in DILATIONS[1:]],
            pl.BlockSpec((1, GW, tm), lambda i, j: (j, 0, i)),
            pl.BlockSpec((1, tm, GW), lambda i, j: (j, i, 0)),
            pl.BlockSpec((1, tm // tk, N_HEADS * V_ROWS, tk), lambda i, j: (j, i, 0, 0)),
        ],
        out_shape=[
            jax.ShapeDtypeStruct((b, s, 3 * GW), F32),
            jax.ShapeDtypeStruct((b, s, 3 * GW), BF16),
            *[jax.ShapeDtypeStruct((b, s // dil, dil * 3 * GW), BF16) for dil in DILATIONS[1:]],
            jax.ShapeDtypeStruct((b, GW, s), BF16),
            jax.ShapeDtypeStruct((b, s, GW), BF16),
            jax.ShapeDtypeStruct((b, s // tk, N_HEADS * V_ROWS, tk), BF16),
        ],
        scratch_shapes=[
            pltpu.VMEM((tm, GW), F32),
            pltpu.VMEM((3 * GW // LANES, tm, LANES), F32),
        ],
        compiler_params=_params("arbitrary", "arbitrary"),
        name="norm_proj",
    )(x, g, w, rope_c, rope_d)


def _local_kernel(u_ref, dww_ref, dwb_ref, lng_ref, lnb_ref, pww_ref, plw_ref, pls_ref,
                  y_ref, zbuf, pbuf, shift_ref, *, ts):
    si = pl.program_id(1)

    @pl.when(si == 0)
    def _():
        zbuf[0:HALO, :] = jnp.zeros((HALO, GW), F32)
        pbuf[0:HALO, :] = jnp.zeros((HALO, GW), F32)

    a = u_ref[0, :, 0:GW]
    b = u_ref[0, :, GW:2 * GW]
    zbuf[HALO:HALO + ts, :] = a * jax.nn.sigmoid(b)
    pbuf[HALO:HALO + ts, :] = u_ref[0, :, 2 * GW:3 * GW]

    def by_phase(buf, shifted, offsets, term):
        total = None
        for phase in range(SUBLANES):
            offs = [o for o in offsets if o % SUBLANES == phase]
            if not offs:
                continue
            src = buf
            if phase:
                n = max(offs) - phase + ts
                shifted[phase, 0:n, :] = buf[phase:phase + n, :]
                src = shifted.at[phase]
            for o in offs:
                t = term(o, src[o - phase:o - phase + ts, :])
                total = t if total is None else total + t
        return total

    base = HALO - (CONV_WIDTH - 1)
    acc = by_phase(zbuf, shift_ref.at[0], range(base, base + CONV_WIDTH),
                   lambda o, rows: dww_ref[o - base:o - base + 1, :] * rows)
    z = acc + dwb_ref[...]
    mu = jnp.mean(z, axis=-1, keepdims=True)
    zc = z - mu
    var = jnp.mean(zc * zc, axis=-1, keepdims=True)
    zn = zc * lax.rsqrt(var + NORM_EPS) * lng_ref[...] + lnb_ref[...]
    zs = zn * jax.nn.sigmoid(zn)
    y_ref[0, :, 0:GW] = jnp.dot(zs.astype(BF16), pww_ref[...], preferred_element_type=F32).astype(BF16)

    lane = lax.broadcasted_iota(jnp.int32, (ts, GW), 1)
    pool_group = GW // len(POOL_WINDOWS)
    win = jnp.zeros((ts, GW), jnp.int32)
    for gi, w in enumerate(POOL_WINDOWS):
        win = jnp.where(lane >= gi * pool_group, w, win)
    p0 = pbuf[HALO:HALO + ts, :]
    n = HALO + ts
    sums = {1: pbuf}
    level, start = 1, 0
    while level < max(POOL_WINDOWS):
        prev = sums[level]
        start = -(-(start + level) // SUBLANES) * SUBLANES
        dst = shift_ref.at[1, len(sums) - 1]
        dst[start:n, :] = prev[start:n, :] + prev[start - level:n - level, :]
        level *= 2
        sums[level] = dst
    assert start <= HALO
    tot = sums[POOL_WINDOWS[0]][HALO:n, :]
    for gi, w in enumerate(POOL_WINDOWS[1:], start=1):
        tot = jnp.where(lane >= gi * pool_group, sums[w][HALO:n, :], tot)
    pos = si * ts + lax.broadcasted_iota(jnp.int32, (ts, GW), 0)
    cnt = jnp.minimum(pos + 1, win).astype(F32)
    pooled = tot / cnt - p0
    yb = jnp.dot(pooled.astype(BF16), plw_ref[...], preferred_element_type=F32) * pls_ref[...]
    y_ref[0, :, GW:2 * GW] = yb.astype(BF16)

    zbuf[0:HALO, :] = zbuf[ts:ts + HALO, :]
    pbuf[0:HALO, :] = pbuf[ts:ts + HALO, :]


def _local_mixers(uab, dww, dwb, lng, lnb, pww, plw, pls, ts):
    b, s, _ = uab.shape
    vec = lambda: pl.BlockSpec((1, GW), lambda i, j: (0, 0))
    mat = lambda: pl.BlockSpec((GW, GW), lambda i, j: (0, 0))
    return pl.pallas_call(
        functools.partial(_local_kernel, ts=ts),
        grid=(b, s // ts),
        in_specs=[
            pl.BlockSpec((1, ts, 3 * GW), lambda i, j: (i, j, 0)),
            pl.BlockSpec((HALO, GW), lambda i, j: (0, 0)),
            vec(), vec(), vec(), mat(), mat(), vec(),
        ],
        out_specs=pl.BlockSpec((1, ts, 2 * GW), lambda i, j: (i, j, 0)),
        out_shape=jax.ShapeDtypeStruct((b, s, 2 * GW), BF16),
        scratch_shapes=[
            pltpu.VMEM((HALO + ts, GW), F32),
            pltpu.VMEM((HALO + ts, GW), F32),
            pltpu.VMEM((2, SUBLANES, HALO + ts, GW), F32),
        ],
        compiler_params=_params("arbitrary", "arbitrary"),
        name="local_mixers",
    )(uab, dww, dwb, lng, lnb, pww, plw, pls)


def _band_attn_kernel(q_ref, k_ref, v_ref, o_ref, lse_ref, s_ref):
    row = lax.broadcasted_iota(jnp.int32, (BAND, 2 * BAND), 0)
    col = lax.broadcasted_iota(jnp.int32, (BAND, 2 * BAND), 1)
    halves = [_lanes_in((BAND, LANES), sub * HEAD_DIM, HEAD_DIM) for sub in range(LANES // HEAD_DIM)]

    def starts(n):
        return pl.multiple_of(n * BAND, BAND), pl.multiple_of(jnp.maximum(n - 1, 0) * BAND, BAND)

    def scores(n, slot):
        q0, k0 = starts(n)
        q = q_ref[0, pl.ds(q0, BAND), :].astype(F32)
        kw = k_ref[0, pl.ds(k0, 2 * BAND), :]
        dist = (row - col) + (q0 - k0)
        mask = (dist >= 0) & (dist <= BAND)
        for h in range(N_HEADS):
            qh = jnp.where(_lanes_in((BAND, GW), h * HEAD_DIM, HEAD_DIM), q, 0.0).astype(BF16)
            s = lax.dot_general(qh, kw, NT_DIMS, preferred_element_type=F32)
            s_ref[slot * N_HEADS + h] = jnp.where(mask, s, -jnp.inf)

    def values(n, slot):
        q0, k0 = starts(n)
        for tile in range(GW // LANES):
            vw = v_ref[0, pl.ds(k0, 2 * BAND), tile * LANES:(tile + 1) * LANES]
            o = jnp.zeros((BAND, LANES), F32)
            lse = jnp.zeros((BAND, LANES), F32)
            for sub in range(LANES // HEAD_DIM):
                s = s_ref[slot * N_HEADS + tile * (LANES // HEAD_DIM) + sub]
                m = jnp.max(s, axis=-1, keepdims=True)
                p = jnp.exp(s - m)
                l = jnp.sum(p, axis=-1, keepdims=True)
                acc = jnp.dot(p.astype(BF16), vw, preferred_element_type=F32)
                mine = halves[sub]
                o = jnp.where(mine, acc / l, o)
                lse = jnp.where(mine, m + jnp.log(l), lse)
            o_ref[0, pl.ds(q0, BAND), tile * LANES:(tile + 1) * LANES] = o.astype(o_ref.dtype)
            lse_ref[0, pl.ds(q0, BAND), tile * LANES:(tile + 1) * LANES] = lse

    group = s_ref.shape[0] // N_HEADS

    def step(t, carry):
        for slot in range(group):
            scores(group * t + slot, slot)
        for slot in range(group):
            values(group * t + slot, slot)
        return carry

    lax.fori_loop(0, q_ref.shape[1] // (group * BAND), step, 0)


def _band_attn(qkv, dil):
    b, l, _ = qkv.shape
    assert l % (2 * BAND) == 0
    group = 4 if l % (4 * BAND) == 0 else 2
    part = lambda c: pl.BlockSpec((1, l, GW), lambda i, r: (i, 0, 3 * r + c))
    out = pl.BlockSpec((1, l, GW), lambda i, r: (i, 0, r))
    return pl.pallas_call(
        _band_attn_kernel,
        grid=(b, dil),
        in_specs=[part(0), part(1), part(2)],
        out_specs=[out, out],
        out_shape=[jax.ShapeDtypeStruct((b, l, dil * GW), BF16), jax.ShapeDtypeStruct((b, l, dil * GW), F32)],
        scratch_shapes=[pltpu.VMEM((group * N_HEADS, BAND, 2 * BAND), F32)],
        compiler_params=_params("arbitrary", "arbitrary"),
        name="band_attn",
    )(qkv, qkv, qkv)


def _diff_attn_kernel(lam_ref, g_ref, qt_ref, k_ref, vt_ref, o_ref, qm_ref, m_ref, acc_ref, yt_ref,
                      st_ref, mx_ref, *, tq, lam_init):
    n_chain = 2 * N_HEADS
    i = pl.program_id(1)
    @pl.when((pl.program_id(0) == 0) & (i == 0))
    def _():
        qm_ref[...] = jnp.zeros(qm_ref.shape, BF16)

    for c in range(n_chain):
        qm_ref[c, c * QK_DIM:(c + 1) * QK_DIM, :] = qt_ref[0, c * QK_DIM:(c + 1) * QK_DIM, :]
    m_ref[...] = jnp.full(m_ref.shape, -jnp.inf, F32)
    acc_ref[...] = jnp.zeros(acc_ref.shape, F32)
    tk = st_ref.shape[1]
    ratio = tq // tk
    ahead = (lax.broadcasted_iota(jnp.int32, (tk, tq), 0)
             - lax.broadcasted_iota(jnp.int32, (tk, tq), 1))

    def first_query(diag):
        return 0 if diag is None else diag * tk

    def score(c, kb, diag):
        lo = first_query(diag)
        st = jnp.dot(kb, qm_ref[c, :, lo:], preferred_element_type=F32)
        if diag is not None:
            st = jnp.where(ahead[:, lo:] <= -diag * tk, st, -jnp.inf)
        st_ref[c, :, lo:] = st
        mx_ref[c, :, lo:] = jnp.max(st, axis=0, keepdims=True)

    def value(c, j, diag=None):
        h = c % N_HEADS
        lo = first_query(diag)
        m_prev = m_ref[c, :, lo:]
        m_next = jnp.maximum(m_prev, mx_ref[c, :, lo:])
        alpha = jnp.exp2(m_prev - m_next)
        p = jnp.exp2(st_ref[c, :, lo:] - m_next).astype(BF16)
        vt = vt_ref[0, j, h * V_ROWS:(h + 1) * V_ROWS, :]
        acc_ref[c, :, lo:] = alpha * acc_ref[c, :, lo:] + jnp.dot(vt, p, preferred_element_type=F32)
        m_ref[c, :, lo:] = m_next

    def k_block(j):
        return k_ref[0, pl.ds(pl.multiple_of(j * tk, tk), tk), :]

    def scores(j, diag):
        kb = k_block(j)
        for c in range(n_chain):
            score(c, kb, diag)

    def values(j, diag):
        for c in range(n_chain):
            value(c, j, diag)

    def values_then_scores(j, diag):
        kb = k_block(j + 1)
        for c in range(n_chain):
            value(c, j, diag - 1 if diag else None)
            score(c, kb, diag)

    before = ratio * i

    @pl.when(i == 0)
    def _():
        scores(0, 0)

    @pl.when(i > 0)
    def _():
        scores(0, None)

    def body(t, carry):
        values_then_scores(2 * t, None)
        values_then_scores(2 * t + 1, None)
        return carry

    plain = jnp.maximum(before - 1, 0)
    lax.fori_loop(0, plain // 2, body, 0)

    @pl.when(plain % 2 == 1)
    def _():
        values_then_scores(plain - 1, None)

    @pl.when(i > 0)
    def _():
        values_then_scores(before - 1, 0)

    for diag in range(1, ratio):
        values_then_scores(before + diag - 1, diag)
    values(before + ratio - 1, ratio - 1)

    lv = lam_ref[...]
    lam = (jnp.exp(jnp.sum(lv[0:1] * lv[1:2], axis=-1, keepdims=True))
           - jnp.exp(jnp.sum(lv[2:3] * lv[3:4], axis=-1, keepdims=True)) + lam_init)
    for h in range(N_HEADS):
        a1, a2 = acc_ref[h], acc_ref[N_HEADS + h]
        oh = (a1[:HEAD_DIM] / a1[HEAD_DIM:HEAD_DIM + 1]
              - lam * (a2[:HEAD_DIM] / a2[HEAD_DIM:HEAD_DIM + 1]))
        ms = jnp.mean(oh * oh, axis=0, keepdims=True)
        g = g_ref[h * HEAD_DIM:(h + 1) * HEAD_DIM, :]
        yt_ref[h * HEAD_DIM:(h + 1) * HEAD_DIM, :] = oh * lax.rsqrt(ms + NORM_EPS) * (g * (1.0 - lam_init))
    o_ref[0] = jnp.transpose(yt_ref[...]).astype(BF16)


def _diff_attn(qt, k, vt, lam_vecs, g_col, lam_init, tq):
    b, s, _ = k.shape
    nk, tk = vt.shape[1], vt.shape[3]
    assert tq % tk == 0 and s % tq == 0
    n_chain = 2 * N_HEADS
    return pl.pallas_call(
        functools.partial(_diff_attn_kernel, tq=tq, lam_init=lam_init),
        grid=(b, s // tq),
        in_specs=[
            pl.BlockSpec((4, QK_DIM), lambda i, j: (0, 0)),
            pl.BlockSpec((GW, 1), lambda i, j: (0, 0)),
            pl.BlockSpec((1, GW, tq), lambda i, j: (i, 0, j)),
            pl.BlockSpec((1, s, GW), lambda i, j: (i, 0, 0)),
            pl.BlockSpec((1, nk, N_HEADS * V_ROWS, tk), lambda i, j: (i, 0, 0, 0)),
        ],
        out_specs=pl.BlockSpec((1, tq, GW), lambda i, j: (i, j, 0)),
        out_shape=jax.ShapeDtypeStruct((b, s, GW), BF16),
        scratch_shapes=[
            pltpu.VMEM((n_chain, GW, tq), BF16),
            pltpu.VMEM((n_chain, 1, tq), F32),
            pltpu.VMEM((n_chain, V_ROWS, tq), F32),
            pltpu.VMEM((GW, tq), F32),
            pltpu.VMEM((n_chain, tk, tq), F32),
            pltpu.VMEM((n_chain, 1, tq), F32),
        ],
        compiler_params=_params("arbitrary", "arbitrary"),
        name="diff_attn",
    )(lam_vecs, g_col, qt, k, vt)


def _out_proj_kernel(x_ref, yab_ref, o1_ref, o2_ref, o3_ref, l1_ref, l2_ref, l3_ref, yd_ref, w_ref, out_ref,
                     nat_ref):
    tm = x_ref.shape[0]

    def natural(ref, dil, slot):
        for r in range(dil):
            for half in range(GW // LANES):
                col = r * GW + half * LANES
                nat_ref[2 * slot + half, pl.ds(r, tm // dil, stride=dil), :] = ref[:, col:col + LANES].astype(F32)
        return jnp.concatenate([nat_ref[2 * slot], nat_ref[2 * slot + 1]], axis=1)

    d2, d3 = DILATIONS[1:]
    l1, l2, l3 = l1_ref[...], natural(l2_ref, d2, 0), natural(l3_ref, d3, 1)
    o2, o3 = natural(o2_ref, d2, 2), natural(o3_ref, d3, 3)
    m = jnp.maximum(jnp.maximum(l1, l2), l3)
    w1, w2, w3 = jnp.exp(l1 - m), jnp.exp(l2 - m), jnp.exp(l3 - m)
    yc = (w1 * o1_ref[...] + w2 * o2 + w3 * o3) / (w1 + w2 + w3)
    acc = jnp.dot(yab_ref[...], w_ref[0:2 * GW, :], preferred_element_type=F32)
    acc = acc + jnp.dot(yc.astype(BF16), w_ref[2 * GW:3 * GW, :], preferred_element_type=F32)
    acc = acc + jnp.dot(yd_ref[...], w_ref[3 * GW:4 * GW, :], preferred_element_type=F32)
    out_ref[...] = x_ref[...] + acc


def _out_proj(x, yab, os_, lses, yd, w, tm):
    t, d = x.shape
    rows = lambda c: pl.BlockSpec((tm, c), lambda i: (i, 0))
    branch = [pl.BlockSpec((tm // dil, dil * GW), lambda i: (i, 0)) for dil in DILATIONS]
    return pl.pallas_call(
        _out_proj_kernel,
        grid=(t // tm,),
        in_specs=[rows(d), rows(2 * GW)] + branch * 2 + [rows(GW), pl.BlockSpec((d, d), lambda i: (0, 0))],
        out_specs=rows(d),
        out_shape=jax.ShapeDtypeStruct((t, d), F32),
        scratch_shapes=[pltpu.VMEM((4 * GW // LANES, tm, LANES), F32)],
        compiler_params=_params("arbitrary"),
        name="out_proj",
    )(x, yab, *os_, *lses, yd, w)


def _ffn_kernel(x_ref, g_ref, wg_ref, wu_ref, wd_ref, out_ref, acc_ref, *, tf):
    hn = _rms(x_ref[...], g_ref[...]).astype(BF16)
    acc_ref[...] = x_ref[...]
    for f0 in range(0, wg_ref.shape[1], tf):
        gate = jnp.dot(hn, wg_ref[:, f0:f0 + tf], preferred_element_type=F32)
        up = jnp.dot(hn, wu_ref[:, f0:f0 + tf], preferred_element_type=F32)
        act = (gate * jax.nn.sigmoid(gate) * up).astype(BF16)
        acc_ref[...] += jnp.dot(act, wd_ref[f0:f0 + tf, :], preferred_element_type=F32)
    out_ref[...] = acc_ref[...]


def _ffn(x, g, wg, wu, wd, tm, tf):
    t, d = x.shape
    ff = wg.shape[1]
    assert ff % tf == 0
    whole = lambda shape: pl.BlockSpec(shape, lambda i: (0, 0))
    return pl.pallas_call(
        functools.partial(_ffn_kernel, tf=tf),
        grid=(t // tm,),
        in_specs=[pl.BlockSpec((tm, d), lambda i: (i, 0)), whole((1, d)), whole((d, ff)), whole((d, ff)),
                  whole((ff, d))],
        out_specs=pl.BlockSpec((tm, d), lambda i: (i, 0)),
        out_shape=jax.ShapeDtypeStruct((t, d), F32),
        scratch_shapes=[pltpu.VMEM((tm, d), F32)],
        compiler_params=_params("arbitrary"),
        name="ffn_dense",
    )(x, g, wg, wu, wd)


META_E1, META_E2, META_RANK1, META_RANK2, META_G1, META_G2 = range(6)


def _router_kernel(x_ref, g_ref, rhi_ref, rlo_ref, hn_ref, meta_ref, after_ref, carry_ref, *, blocks_per_chunk):
    @pl.when(pl.program_id(0) % blocks_per_chunk == 0)
    def _():
        carry_ref[...] = jnp.zeros_like(carry_ref)

    h = _rms(x_ref[...], g_ref[...])
    hn_ref[...] = h.astype(BF16)
    h_hi = h.astype(BF16)
    h_lo = (h - h_hi.astype(F32)).astype(BF16)
    nt = lambda a, b: lax.dot_general(a, b, NT_DIMS, preferred_element_type=F32)
    logits = nt(rhi_ref[...], h_hi) + nt(rlo_ref[...], h_hi) + nt(rhi_ref[...], h_lo)
    tm = logits.shape[1]
    expert = lax.broadcasted_iota(jnp.int32, logits.shape, 0)
    m1 = jnp.max(logits, axis=0, keepdims=True)
    i1 = jnp.min(jnp.where(logits == m1, expert, N_EXPERTS), axis=0, keepdims=True)
    rest = jnp.where(expert == i1, -jnp.inf, logits)
    m2 = jnp.max(rest, axis=0, keepdims=True)
    i2 = jnp.min(jnp.where(rest == m2, expert, N_EXPERTS), axis=0, keepdims=True)
    e2 = jnp.exp(m2 - m1)
    g1 = 1.0 / (1.0 + e2)
    g2 = e2 / (1.0 + e2)

    oh1 = expert == i1
    oh2 = expert == i2
    both = jnp.where(oh1, 1.0, jnp.where(oh2, 1.0, 0.0))
    row = lax.broadcasted_iota(jnp.int32, (tm, tm), 0)
    col = lax.broadcasted_iota(jnp.int32, (tm, tm), 1)
    earlier = jnp.where(row < col, 1.0, 0.0).astype(BF16)
    carry = carry_ref[:, 0:1]
    prefix = jnp.dot(both.astype(BF16), earlier, preferred_element_type=F32) + carry
    rank1 = jnp.sum(jnp.where(oh1, prefix, 0.0), axis=0, keepdims=True)
    rank2 = jnp.sum(jnp.where(oh2, prefix, 0.0), axis=0, keepdims=True)
    total = carry + jnp.sum(both, axis=1, keepdims=True)

    meta = jnp.zeros(logits.shape, F32)
    for k, val in ((META_E1, i1.astype(F32)), (META_E2, i2.astype(F32)), (META_RANK1, rank1),
                   (META_RANK2, rank2), (META_G1, g1), (META_G2, g2)):
        meta = jnp.where(expert == k, val, meta)
    meta_ref[...] = meta
    after_ref[0] = jnp.broadcast_to(total, after_ref.shape[1:])
    carry_ref[...] = jnp.broadcast_to(total, carry_ref.shape)


def _router(x, g, router_w, tm, blocks_per_chunk):
    t, d = x.shape
    nblk = t // tm
    assert router_w.shape == (d, N_EXPERTS) and N_EXPERTS == SUBLANES
    r_t = router_w.T
    r_hi = r_t.astype(BF16)
    counts = pl.BlockSpec((1, N_EXPERTS, LANES), lambda i: (i, 0, 0))
    return pl.pallas_call(
        functools.partial(_router_kernel, blocks_per_chunk=blocks_per_chunk),
        grid=(nblk,),
        in_specs=[
            pl.BlockSpec((tm, d), lambda i: (i, 0)),
            pl.BlockSpec((1, d), lambda i: (0, 0)),
            pl.BlockSpec((N_EXPERTS, d), lambda i: (0, 0)),
            pl.BlockSpec((N_EXPERTS, d), lambda i: (0, 0)),
        ],
        out_specs=[
            pl.BlockSpec((tm, d), lambda i: (i, 0)),
            pl.BlockSpec((N_EXPERTS, tm), lambda i: (0, i)),
            counts,
        ],
        out_shape=[
            jax.ShapeDtypeStruct((t, d), BF16),
            jax.ShapeDtypeStruct((N_EXPERTS, t), F32),
            jax.ShapeDtypeStruct((nblk, N_EXPERTS, LANES), F32),
        ],
        scratch_shapes=[pltpu.VMEM((N_EXPERTS, LANES), F32)],
        compiler_params=_params("arbitrary"),
        name="router",
    )(x, g, r_hi, (r_t - r_hi.astype(F32)).astype(BF16))


SLOT_ROW1, SLOT_ROW2, SLOT_G1, SLOT_G2 = range(4)


def _moe_kernel(nt_ref, toff_ref, cuts_ref, hn_ref, slotc_ref, slotr_ref, wg_ref, wu_ref, wd_ref, out_ref,
                xs_ref, ys_ref, *, tr, sub, win):
    c, e, f = pl.program_id(0), pl.program_id(1), pl.program_id(2)
    last_f = pl.num_programs(2) - 1
    grp = c * N_EXPERTS + e
    n_tiles = nt_ref[grp]
    row0 = toff_ref[grp] * tr
    tc = hn_ref.shape[0]
    nsub = tc // sub

    @pl.when((e == 0) & (f == 0))
    def _():
        out_ref[...] = jnp.zeros_like(out_ref)

    def rows_of(r):
        return pl.ds(pl.multiple_of(r * tr, tr), tr)

    def token_window(r):
        base = grp * nsub
        first = r * tr
        last = jnp.minimum(first + tr, cuts_ref[base + nsub - 1]) - 1
        lo = hi = 0
        for k in range(nsub - 1):
            cut = cuts_ref[base + k]
            lo = lo + (cut <= first).astype(jnp.int32)
            hi = hi + (cut <= last).astype(jnp.int32)
        start = jnp.minimum(lo, nsub - win)
        return hi < start + win, pl.multiple_of(start * sub, sub)

    def spans(r, fn):
        covered, tok0 = token_window(r)

        @pl.when(covered)
        def _():
            fn(r, tok0, win * sub)

        @pl.when(jnp.logical_not(covered))
        def _():
            fn(r, 0, tc)

    def gather_span(r, tok0, ntok):
        toks = pl.ds(tok0, ntok)
        target = (lax.broadcasted_iota(jnp.int32, (tr, ntok), 0) + (row0 + r * tr)).astype(F32)
        hit1 = slotr_ref[SLOT_ROW1:SLOT_ROW1 + 1, toks] == target
        hit2 = slotr_ref[SLOT_ROW2:SLOT_ROW2 + 1, toks] == target
        oh = jnp.where(hit1, 1.0, jnp.where(hit2, 1.0, 0.0)).astype(BF16)
        xs_ref[rows_of(r), :] = jnp.dot(oh, hn_ref[toks, :], preferred_element_type=F32).astype(BF16)

    def gather_tile(r, carry):
        spans(r, gather_span)
        return carry

    def ffn_tiles(r, count, first):
        rows = pl.ds(pl.multiple_of(r * tr, tr), count * tr)
        xt = xs_ref[rows, :]
        for f0 in range(0, wg_ref.shape[1], FFN_SUBTILE):
            gate = jnp.dot(xt, wg_ref[:, f0:f0 + FFN_SUBTILE], preferred_element_type=F32)
            up = jnp.dot(xt, wu_ref[:, f0:f0 + FFN_SUBTILE], preferred_element_type=F32)
            act = (gate * jax.nn.sigmoid(gate) * up).astype(BF16)
            y = jnp.dot(act, wd_ref[f0:f0 + FFN_SUBTILE, :], preferred_element_type=F32)
            if first and f0 == 0:
                ys_ref[rows, :] = y
            else:
                ys_ref[rows, :] += y

    def scatter_span(r, tok0, ntok):
        toks = pl.ds(tok0, ntok)
        yb = ys_ref[rows_of(r), :].astype(BF16)
        target = (lax.broadcasted_iota(jnp.int32, (ntok, tr), 1) + (row0 + r * tr)).astype(F32)
        hit1 = slotc_ref[toks, SLOT_ROW1:SLOT_ROW1 + 1] == target
        hit2 = slotc_ref[toks, SLOT_ROW2:SLOT_ROW2 + 1] == target
        gh = jnp.where(hit1, slotc_ref[toks, SLOT_G1:SLOT_G1 + 1],
                       jnp.where(hit2, slotc_ref[toks, SLOT_G2:SLOT_G2 + 1], 0.0)).astype(BF16)
        out_ref[toks, :] += jnp.dot(gh, yb, preferred_element_type=F32)

    def scatter_tile(r, carry):
        spans(r, scatter_span)
        return carry

    def ffn_all(first):
        def pair(t, carry):
            ffn_tiles(2 * t, 2, first)
            return carry

        lax.fori_loop(0, n_tiles // 2, pair, 0)

        @pl.when(n_tiles % 2 == 1)
        def _():
            ffn_tiles(n_tiles - 1, 1, first)

    @pl.when(f == 0)
    def _():
        lax.fori_loop(0, n_tiles, gather_tile, 0)
        ffn_all(True)

    @pl.when(f > 0)
    def _():
        ffn_all(False)

    @pl.when(f == last_f)
    def _():
        lax.fori_loop(0, n_tiles, scatter_tile, 0)


def _moe(hn, slot_cols, slot_rows, nt, toff, cuts, wg, wu, wd, tc, tr, tf, sub, win):
    t, d = hn.shape
    ne, _, ff = wg.shape
    grid_spec = pltpu.PrefetchScalarGridSpec(
        num_scalar_prefetch=3,
        grid=(t // tc, ne, ff // tf),
        in_specs=[
            pl.BlockSpec((tc, d), lambda c, e, f, *_: (c, 0), pipeline_mode=pl.Buffered(1)),
            pl.BlockSpec((tc, 4), lambda c, e, f, *_: (c, 0), pipeline_mode=pl.Buffered(1)),
            pl.BlockSpec((8, tc), lambda c, e, f, *_: (0, c), pipeline_mode=pl.Buffered(1)),
            pl.BlockSpec((None, d, tf), lambda c, e, f, *_: (e, 0, f)),
            pl.BlockSpec((None, d, tf), lambda c, e, f, *_: (e, 0, f)),
            pl.BlockSpec((None, tf, d), lambda c, e, f, *_: (e, f, 0)),
        ],
        out_specs=pl.BlockSpec((tc, d), lambda c, e, f, *_: (c, 0), pipeline_mode=pl.Buffered(1)),
        scratch_shapes=[
            pltpu.VMEM((tc + tr, d), BF16),
            pltpu.VMEM((tc + tr, d), F32),
        ],
    )
    return pl.pallas_call(
        functools.partial(_moe_kernel, tr=tr, sub=sub, win=win),
        grid_spec=grid_spec,
        out_shape=jax.ShapeDtypeStruct((t, d), F32),
        compiler_params=_params("arbitrary", "arbitrary", "arbitrary"),
        name="moe",
    )(nt, toff, cuts, hn, slot_cols, slot_rows, wg, wu, wd)


def _final_kernel(x_ref, y_ref, g_ref, out_ref):
    out_ref[...] = _rms(x_ref[...] + y_ref[...], g_ref[...])


def _final_norm(x, y, g, tm):
    t, d = x.shape
    rows = pl.BlockSpec((tm, d), lambda i: (i, 0))
    return pl.pallas_call(
        _final_kernel,
        grid=(t // tm,),
        in_specs=[rows, rows, pl.BlockSpec((1, d), lambda i: (0, 0))],
        out_specs=rows,
        out_shape=jax.ShapeDtypeStruct((t, d), F32),
        compiler_params=_params("arbitrary"),
        name="final_norm",
    )(x, y, g)


def _routed_swiglu(x, norm_g, router_w, wg, wu, wd, final_g, tc, tr, tf, sub, win):
    t, d = x.shape
    nc, nsub = t // tc, tc // sub
    hn, meta, after = _router(x, norm_g, router_w, tm=sub, blocks_per_chunk=nsub)
    through = after[:, :, 0].astype(jnp.int32).reshape(nc, nsub, N_EXPERTS)
    cuts = through.transpose(0, 2, 1)
    total = through[:, -1]
    nt = (total + tr - 1) // tr
    toff = jnp.cumsum(nt, axis=1) - nt
    first_row = jnp.repeat((toff * tr).astype(F32), tc, axis=0)
    experts = jnp.arange(N_EXPERTS, dtype=F32)
    pick = lambda e: jnp.sum(jnp.where(e[:, None] == experts, first_row, 0.0), axis=1)
    row1 = pick(meta[META_E1]) + meta[META_RANK1]
    row2 = pick(meta[META_E2]) + meta[META_RANK2]
    slot_rows = jnp.zeros((8, t), F32).at[0:4].set(jnp.stack([row1, row2, meta[META_G1], meta[META_G2]]))
    slot_cols = slot_rows[0:4].T
    y = _moe(hn, slot_cols, slot_rows, nt.reshape(-1), toff.reshape(-1), cuts.reshape(-1), wg, wu, wd,
             tc=tc, tr=tr, tf=tf, sub=sub, win=win)
    return _final_norm(x, y, final_g, tm=min(1024, t))


def _permute_w_in(w):
    def perm(block):
        d = block.shape[0]
        return block.reshape(d, N_HEADS, 2, QK_DIM).transpose(0, 2, 1, 3).reshape(d, GW)
    parts = [w[:, :6 * GW], perm(w[:, 6 * GW:7 * GW]), perm(w[:, 7 * GW:8 * GW]), w[:, 8 * GW:]]
    return jnp.concatenate(parts, axis=1)


def _block_diag(pool_w):
    n, p, _ = pool_w.shape
    out = jnp.zeros((n * p, n * p), pool_w.dtype)
    for gi in range(n):
        out = out.at[gi * p:(gi + 1) * p, gi * p:(gi + 1) * p].set(pool_w[gi])
    return out


def _token_mixing(x, layer, lam_init, p, rope_c, rope_d):
    b, s, d = x.shape
    row = lambda v: v.reshape(1, -1)
    w_in = _permute_w_in(p["w_in"][layer]).astype(BF16)
    uab, qkvc, qkvc4, qkvc16, qdt, kd, vdt = _norm_proj(x, row(p["norm1_g"][layer]), w_in, rope_c, rope_d,
                                                        tm=512, tk=256)
    dww = jnp.zeros((HALO, GW), F32).at[:CONV_WIDTH].set(p["conv_dw_w"][layer])
    yab = _local_mixers(
        uab, dww, row(p["conv_dw_b"][layer]), row(p["conv_ln_g"][layer]), row(p["conv_ln_b"][layer]),
        p["conv_pw_w"][layer].astype(BF16), _block_diag(p["pool_w"][layer]).astype(BF16),
        row(p["pool_scale"][layer]), ts=512)
    os_, lses = [], []
    for dil, qkv in zip(DILATIONS, (qkvc, qkvc4, qkvc16)):
        o, lse = _band_attn(qkv, dil)
        os_.append(o.reshape(b * s // dil, dil * GW))
        lses.append(lse.reshape(b * s // dil, dil * GW))
    yd = _diff_attn(qdt, kd, vdt, p["diff_lam"][layer], jnp.tile(p["diff_ln_g"][layer], N_HEADS).reshape(GW, 1),
                    lam_init, tq=512)
    return _out_proj(x.reshape(b * s, d), yab.reshape(b * s, 2 * GW), os_, lses, yd.reshape(b * s, GW),
                     p["w_out"][layer].astype(BF16), tm=512)


def kernel(x, norm1_g, w_in, conv_dw_w, conv_dw_b, conv_ln_g, conv_ln_b, conv_pw_w, pool_w, pool_scale,
           diff_lam, diff_ln_g, w_out, norm2_g, ffn_w_gate, ffn_w_up, ffn_w_down, moe_router, moe_w_gate,
           moe_w_up, moe_w_down, final_g):
    b, s, d = x.shape
    p = dict(norm1_g=norm1_g, w_in=w_in, conv_dw_w=conv_dw_w, conv_dw_b=conv_dw_b, conv_ln_g=conv_ln_g,
             conv_ln_b=conv_ln_b, conv_pw_w=conv_pw_w, pool_w=pool_w, pool_scale=pool_scale,
             diff_lam=diff_lam, diff_ln_g=diff_ln_g, w_out=w_out)
    rope_c = _rope_table(s, HEAD_DIM, HEAD_DIM // 4)
    rope_d = _rope_table(s, QK_DIM, QK_DIM // 4)
    row = lambda v: v.reshape(1, -1)

    x1 = _token_mixing(x, 0, 0.8 - 0.6 * math.exp(-0.3 * 0), p, rope_c, rope_d)
    x2 = _ffn(x1, row(norm2_g[0]), ffn_w_gate[0].astype(BF16), ffn_w_up[0].astype(BF16),
              ffn_w_down[0].astype(BF16), tm=512, tf=256)

    x3 = _token_mixing(x2.reshape(b, s, d), 1, 0.8 - 0.6 * math.exp(-0.3 * 1), p, rope_c, rope_d)
    out = _routed_swiglu(x3, row(norm2_g[1]), moe_router[0], moe_w_gate[0].astype(BF16),
                         moe_w_up[0].astype(BF16), moe_w_down[0].astype(BF16), row(final_g),
                         tc=2048, tr=256, tf=1792, sub=256, win=5)
    return out.reshape(b, s, d)
```
